```python
import math
import jax, jax.numpy as jnp
from jax import lax
import numpy as np


D_MODEL = 1024
BATCH = 4
SEQ = 4096
DEPTH = 4

HEAD_DIM = 64
ROPE_DIM = HEAD_DIM // 4
ROPE_THETA = 500000.0
Q_BLOCK = 128
FOX_HEADS = D_MODEL // (2 * HEAD_DIM)
NSA_HEADS = D_MODEL // (2 * HEAD_DIM)
NSA_KV_GROUPS = 2
NSA_HPG = NSA_HEADS // NSA_KV_GROUPS
CMP_BLOCK = 32
CMP_STRIDE = 16
CMP_HIDDEN = 2 * HEAD_DIM
SLC_BLOCK = 64
SLC_TOPK = 16
WINDOW = 512
DIFF_HEADS = D_MODEL // (2 * HEAD_DIM)
MEM_TOKENS = 256
MEM_HEADS = 4
D_FF = 256 * math.ceil(8 * D_MODEL / 3 / 256)
N_EVEN = (DEPTH + 1) // 2
N_ODD = DEPTH // 2
RMS_EPS = 1e-6

FOX_W = FOX_HEADS * HEAD_DIM
NSA_W = NSA_HEADS * HEAD_DIM
NSA_KV_W = NSA_KV_GROUPS * HEAD_DIM
EVEN_SPLITS = (FOX_W, FOX_W, FOX_W, FOX_HEADS, NSA_W) + (NSA_KV_W,) * 6 + (3 * NSA_HEADS,)
EVEN_IN = sum(EVEN_SPLITS)
MIX_W = FOX_W + NSA_W
DIFF_QK_W = DIFF_HEADS * 2 * HEAD_DIM
DIFF_V_W = DIFF_HEADS * 2 * HEAD_DIM
ODD_IN = 2 * DIFF_QK_W + DIFF_V_W
MEM_W = MEM_HEADS * HEAD_DIM

kernel_name = 'fox_nsa_diff_hybrid_trunk'


def rms_norm(x, g):
    xf = x.astype(jnp.float32)
    y = xf * lax.rsqrt(jnp.mean(xf * xf, axis=-1, keepdims=True) + RMS_EPS)
    return (y * g.astype(jnp.float32)).astype(x.dtype)


def rope_tables(positions):
    inv = ROPE_THETA ** (-jnp.arange(0, ROPE_DIM, 2, dtype=jnp.float32) / ROPE_DIM)
    ang = positions.astype(jnp.float32)[..., None] * inv
    return jnp.cos(ang), jnp.sin(ang)


def apply_partial_rope(x, cos, sin):
    half = ROPE_DIM // 2
    x1, x2, rest = x[..., :half], x[..., half:ROPE_DIM], x[..., ROPE_DIM:]
    c = cos[:, :, None, :].astype(x.dtype)
    s = sin[:, :, None, :].astype(x.dtype)
    return jnp.concatenate([x1 * c - x2 * s, x2 * c + x1 * s, rest], axis=-1)


def masked_softmax(logits, mask):
    z = jnp.where(mask, logits.astype(jnp.float32), -jnp.inf)
    m = jnp.max(z, axis=-1, keepdims=True)
    m = jnp.where(jnp.isfinite(m), m, 0.0)
    p = jnp.exp(z - m)
    return p / jnp.maximum(jnp.sum(p, axis=-1, keepdims=True), 1e-30)


def stack_blocks(out, B, T):
    return jnp.moveaxis(out, 0, 1).reshape((B, T) + out.shape[3:])


def fox_attention(q, k, v, log_f):
    B, T, H, dh = q.shape
    c = jnp.cumsum(log_f, axis=1)
    c_k = jnp.transpose(c, (0, 2, 1))[:, :, None, :]
    kpos = jnp.arange(T)
    scale = dh ** -0.5

    def block(i):
        q0 = i * Q_BLOCK
        qb = lax.dynamic_slice_in_dim(q, q0, Q_BLOCK, 1)
        cq = jnp.transpose(lax.dynamic_slice_in_dim(c, q0, Q_BLOCK, 1), (0, 2, 1))[..., None]
        s = jnp.einsum('bqhd,bkhd->bhqk', qb, k).astype(jnp.float32) * scale + (cq - c_k)
        qpos = q0 + jnp.arange(Q_BLOCK)
        p = masked_softmax(s, kpos[None, :] <= qpos[:, None])
        return jnp.einsum('bhqk,bkhd->bqhd', p.astype(v.dtype), v)

    return stack_blocks(lax.map(block, jnp.arange(T // Q_BLOCK)), B, T)


def compress(x, pos, w1, w2):
    B, T, G, dh = x.shape
    nc = (T - CMP_BLOCK) // CMP_STRIDE + 1
    idx = np.arange(nc)[:, None] * CMP_STRIDE + np.arange(CMP_BLOCK)[None, :]
    blocks = x[:, idx] + pos[None, None, :, None, :].astype(x.dtype)
    flat = jnp.moveaxis(blocks, 3, 2).reshape(B, nc, G, CMP_BLOCK * dh)
    return jax.nn.gelu(flat @ w1) @ w2


def nsa_attention(q, kc, vc, ks, vs, kw, vw, gates):
    B, T, H, dh = q.shape
    G = NSA_KV_GROUPS
    nc = kc.shape[1]
    ns = T // SLC_BLOCK
    n_sel = min(SLC_TOPK, ns)
    scale = dh ** -0.5
    cmp_start = np.arange(nc) * CMP_STRIDE
    slc_start = np.arange(ns) * SLC_BLOCK
    cmp_end = jnp.asarray(cmp_start + CMP_BLOCK - 1)
    ov = np.clip(np.minimum(cmp_start[:, None] + CMP_BLOCK, slc_start[None, :] + SLC_BLOCK)
                 - np.maximum(cmp_start[:, None], slc_start[None, :]), 0, None) / CMP_BLOCK
    overlap = jnp.asarray(ov, dtype=jnp.float32)
    blk = jnp.arange(ns)
    k_blocks = jnp.transpose(ks.reshape(B, ns, SLC_BLOCK, G, dh), (0, 3, 1, 2, 4))
    v_blocks = jnp.transpose(vs.reshape(B, ns, SLC_BLOCK, G, dh), (0, 3, 1, 2, 4))
    kw_pad = jnp.pad(kw, ((0, 0), (WINDOW, 0), (0, 0), (0, 0)))
    vw_pad = jnp.pad(vw, ((0, 0), (WINDOW, 0), (0, 0), (0, 0)))
    bi = jnp.arange(B)[:, None, None, None]
    gi = jnp.arange(G)[None, :, None, None]

    def block(i):
        q0 = i * Q_BLOCK
        qpos = q0 + jnp.arange(Q_BLOCK)
        qg = lax.dynamic_slice_in_dim(q, q0, Q_BLOCK, 1).reshape(B, Q_BLOCK, G, NSA_HPG, dh)
        gb = lax.dynamic_slice_in_dim(gates, q0, Q_BLOCK, 1).reshape(B, Q_BLOCK, G, NSA_HPG, 3)
        s_c = jnp.einsum('bqgnd,bcgd->bgnqc', qg, kc).astype(jnp.float32) * scale
        p_c = masked_softmax(s_c, cmp_end[None, :] <= qpos[:, None])
        o_c = jnp.einsum('bgnqc,bcgd->bqgnd', p_c.astype(vc.dtype), vc)
        imp = jnp.einsum('bgnqc,cs->bgqs', p_c, overlap)
        cur = (qpos // SLC_BLOCK)[:, None]
        valid = blk[None, :] * SLC_BLOCK <= qpos[:, None]
        forced = (blk[None, :] == 0) | (blk[None, :] == cur) | (blk[None, :] == cur - 1)
        score = jnp.where(valid, jnp.where(forced, jnp.inf, imp), -jnp.inf)
        _, sel = lax.top_k(score, n_sel)
        k_sel = k_blocks[bi, gi, sel]
        v_sel = v_blocks[bi, gi, sel].reshape(B, G, Q_BLOCK, n_sel * SLC_BLOCK, dh)
        tok = sel[..., None] * SLC_BLOCK + jnp.arange(SLC_BLOCK)
        smask = (tok <= qpos[None, None, :, None, None]).reshape(B, G, 1, Q_BLOCK, n_sel * SLC_BLOCK)
        s_s = jnp.einsum('bqgnd,bgqkld->bgnqkl', qg, k_sel).astype(jnp.float32) * scale
        p_s = masked_softmax(s_s.reshape(B, G, NSA_HPG, Q_BLOCK, n_sel * SLC_BLOCK), smask)
        o_s = jnp.einsum('bgnqm,bgqmd->bqgnd', p_s.astype(v_sel.dtype), v_sel)
        kwb = lax.dynamic_slice_in_dim(kw_pad, q0, WINDOW + Q_BLOCK, 1)
        vwb = lax.dynamic_slice_in_dim(vw_pad, q0, WINDOW + Q_BLOCK, 1)
        kpos = q0 - WINDOW + jnp.arange(WINDOW + Q_BLOCK)
        wmask = ((kpos[None, :] <= qpos[:, None]) & (kpos[None, :] > qpos[:, None] - WINDOW)
                 & (kpos[None, :] >= 0))
        s_w = jnp.einsum('bqgnd,bkgd->bgnqk', qg, kwb).astype(jnp.float32) * scale
        p_w = masked_softmax(s_w, wmask)
        o_w = jnp.einsum('bgnqk,bkgd->bqgnd', p_w.astype(vwb.dtype), vwb)
        o = gb[..., 0:1] * o_c + gb[..., 1:2] * o_s + gb[..., 2:3] * o_w
        return o.reshape(B, Q_BLOCK, H, dh)

    return stack_blocks(lax.map(block, jnp.arange(T // Q_BLOCK)), B, T)


def diff_attention(q1, q2, k1, k2, v, lam):
    B, T, H, dh = q1.shape
    kpos = jnp.arange(T)
    scale = dh ** -0.5

    def block(i):
        q0 = i * Q_BLOCK
        qpos = q0 + jnp.arange(Q_BLOCK)
        mask = kpos[None, :] <= qpos[:, None]
        qb1 = lax.dynamic_slice_in_dim(q1, q0, Q_BLOCK, 1)
        qb2 = lax.dynamic_slice_in_dim(q2, q0, Q_BLOCK, 1)
        a1 = masked_softmax(jnp.einsum('bqhd,bkhd->bhqk', qb1, k1).astype(jnp.float32) * scale, mask)
        a2 = masked_softmax(jnp.einsum('bqhd,bkhd->bhqk', qb2, k2).astype(jnp.float32) * scale, mask)
        p = a1 - lam * a2
        return jnp.einsum('bhqk,bkhd->bqhd', p.astype(v.dtype), v)

    return stack_blocks(lax.map(block, jnp.arange(T // Q_BLOCK)), B, T)


def even_mixer(h, cos, sin, w_in, f_bias, cpk, c1k, c2k, cpv, c1v, c2v, w_out):
    B, T, _ = h.shape
    G = NSA_KV_GROUPS
    offsets = [int(o) for o in np.cumsum(EVEN_SPLITS)[:-1]]
    fq, fk, fv, fl, nq, kc, vc, ksl, vsl, kwn, vwn, gl = jnp.split(h @ w_in, offsets, axis=-1)
    heads = lambda t, n: t.reshape(B, T, n, HEAD_DIM)
    rope = lambda t: apply_partial_rope(t, cos, sin)
    log_f = jax.nn.log_sigmoid(fl.astype(jnp.float32) + f_bias.astype(jnp.float32))
    o_fox = fox_attention(heads(fq, FOX_HEADS), heads(fk, FOX_HEADS), heads(fv, FOX_HEADS), log_f)
    o_nsa = nsa_attention(
        rope(heads(nq, NSA_HEADS)),
        compress(rope(heads(kc, G)), cpk, c1k, c2k),
        compress(heads(vc, G), cpv, c1v, c2v),
        rope(heads(ksl, G)), heads(vsl, G),
        rope(heads(kwn, G)), heads(vwn, G),
        jax.nn.sigmoid(gl).reshape(B, T, NSA_HEADS, 3))
    o = jnp.concatenate([o_fox.reshape(B, T, FOX_W), o_nsa.reshape(B, T, NSA_W)], axis=-1)
    return o @ w_out


def diff_mixer(h, cos, sin, w_in, lam_p, subln_g, w_out, layer):
    B, T, _ = h.shape
    q, k, v = jnp.split(h @ w_in, [DIFF_QK_W, 2 * DIFF_QK_W], axis=-1)
    q = apply_partial_rope(q.reshape(B, T, 2 * DIFF_HEADS, HEAD_DIM), cos, sin).reshape(B, T, DIFF_HEADS, 2, HEAD_DIM)
    k = apply_partial_rope(k.reshape(B, T, 2 * DIFF_HEADS, HEAD_DIM), cos, sin).reshape(B, T, DIFF_HEADS, 2, HEAD_DIM)
    v = v.reshape(B, T, DIFF_HEADS, 2 * HEAD_DIM)
    lam_init = 0.8 - 0.6 * math.exp(-0.3 * layer)
    lp = lam_p.astype(jnp.float32)
    lam = jnp.exp(jnp.sum(lp[0] * lp[1])) - jnp.exp(jnp.sum(lp[2] * lp[3])) + lam_init
    o = diff_attention(q[..., 0, :], q[..., 1, :], k[..., 0, :], k[..., 1, :], v, lam)
    o = rms_norm(o, subln_g) * (1.0 - lam_init)
    return o.reshape(B, T, DIFF_V_W) @ w_out


def memory_cross_attention(h, mem_n, wq, wk, wv, wo):
    B, T, _ = h.shape
    M = mem_n.shape[1]
    q = (h @ wq).reshape(B, T, MEM_HEADS, HEAD_DIM)
    k = (mem_n @ wk).reshape(B, M, MEM_HEADS, HEAD_DIM)
    v = (mem_n @ wv).reshape(B, M, MEM_HEADS, HEAD_DIM)
    s = jnp.einsum('bqhd,bmhd->bhqm', q, k).astype(jnp.float32) * HEAD_DIM ** -0.5
    p = jax.nn.softmax(s, axis=-1)
    o = jnp.einsum('bhqm,bmhd->bqhd', p.astype(v.dtype), v).reshape(B, T, MEM_W)
    return o @ wo


def swiglu(h, wg, wu, wd):
    return (jax.nn.silu(h @ wg) * (h @ wu)) @ wd


def setup_inputs(seed: int = 0) -> dict:
    key = jax.random.key(seed)
    ks = jax.random.split(key, 32)

    def nrm(k, shape, fan_in):
        return jax.random.normal(k, shape, jnp.float32) * (fan_in ** -0.5)

    def gain(k, shape):
        return 1.0 + 0.02 * jax.random.normal(k, shape, jnp.float32)

    D = D_MODEL
    flat_cmp = CMP_BLOCK * HEAD_DIM
    return {
        'x': jax.random.normal(ks[0], (BATCH, SEQ, D), jnp.float32),
        'mem': jax.random.normal(ks[1], (BATCH, MEM_TOKENS, D), jnp.float32),
        'positions': jnp.broadcast_to(jnp.arange(SEQ, dtype=jnp.int32), (BATCH, SEQ)),
        'sandwich_g': gain(ks[2], (DEPTH, 6, D)),
        'mem_norm_g': gain(ks[3], (DEPTH, D)),
        'ev_w_in': nrm(ks[4], (N_EVEN, D, EVEN_IN), D),
        'ev_fox_fbias': 4.0 + 0.5 * jax.random.normal(ks[5], (N_EVEN, FOX_HEADS), jnp.float32),
        'ev_cmp_pos_k': 0.02 * jax.random.normal(ks[6], (N_EVEN, CMP_BLOCK, HEAD_DIM), jnp.float32),
        'ev_cmp_w1_k': nrm(ks[7], (N_EVEN, flat_cmp, CMP_HIDDEN), flat_cmp),
        'ev_cmp_w2_k': nrm(ks[8], (N_EVEN, CMP_HIDDEN, HEAD_DIM), CMP_HIDDEN),
        'ev_cmp_pos_v': 0.02 * jax.random.normal(ks[9], (N_EVEN, CMP_BLOCK, HEAD_DIM), jnp.float32),
        'ev_cmp_w1_v': nrm(ks[10], (N_EVEN, flat_cmp, CMP_HIDDEN), flat_cmp),
        'ev_cmp_w2_v': nrm(ks[11], (N_EVEN, CMP_HIDDEN, HEAD_DIM), CMP_HIDDEN),
        'ev_w_out': nrm(ks[12], (N_EVEN, MIX_W, D), MIX_W),
        'od_w_in': nrm(ks[13], (N_ODD, D, ODD_IN), D),
        'od_lambda': 0.1 * jax.random.normal(ks[14], (N_ODD, 4, HEAD_DIM), jnp.float32),
        'od_subln_g': gain(ks[15], (N_ODD, 2 * HEAD_DIM)),
        'od_w_out': nrm(ks[16], (N_ODD, DIFF_V_W, D), DIFF_V_W),
        'ca_wq': nrm(ks[17], (DEPTH, D, MEM_W), D),
        'ca_wk': nrm(ks[18], (DEPTH, D, MEM_W), D),
        'ca_wv': nrm(ks[19], (DEPTH, D, MEM_W), D),
        'ca_wo': nrm(ks[20], (DEPTH, MEM_W, D), MEM_W),
        'ffn_wg': nrm(ks[21], (DEPTH, D, D_FF), D),
        'ffn_wu': nrm(ks[22], (DEPTH, D, D_FF), D),
        'ffn_wd': nrm(ks[23], (DEPTH, D_FF, D), D_FF),
    }


def reference(x, mem, positions, sandwich_g, mem_norm_g, ev_w_in, ev_fox_fbias,
              ev_cmp_pos_k, ev_cmp_w1_k, ev_cmp_w2_k, ev_cmp_pos_v, ev_cmp_w1_v, ev_cmp_w2_v,
              ev_w_out, od_w_in, od_lambda, od_subln_g, od_w_out,
              ca_wq, ca_wk, ca_wv, ca_wo, ffn_wg, ffn_wu, ffn_wd):
    cos, sin = rope_tables(positions)
    for layer in range(DEPTH):
        g = sandwich_g[layer]
        h = rms_norm(x, g[0])
        if layer % 2 == 0:
            e = layer // 2
            y = even_mixer(h, cos, sin, ev_w_in[e], ev_fox_fbias[e],
                           ev_cmp_pos_k[e], ev_cmp_w1_k[e], ev_cmp_w2_k[e],
                           ev_cmp_pos_v[e], ev_cmp_w1_v[e], ev_cmp_w2_v[e], ev_w_out[e])
        else:
            o = layer // 2
            y = diff_mixer(h, cos, sin, od_w_in[o], od_lambda[o], od_subln_g[o], od_w_out[o], layer)
        x = x + rms_norm(y, g[1])
        mem_n = rms_norm(mem, mem_norm_g[layer])
        y = memory_cross_attention(rms_norm(x, g[2]), mem_n, ca_wq[layer], ca_wk[layer], ca_wv[layer], ca_wo[layer])
        x = x + rms_norm(y, g[3])
        y = swiglu(rms_norm(x, g[4]), ffn_wg[layer], ffn_wu[layer], ffn_wd[layer])
        x = x + rms_norm(y, g[5])
    return x
```

```python
import math
from functools import partial

import numpy as np
import jax
import jax.numpy as jnp
from jax import lax
from jax.experimental import pallas as pl
from jax.experimental.pallas import tpu as pltpu

F32 = jnp.float32
BF16 = jnp.bfloat16

D_MODEL = 1024
HEAD_DIM = 64
LANES = 128
ROPE_DIM = HEAD_DIM // 4
ROPE_THETA = 500000.0
FOX_HEADS = 8
NSA_HEADS = 8
NSA_KV_GROUPS = 2
NSA_HPG = NSA_HEADS // NSA_KV_GROUPS
CMP_BLOCK = 32
CMP_STRIDE = 16
CMP_HIDDEN = 2 * HEAD_DIM
SLC_BLOCK = 64
SLC_TOPK = 16
WINDOW = 512
DIFF_HEADS = 8
MEM_HEADS = 4
MEM_W = MEM_HEADS * HEAD_DIM
RMS_EPS = 1e-6
Q_SCALE = HEAD_DIM ** -0.5
NEG = -1e30
AUG_PER_HEAD = 6

EV_MAIN = 2816
FOX_W = FOX_HEADS * HEAD_DIM
NSA_W = NSA_HEADS * HEAD_DIM
CH_FQ, CH_FK, CH_FV, CH_NQ = 0, 4, 8, 12
CH_KC, CH_VC, CH_KS, CH_VS, CH_KW, CH_VW = 16, 17, 18, 19, 20, 21
EV_ROPE_CHUNKS = tuple(range(CH_NQ, CH_NQ + 4)) + (CH_KC, CH_KS, CH_KW)
EV_QSCALE_CHUNKS = tuple(range(CH_FQ, CH_FQ + 4)) + tuple(range(CH_NQ, CH_NQ + 4))

VMEM_LIMIT = 56 * 1024 * 1024


def _cparams(sem):
    return pltpu.CompilerParams(dimension_semantics=sem, vmem_limit_bytes=VMEM_LIMIT)


def _rms(x, g):
    return x * lax.rsqrt(jnp.mean(x * x, axis=-1, keepdims=True) + RMS_EPS) * g


def _split3(x):
    hi = x.astype(BF16)
    r1 = x - hi.astype(F32)
    mid = r1.astype(BF16)
    lo = (r1 - mid.astype(F32)).astype(BF16)
    return hi, mid, lo


def _dot(a, b):
    return jnp.dot(a, b, preferred_element_type=F32)


def _dot_nt(a, b):
    return lax.dot_general(a, b, (((1,), (1,)), ((), ())), preferred_element_type=F32)


def _lane_iota(n=LANES):
    return lax.broadcasted_iota(jnp.int32, (1, n), 1)


def _softmax_step(s, v, carry):
    m, l, acc = carry
    m_new = jnp.maximum(m, jnp.max(s, axis=-1, keepdims=True))
    alpha = jnp.exp(m - m_new)
    p = jnp.exp(s - m_new)
    l = alpha * l + jnp.sum(p, axis=-1, keepdims=True)
    acc = alpha * acc + _dot(p.astype(BF16), v)
    return m_new, l, acc


def _softmax_init(tq):
    return (jnp.full((tq, 1), NEG, F32), jnp.zeros((tq, 1), F32), jnp.zeros((tq, LANES), F32))


def _rope_kernel(pos_ref, inv_ref, m1_ref, m2_ref, c_ref, s1_ref, s2_ref):
    ang = pos_ref[...].astype(F32) * inv_ref[...]
    c_ref[...] = jnp.cos(ang)
    sn = jnp.sin(ang)
    s1_ref[...] = -sn * m1_ref[...]
    s2_ref[...] = sn * m2_ref[...]


def _rope_tables(positions, tm):
    n = positions.size
    inv = ROPE_THETA ** (-jnp.arange(0, ROPE_DIM, 2, dtype=F32) / ROPE_DIM)
    lane = np.arange(LANES) % HEAD_DIM
    half = ROPE_DIM // 2
    inv_l = jnp.where(lane < ROPE_DIM, inv[lane % half], 0.0).reshape(1, LANES).astype(F32)
    m1 = jnp.asarray((lane < half).astype(np.float32)).reshape(1, LANES)
    m2 = jnp.asarray(((lane >= half) & (lane < ROPE_DIM)).astype(np.float32)).reshape(1, LANES)
    row = pl.BlockSpec((1, LANES), lambda i: (0, 0))
    tab = pl.BlockSpec((tm, LANES), lambda i: (i, 0))
    return pl.pallas_call(
        _rope_kernel,
        grid=(n // tm,),
        in_specs=[pl.BlockSpec((tm, 1), lambda i: (i, 0)), row, row, row],
        out_specs=[tab, tab, tab],
        out_shape=[jax.ShapeDtypeStruct((n, LANES), F32)] * 3,
        compiler_params=_cparams(("parallel",)),
        name="rope_tables",
    )(positions.reshape(n, 1), inv_l, m1, m2)


def _apply_rope(y, c, s1, s2):
    half = ROPE_DIM // 2
    return y * c + pltpu.roll(y, LANES - half, 1) * s1 + pltpu.roll(y, half, 1) * s2


def _even_proj_kernel(x_ref, g_ref, w_ref, ws_ref, fb_ref, c_ref, s1_ref, s2_ref, main_ref, small_ref):
    h = _rms(x_ref[...], g_ref[...]).astype(BF16)
    c, s1, s2 = c_ref[...], s1_ref[...], s2_ref[...]
    for ch in range(EV_MAIN // LANES):
        y = _dot(h, w_ref[:, ch * LANES:(ch + 1) * LANES])
        if ch in EV_ROPE_CHUNKS:
            y = _apply_rope(y, c, s1, s2)
        if ch in EV_QSCALE_CHUNKS:
            y = y * Q_SCALE
        main_ref[:, ch * LANES:(ch + 1) * LANES] = y.astype(BF16)
    ys = _dot(h, ws_ref[...])
    z = ys + fb_ref[...]
    log_f = jnp.minimum(z, 0.0) - jnp.log(1.0 + jnp.exp(-jnp.abs(z)))
    gate = 1.0 / (1.0 + jnp.exp(-ys))
    small_ref[...] = jnp.where(_lane_iota() < FOX_HEADS, log_f, gate)


def _even_proj(x2, g, w_main, w_small, fb_row, c, s1, s2, tm):
    n = x2.shape[0]
    const = lambda shape: pl.BlockSpec(shape, lambda i: (0, 0))
    tab = pl.BlockSpec((tm, LANES), lambda i: (i, 0))
    return pl.pallas_call(
        _even_proj_kernel,
        grid=(n // tm,),
        in_specs=[pl.BlockSpec((tm, D_MODEL), lambda i: (i, 0)), const((1, D_MODEL)),
                  const((D_MODEL, EV_MAIN)), const((D_MODEL, LANES)), const((1, LANES)), tab, tab, tab],
        out_specs=[pl.BlockSpec((tm, EV_MAIN), lambda i: (i, 0)), tab],
        out_shape=[jax.ShapeDtypeStruct((n, EV_MAIN), BF16), jax.ShapeDtypeStruct((n, LANES), F32)],
        compiler_params=_cparams(("parallel",)),
        name="even_proj",
    )(x2, g, w_main, w_small, fb_row, c, s1, s2)


ODD_IN = 3 * D_MODEL


def _odd_proj_kernel(x_ref, g_ref, w_ref, c_ref, s1_ref, s2_ref, out_ref):
    h = _rms(x_ref[...], g_ref[...]).astype(BF16)
    c, s1, s2 = c_ref[...], s1_ref[...], s2_ref[...]
    n_qk = 2 * D_MODEL // LANES
    for ch in range(ODD_IN // LANES):
        y = _dot(h, w_ref[:, ch * LANES:(ch + 1) * LANES])
        if ch < n_qk:
            y = _apply_rope(y, c, s1, s2)
        if ch < n_qk // 2:
            y = y * Q_SCALE
        out_ref[:, ch * LANES:(ch + 1) * LANES] = y.astype(BF16)


def _odd_proj(x2, g, w, c, s1, s2, tm):
    n = x2.shape[0]
    const = lambda shape: pl.BlockSpec(shape, lambda i: (0, 0))
    tab = pl.BlockSpec((tm, LANES), lambda i: (i, 0))
    return pl.pallas_call(
        _odd_proj_kernel,
        grid=(n // tm,),
        in_specs=[pl.BlockSpec((tm, D_MODEL), lambda i: (i, 0)), const((1, D_MODEL)),
                  const((D_MODEL, ODD_IN)), tab, tab, tab],
        out_specs=pl.BlockSpec((tm, ODD_IN), lambda i: (i, 0)),
        out_shape=jax.ShapeDtypeStruct((n, ODD_IN), BF16),
        compiler_params=_cparams(("parallel",)),
        name="odd_proj",
    )(x2, g, w, c, s1, s2)


def _fox_aug_kernel(lf_ref, tril_ref, eq_ref, ek_ref, oneq_ref, onek_ref, aq_ref, ak_ref, carry_ref):
    @pl.when(pl.program_id(1) == 0)
    def _():
        carry_ref[...] = jnp.zeros_like(carry_ref)

    tril = tril_ref[...]
    c = carry_ref[...]
    for piece in _split3(lf_ref[0]):
        c = c + _dot(tril, piece)
    carry_ref[...] = c[-1:, :]
    aq = oneq_ref[...]
    ak = onek_ref[...]
    for r, piece in enumerate(_split3(c)):
        aq = aq + _dot(piece, eq_ref[r])
        ak = ak - _dot(piece, ek_ref[r])
    aq_ref[0] = aq.astype(BF16)
    ak_ref[0] = ak.astype(BF16)


def _fox_aug(small, tc):
    b, t, _ = small.shape
    tril = jnp.asarray(np.tril(np.ones((tc, tc), np.float32)), BF16)
    eq = np.zeros((3, LANES, LANES), np.float32)
    ek = np.zeros((3, LANES, LANES), np.float32)
    oneq = np.zeros((1, LANES), np.float32)
    onek = np.zeros((1, LANES), np.float32)
    for h in range(FOX_HEADS):
        for r in range(3):
            eq[r, h, AUG_PER_HEAD * h + r] = 1.0
            ek[r, h, AUG_PER_HEAD * h + 3 + r] = 1.0
            oneq[0, AUG_PER_HEAD * h + 3 + r] = 1.0
            onek[0, AUG_PER_HEAD * h + r] = 1.0
    const2 = lambda shape: pl.BlockSpec(shape, lambda bi, i: (0,) * len(shape))
    blk = pl.BlockSpec((1, tc, LANES), lambda bi, i: (bi, i, 0))
    return pl.pallas_call(
        _fox_aug_kernel,
        grid=(b, t // tc),
        in_specs=[blk, const2((tc, tc)), const2((3, LANES, LANES)), const2((3, LANES, LANES)),
                  const2((1, LANES)), const2((1, LANES))],
        out_specs=[blk, blk],
        out_shape=[jax.ShapeDtypeStruct((b, t, LANES), BF16)] * 2,
        scratch_shapes=[pltpu.VMEM((1, LANES), F32)],
        compiler_params=_cparams(("parallel", "arbitrary")),
        name="fox_aug",
    )(small, tril, jnp.asarray(eq, BF16), jnp.asarray(ek, BF16), jnp.asarray(oneq), jnp.asarray(onek))


def _causal_mask(tq):
    row = lax.broadcasted_iota(jnp.int32, (tq, tq), 0)
    col = lax.broadcasted_iota(jnp.int32, (tq, tq), 1)
    return row, col


def _fox_kernel(q_ref, aq_ref, k_ref, ak_ref, v_ref, o_ref, *, tq):
    pair = pl.program_id(1)
    i = pl.program_id(2)
    lane = _lane_iota()
    q2 = q_ref[0]
    qa = aq_ref[0]
    row, col = _causal_mask(tq)
    outs = []
    for hh in range(2):
        head = 2 * pair + hh
        qh = jnp.where(lane // HEAD_DIM == hh, q2, jnp.zeros_like(q2))
        in_head = (lane >= AUG_PER_HEAD * head) & (lane < AUG_PER_HEAD * (head + 1))
        ah = jnp.where(in_head, qa, jnp.zeros_like(qa))
        qcat = jnp.concatenate([qh, ah], axis=1)

        def step(j, carry, diag):
            ks = pl.multiple_of(j * tq, tq)
            kcat = jnp.concatenate([k_ref[0, pl.ds(ks, tq), :], ak_ref[0, pl.ds(ks, tq), :]], axis=1)
            s = _dot_nt(qcat, kcat)
            if diag:
                s = jnp.where(row >= col, s, NEG)
            return _softmax_step(s, v_ref[0, pl.ds(ks, tq), :], carry)

        carry = lax.fori_loop(0, i, partial(step, diag=False), _softmax_init(tq))
        _, l, acc = step(i, carry, True)
        outs.append(acc / l)
    o_ref[0] = jnp.where(lane < HEAD_DIM, outs[0], outs[1]).astype(BF16)


def _fox_attention(main, aq, ak, tq):
    b, t, _ = main.shape
    qspec = lambda ch: pl.BlockSpec((1, tq, LANES), lambda bi, p, i: (bi, i, ch + p))
    kspec = lambda ch: pl.BlockSpec((1, t, LANES), lambda bi, p, i: (bi, 0, ch + p))
    return pl.pallas_call(
        partial(_fox_kernel, tq=tq),
        grid=(b, FOX_HEADS // 2, t // tq),
        in_specs=[qspec(CH_FQ), pl.BlockSpec((1, tq, LANES), lambda bi, p, i: (bi, i, 0)),
                  kspec(CH_FK), pl.BlockSpec((1, t, LANES), lambda bi, p, i: (bi, 0, 0)), kspec(CH_FV)],
        out_specs=pl.BlockSpec((1, tq, LANES), lambda bi, p, i: (bi, i, p)),
        out_shape=jax.ShapeDtypeStruct((b, t, FOX_W), BF16),
        compiler_params=_cparams(("parallel", "parallel", "parallel")),
        name="fox_attention",
    )(main, aq, main, ak, main)


def _compress_kernel(xk_ref, xv_ref, w1k_ref, w2k_ref, pk_ref, w1v_ref, w2v_ref, pv_ref, kc_ref, vc_ref):
    half = CMP_STRIDE * HEAD_DIM
    for x_ref, w1_ref, w2_ref, p_ref, o_ref in ((xk_ref, w1k_ref, w2k_ref, pk_ref, kc_ref),
                                                 (xv_ref, w1v_ref, w2v_ref, pv_ref, vc_ref)):
        w1 = w1_ref[...]
        pos_h = _dot(p_ref[...], w1)
        out = None
        for g in range(NSA_KV_GROUPS):
            x = x_ref[0, g]
            first = _dot(x, w1[:half])
            second = _dot(x, w1[half:])
            nrow = first.shape[0]
            hid = first + pltpu.roll(second, nrow - 1, 0) + pos_h
            a = jax.nn.gelu(hid, approximate=True).astype(BF16)
            y = _dot(a, w2_ref[g])
            out = y if out is None else out + y
        o_ref[0] = out.astype(BF16)


def _compress(xk, xv, w1k, w2k, pk, w1v, w2v, pv):
    b, g, nchunk, width = xk.shape
    xspec = pl.BlockSpec((1, g, nchunk, width), lambda bi: (bi, 0, 0, 0))
    const = lambda shape: pl.BlockSpec(shape, lambda bi: (0,) * len(shape))
    wspecs = [const((width * 2, CMP_HIDDEN)), const((g, CMP_HIDDEN, LANES)), const((1, width * 2))]
    ospec = pl.BlockSpec((1, nchunk, LANES), lambda bi: (bi, 0, 0))
    return pl.pallas_call(
        _compress_kernel,
        grid=(b,),
        in_specs=[xspec, xspec] + wspecs + wspecs,
        out_specs=[ospec, ospec],
        out_shape=[jax.ShapeDtypeStruct((b, nchunk, LANES), BF16)] * 2,
        compiler_params=_cparams(("parallel",)),
        name="nsa_compress",
    )(xk, xv, w1k, w2k, pk, w1v, w2v, pv)


def _gate_col(small, head, branch):
    idx = FOX_HEADS + 3 * head + branch
    return jnp.sum(jnp.where(_lane_iota() == idx, small, 0.0), axis=-1, keepdims=True)


def _nsa_select_kernel(q_ref, kc_ref, vc_ref, ov_ref, small_ref, oc_ref, sel_ref, *, tq, ns, n_sel):
    q0 = pl.program_id(1) * tq
    lane = _lane_iota()
    kc = kc_ref[0]
    vc = vc_ref[0]
    ncp = kc.shape[0]
    small = small_ref[0]
    qpos = q0 + lax.broadcasted_iota(jnp.int32, (tq, 1), 0)
    cmp_end = lax.broadcasted_iota(jnp.int32, (1, ncp), 1) * CMP_STRIDE + (CMP_BLOCK - 1)
    cmask = cmp_end <= qpos
    psum = [jnp.zeros((tq, ncp), F32) for _ in range(NSA_KV_GROUPS)]
    for n in range(NSA_HPG):
        q2 = q_ref[0, :, n * LANES:(n + 1) * LANES]
        outs = []
        for g in range(NSA_KV_GROUPS):
            qh = jnp.where(lane // HEAD_DIM == g, q2, jnp.zeros_like(q2))
            z = jnp.where(cmask, _dot_nt(qh, kc), -jnp.inf)
            m = jnp.max(z, axis=-1, keepdims=True)
            m = jnp.where(m == -jnp.inf, 0.0, m)
            p = jnp.exp(z - m)
            p = p / jnp.maximum(jnp.sum(p, axis=-1, keepdims=True), 1e-30)
            psum[g] = psum[g] + p
            outs.append(_gate_col(small, g * NSA_HPG + n, 0) * _dot(p.astype(BF16), vc))
        oc_ref[0, :, n * LANES:(n + 1) * LANES] = jnp.where(lane < HEAD_DIM, outs[0], outs[1])

    blk = lane
    cur = qpos // SLC_BLOCK
    valid = blk * SLC_BLOCK <= qpos
    forced = (blk == 0) | (blk == cur) | (blk == cur - 1)
    for g in range(NSA_KV_GROUPS):
        imp = jnp.zeros((tq, LANES), F32)
        for piece in _split3(psum[g]):
            imp = imp + _dot(piece, ov_ref[...])
        score = jnp.where(valid, jnp.where(forced, jnp.inf, imp), -jnp.inf)
        rank = jnp.zeros((tq, LANES), jnp.int32)
        for i in range(ns):
            col = score[:, i:i + 1]
            ahead = (col > score) | ((col == score) & (lane > i))
            rank = rank + ahead.astype(jnp.int32)
        sel = (rank < n_sel) & (lane < ns)
        sel_ref[0, :, g * LANES:(g + 1) * LANES] = jnp.where(sel, 0.0, NEG).astype(BF16)


def _nsa_select(main, kc, vc, overlap, small, tq, ns, n_sel):
    b, t, _ = main.shape
    ncp = kc.shape[1]
    return pl.pallas_call(
        partial(_nsa_select_kernel, tq=tq, ns=ns, n_sel=n_sel),
        grid=(b, t // tq),
        in_specs=[pl.BlockSpec((1, tq, NSA_W), lambda bi, i: (bi, i, CH_NQ * LANES // NSA_W)),
                  pl.BlockSpec((1, ncp, LANES), lambda bi, i: (bi, 0, 0)),
                  pl.BlockSpec((1, ncp, LANES), lambda bi, i: (bi, 0, 0)),
                  pl.BlockSpec((ncp, LANES), lambda bi, i: (0, 0)),
                  pl.BlockSpec((1, tq, LANES), lambda bi, i: (bi, i, 0))],
        out_specs=[pl.BlockSpec((1, tq, NSA_W), lambda bi, i: (bi, i, 0)),
                   pl.BlockSpec((1, tq, NSA_KV_GROUPS * LANES), lambda bi, i: (bi, i, 0))],
        out_shape=[jax.ShapeDtypeStruct((b, t, NSA_W), F32),
                   jax.ShapeDtypeStruct((b, t, NSA_KV_GROUPS * LANES), BF16)],
        compiler_params=_cparams(("parallel", "parallel")),
        name="nsa_select",
    )(main, kc, vc, overlap, small)


def _nsa_flash_kernel(q_ref, sel_ref, oc_ref, small_ref, ks_ref, vs_ref, kw_ref, vw_ref, hot_ref, o_ref, *, tq):
    n = pl.program_id(1)
    i = pl.program_id(2)
    lane = _lane_iota()
    q2 = q_ref[0]
    small = small_ref[0]
    row, col = _causal_mask(tq)
    w_tiles = WINDOW // tq
    outs = []
    for g in range(NSA_KV_GROUPS):
        head = g * NSA_HPG + n
        qh = jnp.where(lane // HEAD_DIM == g, q2, jnp.zeros_like(q2))
        qcat = jnp.concatenate([qh, sel_ref[0, :, g * LANES:(g + 1) * LANES]], axis=1)

        def sel_step(j, carry, diag):
            ks = pl.multiple_of(j * tq, tq)
            kcat = jnp.concatenate([ks_ref[0, pl.ds(ks, tq), :], hot_ref[pl.ds(ks, tq), :]], axis=1)
            s = _dot_nt(qcat, kcat)
            if diag:
                s = jnp.where(row >= col, s, NEG)
            return _softmax_step(s, vs_ref[0, pl.ds(ks, tq), :], carry)

        carry = lax.fori_loop(0, i, partial(sel_step, diag=False), _softmax_init(tq))
        _, l_s, acc_s = sel_step(i, carry, True)

        def win_step(j, carry, kind):
            ks = pl.multiple_of(j * tq, tq)
            s = _dot_nt(qh, kw_ref[0, pl.ds(ks, tq), :])
            if kind == "diag":
                s = jnp.where(row >= col, s, NEG)
            elif kind == "tail":
                s = jnp.where(col > row, s, NEG)
            return _softmax_step(s, vw_ref[0, pl.ds(ks, tq), :], carry)

        carry = _softmax_init(tq)
        carry = lax.cond(i >= w_tiles, lambda c: win_step(i - w_tiles, c, "tail"), lambda c: c, carry)
        carry = lax.fori_loop(jnp.maximum(i - w_tiles + 1, 0), i, partial(win_step, kind="full"), carry)
        _, l_w, acc_w = win_step(i, carry, "diag")
        outs.append(_gate_col(small, head, 1) * (acc_s / l_s) + _gate_col(small, head, 2) * (acc_w / l_w))
    o_ref[0] = (oc_ref[0] + jnp.where(lane < HEAD_DIM, outs[0], outs[1])).astype(BF16)


def _nsa_flash(main, sel, oc, small, hot, tq):
    b, t, _ = main.shape
    tile = lambda ch: pl.BlockSpec((1, tq, LANES), lambda bi, n, i: (bi, i, ch + n))
    full = lambda ch: pl.BlockSpec((1, t, LANES), lambda bi, n, i: (bi, 0, ch))
    return pl.pallas_call(
        partial(_nsa_flash_kernel, tq=tq),
        grid=(b, NSA_HPG, t // tq),
        in_specs=[tile(CH_NQ),
                  pl.BlockSpec((1, tq, NSA_KV_GROUPS * LANES), lambda bi, n, i: (bi, i, 0)),
                  tile(0),
                  pl.BlockSpec((1, tq, LANES), lambda bi, n, i: (bi, i, 0)),
                  full(CH_KS), full(CH_VS), full(CH_KW), full(CH_VW),
                  pl.BlockSpec((t, LANES), lambda bi, n, i: (0, 0))],
        out_specs=tile(0),
        out_shape=jax.ShapeDtypeStruct((b, t, NSA_W), BF16),
        compiler_params=_cparams(("parallel", "parallel", "parallel")),
        name="nsa_flash",
    )(main, sel, oc, small, main, main, main, main, hot)


def _diff_kernel(q_ref, k_ref, v_ref, lam_ref, g_ref, o_ref, *, tq, lam_init):
    i = pl.program_id(2)
    lane = _lane_iota()
    q2 = q_ref[0]
    row, col = _causal_mask(tq)
    lp = lam_ref[...]
    lam = (jnp.exp(jnp.sum(lp[0:1] * lp[1:2], axis=-1, keepdims=True))
           - jnp.exp(jnp.sum(lp[2:3] * lp[3:4], axis=-1, keepdims=True)) + lam_init)
    outs = []
    for comp in range(2):
        qh = jnp.where(lane // HEAD_DIM == comp, q2, jnp.zeros_like(q2))

        def step(j, carry, diag):
            ks = pl.multiple_of(j * tq, tq)
            s = _dot_nt(qh, k_ref[0, pl.ds(ks, tq), :])
            if diag:
                s = jnp.where(row >= col, s, NEG)
            return _softmax_step(s, v_ref[0, pl.ds(ks, tq), :], carry)

        carry = lax.fori_loop(0, i, partial(step, diag=False), _softmax_init(tq))
        _, l, acc = step(i, carry, True)
        outs.append(acc / l)
    o = outs[0] - lam * outs[1]
    o_ref[0] = (_rms(o, g_ref[...]) * (1.0 - lam_init)).astype(BF16)


def _diff_attention(proj, lam_p, subln_g, tq, lam_init):
    b, t, _ = proj.shape
    nh = DIFF_HEADS
    tile = lambda ch: pl.BlockSpec((1, tq, LANES), lambda bi, h, i: (bi, i, ch + h))
    full = lambda ch: pl.BlockSpec((1, t, LANES), lambda bi, h, i: (bi, 0, ch + h))
    return pl.pallas_call(
        partial(_diff_kernel, tq=tq, lam_init=lam_init),
        grid=(b, nh, t // tq),
        in_specs=[tile(0), full(nh), full(2 * nh),
                  pl.BlockSpec((4, LANES), lambda bi, h, i: (0, 0)),
                  pl.BlockSpec((1, LANES), lambda bi, h, i: (0, 0))],
        out_specs=tile(0),
        out_shape=jax.ShapeDtypeStruct((b, t, nh * LANES), BF16),
        compiler_params=_cparams(("parallel", "parallel", "parallel")),
        name="diff_attention",
    )(proj, proj, proj, lam_p, subln_g)


def _out_proj_kernel(*refs, n_in):
    a_refs, w_refs = refs[:n_in], refs[n_in:2 * n_in]
    x_ref, g_ref, o_ref = refs[2 * n_in:]
    y = None
    for a_ref, w_ref in zip(a_refs, w_refs):
        t = _dot(a_ref[...], w_ref[...])
        y = t if y is None else y + t
    o_ref[...] = x_ref[...] + _rms(y, g_ref[...])


def _out_proj(acts, weights, x2, g, tm):
    n = x2.shape[0]
    n_in = len(acts)
    const = lambda shape: pl.BlockSpec(shape, lambda i: (0, 0))
    return pl.pallas_call(
        partial(_out_proj_kernel, n_in=n_in),
        grid=(n // tm,),
        in_specs=[pl.BlockSpec((tm, a.shape[1]), lambda i: (i, 0)) for a in acts]
                 + [const(w.shape) for w in weights]
                 + [pl.BlockSpec((tm, D_MODEL), lambda i: (i, 0)), const((1, D_MODEL))],
        out_specs=pl.BlockSpec((tm, D_MODEL), lambda i: (i, 0)),
        out_shape=jax.ShapeDtypeStruct((n, D_MODEL), F32),
        compiler_params=_cparams(("parallel",)),
        name="out_proj",
    )(*acts, *weights, x2, g)


def _mem_kv_kernel(mem_ref, g_ref, wk_ref, wv_ref, k_ref, v_ref):
    mn = _rms(mem_ref[0], g_ref[0]).astype(BF16)
    k_ref[0, 0] = _dot(mn, wk_ref[0]).astype(BF16)
    v_ref[0, 0] = _dot(mn, wv_ref[0]).astype(BF16)


def _mem_kv(mem, mem_norm_g, wk, wv):
    depth = wk.shape[0]
    b, m, d = mem.shape
    wspec = pl.BlockSpec((1, d, MEM_W), lambda l, bi: (l, 0, 0))
    ospec = pl.BlockSpec((1, 1, m, MEM_W), lambda l, bi: (l, bi, 0, 0))
    return pl.pallas_call(
        _mem_kv_kernel,
        grid=(depth, b),
        in_specs=[pl.BlockSpec((1, m, d), lambda l, bi: (bi, 0, 0)),
                  pl.BlockSpec((1, 1, d), lambda l, bi: (l, 0, 0)), wspec, wspec],
        out_specs=[ospec, ospec],
        out_shape=[jax.ShapeDtypeStruct((depth, b, m, MEM_W), BF16)] * 2,
        compiler_params=_cparams(("parallel", "parallel")),
        name="mem_kv",
    )(mem, mem_norm_g.reshape(depth, 1, d), wk, wv)


def _cross_kernel(x_ref, gin_ref, gout_ref, wq_ref, k_ref, v_ref, wo_ref, o_ref):
    lane = _lane_iota()
    x = x_ref[...]
    h = _rms(x, gin_ref[...]).astype(BF16)
    q = (_dot(h, wq_ref[...]) * Q_SCALE).astype(BF16)
    chunks = []
    for ch in range(MEM_W // LANES):
        q2 = q[:, ch * LANES:(ch + 1) * LANES]
        k2 = k_ref[0, 0, :, ch * LANES:(ch + 1) * LANES]
        v2 = v_ref[0, 0, :, ch * LANES:(ch + 1) * LANES]
        outs = []
        for hh in range(2):
            qh = jnp.where(lane // HEAD_DIM == hh, q2, jnp.zeros_like(q2))
            s = _dot_nt(qh, k2)
            p = jnp.exp(s - jnp.max(s, axis=-1, keepdims=True))
            p = p / jnp.sum(p, axis=-1, keepdims=True)
            outs.append(_dot(p.astype(BF16), v2))
        chunks.append(jnp.where(lane < HEAD_DIM, outs[0], outs[1]).astype(BF16))
    y = _dot(jnp.concatenate(chunks, axis=1), wo_ref[...])
    o_ref[...] = x + _rms(y, gout_ref[...])


def _cross_attention(x2, gin, gout, wq, mem_k, mem_v, wo, layer, tm, rows_per_batch):
    n = x2.shape[0]
    m = mem_k.shape[2]
    const = lambda shape: pl.BlockSpec(shape, lambda i: (0, 0))
    kv = pl.BlockSpec((1, 1, m, MEM_W), lambda i: (layer, i // rows_per_batch, 0, 0))
    return pl.pallas_call(
        _cross_kernel,
        grid=(n // tm,),
        in_specs=[pl.BlockSpec((tm, D_MODEL), lambda i: (i, 0)), const((1, D_MODEL)), const((1, D_MODEL)),
                  const((D_MODEL, MEM_W)), kv, kv, const((MEM_W, D_MODEL))],
        out_specs=pl.BlockSpec((tm, D_MODEL), lambda i: (i, 0)),
        out_shape=jax.ShapeDtypeStruct((n, D_MODEL), F32),
        compiler_params=_cparams(("parallel",)),
        name="cross_attention",
    )(x2, gin, gout, wq, mem_k, mem_v, wo)


def _ffn_kernel(x_ref, gin_ref, gout_ref, wg_ref, wu_ref, wd_ref, o_ref, *, chunk):
    x = x_ref[...]
    h = _rms(x, gin_ref[...]).astype(BF16)
    d_ff = wg_ref.shape[1]
    y = None
    for c0 in range(0, d_ff, chunk):
        gate = _dot(h, wg_ref[:, c0:c0 + chunk])
        up = _dot(h, wu_ref[:, c0:c0 + chunk])
        a = (gate * (1.0 / (1.0 + jnp.exp(-gate))) * up).astype(BF16)
        t = _dot(a, wd_ref[c0:c0 + chunk, :])
        y = t if y is None else y + t
    o_ref[...] = x + _rms(y, gout_ref[...])


def _ffn(x2, gin, gout, wg, wu, wd, tm, chunk):
    n = x2.shape[0]
    d_ff = wg.shape[1]
    const = lambda shape: pl.BlockSpec(shape, lambda i: (0, 0), pipeline_mode=pl.Buffered(1))
    return pl.pallas_call(
        partial(_ffn_kernel, chunk=chunk),
        grid=(n // tm,),
        in_specs=[pl.BlockSpec((tm, D_MODEL), lambda i: (i, 0)), const((1, D_MODEL)), const((1, D_MODEL)),
                  const((D_MODEL, d_ff)), const((D_MODEL, d_ff)), const((d_ff, D_MODEL))],
        out_specs=pl.BlockSpec((tm, D_MODEL), lambda i: (i, 0)),
        out_shape=jax.ShapeDtypeStruct((n, D_MODEL), F32),
        compiler_params=_cparams(("parallel",)),
        name="ffn",
    )(x2, gin, gout, wg, wu, wd)


def _even_weights(w_in, w_out):
    offs = np.cumsum((FOX_W, FOX_W, FOX_W, FOX_HEADS, NSA_W) + (NSA_KV_GROUPS * HEAD_DIM,) * 6 + (3 * NSA_HEADS,))
    fl0, nq0, kv0, gl0, end = int(offs[2]), int(offs[3]), int(offs[4]), int(offs[10]), int(offs[11])
    perm = np.concatenate([np.arange(HEAD_DIM) + HEAD_DIM * (g * NSA_HPG + n)
                           for n in range(NSA_HPG) for g in range(NSA_KV_GROUPS)])
    w_main = jnp.concatenate([w_in[:, :fl0], w_in[:, nq0 + perm], w_in[:, kv0:gl0]], axis=1).astype(BF16)
    w_small = jnp.concatenate([w_in[:, fl0:nq0], w_in[:, gl0:end],
                               jnp.zeros((D_MODEL, LANES - FOX_HEADS - 3 * NSA_HEADS), w_in.dtype)],
                              axis=1).astype(BF16)
    w_out_fox = w_out[:FOX_W].astype(BF16)
    w_out_nsa = w_out[FOX_W + perm].astype(BF16)
    return w_main, w_small, w_out_fox, w_out_nsa


def _overlap_matrix(t, ncp):
    nc = (t - CMP_BLOCK) // CMP_STRIDE + 1
    ns = t // SLC_BLOCK
    cs = np.arange(nc) * CMP_STRIDE
    ss = np.arange(ns) * SLC_BLOCK
    ov = np.clip(np.minimum(cs[:, None] + CMP_BLOCK, ss[None, :] + SLC_BLOCK)
                 - np.maximum(cs[:, None], ss[None, :]), 0, None) / CMP_BLOCK
    full = np.zeros((ncp, LANES), np.float32)
    full[:nc, :ns] = ov
    return jnp.asarray(full, BF16), ns


def _stride_chunks(main, ch):
    b, t, _ = main.shape
    x = main[:, :, ch * LANES:(ch + 1) * LANES].reshape(b, t // CMP_STRIDE, CMP_STRIDE, NSA_KV_GROUPS, HEAD_DIM)
    return jnp.transpose(x, (0, 3, 1, 2, 4)).reshape(b, NSA_KV_GROUPS, t // CMP_STRIDE, CMP_STRIDE * HEAD_DIM)


def _pad_w2(w2):
    out = jnp.zeros((NSA_KV_GROUPS, CMP_HIDDEN, LANES), BF16)
    for g in range(NSA_KV_GROUPS):
        out = out.at[g, :, g * HEAD_DIM:(g + 1) * HEAD_DIM].set(w2.astype(BF16))
    return out


def kernel(x, mem, positions, sandwich_g, mem_norm_g, ev_w_in, ev_fox_fbias, ev_cmp_pos_k, ev_cmp_w1_k, ev_cmp_w2_k, ev_cmp_pos_v, ev_cmp_w1_v, ev_cmp_w2_v, ev_w_out, od_w_in, od_lambda, od_subln_g, od_w_out, ca_wq, ca_wk, ca_wv, ca_wo, ffn_wg, ffn_wu, ffn_wd):
    b, t, d = x.shape
    depth = sandwich_g.shape[0]
    n = b * t
    tm = 256
    tq = 256
    assert d == D_MODEL and t % tq == 0 and WINDOW % tq == 0 and t % tm == 0

    c_tab, s1_tab, s2_tab = _rope_tables(positions, 512)
    mem_k, mem_v = _mem_kv(mem, mem_norm_g, ca_wk.astype(BF16), ca_wv.astype(BF16))
    ncp = t // CMP_STRIDE
    overlap, ns = _overlap_matrix(t, ncp)
    n_sel = min(SLC_TOPK, ns)
    hot = jnp.asarray((np.arange(t)[:, None] // SLC_BLOCK == np.arange(LANES)[None, :]).astype(np.float32), BF16)
    gain = lambda l, j: sandwich_g[l, j].reshape(1, d)

    x2 = x.reshape(n, d)
    for layer in range(depth):
        if layer % 2 == 0:
            e = layer // 2
            w_main, w_small, w_out_fox, w_out_nsa = _even_weights(ev_w_in[e], ev_w_out[e])
            fb_row = jnp.zeros((1, LANES), F32).at[0, :FOX_HEADS].set(ev_fox_fbias[e].astype(F32))
            main, small = _even_proj(x2, gain(layer, 0), w_main, w_small, fb_row, c_tab, s1_tab, s2_tab, tm)
            main = main.reshape(b, t, EV_MAIN)
            small = small.reshape(b, t, LANES)
            aq, ak = _fox_aug(small, 256)
            o_fox = _fox_attention(main, aq, ak, tq)
            kc, vc = _compress(
                _stride_chunks(main, CH_KC), _stride_chunks(main, CH_VC),
                ev_cmp_w1_k[e].astype(BF16), _pad_w2(ev_cmp_w2_k[e]), ev_cmp_pos_k[e].reshape(1, -1).astype(BF16),
                ev_cmp_w1_v[e].astype(BF16), _pad_w2(ev_cmp_w2_v[e]), ev_cmp_pos_v[e].reshape(1, -1).astype(BF16))
            oc, sel = _nsa_select(main, kc, vc, overlap, small, 128, ns, n_sel)
            o_nsa = _nsa_flash(main, sel, oc, small, hot, tq)
            x2 = _out_proj([o_fox.reshape(n, FOX_W), o_nsa.reshape(n, NSA_W)], [w_out_fox, w_out_nsa],
                           x2, gain(layer, 1), tm)
        else:
            o = layer // 2
            proj = _odd_proj(x2, gain(layer, 0), od_w_in[o].astype(BF16), c_tab, s1_tab, s2_tab, tm)
            lam_init = 0.8 - 0.6 * math.exp(-0.3 * layer)
            lam_p = jnp.pad(od_lambda[o].astype(F32), ((0, 0), (0, LANES - HEAD_DIM)))
            attn = _diff_attention(proj.reshape(b, t, ODD_IN), lam_p, od_subln_g[o].reshape(1, LANES).astype(F32),
                                   tq, lam_init)
            x2 = _out_proj([attn.reshape(n, D_MODEL)], [od_w_out[o].astype(BF16)], x2, gain(layer, 1), tm)
        x2 = _cross_attention(x2, gain(layer, 2), gain(layer, 3), ca_wq[layer].astype(BF16), mem_k, mem_v,
                              ca_wo[layer].astype(BF16), layer, tm, t // tm)
        x2 = _ffn(x2, gain(layer, 4), gain(layer, 5), ffn_wg[layer].astype(BF16), ffn_wu[layer].astype(BF16),
                  ffn_wd[layer].astype(BF16), tm, 256)
    return x2.reshape(b, t, d)
```

```python
import math
from functools import partial

import numpy as np
import jax
import jax.numpy as jnp
from jax import lax
from jax.experimental import pallas as pl
from jax.experimental.pallas import tpu as pltpu

F32 = jnp.float32
BF16 = jnp.bfloat16

D_MODEL = 1024
HEAD_DIM = 64
LANES = 128
ROPE_DIM = HEAD_DIM // 4
ROPE_THETA = 500000.0
FOX_HEADS = 8
NSA_HEADS = 8
NSA_KV_GROUPS = 2
NSA_HPG = NSA_HEADS // NSA_KV_GROUPS
CMP_BLOCK = 32
CMP_STRIDE = 16
CMP_HIDDEN = 2 * HEAD_DIM
SLC_BLOCK = 64
SLC_TOPK = 16
WINDOW = 512
DIFF_HEADS = 8
MEM_HEADS = 4
MEM_W = MEM_HEADS * HEAD_DIM
RMS_EPS = 1e-6
Q_SCALE = HEAD_DIM ** -0.5
NEG = -1e30
AUG_PER_HEAD = 6

FOX_W = FOX_HEADS * HEAD_DIM
NSA_W = NSA_HEADS * HEAD_DIM
CH_FQ, CH_FK, CH_NQ, CH_KC, CH_VC, CH_KS, CH_KW = 0, 4, 8, 12, 13, 14, 15
EV_MAIN = 16 * LANES
EV_ROPE_CHUNKS = tuple(range(CH_NQ, CH_NQ + 4)) + (CH_KC, CH_KS, CH_KW)
EV_QSCALE_CHUNKS = tuple(range(CH_FQ, CH_FQ + 4)) + tuple(range(CH_NQ, CH_NQ + 4))
VT_FV, VT_VS, VT_VW = 0, 4, 5
EV_VT = 6 * LANES
OD_MAIN = 2 * D_MODEL
OD_VT = D_MODEL

VMEM_LIMIT = 56 * 1024 * 1024


def _cparams(sem):
    return pltpu.CompilerParams(dimension_semantics=sem, vmem_limit_bytes=VMEM_LIMIT)


def _rms(x, g):
    return x * lax.rsqrt(jnp.mean(x * x, axis=-1, keepdims=True) + RMS_EPS) * g


def _split3(x):
    hi = x.astype(BF16)
    r1 = x - hi.astype(F32)
    mid = r1.astype(BF16)
    lo = (r1 - mid.astype(F32)).astype(BF16)
    return hi, mid, lo


def _dot(a, b):
    return jnp.dot(a, b, preferred_element_type=F32)


def _dot_nt(a, b):
    return lax.dot_general(a, b, (((1,), (1,)), ((), ())), preferred_element_type=F32)


def _lane_iota(n=LANES):
    return lax.broadcasted_iota(jnp.int32, (1, n), 1)


def _half_mask(q2, half):
    return jnp.where(_lane_iota() // HEAD_DIM == half, q2, jnp.zeros_like(q2))


def _positions_t(j, tk, q0, tq):
    kpos = j * tk + lax.broadcasted_iota(jnp.int32, (tk, 1), 0)
    qpos = q0 + lax.broadcasted_iota(jnp.int32, (1, tq), 1)
    return kpos, qpos


def _stream(base, n_pairs, n_tail, scores, mask, values, st_ref, acc_ref):
    n_chain = acc_ref.shape[0]
    tq = acc_ref.shape[2]

    def park(j, slot):
        for c, st in enumerate(scores(j)):
            st_ref[slot, c] = st

    def step(j, slot, stats, masked, prefetch):
        parts = []
        for c in range(n_chain):
            st = st_ref[slot, c]
            if masked:
                st = mask(j, st)
            m, l = stats[c]
            m_new = jnp.maximum(m, jnp.max(st, axis=0, keepdims=True))
            alpha = jnp.exp(m - m_new)
            p = jnp.exp(st - m_new)
            l_new = alpha * l + jnp.sum(p, axis=0, keepdims=True)
            parts.append((m_new, l_new, alpha, _dot(values(c, j), p.astype(BF16))))
            if c == 0 and prefetch:
                park(j + 1, 1 - slot)
        for c, (_, _, alpha, pv) in enumerate(parts):
            acc_ref[c] = alpha * acc_ref[c] + pv
        return tuple((m_new, l_new) for m_new, l_new, _, _ in parts)

    def pair(u, stats):
        j = base + 2 * u
        return step(j + 1, 1, step(j, 0, stats, False, True), False, True)

    acc_ref[...] = jnp.zeros_like(acc_ref)
    park(base, 0)
    stats = ((jnp.full((1, tq), NEG, F32), jnp.zeros((1, tq), F32)),) * n_chain
    stats = lax.fori_loop(0, n_pairs, pair, stats)
    for jj in range(n_tail):
        stats = step(base + 2 * n_pairs + jj, jj % 2, stats, True, jj + 1 < n_tail)
    return [(l, acc_ref[c]) for c, (_, l) in enumerate(stats)]


def _stream_scratch(n_chain, dv, tq, tk):
    return [pltpu.VMEM((2, n_chain, tk, tq), F32), pltpu.VMEM((n_chain, dv, tq), F32)]


def _rope_kernel(pos_ref, inv_ref, m1_ref, m2_ref, c_ref, s1_ref, s2_ref):
    ang = pos_ref[...].astype(F32) * inv_ref[...]
    c_ref[...] = jnp.cos(ang)
    sn = jnp.sin(ang)
    s1_ref[...] = -sn * m1_ref[...]
    s2_ref[...] = sn * m2_ref[...]


def _rope_tables(positions, tm):
    n = positions.size
    inv = ROPE_THETA ** (-jnp.arange(0, ROPE_DIM, 2, dtype=F32) / ROPE_DIM)
    lane = np.arange(LANES) % HEAD_DIM
    half = ROPE_DIM // 2
    inv_l = jnp.where(lane < ROPE_DIM, inv[lane % half], 0.0).reshape(1, LANES).astype(F32)
    m1 = jnp.asarray((lane < half).astype(np.float32)).reshape(1, LANES)
    m2 = jnp.asarray(((lane >= half) & (lane < ROPE_DIM)).astype(np.float32)).reshape(1, LANES)
    row = pl.BlockSpec((1, LANES), lambda i: (0, 0))
    tab = pl.BlockSpec((tm, LANES), lambda i: (i, 0))
    return pl.pallas_call(
        _rope_kernel,
        grid=(n // tm,),
        in_specs=[pl.BlockSpec((tm, 1), lambda i: (i, 0)), row, row, row],
        out_specs=[tab, tab, tab],
        out_shape=[jax.ShapeDtypeStruct((n, LANES), F32)] * 3,
        compiler_params=_cparams(("parallel",)),
        name="rope_tables",
    )(positions.reshape(n, 1), inv_l, m1, m2)


def _apply_rope(y, c, s1, s2):
    half = ROPE_DIM // 2
    return y * c + pltpu.roll(y, LANES - half, 1) * s1 + pltpu.roll(y, half, 1) * s2


def _project_chunks(h, w_ref, tabs, main_ref, vt_ref, n_main, n_vt, rope_chunks, qscale_chunks):
    c, s1, s2 = tabs
    for ch in range(n_main):
        y = _dot(h, w_ref[:, ch * LANES:(ch + 1) * LANES])
        if ch in rope_chunks:
            y = _apply_rope(y, c, s1, s2)
        if ch in qscale_chunks:
            y = y * Q_SCALE
        main_ref[0, :, ch * LANES:(ch + 1) * LANES] = y.astype(BF16)
    for ch in range(n_vt):
        y = _dot(h, w_ref[:, (n_main + ch) * LANES:(n_main + ch + 1) * LANES])
        vt_ref[0, ch * LANES:(ch + 1) * LANES, :] = y.T.astype(BF16)


def _even_proj_kernel(x_ref, g_ref, w_ref, ws_ref, fb_ref, c_ref, s1_ref, s2_ref, main_ref, vt_ref, small_ref):
    h = _rms(x_ref[0], g_ref[...]).astype(BF16)
    _project_chunks(h, w_ref, (c_ref[0], s1_ref[0], s2_ref[0]), main_ref, vt_ref,
                    EV_MAIN // LANES, EV_VT // LANES, EV_ROPE_CHUNKS, EV_QSCALE_CHUNKS)
    ys = _dot(h, ws_ref[...])
    z = ys + fb_ref[...]
    log_f = jnp.minimum(z, 0.0) - jnp.log(1.0 + jnp.exp(-jnp.abs(z)))
    gate = 1.0 / (1.0 + jnp.exp(-ys))
    small_ref[0] = jnp.where(_lane_iota() < FOX_HEADS, log_f, gate)


def _even_proj(x3, g, w, w_small, fb_row, tabs, tm):
    b, t, _ = x3.shape
    const = lambda shape: pl.BlockSpec(shape, lambda bi, i: (0, 0))
    tab = pl.BlockSpec((1, tm, LANES), lambda bi, i: (bi, i, 0))
    return pl.pallas_call(
        _even_proj_kernel,
        grid=(b, t // tm),
        in_specs=[pl.BlockSpec((1, tm, D_MODEL), lambda bi, i: (bi, i, 0)), const((1, D_MODEL)),
                  const((D_MODEL, EV_MAIN + EV_VT)), const((D_MODEL, LANES)), const((1, LANES)), tab, tab, tab],
        out_specs=[pl.BlockSpec((1, tm, EV_MAIN), lambda bi, i: (bi, i, 0)),
                   pl.BlockSpec((1, EV_VT, tm), lambda bi, i: (bi, 0, i)), tab],
        out_shape=[jax.ShapeDtypeStruct((b, t, EV_MAIN), BF16), jax.ShapeDtypeStruct((b, EV_VT, t), BF16),
                   jax.ShapeDtypeStruct((b, t, LANES), F32)],
        compiler_params=_cparams(("parallel", "parallel")),
        name="even_proj",
    )(x3, g, w, w_small, fb_row, *tabs)


def _odd_proj_kernel(x_ref, g_ref, w_ref, c_ref, s1_ref, s2_ref, main_ref, vt_ref):
    h = _rms(x_ref[0], g_ref[...]).astype(BF16)
    n_main = OD_MAIN // LANES
    _project_chunks(h, w_ref, (c_ref[0], s1_ref[0], s2_ref[0]), main_ref, vt_ref,
                    n_main, OD_VT // LANES, tuple(range(n_main)), tuple(range(n_main // 2)))


def _odd_proj(x3, g, w, tabs, tm):
    b, t, _ = x3.shape
    const = lambda shape: pl.BlockSpec(shape, lambda bi, i: (0, 0))
    tab = pl.BlockSpec((1, tm, LANES), lambda bi, i: (bi, i, 0))
    return pl.pallas_call(
        _odd_proj_kernel,
        grid=(b, t // tm),
        in_specs=[pl.BlockSpec((1, tm, D_MODEL), lambda bi, i: (bi, i, 0)), const((1, D_MODEL)),
                  const((D_MODEL, OD_MAIN + OD_VT)), tab, tab, tab],
        out_specs=[pl.BlockSpec((1, tm, OD_MAIN), lambda bi, i: (bi, i, 0)),
                   pl.BlockSpec((1, OD_VT, tm), lambda bi, i: (bi, 0, i))],
        out_shape=[jax.ShapeDtypeStruct((b, t, OD_MAIN), BF16), jax.ShapeDtypeStruct((b, OD_VT, t), BF16)],
        compiler_params=_cparams(("parallel", "parallel")),
        name="odd_proj",
    )(x3, g, w, *tabs)


def _fox_aug_kernel(lf_ref, tril_ref, eq_ref, ek_ref, oneq_ref, onek_ref, aq_ref, ak_ref, carry_ref):
    @pl.when(pl.program_id(1) == 0)
    def _():
        carry_ref[...] = jnp.zeros_like(carry_ref)

    tril = tril_ref[...]
    c = carry_ref[...]
    for piece in _split3(lf_ref[0]):
        c = c + _dot(tril, piece)
    carry_ref[...] = c[-1:, :]
    aq = oneq_ref[...]
    ak = onek_ref[...]
    for r, piece in enumerate(_split3(c)):
        aq = aq + _dot(piece, eq_ref[r])
        ak = ak - _dot(piece, ek_ref[r])
    aq_ref[0] = aq.astype(BF16)
    ak_ref[0] = ak.astype(BF16)


def _fox_aug(small, tc):
    b, t, _ = small.shape
    tril = jnp.asarray(np.tril(np.ones((tc, tc), np.float32)), BF16)
    eq = np.zeros((3, LANES, LANES), np.float32)
    ek = np.zeros((3, LANES, LANES), np.float32)
    oneq = np.zeros((1, LANES), np.float32)
    onek = np.zeros((1, LANES), np.float32)
    for h in range(FOX_HEADS):
        for r in range(3):
            eq[r, h, AUG_PER_HEAD * h + r] = 1.0
            ek[r, h, AUG_PER_HEAD * h + 3 + r] = 1.0
            oneq[0, AUG_PER_HEAD * h + 3 + r] = 1.0
            onek[0, AUG_PER_HEAD * h + r] = 1.0
    const2 = lambda shape: pl.BlockSpec(shape, lambda bi, i: (0,) * len(shape))
    blk = pl.BlockSpec((1, tc, LANES), lambda bi, i: (bi, i, 0))
    return pl.pallas_call(
        _fox_aug_kernel,
        grid=(b, t // tc),
        in_specs=[blk, const2((tc, tc)), const2((3, LANES, LANES)), const2((3, LANES, LANES)),
                  const2((1, LANES)), const2((1, LANES))],
        out_specs=[blk, blk],
        out_shape=[jax.ShapeDtypeStruct((b, t, LANES), BF16)] * 2,
        scratch_shapes=[pltpu.VMEM((1, LANES), F32)],
        compiler_params=_cparams(("parallel", "arbitrary")),
        name="fox_aug",
    )(small, tril, jnp.asarray(eq, BF16), jnp.asarray(ek, BF16), jnp.asarray(oneq), jnp.asarray(onek))


def _fox_kernel(q_ref, aq_ref, k_ref, ak_ref, vt_ref, o_ref, st_ref, acc_ref, *, tq, tk):
    pair = pl.program_id(1)
    i = pl.program_id(2)
    lane = _lane_iota()
    q2 = q_ref[0]
    qa = aq_ref[0]
    qcats = []
    for hh in range(2):
        head = 2 * pair + hh
        in_head = (lane >= AUG_PER_HEAD * head) & (lane < AUG_PER_HEAD * (head + 1))
        qcats.append(jnp.concatenate([_half_mask(q2, hh), jnp.where(in_head, qa, jnp.zeros_like(qa))], axis=1))

    def scores(j):
        ks = pl.multiple_of(j * tk, tk)
        kcat = jnp.concatenate([k_ref[0, pl.ds(ks, tk), :], ak_ref[0, pl.ds(ks, tk), :]], axis=1)
        return tuple(_dot_nt(kcat, qcats[hh]) for hh in range(2))

    def mask(j, st):
        kpos, qpos = _positions_t(j, tk, i * tq, tq)
        return jnp.where(kpos <= qpos, st, NEG)

    def values(hh, j):
        return vt_ref[0, hh * HEAD_DIM:(hh + 1) * HEAD_DIM, pl.ds(pl.multiple_of(j * tk, tk), tk)]

    res = _stream(0, i * (tq // (2 * tk)), tq // tk, scores, mask, values, st_ref, acc_ref)
    ot = jnp.concatenate([acc / l for l, acc in res], axis=0)
    o_ref[0] = ot.T.astype(BF16)


def _fox_attention(main, aq, ak, vt, tq, tk):
    b, t, _ = main.shape
    return pl.pallas_call(
        partial(_fox_kernel, tq=tq, tk=tk),
        grid=(b, FOX_HEADS // 2, t // tq),
        in_specs=[pl.BlockSpec((1, tq, LANES), lambda bi, p, i: (bi, i, CH_FQ + p)),
                  pl.BlockSpec((1, tq, LANES), lambda bi, p, i: (bi, i, 0)),
                  pl.BlockSpec((1, t, LANES), lambda bi, p, i: (bi, 0, CH_FK + p)),
                  pl.BlockSpec((1, t, LANES), lambda bi, p, i: (bi, 0, 0)),
                  pl.BlockSpec((1, LANES, t), lambda bi, p, i: (bi, VT_FV + p, 0))],
        out_specs=pl.BlockSpec((1, tq, LANES), lambda bi, p, i: (bi, i, p)),
        out_shape=jax.ShapeDtypeStruct((b, t, FOX_W), BF16),
        scratch_shapes=_stream_scratch(2, HEAD_DIM, tq, tk),
        compiler_params=_cparams(("parallel", "parallel", "parallel")),
        name="fox_attention",
    )(main, aq, main, ak, vt)


def _compress_kernel(xk_ref, xv_ref, w1k_ref, w2k_ref, pk_ref, w1v_ref, w2v_ref, pv_ref, kc_ref, vc_ref):
    half = CMP_STRIDE * HEAD_DIM
    for x_ref, w1_ref, w2_ref, p_ref, o_ref in ((xk_ref, w1k_ref, w2k_ref, pk_ref, kc_ref),
                                                 (xv_ref, w1v_ref, w2v_ref, pv_ref, vc_ref)):
        w1 = w1_ref[...]
        pos_h = _dot(p_ref[...], w1)
        out = None
        for g in range(NSA_KV_GROUPS):
            x = x_ref[0, g]
            first = _dot(x, w1[:half])
            second = _dot(x, w1[half:])
            nrow = first.shape[0]
            hid = first + pltpu.roll(second, nrow - 1, 0) + pos_h
            a = jax.nn.gelu(hid, approximate=True).astype(BF16)
            y = _dot(a, w2_ref[g])
            out = y if out is None else out + y
        o_ref[0] = out.astype(BF16)


def _compress(xk, xv, w1k, w2k, pk, w1v, w2v, pv):
    b, g, nchunk, width = xk.shape
    xspec = pl.BlockSpec((1, g, nchunk, width), lambda bi: (bi, 0, 0, 0))
    const = lambda shape: pl.BlockSpec(shape, lambda bi: (0,) * len(shape))
    wspecs = [const((width * 2, CMP_HIDDEN)), const((g, CMP_HIDDEN, LANES)), const((1, width * 2))]
    ospec = pl.BlockSpec((1, nchunk, LANES), lambda bi: (bi, 0, 0))
    return pl.pallas_call(
        _compress_kernel,
        grid=(b,),
        in_specs=[xspec, xspec] + wspecs + wspecs,
        out_specs=[ospec, ospec],
        out_shape=[jax.ShapeDtypeStruct((b, nchunk, LANES), BF16)] * 2,
        compiler_params=_cparams(("parallel",)),
        name="nsa_compress",
    )(xk, xv, w1k, w2k, pk, w1v, w2v, pv)


def _gate_col(small, head, branch):
    idx = FOX_HEADS + 3 * head + branch
    return jnp.sum(jnp.where(_lane_iota() == idx, small, 0.0), axis=-1, keepdims=True)


def _nsa_select_kernel(q_ref, kc_ref, vc_ref, ov_ref, small_ref, oc_ref, sel_ref, *, tq, ns, n_sel):
    q0 = pl.program_id(1) * tq
    lane = _lane_iota()
    kc = kc_ref[0]
    vc = vc_ref[0]
    ncp = kc.shape[0]
    small = small_ref[0]
    qpos = q0 + lax.broadcasted_iota(jnp.int32, (tq, 1), 0)
    cmp_end = lax.broadcasted_iota(jnp.int32, (1, ncp), 1) * CMP_STRIDE + (CMP_BLOCK - 1)
    cmask = cmp_end <= qpos
    psum = [jnp.zeros((tq, ncp), F32) for _ in range(NSA_KV_GROUPS)]
    for n in range(NSA_HPG):
        q2 = q_ref[0, :, n * LANES:(n + 1) * LANES]
        outs = []
        for g in range(NSA_KV_GROUPS):
            z = jnp.where(cmask, _dot_nt(_half_mask(q2, g), kc), -jnp.inf)
            m = jnp.max(z, axis=-1, keepdims=True)
            m = jnp.where(m == -jnp.inf, 0.0, m)
            p = jnp.exp(z - m)
            p = p / jnp.maximum(jnp.sum(p, axis=-1, keepdims=True), 1e-30)
            psum[g] = psum[g] + p
            outs.append(_gate_col(small, g * NSA_HPG + n, 0) * _dot(p.astype(BF16), vc))
        oc_ref[0, :, n * LANES:(n + 1) * LANES] = jnp.where(lane < HEAD_DIM, outs[0], outs[1])

    blk = lane
    cur = qpos // SLC_BLOCK
    valid = blk * SLC_BLOCK <= qpos
    forced = (blk == 0) | (blk == cur) | (blk == cur - 1)
    for g in range(NSA_KV_GROUPS):
        imp = jnp.zeros((tq, LANES), F32)
        for piece in _split3(psum[g]):
            imp = imp + _dot(piece, ov_ref[...])
        score = jnp.where(valid, jnp.where(forced, jnp.inf, imp), -jnp.inf)
        rank = jnp.zeros((tq, LANES), jnp.int32)
        for i in range(ns):
            col = score[:, i:i + 1]
            ahead = (col > score) | ((col == score) & (lane > i))
            rank = rank + ahead.astype(jnp.int32)
        sel = (rank < n_sel) & (lane < ns)
        sel_ref[0, :, g * LANES:(g + 1) * LANES] = jnp.where(sel, 0.0, NEG).astype(BF16)


def _nsa_select(main, kc, vc, overlap, small, tq, ns, n_sel):
    b, t, _ = main.shape
    ncp = kc.shape[1]
    return pl.pallas_call(
        partial(_nsa_select_kernel, tq=tq, ns=ns, n_sel=n_sel),
        grid=(b, t // tq),
        in_specs=[pl.BlockSpec((1, tq, NSA_W), lambda bi, i: (bi, i, CH_NQ * LANES // NSA_W)),
                  pl.BlockSpec((1, ncp, LANES), lambda bi, i: (bi, 0, 0)),
                  pl.BlockSpec((1, ncp, LANES), lambda bi, i: (bi, 0, 0)),
                  pl.BlockSpec((ncp, LANES), lambda bi, i: (0, 0)),
                  pl.BlockSpec((1, tq, LANES), lambda bi, i: (bi, i, 0))],
        out_specs=[pl.BlockSpec((1, tq, NSA_W), lambda bi, i: (bi, i, 0)),
                   pl.BlockSpec((1, tq, NSA_KV_GROUPS * LANES), lambda bi, i: (bi, i, 0))],
        out_shape=[jax.ShapeDtypeStruct((b, t, NSA_W), F32),
                   jax.ShapeDtypeStruct((b, t, NSA_KV_GROUPS * LANES), BF16)],
        compiler_params=_cparams(("parallel", "parallel")),
        name="nsa_select",
    )(main, kc, vc, overlap, small)


def _nsa_flash_kernel(q_ref, sel_ref, oc_ref, small_ref, ks_ref, kw_ref, hot_ref, vst_ref, vwt_ref, o_ref,
                      st_ref, acc_ref, *, tq, tk):
    n = pl.program_id(1)
    i = pl.program_id(2)
    q0 = i * tq
    q2 = q_ref[0]
    small = small_ref[0]
    qhs = [_half_mask(q2, g) for g in range(NSA_KV_GROUPS)]
    qcats = [jnp.concatenate([qhs[g], sel_ref[0, :, g * LANES:(g + 1) * LANES]], axis=1)
             for g in range(NSA_KV_GROUPS)]
    rows = lambda g: slice(g * HEAD_DIM, (g + 1) * HEAD_DIM)

    def sel_scores(j):
        ks = pl.multiple_of(j * tk, tk)
        kcat = jnp.concatenate([ks_ref[0, pl.ds(ks, tk), :], hot_ref[pl.ds(ks, tk), :]], axis=1)
        return tuple(_dot_nt(kcat, qcats[g]) for g in range(NSA_KV_GROUPS))

    def sel_mask(j, st):
        kpos, qpos = _positions_t(j, tk, q0, tq)
        return jnp.where(kpos <= qpos, st, NEG)

    tile_start = lambda j: pl.multiple_of(jnp.maximum(j, 0) * tk, tk)

    def win_scores(j):
        kw = kw_ref[0, pl.ds(tile_start(j), tk), :]
        return tuple(_dot_nt(kw, qhs[g]) for g in range(NSA_KV_GROUPS))

    def win_mask(j, st):
        kpos, qpos = _positions_t(j, tk, q0, tq)
        return jnp.where((kpos <= qpos) & (kpos > qpos - WINDOW) & (kpos >= 0), st, NEG)

    values = lambda ref: lambda g, j: ref[0, rows(g), pl.ds(tile_start(j), tk)]
    lane = _lane_iota()
    n_win = (WINDOW + tq) // tk
    o = oc_ref[0]
    for branch, args in ((1, (0, i * (tq // (2 * tk)), tq // tk, sel_scores, sel_mask, values(vst_ref))),
                         (2, ((i + 1) * (tq // tk) - n_win, 0, n_win, win_scores, win_mask, values(vwt_ref)))):
        res = _stream(*args, st_ref, acc_ref)
        ot = jnp.concatenate([acc / l for l, acc in res], axis=0)
        gate = jnp.where(lane < HEAD_DIM, _gate_col(small, n, branch), _gate_col(small, NSA_HPG + n, branch))
        o = o + gate * ot.T
    o_ref[0] = o.astype(BF16)


def _nsa_flash(main, sel, oc, small, hot, vt, tq, tk):
    b, t, _ = main.shape
    tile = lambda ch: pl.BlockSpec((1, tq, LANES), lambda bi, n, i: (bi, i, ch + n))
    full = lambda ch: pl.BlockSpec((1, t, LANES), lambda bi, n, i: (bi, 0, ch))
    vfull = lambda ch: pl.BlockSpec((1, LANES, t), lambda bi, n, i: (bi, ch, 0))
    return pl.pallas_call(
        partial(_nsa_flash_kernel, tq=tq, tk=tk),
        grid=(b, NSA_HPG, t // tq),
        in_specs=[tile(CH_NQ),
                  pl.BlockSpec((1, tq, NSA_KV_GROUPS * LANES), lambda bi, n, i: (bi, i, 0)),
                  tile(0),
                  pl.BlockSpec((1, tq, LANES), lambda bi, n, i: (bi, i, 0)),
                  full(CH_KS), full(CH_KW),
                  pl.BlockSpec((t, LANES), lambda bi, n, i: (0, 0)),
                  vfull(VT_VS), vfull(VT_VW)],
        out_specs=tile(0),
        out_shape=jax.ShapeDtypeStruct((b, t, NSA_W), BF16),
        scratch_shapes=_stream_scratch(NSA_KV_GROUPS, HEAD_DIM, tq, tk),
        compiler_params=_cparams(("parallel", "parallel", "parallel")),
        name="nsa_flash",
    )(main, sel, oc, small, main, main, hot, vt, vt)


def _diff_kernel(q_ref, k_ref, vt_ref, lam_ref, g_ref, o_ref, st_ref, acc_ref, *, tq, tk, lam_init):
    i = pl.program_id(2)
    q2 = q_ref[0]
    lp = lam_ref[...]
    lam = (jnp.exp(jnp.sum(lp[0:1] * lp[1:2], axis=-1, keepdims=True))
           - jnp.exp(jnp.sum(lp[2:3] * lp[3:4], axis=-1, keepdims=True)) + lam_init)
    qhs = [_half_mask(q2, comp) for comp in range(2)]

    def scores(j):
        k2 = k_ref[0, pl.ds(pl.multiple_of(j * tk, tk), tk), :]
        return tuple(_dot_nt(k2, qhs[comp]) for comp in range(2))

    def mask(j, st):
        kpos, qpos = _positions_t(j, tk, i * tq, tq)
        return jnp.where(kpos <= qpos, st, NEG)

    def values(comp, j):
        return vt_ref[0, :, pl.ds(pl.multiple_of(j * tk, tk), tk)]

    (l1, acc1), (l2, acc2) = _stream(0, i * (tq // (2 * tk)), tq // tk, scores, mask, values, st_ref, acc_ref)
    o = (acc1 / l1 - lam * (acc2 / l2)).T
    o_ref[0] = (_rms(o, g_ref[...]) * (1.0 - lam_init)).astype(BF16)


def _diff_attention(main, vt, lam_p, subln_g, tq, tk, lam_init):
    b, t, _ = main.shape
    nh = DIFF_HEADS
    return pl.pallas_call(
        partial(_diff_kernel, tq=tq, tk=tk, lam_init=lam_init),
        grid=(b, nh, t // tq),
        in_specs=[pl.BlockSpec((1, tq, LANES), lambda bi, h, i: (bi, i, h)),
                  pl.BlockSpec((1, t, LANES), lambda bi, h, i: (bi, 0, nh + h)),
                  pl.BlockSpec((1, LANES, t), lambda bi, h, i: (bi, h, 0)),
                  pl.BlockSpec((4, LANES), lambda bi, h, i: (0, 0)),
                  pl.BlockSpec((1, LANES), lambda bi, h, i: (0, 0))],
        out_specs=pl.BlockSpec((1, tq, LANES), lambda bi, h, i: (bi, i, h)),
        out_shape=jax.ShapeDtypeStruct((b, t, nh * LANES), BF16),
        scratch_shapes=_stream_scratch(2, LANES, tq, tk),
        compiler_params=_cparams(("parallel", "parallel", "parallel")),
        name="diff_attention",
    )(main, main, vt, lam_p, subln_g)


def _out_proj_kernel(*refs, n_in):
    a_refs, w_refs = refs[:n_in], refs[n_in:2 * n_in]
    x_ref, g_ref, o_ref = refs[2 * n_in:]
    y = None
    for a_ref, w_ref in zip(a_refs, w_refs):
        t = _dot(a_ref[...], w_ref[...])
        y = t if y is None else y + t
    o_ref[...] = x_ref[...] + _rms(y, g_ref[...])


def _out_proj(acts, weights, x2, g, tm):
    n = x2.shape[0]
    n_in = len(acts)
    const = lambda shape: pl.BlockSpec(shape, lambda i: (0, 0))
    return pl.pallas_call(
        partial(_out_proj_kernel, n_in=n_in),
        grid=(n // tm,),
        in_specs=[pl.BlockSpec((tm, a.shape[1]), lambda i: (i, 0)) for a in acts]
                 + [const(w.shape) for w in weights]
                 + [pl.BlockSpec((tm, D_MODEL), lambda i: (i, 0)), const((1, D_MODEL))],
        out_specs=pl.BlockSpec((tm, D_MODEL), lambda i: (i, 0)),
        out_shape=jax.ShapeDtypeStruct((n, D_MODEL), F32),
        compiler_params=_cparams(("parallel",)),
        name="out_proj",
    )(*acts, *weights, x2, g)


def _mem_kv_kernel(mem_ref, g_ref, wk_ref, wv_ref, k_ref, v_ref):
    mn = _rms(mem_ref[0], g_ref[0]).astype(BF16)
    k_ref[0, 0] = _dot(mn, wk_ref[0]).astype(BF16)
    v_ref[0, 0] = _dot(mn, wv_ref[0]).astype(BF16)


def _mem_kv(mem, mem_norm_g, wk, wv):
    depth = wk.shape[0]
    b, m, d = mem.shape
    wspec = pl.BlockSpec((1, d, MEM_W), lambda l, bi: (l, 0, 0))
    ospec = pl.BlockSpec((1, 1, m, MEM_W), lambda l, bi: (l, bi, 0, 0))
    return pl.pallas_call(
        _mem_kv_kernel,
        grid=(depth, b),
        in_specs=[pl.BlockSpec((1, m, d), lambda l, bi: (bi, 0, 0)),
                  pl.BlockSpec((1, 1, d), lambda l, bi: (l, 0, 0)), wspec, wspec],
        out_specs=[ospec, ospec],
        out_shape=[jax.ShapeDtypeStruct((depth, b, m, MEM_W), BF16)] * 2,
        compiler_params=_cparams(("parallel", "parallel")),
        name="mem_kv",
    )(mem, mem_norm_g.reshape(depth, 1, d), wk, wv)


def _cross_kernel(x_ref, gin_ref, gout_ref, wq_ref, k_ref, v_ref, wo_ref, o_ref):
    lane = _lane_iota()
    x = x_ref[...]
    h = _rms(x, gin_ref[...]).astype(BF16)
    q = (_dot(h, wq_ref[...]) * Q_SCALE).astype(BF16)
    chunks = []
    for ch in range(MEM_W // LANES):
        q2 = q[:, ch * LANES:(ch + 1) * LANES]
        k2 = k_ref[0, 0, :, ch * LANES:(ch + 1) * LANES]
        v2 = v_ref[0, 0, :, ch * LANES:(ch + 1) * LANES]
        outs = []
        for hh in range(2):
            s = _dot_nt(_half_mask(q2, hh), k2)
            p = jnp.exp(s - jnp.max(s, axis=-1, keepdims=True))
            p = p / jnp.sum(p, axis=-1, keepdims=True)
            outs.append(_dot(p.astype(BF16), v2))
        chunks.append(jnp.where(lane < HEAD_DIM, outs[0], outs[1]).astype(BF16))
    y = _dot(jnp.concatenate(chunks, axis=1), wo_ref[...])
    o_ref[...] = x + _rms(y, gout_ref[...])


def _cross_attention(x2, gin, gout, wq, mem_k, mem_v, wo, layer, tm, rows_per_batch):
    n = x2.shape[0]
    m = mem_k.shape[2]
    const = lambda shape: pl.BlockSpec(shape, lambda i: (0, 0))
    kv = pl.BlockSpec((1, 1, m, MEM_W), lambda i: (layer, i // rows_per_batch, 0, 0))
    return pl.pallas_call(
        _cross_kernel,
        grid=(n // tm,),
        in_specs=[pl.BlockSpec((tm, D_MODEL), lambda i: (i, 0)), const((1, D_MODEL)), const((1, D_MODEL)),
                  const((D_MODEL, MEM_W)), kv, kv, const((MEM_W, D_MODEL))],
        out_specs=pl.BlockSpec((tm, D_MODEL), lambda i: (i, 0)),
        out_shape=jax.ShapeDtypeStruct((n, D_MODEL), F32),
        compiler_params=_cparams(("parallel",)),
        name="cross_attention",
    )(x2, gin, gout, wq, mem_k, mem_v, wo)


def _ffn_kernel(x_ref, gin_ref, gout_ref, wg_ref, wu_ref, wd_ref, o_ref, *, chunk):
    x = x_ref[...]
    h = _rms(x, gin_ref[...]).astype(BF16)
    d_ff = wg_ref.shape[1]
    y = None
    for c0 in range(0, d_ff, chunk):
        gate = _dot(h, wg_ref[:, c0:c0 + chunk])
        up = _dot(h, wu_ref[:, c0:c0 + chunk])
        a = (gate * (1.0 / (1.0 + jnp.exp(-gate))) * up).astype(BF16)
        t = _dot(a, wd_ref[c0:c0 + chunk, :])
        y = t if y is None else y + t
    o_ref[...] = x + _rms(y, gout_ref[...])


def _ffn(x2, gin, gout, wg, wu, wd, tm, chunk):
    n = x2.shape[0]
    d_ff = wg.shape[1]
    const = lambda shape: pl.BlockSpec(shape, lambda i: (0, 0), pipeline_mode=pl.Buffered(1))
    return pl.pallas_call(
        partial(_ffn_kernel, chunk=chunk),
        grid=(n // tm,),
        in_specs=[pl.BlockSpec((tm, D_MODEL), lambda i: (i, 0)), const((1, D_MODEL)), const((1, D_MODEL)),
                  const((D_MODEL, d_ff)), const((D_MODEL, d_ff)), const((d_ff, D_MODEL))],
        out_specs=pl.BlockSpec((tm, D_MODEL), lambda i: (i, 0)),
        out_shape=jax.ShapeDtypeStruct((n, D_MODEL), F32),
        compiler_params=_cparams(("parallel",)),
        name="ffn",
    )(x2, gin, gout, wg, wu, wd)


def _even_weights(w_in, w_out):
    kvw = NSA_KV_GROUPS * HEAD_DIM
    offs = [int(o) for o in np.cumsum((FOX_W, FOX_W, FOX_W, FOX_HEADS, NSA_W) + (kvw,) * 6 + (3 * NSA_HEADS,))]
    fk0, fv0, fl0, nq0, kc0, vc0, ks0, vs0, kw0, vw0, gl0, end = offs
    perm = np.concatenate([np.arange(HEAD_DIM) + HEAD_DIM * (g * NSA_HPG + n)
                           for n in range(NSA_HPG) for g in range(NSA_KV_GROUPS)])
    cols = lambda a, b: w_in[:, a:b]
    w = jnp.concatenate([cols(0, fv0), w_in[:, nq0 + perm], cols(kc0, vc0), cols(vc0, ks0), cols(ks0, vs0),
                         cols(kw0, vw0), cols(fv0, fl0), cols(vs0, kw0), cols(vw0, gl0)], axis=1).astype(BF16)
    w_small = jnp.concatenate([cols(fl0, nq0), cols(gl0, end),
                               jnp.zeros((D_MODEL, LANES - FOX_HEADS - 3 * NSA_HEADS), w_in.dtype)],
                              axis=1).astype(BF16)
    w_out_fox = w_out[:FOX_W].astype(BF16)
    w_out_nsa = w_out[FOX_W + perm].astype(BF16)
    return w, w_small, w_out_fox, w_out_nsa


def _overlap_matrix(t, ncp):
    nc = (t - CMP_BLOCK) // CMP_STRIDE + 1
    ns = t // SLC_BLOCK
    cs = np.arange(nc) * CMP_STRIDE
    ss = np.arange(ns) * SLC_BLOCK
    ov = np.clip(np.minimum(cs[:, None] + CMP_BLOCK, ss[None, :] + SLC_BLOCK)
                 - np.maximum(cs[:, None], ss[None, :]), 0, None) / CMP_BLOCK
    full = np.zeros((ncp, LANES), np.float32)
    full[:nc, :ns] = ov
    return jnp.asarray(full, BF16), ns


def _stride_chunks(main, ch):
    b, t, _ = main.shape
    x = main[:, :, ch * LANES:(ch + 1) * LANES].reshape(b, t // CMP_STRIDE, CMP_STRIDE, NSA_KV_GROUPS, HEAD_DIM)
    return jnp.transpose(x, (0, 3, 1, 2, 4)).reshape(b, NSA_KV_GROUPS, t // CMP_STRIDE, CMP_STRIDE * HEAD_DIM)


def _pad_w2(w2):
    out = jnp.zeros((NSA_KV_GROUPS, CMP_HIDDEN, LANES), BF16)
    for g in range(NSA_KV_GROUPS):
        out = out.at[g, :, g * HEAD_DIM:(g + 1) * HEAD_DIM].set(w2.astype(BF16))
    return out


def kernel(x, mem, positions, sandwich_g, mem_norm_g, ev_w_in, ev_fox_fbias, ev_cmp_pos_k, ev_cmp_w1_k, ev_cmp_w2_k, ev_cmp_pos_v, ev_cmp_w1_v, ev_cmp_w2_v, ev_w_out, od_w_in, od_lambda, od_subln_g, od_w_out, ca_wq, ca_wk, ca_wv, ca_wo, ffn_wg, ffn_wu, ffn_wd):
    b, t, d = x.shape
    depth = sandwich_g.shape[0]
    n = b * t
    tm = 256
    tq, tk = 512, 256
    assert d == D_MODEL and t % tq == 0 and tq % (2 * tk) == 0 and WINDOW % tk == 0 and t % tm == 0

    tabs = tuple(a.reshape(b, t, LANES) for a in _rope_tables(positions, 512))
    mem_k, mem_v = _mem_kv(mem, mem_norm_g, ca_wk.astype(BF16), ca_wv.astype(BF16))
    ncp = t // CMP_STRIDE
    overlap, ns = _overlap_matrix(t, ncp)
    n_sel = min(SLC_TOPK, ns)
    hot = jnp.asarray((np.arange(t)[:, None] // SLC_BLOCK == np.arange(LANES)[None, :]).astype(np.float32), BF16)
    gain = lambda l, j: sandwich_g[l, j].reshape(1, d)

    x2 = x.reshape(n, d)
    for layer in range(depth):
        x3 = x2.reshape(b, t, d)
        if layer % 2 == 0:
            e = layer // 2
            w, w_small, w_out_fox, w_out_nsa = _even_weights(ev_w_in[e], ev_w_out[e])
            fb_row = jnp.zeros((1, LANES), F32).at[0, :FOX_HEADS].set(ev_fox_fbias[e].astype(F32))
            main, vt, small = _even_proj(x3, gain(layer, 0), w, w_small, fb_row, tabs, tm)
            aq, ak = _fox_aug(small, 256)
            o_fox = _fox_attention(main, aq, ak, vt, tq, tk)
            kc, vc = _compress(
                _stride_chunks(main, CH_KC), _stride_chunks(main, CH_VC),
                ev_cmp_w1_k[e].astype(BF16), _pad_w2(ev_cmp_w2_k[e]), ev_cmp_pos_k[e].reshape(1, -1).astype(BF16),
                ev_cmp_w1_v[e].astype(BF16), _pad_w2(ev_cmp_w2_v[e]), ev_cmp_pos_v[e].reshape(1, -1).astype(BF16))
            oc, sel = _nsa_select(main, kc, vc, overlap, small, 128, ns, n_sel)
            o_nsa = _nsa_flash(main, sel, oc, small, hot, vt, tq, tk)
            x2 = _out_proj([o_fox.reshape(n, FOX_W), o_nsa.reshape(n, NSA_W)], [w_out_fox, w_out_nsa],
                           x2, gain(layer, 1), tm)
        else:
            o = layer // 2
            w_in = od_w_in[o].astype(BF16)
            main, vt = _odd_proj(x3, gain(layer, 0), w_in, tabs, tm)
            lam_init = 0.8 - 0.6 * math.exp(-0.3 * layer)
            lam_p = jnp.pad(od_lambda[o].astype(F32), ((0, 0), (0, LANES - HEAD_DIM)))
            attn = _diff_attention(main, vt, lam_p, od_subln_g[o].reshape(1, LANES).astype(F32), tq, tk, lam_init)
            x2 = _out_proj([attn.reshape(n, D_MODEL)], [od_w_out[o].astype(BF16)], x2, gain(layer, 1), tm)
        x2 = _cross_attention(x2, gain(layer, 2), gain(layer, 3), ca_wq[layer].astype(BF16), mem_k, mem_v,
                              ca_wo[layer].astype(BF16), layer, tm, t // tm)
        x2 = _ffn(x2, gain(layer, 4), gain(layer, 5), ffn_wg[layer].astype(BF16), ffn_wu[layer].astype(BF16),
                  ffn_wd[layer].astype(BF16), tm, 256)
    return x2.reshape(b, t, d)
```

```python
import math
from functools import partial

import numpy as np
import jax
import jax.numpy as jnp
from jax import lax
from jax.experimental import pallas as pl
from jax.experimental.pallas import tpu as pltpu

F32 = jnp.float32
BF16 = jnp.bfloat16

D_MODEL = 1024
HEAD_DIM = 64
LANES = 128
ROPE_DIM = HEAD_DIM // 4
ROPE_THETA = 500000.0
FOX_HEADS = 8
NSA_HEADS = 8
NSA_KV_GROUPS = 2
NSA_HPG = NSA_HEADS // NSA_KV_GROUPS
CMP_BLOCK = 32
CMP_STRIDE = 16
CMP_HIDDEN = 2 * HEAD_DIM
SLC_BLOCK = 64
SLC_TOPK = 16
WINDOW = 512
DIFF_HEADS = 8
MEM_HEADS = 4
MEM_W = MEM_HEADS * HEAD_DIM
RMS_EPS = 1e-6
Q_SCALE = HEAD_DIM ** -0.5
LOG2E = math.log2(math.e)
Q_SCALE_LOG2 = Q_SCALE * LOG2E
NEG = -1e30
AUG_PER_HEAD = 6
ONES_ROWS = 16

FOX_W = FOX_HEADS * HEAD_DIM
NSA_W = NSA_HEADS * HEAD_DIM
CH_FQ, CH_FK, CH_NQ, CH_KC, CH_VC, CH_KS, CH_KW = 0, 4, 8, 12, 13, 14, 15
EV_MAIN = 16 * LANES
EV_ROPE_CHUNKS = tuple(range(CH_NQ, CH_NQ + 4)) + (CH_KC, CH_KS, CH_KW)
EV_QSCALE_CHUNKS = tuple(range(CH_FQ, CH_FQ + 4)) + tuple(range(CH_NQ, CH_NQ + 4))
VT_FV, VT_VS, VT_VW = 0, 4, 5
EV_VT = 6 * LANES
OD_MAIN = 2 * D_MODEL
OD_VT = D_MODEL

VMEM_LIMIT = 56 * 1024 * 1024


def _cparams(sem):
    return pltpu.CompilerParams(dimension_semantics=sem, vmem_limit_bytes=VMEM_LIMIT)


def _rms(x, g):
    return x * lax.rsqrt(jnp.mean(x * x, axis=-1, keepdims=True) + RMS_EPS) * g


def _split3(x):
    hi = x.astype(BF16)
    r1 = x - hi.astype(F32)
    mid = r1.astype(BF16)
    lo = (r1 - mid.astype(F32)).astype(BF16)
    return hi, mid, lo


def _dot(a, b):
    return jnp.dot(a, b, preferred_element_type=F32)


def _dot_nt(a, b):
    return lax.dot_general(a, b, (((1,), (1,)), ((), ())), preferred_element_type=F32)


def _lane_iota(n=LANES):
    return lax.broadcasted_iota(jnp.int32, (1, n), 1)


def _half_mask(q2, half):
    return jnp.where(_lane_iota() // HEAD_DIM == half, q2, jnp.zeros_like(q2))


def _positions_t(j, tk, q0, tq):
    kpos = j * tk + lax.broadcasted_iota(jnp.int32, (tk, 1), 0)
    qpos = q0 + lax.broadcasted_iota(jnp.int32, (1, tq), 1)
    return kpos, qpos


def _stream(base, n_pairs, n_tail, scores, mask, values, st_ref, acc_ref):
    n_chain, dv, tq = acc_ref.shape

    def park(j, slot):
        for c, st in enumerate(scores(j)):
            st_ref[slot, c] = st

    def step(j, slot, stats, masked, prefetch):
        if prefetch:
            park(j + 1, 1 - slot)
        parts = []
        for c in range(n_chain):
            st = st_ref[slot, c]
            if masked:
                st = mask(j, st)
            m, _ = stats[c]
            m_new = jnp.maximum(m, jnp.max(st, axis=0, keepdims=True))
            p = jnp.exp2(st - m_new).astype(BF16)
            vt1 = jnp.concatenate([values(c, j), jnp.ones((ONES_ROWS, p.shape[0]), BF16)], axis=0)
            parts.append((m_new, jnp.exp2(m - m_new), _dot(vt1, p)))
        out = []
        for c, (m_new, alpha, pv) in enumerate(parts):
            acc_ref[c] = alpha * acc_ref[c] + pv[:dv]
            out.append((m_new, alpha * stats[c][1] + pv[dv:dv + 1]))
        return tuple(out)

    def pair(u, stats):
        j = base + 2 * u
        return step(j + 1, 1, step(j, 0, stats, False, True), False, True)

    acc_ref[...] = jnp.zeros_like(acc_ref)
    park(base, 0)
    stats = ((jnp.full((1, tq), NEG, F32), jnp.zeros((1, tq), F32)),) * n_chain
    stats = lax.fori_loop(0, n_pairs, pair, stats)
    for jj in range(n_tail):
        stats = step(base + 2 * n_pairs + jj, jj % 2, stats, True, jj + 1 < n_tail)
    return [(l, acc_ref[c]) for c, (_, l) in enumerate(stats)]


def _stream_scratch(n_chain, dv, tq, tk):
    return [pltpu.VMEM((2, n_chain, tk, tq), F32), pltpu.VMEM((n_chain, dv, tq), F32)]


def _rope_kernel(pos_ref, inv_ref, m1_ref, m2_ref, c_ref, s1_ref, s2_ref):
    ang = pos_ref[...].astype(F32) * inv_ref[...]
    c_ref[...] = jnp.cos(ang)
    sn = jnp.sin(ang)
    s1_ref[...] = -sn * m1_ref[...]
    s2_ref[...] = sn * m2_ref[...]


def _rope_tables(positions, tm):
    n = positions.size
    inv = ROPE_THETA ** (-jnp.arange(0, ROPE_DIM, 2, dtype=F32) / ROPE_DIM)
    lane = np.arange(LANES) % HEAD_DIM
    half = ROPE_DIM // 2
    inv_l = jnp.where(lane < ROPE_DIM, inv[lane % half], 0.0).reshape(1, LANES).astype(F32)
    m1 = jnp.asarray((lane < half).astype(np.float32)).reshape(1, LANES)
    m2 = jnp.asarray(((lane >= half) & (lane < ROPE_DIM)).astype(np.float32)).reshape(1, LANES)
    row = pl.BlockSpec((1, LANES), lambda i: (0, 0))
    tab = pl.BlockSpec((tm, LANES), lambda i: (i, 0))
    return pl.pallas_call(
        _rope_kernel,
        grid=(n // tm,),
        in_specs=[pl.BlockSpec((tm, 1), lambda i: (i, 0)), row, row, row],
        out_specs=[tab, tab, tab],
        out_shape=[jax.ShapeDtypeStruct((n, LANES), F32)] * 3,
        compiler_params=_cparams(("parallel",)),
        name="rope_tables",
    )(positions.reshape(n, 1), inv_l, m1, m2)


def _apply_rope(y, c, s1, s2):
    half = ROPE_DIM // 2
    return y * c + pltpu.roll(y, LANES - half, 1) * s1 + pltpu.roll(y, half, 1) * s2


def _project_chunks(h, w_ref, tabs, main_ref, vt_ref, n_main, n_vt, rope_chunks, qscale_chunks):
    c, s1, s2 = tabs
    for ch in range(n_main):
        y = _dot(h, w_ref[:, ch * LANES:(ch + 1) * LANES])
        if ch in rope_chunks:
            y = _apply_rope(y, c, s1, s2)
        if ch in qscale_chunks:
            y = y * Q_SCALE_LOG2
        main_ref[0, :, ch * LANES:(ch + 1) * LANES] = y.astype(BF16)
    for ch in range(n_vt):
        y = _dot(h, w_ref[:, (n_main + ch) * LANES:(n_main + ch + 1) * LANES])
        vt_ref[0, ch * LANES:(ch + 1) * LANES, :] = y.T.astype(BF16)


def _even_proj_kernel(x_ref, g_ref, w_ref, ws_ref, fb_ref, c_ref, s1_ref, s2_ref, main_ref, vt_ref, small_ref):
    h = _rms(x_ref[0], g_ref[...]).astype(BF16)
    _project_chunks(h, w_ref, (c_ref[0], s1_ref[0], s2_ref[0]), main_ref, vt_ref,
                    EV_MAIN // LANES, EV_VT // LANES, EV_ROPE_CHUNKS, EV_QSCALE_CHUNKS)
    ys = _dot(h, ws_ref[...])
    z = ys + fb_ref[...]
    log_f = jnp.minimum(z, 0.0) - jnp.log(1.0 + jnp.exp(-jnp.abs(z)))
    gate = 1.0 / (1.0 + jnp.exp(-ys))
    small_ref[0] = jnp.where(_lane_iota() < FOX_HEADS, log_f, gate)


def _even_proj(x3, g, w, w_small, fb_row, tabs, tm):
    b, t, _ = x3.shape
    const = lambda shape: pl.BlockSpec(shape, lambda bi, i: (0, 0))
    tab = pl.BlockSpec((1, tm, LANES), lambda bi, i: (bi, i, 0))
    return pl.pallas_call(
        _even_proj_kernel,
        grid=(b, t // tm),
        in_specs=[pl.BlockSpec((1, tm, D_MODEL), lambda bi, i: (bi, i, 0)), const((1, D_MODEL)),
                  const((D_MODEL, EV_MAIN + EV_VT)), const((D_MODEL, LANES)), const((1, LANES)), tab, tab, tab],
        out_specs=[pl.BlockSpec((1, tm, EV_MAIN), lambda bi, i: (bi, i, 0)),
                   pl.BlockSpec((1, EV_VT, tm), lambda bi, i: (bi, 0, i)), tab],
        out_shape=[jax.ShapeDtypeStruct((b, t, EV_MAIN), BF16), jax.ShapeDtypeStruct((b, EV_VT, t), BF16),
                   jax.ShapeDtypeStruct((b, t, LANES), F32)],
        compiler_params=_cparams(("parallel", "parallel")),
        name="even_proj",
    )(x3, g, w, w_small, fb_row, *tabs)


def _odd_proj_kernel(x_ref, g_ref, w_ref, c_ref, s1_ref, s2_ref, main_ref, vt_ref):
    h = _rms(x_ref[0], g_ref[...]).astype(BF16)
    n_main = OD_MAIN // LANES
    _project_chunks(h, w_ref, (c_ref[0], s1_ref[0], s2_ref[0]), main_ref, vt_ref,
                    n_main, OD_VT // LANES, tuple(range(n_main)), tuple(range(n_main // 2)))


def _odd_proj(x3, g, w, tabs, tm):
    b, t, _ = x3.shape
    const = lambda shape: pl.BlockSpec(shape, lambda bi, i: (0, 0))
    tab = pl.BlockSpec((1, tm, LANES), lambda bi, i: (bi, i, 0))
    return pl.pallas_call(
        _odd_proj_kernel,
        grid=(b, t // tm),
        in_specs=[pl.BlockSpec((1, tm, D_MODEL), lambda bi, i: (bi, i, 0)), const((1, D_MODEL)),
                  const((D_MODEL, OD_MAIN + OD_VT)), tab, tab, tab],
        out_specs=[pl.BlockSpec((1, tm, OD_MAIN), lambda bi, i: (bi, i, 0)),
                   pl.BlockSpec((1, OD_VT, tm), lambda bi, i: (bi, 0, i))],
        out_shape=[jax.ShapeDtypeStruct((b, t, OD_MAIN), BF16), jax.ShapeDtypeStruct((b, OD_VT, t), BF16)],
        compiler_params=_cparams(("parallel", "parallel")),
        name="odd_proj",
    )(x3, g, w, *tabs)


def _fox_aug_kernel(lf_ref, tril_ref, eq_ref, ek_ref, oneq_ref, onek_ref, aq_ref, ak_ref, carry_ref):
    @pl.when(pl.program_id(1) == 0)
    def _():
        carry_ref[...] = jnp.zeros_like(carry_ref)

    tril = tril_ref[...]
    c = carry_ref[...]
    for piece in _split3(lf_ref[0]):
        c = c + _dot(tril, piece)
    carry_ref[...] = c[-1:, :]
    aq = oneq_ref[...]
    ak = onek_ref[...]
    for r, piece in enumerate(_split3(c * LOG2E)):
        aq = aq + _dot(piece, eq_ref[r])
        ak = ak - _dot(piece, ek_ref[r])
    aq_ref[0] = aq.astype(BF16)
    ak_ref[0] = ak.astype(BF16)


def _fox_aug(small, tc):
    b, t, _ = small.shape
    tril = jnp.asarray(np.tril(np.ones((tc, tc), np.float32)), BF16)
    eq = np.zeros((3, LANES, LANES), np.float32)
    ek = np.zeros((3, LANES, LANES), np.float32)
    oneq = np.zeros((1, LANES), np.float32)
    onek = np.zeros((1, LANES), np.float32)
    for h in range(FOX_HEADS):
        for r in range(3):
            eq[r, h, AUG_PER_HEAD * h + r] = 1.0
            ek[r, h, AUG_PER_HEAD * h + 3 + r] = 1.0
            oneq[0, AUG_PER_HEAD * h + 3 + r] = 1.0
            onek[0, AUG_PER_HEAD * h + r] = 1.0
    const2 = lambda shape: pl.BlockSpec(shape, lambda bi, i: (0,) * len(shape))
    blk = pl.BlockSpec((1, tc, LANES), lambda bi, i: (bi, i, 0))
    return pl.pallas_call(
        _fox_aug_kernel,
        grid=(b, t // tc),
        in_specs=[blk, const2((tc, tc)), const2((3, LANES, LANES)), const2((3, LANES, LANES)),
                  const2((1, LANES)), const2((1, LANES))],
        out_specs=[blk, blk],
        out_shape=[jax.ShapeDtypeStruct((b, t, LANES), BF16)] * 2,
        scratch_shapes=[pltpu.VMEM((1, LANES), F32)],
        compiler_params=_cparams(("parallel", "arbitrary")),
        name="fox_aug",
    )(small, tril, jnp.asarray(eq, BF16), jnp.asarray(ek, BF16), jnp.asarray(oneq), jnp.asarray(onek))


def _fox_kernel(q_ref, aq_ref, k_ref, ak_ref, vt_ref, o_ref, st_ref, acc_ref, *, tq, tk):
    pair = pl.program_id(1)
    i = pl.program_id(2)
    lane = _lane_iota()
    q2 = q_ref[0]
    qa = aq_ref[0]
    qcats = []
    for hh in range(2):
        head = 2 * pair + hh
        in_head = (lane >= AUG_PER_HEAD * head) & (lane < AUG_PER_HEAD * (head + 1))
        qcats.append(jnp.concatenate([_half_mask(q2, hh), jnp.where(in_head, qa, jnp.zeros_like(qa))], axis=1))

    def scores(j):
        ks = pl.multiple_of(j * tk, tk)
        kcat = jnp.concatenate([k_ref[0, pl.ds(ks, tk), :], ak_ref[0, pl.ds(ks, tk), :]], axis=1)
        return tuple(_dot_nt(kcat, qcats[hh]) for hh in range(2))

    def mask(j, st):
        kpos, qpos = _positions_t(j, tk, i * tq, tq)
        return jnp.where(kpos <= qpos, st, NEG)

    def values(hh, j):
        return vt_ref[0, hh * HEAD_DIM:(hh + 1) * HEAD_DIM, pl.ds(pl.multiple_of(j * tk, tk), tk)]

    res = _stream(0, i * (tq // (2 * tk)), tq // tk, scores, mask, values, st_ref, acc_ref)
    ot = jnp.concatenate([acc / l for l, acc in res], axis=0)
    o_ref[0] = ot.T.astype(BF16)


def _fox_attention(main, aq, ak, vt, tq, tk):
    b, t, _ = main.shape
    return pl.pallas_call(
        partial(_fox_kernel, tq=tq, tk=tk),
        grid=(b, FOX_HEADS // 2, t // tq),
        in_specs=[pl.BlockSpec((1, tq, LANES), lambda bi, p, i: (bi, i, CH_FQ + p)),
                  pl.BlockSpec((1, tq, LANES), lambda bi, p, i: (bi, i, 0)),
                  pl.BlockSpec((1, t, LANES), lambda bi, p, i: (bi, 0, CH_FK + p)),
                  pl.BlockSpec((1, t, LANES), lambda bi, p, i: (bi, 0, 0)),
                  pl.BlockSpec((1, LANES, t), lambda bi, p, i: (bi, VT_FV + p, 0))],
        out_specs=pl.BlockSpec((1, tq, LANES), lambda bi, p, i: (bi, i, p)),
        out_shape=jax.ShapeDtypeStruct((b, t, FOX_W), BF16),
        scratch_shapes=_stream_scratch(2, HEAD_DIM, tq, tk),
        compiler_params=_cparams(("parallel", "parallel", "parallel")),
        name="fox_attention",
    )(main, aq, main, ak, vt)


def _compress_kernel(xk_ref, xv_ref, w1k_ref, w2k_ref, pk_ref, w1v_ref, w2v_ref, pv_ref, kc_ref, vct_ref):
    half = CMP_STRIDE * HEAD_DIM
    for x_ref, w1_ref, w2_ref, p_ref, o_ref in ((xk_ref, w1k_ref, w2k_ref, pk_ref, kc_ref),
                                                 (xv_ref, w1v_ref, w2v_ref, pv_ref, vct_ref)):
        w1 = w1_ref[...]
        pos_h = _dot(p_ref[...], w1)
        out = None
        for g in range(NSA_KV_GROUPS):
            x = x_ref[0, g]
            first = _dot(x, w1[:half])
            second = _dot(x, w1[half:])
            nrow = first.shape[0]
            hid = first + pltpu.roll(second, nrow - 1, 0) + pos_h
            a = jax.nn.gelu(hid, approximate=True).astype(BF16)
            y = _dot(a, w2_ref[g])
            out = y if out is None else out + y
        o_ref[0] = (out.T if o_ref is vct_ref else out).astype(BF16)


def _compress(xk, xv, w1k, w2k, pk, w1v, w2v, pv):
    b, g, nchunk, width = xk.shape
    xspec = pl.BlockSpec((1, g, nchunk, width), lambda bi: (bi, 0, 0, 0))
    const = lambda shape: pl.BlockSpec(shape, lambda bi: (0,) * len(shape))
    wspecs = [const((width * 2, CMP_HIDDEN)), const((g, CMP_HIDDEN, LANES)), const((1, width * 2))]
    return pl.pallas_call(
        _compress_kernel,
        grid=(b,),
        in_specs=[xspec, xspec] + wspecs + wspecs,
        out_specs=[pl.BlockSpec((1, nchunk, LANES), lambda bi: (bi, 0, 0)),
                   pl.BlockSpec((1, LANES, nchunk), lambda bi: (bi, 0, 0))],
        out_shape=[jax.ShapeDtypeStruct((b, nchunk, LANES), BF16), jax.ShapeDtypeStruct((b, LANES, nchunk), BF16)],
        compiler_params=_cparams(("parallel",)),
        name="nsa_compress",
    )(xk, xv, w1k, w2k, pk, w1v, w2v, pv)


def _gate_col(small, head, branch):
    idx = FOX_HEADS + 3 * head + branch
    return jnp.sum(jnp.where(_lane_iota() == idx, small, 0.0), axis=-1, keepdims=True)


def _nsa_select_kernel(q_ref, kc_ref, vct_ref, ovt_ref, small_ref, oc_ref, sel_ref, *, tq, ns, n_sel):
    q0 = pl.program_id(1) * tq
    lane = _lane_iota()
    kc = kc_ref[0]
    vct = vct_ref[0]
    ncp = kc.shape[0]
    small = small_ref[0]
    qpos = q0 + lax.broadcasted_iota(jnp.int32, (1, tq), 1)
    cmp_end = lax.broadcasted_iota(jnp.int32, (ncp, 1), 0) * CMP_STRIDE + (CMP_BLOCK - 1)
    cmask = cmp_end <= qpos
    psum = [jnp.zeros((ncp, tq), F32) for _ in range(NSA_KV_GROUPS)]
    for n in range(NSA_HPG):
        q2 = q_ref[0, :, n * LANES:(n + 1) * LANES]
        ots = []
        for g in range(NSA_KV_GROUPS):
            z = jnp.where(cmask, _dot_nt(kc, _half_mask(q2, g)), -jnp.inf)
            m = jnp.max(z, axis=0, keepdims=True)
            m = jnp.where(m == -jnp.inf, 0.0, m)
            p = jnp.exp2(z - m)
            p = p / jnp.maximum(jnp.sum(p, axis=0, keepdims=True), 1e-30)
            psum[g] = psum[g] + p
            ots.append(_dot(vct[g * HEAD_DIM:(g + 1) * HEAD_DIM], p.astype(BF16)))
        gate = jnp.where(lane < HEAD_DIM, _gate_col(small, n, 0), _gate_col(small, NSA_HPG + n, 0))
        oc_ref[0, :, n * LANES:(n + 1) * LANES] = gate * jnp.concatenate(ots, axis=0).T

    nsp = ovt_ref.shape[0]
    blk = lax.broadcasted_iota(jnp.int32, (nsp, 1), 0)
    cur = qpos // SLC_BLOCK
    valid = blk * SLC_BLOCK <= qpos
    forced = (blk == 0) | (blk == cur) | (blk == cur - 1)
    masks = []
    for g in range(NSA_KV_GROUPS):
        imp = jnp.zeros((nsp, tq), F32)
        for piece in _split3(psum[g]):
            imp = imp + _dot(ovt_ref[...], piece)
        score = jnp.where(valid, jnp.where(forced, jnp.inf, imp), -jnp.inf)
        rank = jnp.zeros((nsp, tq), jnp.int32)
        for i in range(ns):
            row = score[i:i + 1, :]
            ahead = (row > score) | ((row == score) & (blk > i))
            rank = rank + ahead.astype(jnp.int32)
        sel = (rank < n_sel) & (blk < ns)
        masks.append(jnp.where(sel, 0.0, NEG))
    sel_ref[0] = jnp.concatenate(masks, axis=0).T.astype(BF16)


def _nsa_select(main, kc, vct, overlap_t, small, tq, ns, n_sel):
    b, t, _ = main.shape
    ncp = kc.shape[1]
    nsp = overlap_t.shape[0]
    return pl.pallas_call(
        partial(_nsa_select_kernel, tq=tq, ns=ns, n_sel=n_sel),
        grid=(b, t // tq),
        in_specs=[pl.BlockSpec((1, tq, NSA_W), lambda bi, i: (bi, i, CH_NQ * LANES // NSA_W)),
                  pl.BlockSpec((1, ncp, LANES), lambda bi, i: (bi, 0, 0)),
                  pl.BlockSpec((1, LANES, ncp), lambda bi, i: (bi, 0, 0)),
                  pl.BlockSpec((nsp, ncp), lambda bi, i: (0, 0)),
                  pl.BlockSpec((1, tq, LANES), lambda bi, i: (bi, i, 0))],
        out_specs=[pl.BlockSpec((1, tq, NSA_W), lambda bi, i: (bi, i, 0)),
                   pl.BlockSpec((1, tq, NSA_KV_GROUPS * nsp), lambda bi, i: (bi, i, 0))],
        out_shape=[jax.ShapeDtypeStruct((b, t, NSA_W), F32),
                   jax.ShapeDtypeStruct((b, t, NSA_KV_GROUPS * nsp), BF16)],
        compiler_params=_cparams(("parallel", "parallel")),
        name="nsa_select",
    )(main, kc, vct, overlap_t, small)


def _nsa_flash_kernel(q_ref, sel_ref, oc_ref, small_ref, ks_ref, kw_ref, hot_ref, vst_ref, vwt_ref, o_ref,
                      st_ref, acc_ref, *, tq, tk):
    n = pl.program_id(1)
    i = pl.program_id(2)
    q0 = i * tq
    q2 = q_ref[0]
    small = small_ref[0]
    qhs = [_half_mask(q2, g) for g in range(NSA_KV_GROUPS)]
    qcats = [jnp.concatenate([qhs[g], _half_mask(sel_ref[0], g)], axis=1) for g in range(NSA_KV_GROUPS)]
    rows = lambda g: slice(g * HEAD_DIM, (g + 1) * HEAD_DIM)

    def sel_scores(j):
        ks = pl.multiple_of(j * tk, tk)
        kcat = jnp.concatenate([ks_ref[0, pl.ds(ks, tk), :], hot_ref[pl.ds(ks, tk), :]], axis=1)
        return tuple(_dot_nt(kcat, qcats[g]) for g in range(NSA_KV_GROUPS))

    def sel_mask(j, st):
        kpos, qpos = _positions_t(j, tk, q0, tq)
        return jnp.where(kpos <= qpos, st, NEG)

    tile_start = lambda j: pl.multiple_of(jnp.maximum(j, 0) * tk, tk)

    def win_scores(j):
        kw = kw_ref[0, pl.ds(tile_start(j), tk), :]
        return tuple(_dot_nt(kw, qhs[g]) for g in range(NSA_KV_GROUPS))

    def win_mask(j, st):
        kpos, qpos = _positions_t(j, tk, q0, tq)
        return jnp.where((kpos <= qpos) & (kpos > qpos - WINDOW) & (kpos >= 0), st, NEG)

    values = lambda ref: lambda g, j: ref[0, rows(g), pl.ds(tile_start(j), tk)]
    lane = _lane_iota()
    n_win = (WINDOW + tq) // tk
    o = oc_ref[0]
    for branch, args in ((1, (0, i * (tq // (2 * tk)), tq // tk, sel_scores, sel_mask, values(vst_ref))),
                         (2, ((i + 1) * (tq // tk) - n_win, 0, n_win, win_scores, win_mask, values(vwt_ref)))):
        res = _stream(*args, st_ref, acc_ref)
        ot = jnp.concatenate([acc / l for l, acc in res], axis=0)
        gate = jnp.where(lane < HEAD_DIM, _gate_col(small, n, branch), _gate_col(small, NSA_HPG + n, branch))
        o = o + gate * ot.T
    o_ref[0] = o.astype(BF16)


def _nsa_flash(main, sel, oc, small, hot, vt, tq, tk):
    b, t, _ = main.shape
    tile = lambda ch: pl.BlockSpec((1, tq, LANES), lambda bi, n, i: (bi, i, ch + n))
    full = lambda ch: pl.BlockSpec((1, t, LANES), lambda bi, n, i: (bi, 0, ch))
    vfull = lambda ch: pl.BlockSpec((1, LANES, t), lambda bi, n, i: (bi, ch, 0))
    return pl.pallas_call(
        partial(_nsa_flash_kernel, tq=tq, tk=tk),
        grid=(b, NSA_HPG, t // tq),
        in_specs=[tile(CH_NQ),
                  pl.BlockSpec((1, tq, LANES), lambda bi, n, i: (bi, i, 0)),
                  tile(0),
                  pl.BlockSpec((1, tq, LANES), lambda bi, n, i: (bi, i, 0)),
                  full(CH_KS), full(CH_KW),
                  pl.BlockSpec((t, LANES), lambda bi, n, i: (0, 0)),
                  vfull(VT_VS), vfull(VT_VW)],
        out_specs=tile(0),
        out_shape=jax.ShapeDtypeStruct((b, t, NSA_W), BF16),
        scratch_shapes=_stream_scratch(NSA_KV_GROUPS, HEAD_DIM, tq, tk),
        compiler_params=_cparams(("parallel", "parallel", "parallel")),
        name="nsa_flash",
    )(main, sel, oc, small, main, main, hot, vt, vt)


def _diff_kernel(q_ref, k_ref, vt_ref, lam_ref, g_ref, o_ref, st_ref, acc_ref, *, tq, tk, lam_init):
    i = pl.program_id(2)
    q2 = q_ref[0]
    lp = lam_ref[...]
    lam = (jnp.exp(jnp.sum(lp[0:1] * lp[1:2], axis=-1, keepdims=True))
           - jnp.exp(jnp.sum(lp[2:3] * lp[3:4], axis=-1, keepdims=True)) + lam_init)
    qhs = [_half_mask(q2, comp) for comp in range(2)]

    def scores(j):
        k2 = k_ref[0, pl.ds(pl.multiple_of(j * tk, tk), tk), :]
        return tuple(_dot_nt(k2, qhs[comp]) for comp in range(2))

    def mask(j, st):
        kpos, qpos = _positions_t(j, tk, i * tq, tq)
        return jnp.where(kpos <= qpos, st, NEG)

    def values(comp, j):
        return vt_ref[0, :, pl.ds(pl.multiple_of(j * tk, tk), tk)]

    (l1, acc1), (l2, acc2) = _stream(0, i * (tq // (2 * tk)), tq // tk, scores, mask, values, st_ref, acc_ref)
    o = (acc1 / l1 - lam * (acc2 / l2)).T
    o_ref[0] = (_rms(o, g_ref[...]) * (1.0 - lam_init)).astype(BF16)


def _diff_attention(main, vt, lam_p, subln_g, tq, tk, lam_init):
    b, t, _ = main.shape
    nh = DIFF_HEADS
    return pl.pallas_call(
        partial(_diff_kernel, tq=tq, tk=tk, lam_init=lam_init),
        grid=(b, nh, t // tq),
        in_specs=[pl.BlockSpec((1, tq, LANES), lambda bi, h, i: (bi, i, h)),
                  pl.BlockSpec((1, t, LANES), lambda bi, h, i: (bi, 0, nh + h)),
                  pl.BlockSpec((1, LANES, t), lambda bi, h, i: (bi, h, 0)),
                  pl.BlockSpec((4, LANES), lambda bi, h, i: (0, 0)),
                  pl.BlockSpec((1, LANES), lambda bi, h, i: (0, 0))],
        out_specs=pl.BlockSpec((1, tq, LANES), lambda bi, h, i: (bi, i, h)),
        out_shape=jax.ShapeDtypeStruct((b, t, nh * LANES), BF16),
        scratch_shapes=_stream_scratch(2, LANES, tq, tk),
        compiler_params=_cparams(("parallel", "parallel", "parallel")),
        name="diff_attention",
    )(main, main, vt, lam_p, subln_g)


def _out_proj_kernel(*refs, n_in):
    a_refs, w_refs = refs[:n_in], refs[n_in:2 * n_in]
    x_ref, g_ref, o_ref = refs[2 * n_in:]
    y = None
    for a_ref, w_ref in zip(a_refs, w_refs):
        t = _dot(a_ref[...], w_ref[...])
        y = t if y is None else y + t
    o_ref[...] = x_ref[...] + _rms(y, g_ref[...])


def _out_proj(acts, weights, x2, g, tm):
    n = x2.shape[0]
    n_in = len(acts)
    const = lambda shape: pl.BlockSpec(shape, lambda i: (0, 0))
    return pl.pallas_call(
        partial(_out_proj_kernel, n_in=n_in),
        grid=(n // tm,),
        in_specs=[pl.BlockSpec((tm, a.shape[1]), lambda i: (i, 0)) for a in acts]
                 + [const(w.shape) for w in weights]
                 + [pl.BlockSpec((tm, D_MODEL), lambda i: (i, 0)), const((1, D_MODEL))],
        out_specs=pl.BlockSpec((tm, D_MODEL), lambda i: (i, 0)),
        out_shape=jax.ShapeDtypeStruct((n, D_MODEL), F32),
        compiler_params=_cparams(("parallel",)),
        name="out_proj",
    )(*acts, *weights, x2, g)


def _mem_kv_kernel(mem_ref, g_ref, wk_ref, wv_ref, k_ref, v_ref):
    mn = _rms(mem_ref[0], g_ref[0]).astype(BF16)
    k_ref[0, 0] = _dot(mn, wk_ref[0]).astype(BF16)
    v_ref[0, 0] = _dot(mn, wv_ref[0]).astype(BF16)


def _mem_kv(mem, mem_norm_g, wk, wv):
    depth = wk.shape[0]
    b, m, d = mem.shape
    wspec = pl.BlockSpec((1, d, MEM_W), lambda l, bi: (l, 0, 0))
    ospec = pl.BlockSpec((1, 1, m, MEM_W), lambda l, bi: (l, bi, 0, 0))
    return pl.pallas_call(
        _mem_kv_kernel,
        grid=(depth, b),
        in_specs=[pl.BlockSpec((1, m, d), lambda l, bi: (bi, 0, 0)),
                  pl.BlockSpec((1, 1, d), lambda l, bi: (l, 0, 0)), wspec, wspec],
        out_specs=[ospec, ospec],
        out_shape=[jax.ShapeDtypeStruct((depth, b, m, MEM_W), BF16)] * 2,
        compiler_params=_cparams(("parallel", "parallel")),
        name="mem_kv",
    )(mem, mem_norm_g.reshape(depth, 1, d), wk, wv)


def _cross_kernel(x_ref, gin_ref, gout_ref, wq_ref, k_ref, v_ref, wo_ref, o_ref):
    lane = _lane_iota()
    x = x_ref[...]
    h = _rms(x, gin_ref[...]).astype(BF16)
    q = (_dot(h, wq_ref[...]) * Q_SCALE).astype(BF16)
    chunks = []
    for ch in range(MEM_W // LANES):
        q2 = q[:, ch * LANES:(ch + 1) * LANES]
        k2 = k_ref[0, 0, :, ch * LANES:(ch + 1) * LANES]
        v2 = v_ref[0, 0, :, ch * LANES:(ch + 1) * LANES]
        outs = []
        for hh in range(2):
            s = _dot_nt(_half_mask(q2, hh), k2)
            p = jnp.exp(s - jnp.max(s, axis=-1, keepdims=True))
            p = p / jnp.sum(p, axis=-1, keepdims=True)
            outs.append(_dot(p.astype(BF16), v2))
        chunks.append(jnp.where(lane < HEAD_DIM, outs[0], outs[1]).astype(BF16))
    y = _dot(jnp.concatenate(chunks, axis=1), wo_ref[...])
    o_ref[...] = x + _rms(y, gout_ref[...])


def _cross_attention(x2, gin, gout, wq, mem_k, mem_v, wo, layer, tm, rows_per_batch):
    n = x2.shape[0]
    m = mem_k.shape[2]
    const = lambda shape: pl.BlockSpec(shape, lambda i: (0, 0))
    kv = pl.BlockSpec((1, 1, m, MEM_W), lambda i: (layer, i // rows_per_batch, 0, 0))
    return pl.pallas_call(
        _cross_kernel,
        grid=(n // tm,),
        in_specs=[pl.BlockSpec((tm, D_MODEL), lambda i: (i, 0)), const((1, D_MODEL)), const((1, D_MODEL)),
                  const((D_MODEL, MEM_W)), kv, kv, const((MEM_W, D_MODEL))],
        out_specs=pl.BlockSpec((tm, D_MODEL), lambda i: (i, 0)),
        out_shape=jax.ShapeDtypeStruct((n, D_MODEL), F32),
        compiler_params=_cparams(("parallel",)),
        name="cross_attention",
    )(x2, gin, gout, wq, mem_k, mem_v, wo)


def _ffn_kernel(x_ref, gin_ref, gout_ref, wg_ref, wu_ref, wd_ref, o_ref, *, chunk):
    x = x_ref[...]
    h = _rms(x, gin_ref[...]).astype(BF16)
    d_ff = wg_ref.shape[1]
    y = None
    for c0 in range(0, d_ff, chunk):
        gate = _dot(h, wg_ref[:, c0:c0 + chunk])
        up = _dot(h, wu_ref[:, c0:c0 + chunk])
        a = (gate * (1.0 / (1.0 + jnp.exp(-gate))) * up).astype(BF16)
        t = _dot(a, wd_ref[c0:c0 + chunk, :])
        y = t if y is None else y + t
    o_ref[...] = x + _rms(y, gout_ref[...])


def _ffn(x2, gin, gout, wg, wu, wd, tm, chunk):
    n = x2.shape[0]
    d_ff = wg.shape[1]
    const = lambda shape: pl.BlockSpec(shape, lambda i: (0, 0), pipeline_mode=pl.Buffered(1))
    return pl.pallas_call(
        partial(_ffn_kernel, chunk=chunk),
        grid=(n // tm,),
        in_specs=[pl.BlockSpec((tm, D_MODEL), lambda i: (i, 0)), const((1, D_MODEL)), const((1, D_MODEL)),
                  const((D_MODEL, d_ff)), const((D_MODEL, d_ff)), const((d_ff, D_MODEL))],
        out_specs=pl.BlockSpec((tm, D_MODEL), lambda i: (i, 0)),
        out_shape=jax.ShapeDtypeStruct((n, D_MODEL), F32),
        compiler_params=_cparams(("parallel",)),
        name="ffn",
    )(x2, gin, gout, wg, wu, wd)


def _even_weights(w_in, w_out):
    kvw = NSA_KV_GROUPS * HEAD_DIM
    offs = [int(o) for o in np.cumsum((FOX_W, FOX_W, FOX_W, FOX_HEADS, NSA_W) + (kvw,) * 6 + (3 * NSA_HEADS,))]
    fk0, fv0, fl0, nq0, kc0, vc0, ks0, vs0, kw0, vw0, gl0, end = offs
    perm = np.concatenate([np.arange(HEAD_DIM) + HEAD_DIM * (g * NSA_HPG + n)
                           for n in range(NSA_HPG) for g in range(NSA_KV_GROUPS)])
    cols = lambda a, b: w_in[:, a:b]
    w = jnp.concatenate([cols(0, fv0), w_in[:, nq0 + perm], cols(kc0, vc0), cols(vc0, ks0), cols(ks0, vs0),
                         cols(kw0, vw0), cols(fv0, fl0), cols(vs0, kw0), cols(vw0, gl0)], axis=1).astype(BF16)
    w_small = jnp.concatenate([cols(fl0, nq0), cols(gl0, end),
                               jnp.zeros((D_MODEL, LANES - FOX_HEADS - 3 * NSA_HEADS), w_in.dtype)],
                              axis=1).astype(BF16)
    w_out_fox = w_out[:FOX_W].astype(BF16)
    w_out_nsa = w_out[FOX_W + perm].astype(BF16)
    return w, w_small, w_out_fox, w_out_nsa


def _overlap_matrix_t(t, ncp):
    nc = (t - CMP_BLOCK) // CMP_STRIDE + 1
    ns = t // SLC_BLOCK
    cs = np.arange(nc) * CMP_STRIDE
    ss = np.arange(ns) * SLC_BLOCK
    ov = np.clip(np.minimum(cs[:, None] + CMP_BLOCK, ss[None, :] + SLC_BLOCK)
                 - np.maximum(cs[:, None], ss[None, :]), 0, None) / CMP_BLOCK
    full = np.zeros((LANES // NSA_KV_GROUPS, ncp), np.float32)
    full[:ns, :nc] = ov.T
    return jnp.asarray(full, BF16), ns


def _stride_chunks(main, ch):
    b, t, _ = main.shape
    x = main[:, :, ch * LANES:(ch + 1) * LANES].reshape(b, t // CMP_STRIDE, CMP_STRIDE, NSA_KV_GROUPS, HEAD_DIM)
    return jnp.transpose(x, (0, 3, 1, 2, 4)).reshape(b, NSA_KV_GROUPS, t // CMP_STRIDE, CMP_STRIDE * HEAD_DIM)


def _pad_w2(w2):
    out = jnp.zeros((NSA_KV_GROUPS, CMP_HIDDEN, LANES), BF16)
    for g in range(NSA_KV_GROUPS):
        out = out.at[g, :, g * HEAD_DIM:(g + 1) * HEAD_DIM].set(w2.astype(BF16))
    return out


def kernel(x, mem, positions, sandwich_g, mem_norm_g, ev_w_in, ev_fox_fbias, ev_cmp_pos_k, ev_cmp_w1_k, ev_cmp_w2_k, ev_cmp_pos_v, ev_cmp_w1_v, ev_cmp_w2_v, ev_w_out, od_w_in, od_lambda, od_subln_g, od_w_out, ca_wq, ca_wk, ca_wv, ca_wo, ffn_wg, ffn_wu, ffn_wd):
    b, t, d = x.shape
    depth = sandwich_g.shape[0]
    n = b * t
    tm = 256
    tq, tk = 512, 256
    assert d == D_MODEL and t % tq == 0 and tq % (2 * tk) == 0 and WINDOW % tk == 0 and t % tm == 0

    tabs = tuple(a.reshape(b, t, LANES) for a in _rope_tables(positions, 512))
    mem_k, mem_v = _mem_kv(mem, mem_norm_g, ca_wk.astype(BF16), ca_wv.astype(BF16))
    ncp = t // CMP_STRIDE
    overlap_t, ns = _overlap_matrix_t(t, ncp)
    assert ns <= overlap_t.shape[0]
    n_sel = min(SLC_TOPK, ns)
    hot = jnp.asarray((np.arange(t)[:, None] // SLC_BLOCK == np.arange(LANES)[None, :] % overlap_t.shape[0])
                      .astype(np.float32), BF16)
    gain = lambda l, j: sandwich_g[l, j].reshape(1, d)

    x2 = x.reshape(n, d)
    for layer in range(depth):
        x3 = x2.reshape(b, t, d)
        if layer % 2 == 0:
            e = layer // 2
            w, w_small, w_out_fox, w_out_nsa = _even_weights(ev_w_in[e], ev_w_out[e])
            fb_row = jnp.zeros((1, LANES), F32).at[0, :FOX_HEADS].set(ev_fox_fbias[e].astype(F32))
            main, vt, small = _even_proj(x3, gain(layer, 0), w, w_small, fb_row, tabs, tm)
            aq, ak = _fox_aug(small, 256)
            o_fox = _fox_attention(main, aq, ak, vt, tq, tk)
            kc, vct = _compress(
                _stride_chunks(main, CH_KC), _stride_chunks(main, CH_VC),
                ev_cmp_w1_k[e].astype(BF16), _pad_w2(ev_cmp_w2_k[e]), ev_cmp_pos_k[e].reshape(1, -1).astype(BF16),
                ev_cmp_w1_v[e].astype(BF16), _pad_w2(ev_cmp_w2_v[e]), ev_cmp_pos_v[e].reshape(1, -1).astype(BF16))
            oc, sel = _nsa_select(main, kc, vct, overlap_t, small, 128, ns, n_sel)
            o_nsa = _nsa_flash(main, sel, oc, small, hot, vt, tq, tk)
            x2 = _out_proj([o_fox.reshape(n, FOX_W), o_nsa.reshape(n, NSA_W)], [w_out_fox, w_out_nsa],
                           x2, gain(layer, 1), tm)
        else:
            o = layer // 2
            w_in = od_w_in[o].astype(BF16)
            main, vt = _odd_proj(x3, gain(layer, 0), w_in, tabs, tm)
            lam_init = 0.8 - 0.6 * math.exp(-0.3 * layer)
            lam_p = jnp.pad(od_lambda[o].astype(F32), ((0, 0), (0, LANES - HEAD_DIM)))
            attn = _diff_attention(main, vt, lam_p, od_subln_g[o].reshape(1, LANES).astype(F32), tq, tk, lam_init)
            x2 = _out_proj([attn.reshape(n, D_MODEL)], [od_w_out[o].astype(BF16)], x2, gain(layer, 1), tm)
        x2 = _cross_attention(x2, gain(layer, 2), gain(layer, 3), ca_wq[layer].astype(BF16), mem_k, mem_v,
                              ca_wo[layer].astype(BF16), layer, tm, t // tm)
        x2 = _ffn(x2, gain(layer, 4), gain(layer, 5), ffn_wg[layer].astype(BF16), ffn_wu[layer].astype(BF16),
                  ffn_wd[layer].astype(BF16), tm, 256)
    return x2.reshape(b, t, d)
```

```python
import math
from functools import partial

import numpy as np
import jax
import jax.numpy as jnp
from jax import lax
from jax.experimental import pallas as pl
from jax.experimental.pallas import tpu as pltpu

F32 = jnp.float32
BF16 = jnp.bfloat16

D_MODEL = 1024
HEAD_DIM = 64
LANES = 128
ROPE_DIM = HEAD_DIM // 4
ROPE_THETA = 500000.0
FOX_HEADS = 8
NSA_HEADS = 8
NSA_KV_GROUPS = 2
NSA_HPG = NSA_HEADS // NSA_KV_GROUPS
CMP_BLOCK = 32
CMP_STRIDE = 16
CMP_HIDDEN = 2 * HEAD_DIM
SLC_BLOCK = 64
SLC_TOPK = 16
WINDOW = 512
DIFF_HEADS = 8
MEM_HEADS = 4
MEM_W = MEM_HEADS * HEAD_DIM
RMS_EPS = 1e-6
Q_SCALE = HEAD_DIM ** -0.5
LOG2E = math.log2(math.e)
Q_SCALE_LOG2 = Q_SCALE * LOG2E
NEG = -1e30
AUG_PER_HEAD = 6
ONES_ROWS = 16

FOX_W = FOX_HEADS * HEAD_DIM
NSA_W = NSA_HEADS * HEAD_DIM
CH_FQ, CH_FK, CH_NQ, CH_KC, CH_VC, CH_KS, CH_KW = 0, 4, 8, 12, 13, 14, 15
EV_MAIN = 16 * LANES
EV_ROPE_CHUNKS = tuple(range(CH_NQ, CH_NQ + 4)) + (CH_KC, CH_KS, CH_KW)
EV_QSCALE_CHUNKS = tuple(range(CH_FQ, CH_FQ + 4)) + tuple(range(CH_NQ, CH_NQ + 4))
VT_FV, VT_VS, VT_VW = 0, 4, 5
EV_VT = 6 * LANES
OD_MAIN = 2 * D_MODEL
OD_VT = D_MODEL

VMEM_LIMIT = 56 * 1024 * 1024


def _cparams(sem):
    return pltpu.CompilerParams(dimension_semantics=sem, vmem_limit_bytes=VMEM_LIMIT)


def _rms(x, g):
    return x * lax.rsqrt(jnp.mean(x * x, axis=-1, keepdims=True) + RMS_EPS) * g


def _split3(x):
    hi = x.astype(BF16)
    r1 = x - hi.astype(F32)
    mid = r1.astype(BF16)
    lo = (r1 - mid.astype(F32)).astype(BF16)
    return hi, mid, lo


def _dot(a, b):
    return jnp.dot(a, b, preferred_element_type=F32)


def _dot_nt(a, b):
    return lax.dot_general(a, b, (((1,), (1,)), ((), ())), preferred_element_type=F32)


def _lane_iota(n=LANES):
    return lax.broadcasted_iota(jnp.int32, (1, n), 1)


def _half_mask(q2, half):
    return jnp.where(_lane_iota() // HEAD_DIM == half, q2, jnp.zeros_like(q2))


def _positions_t(j, tk, q0, tq):
    kpos = j * tk + lax.broadcasted_iota(jnp.int32, (tk, 1), 0)
    qpos = q0 + lax.broadcasted_iota(jnp.int32, (1, tq), 1)
    return kpos, qpos


def _stream(base, n_pairs, n_tail, scores, mask, values, st_ref, acc_ref):
    n_chain, dv, tq = acc_ref.shape

    def park(j, slot):
        for c, st in enumerate(scores(j)):
            st_ref[slot, c] = st

    def step(j, slot, stats, masked, prefetch):
        if prefetch:
            park(j + 1, 1 - slot)
        parts = []
        for c in range(n_chain):
            st = st_ref[slot, c]
            if masked:
                st = mask(j, st)
            m, _ = stats[c]
            m_new = jnp.maximum(m, jnp.max(st, axis=0, keepdims=True))
            p = jnp.exp2(st - m_new).astype(BF16)
            vt1 = jnp.concatenate([values(c, j), jnp.ones((ONES_ROWS, p.shape[0]), BF16)], axis=0)
            parts.append((m_new, jnp.exp2(m - m_new), _dot(vt1, p)))
        out = []
        for c, (m_new, alpha, pv) in enumerate(parts):
            acc_ref[c] = alpha * acc_ref[c] + pv[:dv]
            out.append((m_new, alpha * stats[c][1] + pv[dv:dv + 1]))
        return tuple(out)

    def run(j, n_steps, stats):
        for s_ in range(n_steps):
            stats = step(j + s_, s_ % 2, stats, False, True)
        return stats

    acc_ref[...] = jnp.zeros_like(acc_ref)
    park(base, 0)
    stats = ((jnp.full((1, tq), NEG, F32), jnp.zeros((1, tq), F32)),) * n_chain
    n_quads = n_pairs // 2
    stats = lax.fori_loop(0, n_quads, lambda u, st: run(base + 4 * u, 4, st), stats)
    if not isinstance(n_pairs, int) or n_pairs % 2:
        stats = lax.cond(n_pairs % 2 == 1, lambda st: run(base + 4 * n_quads, 2, st), lambda st: st, stats)
    for jj in range(n_tail):
        stats = step(base + 2 * n_pairs + jj, jj % 2, stats, True, jj + 1 < n_tail)
    return [(l, acc_ref[c]) for c, (_, l) in enumerate(stats)]


def _stream_scratch(n_chain, dv, tq, tk):
    return [pltpu.VMEM((2, n_chain, tk, tq), F32), pltpu.VMEM((n_chain, dv, tq), F32)]


def _rope_kernel(pos_ref, inv_ref, m1_ref, m2_ref, c_ref, s1_ref, s2_ref):
    ang = pos_ref[...].astype(F32) * inv_ref[...]
    c_ref[...] = jnp.cos(ang)
    sn = jnp.sin(ang)
    s1_ref[...] = -sn * m1_ref[...]
    s2_ref[...] = sn * m2_ref[...]


def _rope_tables(positions, tm):
    n = positions.size
    inv = ROPE_THETA ** (-jnp.arange(0, ROPE_DIM, 2, dtype=F32) / ROPE_DIM)
    lane = np.arange(LANES) % HEAD_DIM
    half = ROPE_DIM // 2
    inv_l = jnp.where(lane < ROPE_DIM, inv[lane % half], 0.0).reshape(1, LANES).astype(F32)
    m1 = jnp.asarray((lane < half).astype(np.float32)).reshape(1, LANES)
    m2 = jnp.asarray(((lane >= half) & (lane < ROPE_DIM)).astype(np.float32)).reshape(1, LANES)
    row = pl.BlockSpec((1, LANES), lambda i: (0, 0))
    tab = pl.BlockSpec((tm, LANES), lambda i: (i, 0))
    return pl.pallas_call(
        _rope_kernel,
        grid=(n // tm,),
        in_specs=[pl.BlockSpec((tm, 1), lambda i: (i, 0)), row, row, row],
        out_specs=[tab, tab, tab],
        out_shape=[jax.ShapeDtypeStruct((n, LANES), F32)] * 3,
        compiler_params=_cparams(("parallel",)),
        name="rope_tables",
    )(positions.reshape(n, 1), inv_l, m1, m2)


def _apply_rope(y, c, s1, s2):
    half = ROPE_DIM // 2
    return y * c + pltpu.roll(y, LANES - half, 1) * s1 + pltpu.roll(y, half, 1) * s2


def _project_chunks(h, w_ref, tabs, main_ref, vt_ref, n_main, n_vt, rope_chunks, qscale_chunks):
    c, s1, s2 = tabs
    wide = 2 * LANES
    for ch2 in range((n_main + n_vt) // 2):
        y2 = _dot(h, w_ref[:, ch2 * wide:(ch2 + 1) * wide])
        for ch in (2 * ch2, 2 * ch2 + 1):
            y = y2[:, (ch % 2) * LANES:(ch % 2 + 1) * LANES]
            if ch >= n_main:
                vt_ref[0, (ch - n_main) * LANES:(ch - n_main + 1) * LANES, :] = y.T.astype(BF16)
                continue
            if ch in rope_chunks:
                y = _apply_rope(y, c, s1, s2)
            if ch in qscale_chunks:
                y = y * Q_SCALE_LOG2
            main_ref[0, :, ch * LANES:(ch + 1) * LANES] = y.astype(BF16)


def _even_proj_kernel(x_ref, g_ref, w_ref, ws_ref, fb_ref, c_ref, s1_ref, s2_ref, main_ref, vt_ref, small_ref):
    h = _rms(x_ref[0], g_ref[...]).astype(BF16)
    _project_chunks(h, w_ref, (c_ref[0], s1_ref[0], s2_ref[0]), main_ref, vt_ref,
                    EV_MAIN // LANES, EV_VT // LANES, EV_ROPE_CHUNKS, EV_QSCALE_CHUNKS)
    ys = _dot(h, ws_ref[...])
    z = ys + fb_ref[...]
    log_f = jnp.minimum(z, 0.0) - jnp.log(1.0 + jnp.exp(-jnp.abs(z)))
    gate = 1.0 / (1.0 + jnp.exp(-ys))
    small_ref[0] = jnp.where(_lane_iota() < FOX_HEADS, log_f, gate)


def _even_proj(x3, g, w, w_small, fb_row, tabs, tm):
    b, t, _ = x3.shape
    const = lambda shape: pl.BlockSpec(shape, lambda bi, i: (0, 0))
    tab = pl.BlockSpec((1, tm, LANES), lambda bi, i: (bi, i, 0))
    return pl.pallas_call(
        _even_proj_kernel,
        grid=(b, t // tm),
        in_specs=[pl.BlockSpec((1, tm, D_MODEL), lambda bi, i: (bi, i, 0)), const((1, D_MODEL)),
                  const((D_MODEL, EV_MAIN + EV_VT)), const((D_MODEL, LANES)), const((1, LANES)), tab, tab, tab],
        out_specs=[pl.BlockSpec((1, tm, EV_MAIN), lambda bi, i: (bi, i, 0)),
                   pl.BlockSpec((1, EV_VT, tm), lambda bi, i: (bi, 0, i)), tab],
        out_shape=[jax.ShapeDtypeStruct((b, t, EV_MAIN), BF16), jax.ShapeDtypeStruct((b, EV_VT, t), BF16),
                   jax.ShapeDtypeStruct((b, t, LANES), F32)],
        compiler_params=_cparams(("parallel", "parallel")),
        name="even_proj",
    )(x3, g, w, w_small, fb_row, *tabs)


def _odd_proj_kernel(x_ref, g_ref, w_ref, c_ref, s1_ref, s2_ref, main_ref, vt_ref):
    h = _rms(x_ref[0], g_ref[...]).astype(BF16)
    n_main = OD_MAIN // LANES
    _project_chunks(h, w_ref, (c_ref[0], s1_ref[0], s2_ref[0]), main_ref, vt_ref,
                    n_main, OD_VT // LANES, tuple(range(n_main)), tuple(range(n_main // 2)))


def _odd_proj(x3, g, w, tabs, tm):
    b, t, _ = x3.shape
    const = lambda shape: pl.BlockSpec(shape, lambda bi, i: (0, 0))
    tab = pl.BlockSpec((1, tm, LANES), lambda bi, i: (bi, i, 0))
    return pl.pallas_call(
        _odd_proj_kernel,
        grid=(b, t // tm),
        in_specs=[pl.BlockSpec((1, tm, D_MODEL), lambda bi, i: (bi, i, 0)), const((1, D_MODEL)),
                  const((D_MODEL, OD_MAIN + OD_VT)), tab, tab, tab],
        out_specs=[pl.BlockSpec((1, tm, OD_MAIN), lambda bi, i: (bi, i, 0)),
                   pl.BlockSpec((1, OD_VT, tm), lambda bi, i: (bi, 0, i))],
        out_shape=[jax.ShapeDtypeStruct((b, t, OD_MAIN), BF16), jax.ShapeDtypeStruct((b, OD_VT, t), BF16)],
        compiler_params=_cparams(("parallel", "parallel")),
        name="odd_proj",
    )(x3, g, w, *tabs)


def _fox_aug_kernel(lf_ref, tril_ref, eq_ref, ek_ref, oneq_ref, onek_ref, aq_ref, ak_ref, carry_ref):
    @pl.when(pl.program_id(1) == 0)
    def _():
        carry_ref[...] = jnp.zeros_like(carry_ref)

    tril = tril_ref[...]
    c = carry_ref[...]
    for piece in _split3(lf_ref[0]):
        c = c + _dot(tril, piece)
    carry_ref[...] = c[-1:, :]
    aq = oneq_ref[...]
    ak = onek_ref[...]
    for r, piece in enumerate(_split3(c * LOG2E)):
        aq = aq + _dot(piece, eq_ref[r])
        ak = ak - _dot(piece, ek_ref[r])
    aq_ref[0] = aq.astype(BF16)
    ak_ref[0] = ak.astype(BF16)


def _fox_aug(small, tc):
    b, t, _ = small.shape
    tril = jnp.asarray(np.tril(np.ones((tc, tc), np.float32)), BF16)
    eq = np.zeros((3, LANES, LANES), np.float32)
    ek = np.zeros((3, LANES, LANES), np.float32)
    oneq = np.zeros((1, LANES), np.float32)
    onek = np.zeros((1, LANES), np.float32)
    for h in range(FOX_HEADS):
        for r in range(3):
            eq[r, h, AUG_PER_HEAD * h + r] = 1.0
            ek[r, h, AUG_PER_HEAD * h + 3 + r] = 1.0
            oneq[0, AUG_PER_HEAD * h + 3 + r] = 1.0
            onek[0, AUG_PER_HEAD * h + r] = 1.0
    const2 = lambda shape: pl.BlockSpec(shape, lambda bi, i: (0,) * len(shape))
    blk = pl.BlockSpec((1, tc, LANES), lambda bi, i: (bi, i, 0))
    return pl.pallas_call(
        _fox_aug_kernel,
        grid=(b, t // tc),
        in_specs=[blk, const2((tc, tc)), const2((3, LANES, LANES)), const2((3, LANES, LANES)),
                  const2((1, LANES)), const2((1, LANES))],
        out_specs=[blk, blk],
        out_shape=[jax.ShapeDtypeStruct((b, t, LANES), BF16)] * 2,
        scratch_shapes=[pltpu.VMEM((1, LANES), F32)],
        compiler_params=_cparams(("parallel", "arbitrary")),
        name="fox_aug",
    )(small, tril, jnp.asarray(eq, BF16), jnp.asarray(ek, BF16), jnp.asarray(oneq), jnp.asarray(onek))


def _fox_kernel(q_ref, aq_ref, k_ref, ak_ref, vt_ref, o_ref, st_ref, acc_ref, *, tq, tk):
    pair = pl.program_id(1)
    i = pl.program_id(2)
    lane = _lane_iota()
    q2 = q_ref[0]
    qa = aq_ref[0]
    qcats = []
    for hh in range(2):
        head = 2 * pair + hh
        in_head = (lane >= AUG_PER_HEAD * head) & (lane < AUG_PER_HEAD * (head + 1))
        qcats.append(jnp.concatenate([_half_mask(q2, hh), jnp.where(in_head, qa, jnp.zeros_like(qa))], axis=1))

    def scores(j):
        ks = pl.multiple_of(j * tk, tk)
        kcat = jnp.concatenate([k_ref[0, pl.ds(ks, tk), :], ak_ref[0, pl.ds(ks, tk), :]], axis=1)
        return tuple(_dot_nt(kcat, qcats[hh]) for hh in range(2))

    def mask(j, st):
        kpos, qpos = _positions_t(j, tk, i * tq, tq)
        return jnp.where(kpos <= qpos, st, NEG)

    def values(hh, j):
        return vt_ref[0, hh * HEAD_DIM:(hh + 1) * HEAD_DIM, pl.ds(pl.multiple_of(j * tk, tk), tk)]

    res = _stream(0, i * (tq // (2 * tk)), tq // tk, scores, mask, values, st_ref, acc_ref)
    ot = jnp.concatenate([acc / l for l, acc in res], axis=0)
    o_ref[0] = ot.T.astype(BF16)


def _fox_attention(main, aq, ak, vt, tq, tk):
    b, t, _ = main.shape
    return pl.pallas_call(
        partial(_fox_kernel, tq=tq, tk=tk),
        grid=(b, FOX_HEADS // 2, t // tq),
        in_specs=[pl.BlockSpec((1, tq, LANES), lambda bi, p, i: (bi, i, CH_FQ + p)),
                  pl.BlockSpec((1, tq, LANES), lambda bi, p, i: (bi, i, 0)),
                  pl.BlockSpec((1, t, LANES), lambda bi, p, i: (bi, 0, CH_FK + p)),
                  pl.BlockSpec((1, t, LANES), lambda bi, p, i: (bi, 0, 0)),
                  pl.BlockSpec((1, LANES, t), lambda bi, p, i: (bi, VT_FV + p, 0))],
        out_specs=pl.BlockSpec((1, tq, LANES), lambda bi, p, i: (bi, i, p)),
        out_shape=jax.ShapeDtypeStruct((b, t, FOX_W), BF16),
        scratch_shapes=_stream_scratch(2, HEAD_DIM, tq, tk),
        compiler_params=_cparams(("parallel", "parallel", "parallel")),
        name="fox_attention",
    )(main, aq, main, ak, vt)


def _compress_kernel(xk_ref, xv_ref, w1k_ref, w2k_ref, pk_ref, w1v_ref, w2v_ref, pv_ref, kc_ref, vct_ref):
    half = CMP_STRIDE * HEAD_DIM
    for x_ref, w1_ref, w2_ref, p_ref, o_ref in ((xk_ref, w1k_ref, w2k_ref, pk_ref, kc_ref),
                                                 (xv_ref, w1v_ref, w2v_ref, pv_ref, vct_ref)):
        w1 = w1_ref[...]
        pos_h = _dot(p_ref[...], w1)
        out = None
        for g in range(NSA_KV_GROUPS):
            x = x_ref[0, g]
            first = _dot(x, w1[:half])
            second = _dot(x, w1[half:])
            nrow = first.shape[0]
            hid = first + pltpu.roll(second, nrow - 1, 0) + pos_h
            a = jax.nn.gelu(hid, approximate=True).astype(BF16)
            y = _dot(a, w2_ref[g])
            out = y if out is None else out + y
        o_ref[0] = (out.T if o_ref is vct_ref else out).astype(BF16)


def _compress(xk, xv, w1k, w2k, pk, w1v, w2v, pv):
    b, g, nchunk, width = xk.shape
    xspec = pl.BlockSpec((1, g, nchunk, width), lambda bi: (bi, 0, 0, 0))
    const = lambda shape: pl.BlockSpec(shape, lambda bi: (0,) * len(shape))
    wspecs = [const((width * 2, CMP_HIDDEN)), const((g, CMP_HIDDEN, LANES)), const((1, width * 2))]
    return pl.pallas_call(
        _compress_kernel,
        grid=(b,),
        in_specs=[xspec, xspec] + wspecs + wspecs,
        out_specs=[pl.BlockSpec((1, nchunk, LANES), lambda bi: (bi, 0, 0)),
                   pl.BlockSpec((1, LANES, nchunk), lambda bi: (bi, 0, 0))],
        out_shape=[jax.ShapeDtypeStruct((b, nchunk, LANES), BF16), jax.ShapeDtypeStruct((b, LANES, nchunk), BF16)],
        compiler_params=_cparams(("parallel",)),
        name="nsa_compress",
    )(xk, xv, w1k, w2k, pk, w1v, w2v, pv)


def _gate_col(small, head, branch):
    idx = FOX_HEADS + 3 * head + branch
    return jnp.sum(jnp.where(_lane_iota() == idx, small, 0.0), axis=-1, keepdims=True)


def _nsa_select_kernel(q_ref, kc_ref, vct_ref, ovt_ref, small_ref, oc_ref, sel_ref, *, tq, ns, n_sel):
    q0 = pl.program_id(1) * tq
    lane = _lane_iota()
    kc = kc_ref[0]
    vct = vct_ref[0]
    ncp = kc.shape[0]
    small = small_ref[0]
    qpos = q0 + lax.broadcasted_iota(jnp.int32, (1, tq), 1)
    cmp_end = lax.broadcasted_iota(jnp.int32, (ncp, 1), 0) * CMP_STRIDE + (CMP_BLOCK - 1)
    cmask = cmp_end <= qpos
    psum = [jnp.zeros((ncp, tq), F32) for _ in range(NSA_KV_GROUPS)]
    logits = [[_dot_nt(kc, _half_mask(q_ref[0, :, n * LANES:(n + 1) * LANES], g)) for g in range(NSA_KV_GROUPS)]
              for n in range(NSA_HPG)]
    for n in range(NSA_HPG):
        ots = []
        for g in range(NSA_KV_GROUPS):
            z = jnp.where(cmask, logits[n][g], -jnp.inf)
            m = jnp.max(z, axis=0, keepdims=True)
            m = jnp.where(m == -jnp.inf, 0.0, m)
            p = jnp.exp2(z - m)
            p = p / jnp.maximum(jnp.sum(p, axis=0, keepdims=True), 1e-30)
            psum[g] = psum[g] + p
            ots.append(_dot(vct[g * HEAD_DIM:(g + 1) * HEAD_DIM], p.astype(BF16)))
        gate = jnp.where(lane < HEAD_DIM, _gate_col(small, n, 0), _gate_col(small, NSA_HPG + n, 0))
        oc_ref[0, :, n * LANES:(n + 1) * LANES] = gate * jnp.concatenate(ots, axis=0).T

    nsp = ovt_ref.shape[0]
    blk = lax.broadcasted_iota(jnp.int32, (nsp, 1), 0)
    cur = qpos // SLC_BLOCK
    valid = blk * SLC_BLOCK <= qpos
    forced = (blk == 0) | (blk == cur) | (blk == cur - 1)
    scores = []
    for g in range(NSA_KV_GROUPS):
        imp = jnp.zeros((nsp, tq), F32)
        for piece in _split3(psum[g]):
            imp = imp + _dot(ovt_ref[...], piece)
        scores.append(jnp.where(valid, jnp.where(forced, jnp.inf, imp), -jnp.inf))
    ranks = [jnp.zeros((nsp, tq), jnp.int32) for _ in range(NSA_KV_GROUPS)]
    for i in range(ns):
        for g in range(NSA_KV_GROUPS):
            row = scores[g][i:i + 1, :]
            ahead = (row > scores[g]) | ((row == scores[g]) & (blk > i))
            ranks[g] = ranks[g] + ahead.astype(jnp.int32)
    masks = [jnp.where((ranks[g] < n_sel) & (blk < ns), 0.0, NEG) for g in range(NSA_KV_GROUPS)]
    sel_ref[0] = jnp.concatenate(masks, axis=0).T.astype(BF16)


def _nsa_select(main, kc, vct, overlap_t, small, tq, ns, n_sel):
    b, t, _ = main.shape
    ncp = kc.shape[1]
    nsp = overlap_t.shape[0]
    return pl.pallas_call(
        partial(_nsa_select_kernel, tq=tq, ns=ns, n_sel=n_sel),
        grid=(b, t // tq),
        in_specs=[pl.BlockSpec((1, tq, NSA_W), lambda bi, i: (bi, i, CH_NQ * LANES // NSA_W)),
                  pl.BlockSpec((1, ncp, LANES), lambda bi, i: (bi, 0, 0)),
                  pl.BlockSpec((1, LANES, ncp), lambda bi, i: (bi, 0, 0)),
                  pl.BlockSpec((nsp, ncp), lambda bi, i: (0, 0)),
                  pl.BlockSpec((1, tq, LANES), lambda bi, i: (bi, i, 0))],
        out_specs=[pl.BlockSpec((1, tq, NSA_W), lambda bi, i: (bi, i, 0)),
                   pl.BlockSpec((1, tq, NSA_KV_GROUPS * nsp), lambda bi, i: (bi, i, 0))],
        out_shape=[jax.ShapeDtypeStruct((b, t, NSA_W), F32),
                   jax.ShapeDtypeStruct((b, t, NSA_KV_GROUPS * nsp), BF16)],
        compiler_params=_cparams(("parallel", "parallel")),
        name="nsa_select",
    )(main, kc, vct, overlap_t, small)


def _nsa_flash_kernel(q_ref, sel_ref, oc_ref, small_ref, ks_ref, kw_ref, hot_ref, vst_ref, vwt_ref, o_ref,
                      st_ref, acc_ref, *, tq, tk):
    n = pl.program_id(1)
    i = pl.program_id(2)
    q0 = i * tq
    q2 = q_ref[0]
    small = small_ref[0]
    qhs = [_half_mask(q2, g) for g in range(NSA_KV_GROUPS)]
    qcats = [jnp.concatenate([qhs[g], _half_mask(sel_ref[0], g)], axis=1) for g in range(NSA_KV_GROUPS)]
    rows = lambda g: slice(g * HEAD_DIM, (g + 1) * HEAD_DIM)

    def sel_scores(j):
        ks = pl.multiple_of(j * tk, tk)
        kcat = jnp.concatenate([ks_ref[0, pl.ds(ks, tk), :], hot_ref[pl.ds(ks, tk), :]], axis=1)
        return tuple(_dot_nt(kcat, qcats[g]) for g in range(NSA_KV_GROUPS))

    def sel_mask(j, st):
        kpos, qpos = _positions_t(j, tk, q0, tq)
        return jnp.where(kpos <= qpos, st, NEG)

    tile_start = lambda j: pl.multiple_of(jnp.maximum(j, 0) * tk, tk)

    def win_scores(j):
        kw = kw_ref[0, pl.ds(tile_start(j), tk), :]
        return tuple(_dot_nt(kw, qhs[g]) for g in range(NSA_KV_GROUPS))

    def win_mask(j, st):
        kpos, qpos = _positions_t(j, tk, q0, tq)
        return jnp.where((kpos <= qpos) & (kpos > qpos - WINDOW) & (kpos >= 0), st, NEG)

    values = lambda ref: lambda g, j: ref[0, rows(g), pl.ds(tile_start(j), tk)]
    lane = _lane_iota()
    n_win = (WINDOW + tq) // tk
    o = oc_ref[0]
    for branch, args in ((1, (0, i * (tq // (2 * tk)), tq // tk, sel_scores, sel_mask, values(vst_ref))),
                         (2, ((i + 1) * (tq // tk) - n_win, 0, n_win, win_scores, win_mask, values(vwt_ref)))):
        res = _stream(*args, st_ref, acc_ref)
        ot = jnp.concatenate([acc / l for l, acc in res], axis=0)
        gate = jnp.where(lane < HEAD_DIM, _gate_col(small, n, branch), _gate_col(small, NSA_HPG + n, branch))
        o = o + gate * ot.T
    o_ref[0] = o.astype(BF16)


def _nsa_flash(main, sel, oc, small, hot, vt, tq, tk):
    b, t, _ = main.shape
    tile = lambda ch: pl.BlockSpec((1, tq, LANES), lambda bi, n, i: (bi, i, ch + n))
    full = lambda ch: pl.BlockSpec((1, t, LANES), lambda bi, n, i: (bi, 0, ch))
    vfull = lambda ch: pl.BlockSpec((1, LANES, t), lambda bi, n, i: (bi, ch, 0))
    return pl.pallas_call(
        partial(_nsa_flash_kernel, tq=tq, tk=tk),
        grid=(b, NSA_HPG, t // tq),
        in_specs=[tile(CH_NQ),
                  pl.BlockSpec((1, tq, LANES), lambda bi, n, i: (bi, i, 0)),
                  tile(0),
                  pl.BlockSpec((1, tq, LANES), lambda bi, n, i: (bi, i, 0)),
                  full(CH_KS), full(CH_KW),
                  pl.BlockSpec((t, LANES), lambda bi, n, i: (0, 0)),
                  vfull(VT_VS), vfull(VT_VW)],
        out_specs=tile(0),
        out_shape=jax.ShapeDtypeStruct((b, t, NSA_W), BF16),
        scratch_shapes=_stream_scratch(NSA_KV_GROUPS, HEAD_DIM, tq, tk),
        compiler_params=_cparams(("parallel", "parallel", "parallel")),
        name="nsa_flash",
    )(main, sel, oc, small, main, main, hot, vt, vt)


def _diff_kernel(q_ref, k_ref, vt_ref, lam_ref, g_ref, o_ref, st_ref, acc_ref, *, tq, tk, lam_init):
    i = pl.program_id(2)
    q2 = q_ref[0]
    lp = lam_ref[...]
    lam = (jnp.exp(jnp.sum(lp[0:1] * lp[1:2], axis=-1, keepdims=True))
           - jnp.exp(jnp.sum(lp[2:3] * lp[3:4], axis=-1, keepdims=True)) + lam_init)
    qhs = [_half_mask(q2, comp) for comp in range(2)]

    def scores(j):
        k2 = k_ref[0, pl.ds(pl.multiple_of(j * tk, tk), tk), :]
        return tuple(_dot_nt(k2, qhs[comp]) for comp in range(2))

    def mask(j, st):
        kpos, qpos = _positions_t(j, tk, i * tq, tq)
        return jnp.where(kpos <= qpos, st, NEG)

    def values(comp, j):
        return vt_ref[0, :, pl.ds(pl.multiple_of(j * tk, tk), tk)]

    (l1, acc1), (l2, acc2) = _stream(0, i * (tq // (2 * tk)), tq // tk, scores, mask, values, st_ref, acc_ref)
    o = (acc1 / l1 - lam * (acc2 / l2)).T
    o_ref[0] = (_rms(o, g_ref[...]) * (1.0 - lam_init)).astype(BF16)


def _diff_attention(main, vt, lam_p, subln_g, tq, tk, lam_init):
    b, t, _ = main.shape
    nh = DIFF_HEADS
    return pl.pallas_call(
        partial(_diff_kernel, tq=tq, tk=tk, lam_init=lam_init),
        grid=(b, nh, t // tq),
        in_specs=[pl.BlockSpec((1, tq, LANES), lambda bi, h, i: (bi, i, h)),
                  pl.BlockSpec((1, t, LANES), lambda bi, h, i: (bi, 0, nh + h)),
                  pl.BlockSpec((1, LANES, t), lambda bi, h, i: (bi, h, 0)),
                  pl.BlockSpec((4, LANES), lambda bi, h, i: (0, 0)),
                  pl.BlockSpec((1, LANES), lambda bi, h, i: (0, 0))],
        out_specs=pl.BlockSpec((1, tq, LANES), lambda bi, h, i: (bi, i, h)),
        out_shape=jax.ShapeDtypeStruct((b, t, nh * LANES), BF16),
        scratch_shapes=_stream_scratch(2, LANES, tq, tk),
        compiler_params=_cparams(("parallel", "parallel", "parallel")),
        name="diff_attention",
    )(main, main, vt, lam_p, subln_g)


def _out_proj_kernel(*refs, n_in):
    a_refs, w_refs = refs[:n_in], refs[n_in:2 * n_in]
    x_ref, g_ref, o_ref = refs[2 * n_in:]
    y = None
    for a_ref, w_ref in zip(a_refs, w_refs):
        t = _dot(a_ref[...], w_ref[...])
        y = t if y is None else y + t
    o_ref[...] = x_ref[...] + _rms(y, g_ref[...])


def _out_proj(acts, weights, x2, g, tm):
    n = x2.shape[0]
    n_in = len(acts)
    const = lambda shape: pl.BlockSpec(shape, lambda i: (0, 0))
    return pl.pallas_call(
        partial(_out_proj_kernel, n_in=n_in),
        grid=(n // tm,),
        in_specs=[pl.BlockSpec((tm, a.shape[1]), lambda i: (i, 0)) for a in acts]
                 + [const(w.shape) for w in weights]
                 + [pl.BlockSpec((tm, D_MODEL), lambda i: (i, 0)), const((1, D_MODEL))],
        out_specs=pl.BlockSpec((tm, D_MODEL), lambda i: (i, 0)),
        out_shape=jax.ShapeDtypeStruct((n, D_MODEL), F32),
        compiler_params=_cparams(("parallel",)),
        name="out_proj",
    )(*acts, *weights, x2, g)


def _mem_kv_kernel(mem_ref, g_ref, wk_ref, wv_ref, k_ref, v_ref):
    mn = _rms(mem_ref[0], g_ref[0]).astype(BF16)
    k_ref[0, 0] = _dot(mn, wk_ref[0]).astype(BF16)
    v_ref[0, 0] = _dot(mn, wv_ref[0]).astype(BF16)


def _mem_kv(mem, mem_norm_g, wk, wv):
    depth = wk.shape[0]
    b, m, d = mem.shape
    wspec = pl.BlockSpec((1, d, MEM_W), lambda l, bi: (l, 0, 0))
    ospec = pl.BlockSpec((1, 1, m, MEM_W), lambda l, bi: (l, bi, 0, 0))
    return pl.pallas_call(
        _mem_kv_kernel,
        grid=(depth, b),
        in_specs=[pl.BlockSpec((1, m, d), lambda l, bi: (bi, 0, 0)),
                  pl.BlockSpec((1, 1, d), lambda l, bi: (l, 0, 0)), wspec, wspec],
        out_specs=[ospec, ospec],
        out_shape=[jax.ShapeDtypeStruct((depth, b, m, MEM_W), BF16)] * 2,
        compiler_params=_cparams(("parallel", "parallel")),
        name="mem_kv",
    )(mem, mem_norm_g.reshape(depth, 1, d), wk, wv)


def _cross_kernel(x_ref, gin_ref, gout_ref, wq_ref, k_ref, v_ref, wo_ref, o_ref):
    lane = _lane_iota()
    x = x_ref[...]
    h = _rms(x, gin_ref[...]).astype(BF16)
    q = (_dot(h, wq_ref[...]) * Q_SCALE).astype(BF16)
    chunks = []
    for ch in range(MEM_W // LANES):
        q2 = q[:, ch * LANES:(ch + 1) * LANES]
        k2 = k_ref[0, 0, :, ch * LANES:(ch + 1) * LANES]
        v2 = v_ref[0, 0, :, ch * LANES:(ch + 1) * LANES]
        outs = []
        for hh in range(2):
            s = _dot_nt(_half_mask(q2, hh), k2)
            p = jnp.exp(s - jnp.max(s, axis=-1, keepdims=True))
            p = p / jnp.sum(p, axis=-1, keepdims=True)
            outs.append(_dot(p.astype(BF16), v2))
        chunks.append(jnp.where(lane < HEAD_DIM, outs[0], outs[1]).astype(BF16))
    y = _dot(jnp.concatenate(chunks, axis=1), wo_ref[...])
    o_ref[...] = x + _rms(y, gout_ref[...])


def _cross_attention(x2, gin, gout, wq, mem_k, mem_v, wo, layer, tm, rows_per_batch):
    n = x2.shape[0]
    m = mem_k.shape[2]
    const = lambda shape: pl.BlockSpec(shape, lambda i: (0, 0))
    kv = pl.BlockSpec((1, 1, m, MEM_W), lambda i: (layer, i // rows_per_batch, 0, 0))
    return pl.pallas_call(
        _cross_kernel,
        grid=(n // tm,),
        in_specs=[pl.BlockSpec((tm, D_MODEL), lambda i: (i, 0)), const((1, D_MODEL)), const((1, D_MODEL)),
                  const((D_MODEL, MEM_W)), kv, kv, const((MEM_W, D_MODEL))],
        out_specs=pl.BlockSpec((tm, D_MODEL), lambda i: (i, 0)),
        out_shape=jax.ShapeDtypeStruct((n, D_MODEL), F32),
        compiler_params=_cparams(("parallel",)),
        name="cross_attention",
    )(x2, gin, gout, wq, mem_k, mem_v, wo)


def _ffn_kernel(x_ref, gin_ref, gout_ref, wg_ref, wu_ref, wd_ref, o_ref, *, chunk):
    x = x_ref[...]
    h = _rms(x, gin_ref[...]).astype(BF16)
    d_ff = wg_ref.shape[1]
    y = None
    for c0 in range(0, d_ff, chunk):
        gate = _dot(h, wg_ref[:, c0:c0 + chunk])
        up = _dot(h, wu_ref[:, c0:c0 + chunk])
        a = (gate * (1.0 / (1.0 + jnp.exp(-gate))) * up).astype(BF16)
        t = _dot(a, wd_ref[c0:c0 + chunk, :])
        y = t if y is None else y + t
    o_ref[...] = x + _rms(y, gout_ref[...])


def _ffn(x2, gin, gout, wg, wu, wd, tm, chunk):
    n = x2.shape[0]
    d_ff = wg.shape[1]
    const = lambda shape: pl.BlockSpec(shape, lambda i: (0, 0), pipeline_mode=pl.Buffered(1))
    return pl.pallas_call(
        partial(_ffn_kernel, chunk=chunk),
        grid=(n // tm,),
        in_specs=[pl.BlockSpec((tm, D_MODEL), lambda i: (i, 0)), const((1, D_MODEL)), const((1, D_MODEL)),
                  const((D_MODEL, d_ff)), const((D_MODEL, d_ff)), const((d_ff, D_MODEL))],
        out_specs=pl.BlockSpec((tm, D_MODEL), lambda i: (i, 0)),
        out_shape=jax.ShapeDtypeStruct((n, D_MODEL), F32),
        compiler_params=_cparams(("parallel",)),
        name="ffn",
    )(x2, gin, gout, wg, wu, wd)


def _even_weights(w_in, w_out):
    kvw = NSA_KV_GROUPS * HEAD_DIM
    offs = [int(o) for o in np.cumsum((FOX_W, FOX_W, FOX_W, FOX_HEADS, NSA_W) + (kvw,) * 6 + (3 * NSA_HEADS,))]
    fk0, fv0, fl0, nq0, kc0, vc0, ks0, vs0, kw0, vw0, gl0, end = offs
    perm = np.concatenate([np.arange(HEAD_DIM) + HEAD_DIM * (g * NSA_HPG + n)
                           for n in range(NSA_HPG) for g in range(NSA_KV_GROUPS)])
    cols = lambda a, b: w_in[:, a:b]
    w = jnp.concatenate([cols(0, fv0), w_in[:, nq0 + perm], cols(kc0, vc0), cols(vc0, ks0), cols(ks0, vs0),
                         cols(kw0, vw0), cols(fv0, fl0), cols(vs0, kw0), cols(vw0, gl0)], axis=1).astype(BF16)
    w_small = jnp.concatenate([cols(fl0, nq0), cols(gl0, end),
                               jnp.zeros((D_MODEL, LANES - FOX_HEADS - 3 * NSA_HEADS), w_in.dtype)],
                              axis=1).astype(BF16)
    w_out_fox = w_out[:FOX_W].astype(BF16)
    w_out_nsa = w_out[FOX_W + perm].astype(BF16)
    return w, w_small, w_out_fox, w_out_nsa


def _overlap_matrix_t(t, ncp):
    nc = (t - CMP_BLOCK) // CMP_STRIDE + 1
    ns = t // SLC_BLOCK
    cs = np.arange(nc) * CMP_STRIDE
    ss = np.arange(ns) * SLC_BLOCK
    ov = np.clip(np.minimum(cs[:, None] + CMP_BLOCK, ss[None, :] + SLC_BLOCK)
                 - np.maximum(cs[:, None], ss[None, :]), 0, None) / CMP_BLOCK
    full = np.zeros((LANES // NSA_KV_GROUPS, ncp), np.float32)
    full[:ns, :nc] = ov.T
    return jnp.asarray(full, BF16), ns


def _stride_chunks(main, ch):
    b, t, _ = main.shape
    x = main[:, :, ch * LANES:(ch + 1) * LANES].reshape(b, t // CMP_STRIDE, CMP_STRIDE, NSA_KV_GROUPS, HEAD_DIM)
    return jnp.transpose(x, (0, 3, 1, 2, 4)).reshape(b, NSA_KV_GROUPS, t // CMP_STRIDE, CMP_STRIDE * HEAD_DIM)


def _pad_w2(w2):
    out = jnp.zeros((NSA_KV_GROUPS, CMP_HIDDEN, LANES), BF16)
    for g in range(NSA_KV_GROUPS):
        out = out.at[g, :, g * HEAD_DIM:(g + 1) * HEAD_DIM].set(w2.astype(BF16))
    return out


def kernel(x, mem, positions, sandwich_g, mem_norm_g, ev_w_in, ev_fox_fbias, ev_cmp_pos_k, ev_cmp_w1_k, ev_cmp_w2_k, ev_cmp_pos_v, ev_cmp_w1_v, ev_cmp_w2_v, ev_w_out, od_w_in, od_lambda, od_subln_g, od_w_out, ca_wq, ca_wk, ca_wv, ca_wo, ffn_wg, ffn_wu, ffn_wd):
    b, t, d = x.shape
    depth = sandwich_g.shape[0]
    n = b * t
    tm = 256
    tm_ffn = 512
    tq, tk = 512, 256
    assert d == D_MODEL and t % tq == 0 and tq % (2 * tk) == 0 and WINDOW % tk == 0 and t % tm == 0 and n % tm_ffn == 0

    tabs = tuple(a.reshape(b, t, LANES) for a in _rope_tables(positions, 512))
    mem_k, mem_v = _mem_kv(mem, mem_norm_g, ca_wk.astype(BF16), ca_wv.astype(BF16))
    ncp = t // CMP_STRIDE
    overlap_t, ns = _overlap_matrix_t(t, ncp)
    assert ns <= overlap_t.shape[0]
    n_sel = min(SLC_TOPK, ns)
    hot = jnp.asarray((np.arange(t)[:, None] // SLC_BLOCK == np.arange(LANES)[None, :] % overlap_t.shape[0])
                      .astype(np.float32), BF16)
    gain = lambda l, j: sandwich_g[l, j].reshape(1, d)

    x2 = x.reshape(n, d)
    for layer in range(depth):
        x3 = x2.reshape(b, t, d)
        if layer % 2 == 0:
            e = layer // 2
            w, w_small, w_out_fox, w_out_nsa = _even_weights(ev_w_in[e], ev_w_out[e])
            fb_row = jnp.zeros((1, LANES), F32).at[0, :FOX_HEADS].set(ev_fox_fbias[e].astype(F32))
            main, vt, small = _even_proj(x3, gain(layer, 0), w, w_small, fb_row, tabs, tm)
            aq, ak = _fox_aug(small, 256)
            o_fox = _fox_attention(main, aq, ak, vt, tq, tk)
            kc, vct = _compress(
                _stride_chunks(main, CH_KC), _stride_chunks(main, CH_VC),
                ev_cmp_w1_k[e].astype(BF16), _pad_w2(ev_cmp_w2_k[e]), ev_cmp_pos_k[e].reshape(1, -1).astype(BF16),
                ev_cmp_w1_v[e].astype(BF16), _pad_w2(ev_cmp_w2_v[e]), ev_cmp_pos_v[e].reshape(1, -1).astype(BF16))
            oc, sel = _nsa_select(main, kc, vct, overlap_t, small, 256, ns, n_sel)
            o_nsa = _nsa_flash(main, sel, oc, small, hot, vt, tq, tk)
            x2 = _out_proj([o_fox.reshape(n, FOX_W), o_nsa.reshape(n, NSA_W)], [w_out_fox, w_out_nsa],
                           x2, gain(layer, 1), tm)
        else:
            o = layer // 2
            w_in = od_w_in[o].astype(BF16)
            main, vt = _odd_proj(x3, gain(layer, 0), w_in, tabs, tm)
            lam_init = 0.8 - 0.6 * math.exp(-0.3 * layer)
            lam_p = jnp.pad(od_lambda[o].astype(F32), ((0, 0), (0, LANES - HEAD_DIM)))
            attn = _diff_attention(main, vt, lam_p, od_subln_g[o].reshape(1, LANES).astype(F32), tq, tk, lam_init)
            x2 = _out_proj([attn.reshape(n, D_MODEL)], [od_w_out[o].astype(BF16)], x2, gain(layer, 1), tm)
        x2 = _cross_attention(x2, gain(layer, 2), gain(layer, 3), ca_wq[layer].astype(BF16), mem_k, mem_v,
                              ca_wo[layer].astype(BF16), layer, tm, t // tm)
        x2 = _ffn(x2, gain(layer, 4), gain(layer, 5), ffn_wg[layer].astype(BF16), ffn_wu[layer].astype(BF16),
                  ffn_wd[layer].astype(BF16), tm_ffn, 256)
    return x2.reshape(b, t, d)
```

```python
import math
from functools import partial

import numpy as np
import jax
import jax.numpy as jnp
from jax import lax
from jax.experimental import pallas as pl
from jax.experimental.pallas import tpu as pltpu

F32 = jnp.float32
BF16 = jnp.bfloat16

D_MODEL = 1024
HEAD_DIM = 64
LANES = 128
ROPE_DIM = HEAD_DIM // 4
ROPE_THETA = 500000.0
FOX_HEADS = 8
NSA_HEADS = 8
NSA_KV_GROUPS = 2
NSA_HPG = NSA_HEADS // NSA_KV_GROUPS
CMP_BLOCK = 32
CMP_STRIDE = 16
CMP_HIDDEN = 2 * HEAD_DIM
SLC_BLOCK = 64
SLC_TOPK = 16
WINDOW = 512
DIFF_HEADS = 8
MEM_HEADS = 4
MEM_W = MEM_HEADS * HEAD_DIM
RMS_EPS = 1e-6
Q_SCALE = HEAD_DIM ** -0.5
LOG2E = math.log2(math.e)
Q_SCALE_LOG2 = Q_SCALE * LOG2E
NEG = -1e30
AUG_PER_HEAD = 6
ONES_ROWS = 16

FOX_W = FOX_HEADS * HEAD_DIM
NSA_W = NSA_HEADS * HEAD_DIM
CH_FQ, CH_FK, CH_NQ, CH_KC, CH_VC, CH_KS, CH_KW = 0, 4, 8, 12, 13, 14, 15
EV_MAIN = 16 * LANES
EV_ROPE_CHUNKS = tuple(range(CH_NQ, CH_NQ + 4)) + (CH_KC, CH_KS, CH_KW)
EV_QSCALE_CHUNKS = tuple(range(CH_FQ, CH_FQ + 4)) + tuple(range(CH_NQ, CH_NQ + 4))
VT_FV, VT_VS, VT_VW = 0, 4, 5
EV_VT = 6 * LANES
OD_MAIN = 2 * D_MODEL
OD_VT = D_MODEL

VMEM_LIMIT = 56 * 1024 * 1024


def _cparams(sem):
    return pltpu.CompilerParams(dimension_semantics=sem, vmem_limit_bytes=VMEM_LIMIT)


def _rms(x, g):
    return x * lax.rsqrt(jnp.mean(x * x, axis=-1, keepdims=True) + RMS_EPS) * g


def _split3(x):
    hi = x.astype(BF16)
    r1 = x - hi.astype(F32)
    mid = r1.astype(BF16)
    lo = (r1 - mid.astype(F32)).astype(BF16)
    return hi, mid, lo


def _dot(a, b):
    return jnp.dot(a, b, preferred_element_type=F32)


def _dot_nt(a, b):
    return lax.dot_general(a, b, (((1,), (1,)), ((), ())), preferred_element_type=F32)


def _lane_iota(n=LANES):
    return lax.broadcasted_iota(jnp.int32, (1, n), 1)


def _half_mask(q2, half):
    return jnp.where(_lane_iota() // HEAD_DIM == half, q2, jnp.zeros_like(q2))


def _positions_t(j, tk, q0, tq):
    kpos = j * tk + lax.broadcasted_iota(jnp.int32, (tk, 1), 0)
    qpos = q0 + lax.broadcasted_iota(jnp.int32, (1, tq), 1)
    return kpos, qpos


def _stream(base, n_pairs, n_tail, scores, mask, values, st_ref, acc_ref):
    n_chain, dv, tq = acc_ref.shape

    def park(j, slot):
        for c, st in enumerate(scores(j)):
            st_ref[slot, c] = st

    def step(j, slot, stats, masked, prefetch):
        if prefetch:
            park(j + 1, 1 - slot)
        parts = []
        for c in range(n_chain):
            st = st_ref[slot, c]
            if masked:
                st = mask(j, st)
            m, _ = stats[c]
            m_new = jnp.maximum(m, jnp.max(st, axis=0, keepdims=True))
            p = jnp.exp2(st - m_new).astype(BF16)
            vt1 = jnp.concatenate([values(c, j), jnp.ones((ONES_ROWS, p.shape[0]), BF16)], axis=0)
            parts.append((m_new, jnp.exp2(m - m_new), _dot(vt1, p)))
        out = []
        for c, (m_new, alpha, pv) in enumerate(parts):
            acc_ref[c] = alpha * acc_ref[c] + pv[:dv]
            out.append((m_new, alpha * stats[c][1] + pv[dv:dv + 1]))
        return tuple(out)

    def run(j, n_steps, stats):
        for s_ in range(n_steps):
            stats = step(j + s_, s_ % 2, stats, False, True)
        return stats

    acc_ref[...] = jnp.zeros_like(acc_ref)
    park(base, 0)
    stats = ((jnp.full((1, tq), NEG, F32), jnp.zeros((1, tq), F32)),) * n_chain
    n_quads = n_pairs // 2
    stats = lax.fori_loop(0, n_quads, lambda u, st: run(base + 4 * u, 4, st), stats)
    if not isinstance(n_pairs, int) or n_pairs % 2:
        stats = lax.cond(n_pairs % 2 == 1, lambda st: run(base + 4 * n_quads, 2, st), lambda st: st, stats)
    for jj in range(n_tail):
        stats = step(base + 2 * n_pairs + jj, jj % 2, stats, True, jj + 1 < n_tail)
    return [(l, acc_ref[c]) for c, (_, l) in enumerate(stats)]


def _stream_scratch(n_chain, dv, tq, tk):
    return [pltpu.VMEM((2, n_chain, tk, tq), F32), pltpu.VMEM((n_chain, dv, tq), F32)]


def _rope_kernel(pos_ref, inv_ref, m1_ref, m2_ref, c_ref, s1_ref, s2_ref):
    ang = pos_ref[...].astype(F32) * inv_ref[...]
    c_ref[...] = jnp.cos(ang)
    sn = jnp.sin(ang)
    s1_ref[...] = -sn * m1_ref[...]
    s2_ref[...] = sn * m2_ref[...]


def _rope_tables(positions, tm):
    n = positions.size
    inv = ROPE_THETA ** (-jnp.arange(0, ROPE_DIM, 2, dtype=F32) / ROPE_DIM)
    lane = np.arange(LANES) % HEAD_DIM
    half = ROPE_DIM // 2
    inv_l = jnp.where(lane < ROPE_DIM, inv[lane % half], 0.0).reshape(1, LANES).astype(F32)
    m1 = jnp.asarray((lane < half).astype(np.float32)).reshape(1, LANES)
    m2 = jnp.asarray(((lane >= half) & (lane < ROPE_DIM)).astype(np.float32)).reshape(1, LANES)
    row = pl.BlockSpec((1, LANES), lambda i: (0, 0))
    tab = pl.BlockSpec((tm, LANES), lambda i: (i, 0))
    return pl.pallas_call(
        _rope_kernel,
        grid=(n // tm,),
        in_specs=[pl.BlockSpec((tm, 1), lambda i: (i, 0)), row, row, row],
        out_specs=[tab, tab, tab],
        out_shape=[jax.ShapeDtypeStruct((n, LANES), F32)] * 3,
        compiler_params=_cparams(("parallel",)),
        name="rope_tables",
    )(positions.reshape(n, 1), inv_l, m1, m2)


def _apply_rope(y, c, s1, s2):
    half = ROPE_DIM // 2
    return y * c + pltpu.roll(y, LANES - half, 1) * s1 + pltpu.roll(y, half, 1) * s2


def _project_chunks(h, w_ref, tabs, main_ref, vt_ref, n_main, n_vt, rope_chunks, qscale_chunks):
    c, s1, s2 = tabs
    wide = 2 * LANES
    for ch2 in range((n_main + n_vt) // 2):
        y2 = _dot(h, w_ref[:, ch2 * wide:(ch2 + 1) * wide])
        for ch in (2 * ch2, 2 * ch2 + 1):
            y = y2[:, (ch % 2) * LANES:(ch % 2 + 1) * LANES]
            if ch >= n_main:
                vt_ref[0, (ch - n_main) * LANES:(ch - n_main + 1) * LANES, :] = y.T.astype(BF16)
                continue
            if ch in rope_chunks:
                y = _apply_rope(y, c, s1, s2)
            if ch in qscale_chunks:
                y = y * Q_SCALE_LOG2
            main_ref[0, :, ch * LANES:(ch + 1) * LANES] = y.astype(BF16)


def _even_proj_kernel(x_ref, g_ref, w_ref, ws_ref, fb_ref, c_ref, s1_ref, s2_ref, main_ref, vt_ref, small_ref):
    h = _rms(x_ref[0], g_ref[...]).astype(BF16)
    _project_chunks(h, w_ref, (c_ref[0], s1_ref[0], s2_ref[0]), main_ref, vt_ref,
                    EV_MAIN // LANES, EV_VT // LANES, EV_ROPE_CHUNKS, EV_QSCALE_CHUNKS)
    ys = _dot(h, ws_ref[...])
    z = ys + fb_ref[...]
    log_f = jnp.minimum(z, 0.0) - jnp.log(1.0 + jnp.exp(-jnp.abs(z)))
    gate = 1.0 / (1.0 + jnp.exp(-ys))
    small_ref[0] = jnp.where(_lane_iota() < FOX_HEADS, log_f, gate)


def _even_proj(x3, g, w, w_small, fb_row, tabs, tm):
    b, t, _ = x3.shape
    const = lambda shape: pl.BlockSpec(shape, lambda bi, i: (0, 0))
    tab = pl.BlockSpec((1, tm, LANES), lambda bi, i: (bi, i, 0))
    return pl.pallas_call(
        _even_proj_kernel,
        grid=(b, t // tm),
        in_specs=[pl.BlockSpec((1, tm, D_MODEL), lambda bi, i: (bi, i, 0)), const((1, D_MODEL)),
                  const((D_MODEL, EV_MAIN + EV_VT)), const((D_MODEL, LANES)), const((1, LANES)), tab, tab, tab],
        out_specs=[pl.BlockSpec((1, tm, EV_MAIN), lambda bi, i: (bi, i, 0)),
                   pl.BlockSpec((1, EV_VT, tm), lambda bi, i: (bi, 0, i)), tab],
        out_shape=[jax.ShapeDtypeStruct((b, t, EV_MAIN), BF16), jax.ShapeDtypeStruct((b, EV_VT, t), BF16),
                   jax.ShapeDtypeStruct((b, t, LANES), F32)],
        compiler_params=_cparams(("parallel", "parallel")),
        name="even_proj",
    )(x3, g, w, w_small, fb_row, *tabs)


def _odd_proj_kernel(x_ref, g_ref, w_ref, c_ref, s1_ref, s2_ref, main_ref, vt_ref):
    h = _rms(x_ref[0], g_ref[...]).astype(BF16)
    n_main = OD_MAIN // LANES
    _project_chunks(h, w_ref, (c_ref[0], s1_ref[0], s2_ref[0]), main_ref, vt_ref,
                    n_main, OD_VT // LANES, tuple(range(n_main)), tuple(range(n_main // 2)))


def _odd_proj(x3, g, w, tabs, tm):
    b, t, _ = x3.shape
    const = lambda shape: pl.BlockSpec(shape, lambda bi, i: (0, 0))
    tab = pl.BlockSpec((1, tm, LANES), lambda bi, i: (bi, i, 0))
    return pl.pallas_call(
        _odd_proj_kernel,
        grid=(b, t // tm),
        in_specs=[pl.BlockSpec((1, tm, D_MODEL), lambda bi, i: (bi, i, 0)), const((1, D_MODEL)),
                  const((D_MODEL, OD_MAIN + OD_VT)), tab, tab, tab],
        out_specs=[pl.BlockSpec((1, tm, OD_MAIN), lambda bi, i: (bi, i, 0)),
                   pl.BlockSpec((1, OD_VT, tm), lambda bi, i: (bi, 0, i))],
        out_shape=[jax.ShapeDtypeStruct((b, t, OD_MAIN), BF16), jax.ShapeDtypeStruct((b, OD_VT, t), BF16)],
        compiler_params=_cparams(("parallel", "parallel")),
        name="odd_proj",
    )(x3, g, w, *tabs)


def _fox_aug_kernel(lf_ref, tril_ref, eq_ref, ek_ref, oneq_ref, onek_ref, aq_ref, ak_ref, carry_ref):
    @pl.when(pl.program_id(1) == 0)
    def _():
        carry_ref[...] = jnp.zeros_like(carry_ref)

    tril = tril_ref[...]
    c = carry_ref[...]
    for piece in _split3(lf_ref[0]):
        c = c + _dot(tril, piece)
    carry_ref[...] = c[-1:, :]
    aq = oneq_ref[...]
    ak = onek_ref[...]
    for r, piece in enumerate(_split3(c * LOG2E)):
        aq = aq + _dot(piece, eq_ref[r])
        ak = ak - _dot(piece, ek_ref[r])
    aq_ref[0] = aq.astype(BF16)
    ak_ref[0] = ak.astype(BF16)


def _fox_aug(small, tc):
    b, t, _ = small.shape
    tril = jnp.asarray(np.tril(np.ones((tc, tc), np.float32)), BF16)
    eq = np.zeros((3, LANES, LANES), np.float32)
    ek = np.zeros((3, LANES, LANES), np.float32)
    oneq = np.zeros((1, LANES), np.float32)
    onek = np.zeros((1, LANES), np.float32)
    for h in range(FOX_HEADS):
        for r in range(3):
            eq[r, h, AUG_PER_HEAD * h + r] = 1.0
            ek[r, h, AUG_PER_HEAD * h + 3 + r] = 1.0
            oneq[0, AUG_PER_HEAD * h + 3 + r] = 1.0
            onek[0, AUG_PER_HEAD * h + r] = 1.0
    const2 = lambda shape: pl.BlockSpec(shape, lambda bi, i: (0,) * len(shape))
    blk = pl.BlockSpec((1, tc, LANES), lambda bi, i: (bi, i, 0))
    return pl.pallas_call(
        _fox_aug_kernel,
        grid=(b, t // tc),
        in_specs=[blk, const2((tc, tc)), const2((3, LANES, LANES)), const2((3, LANES, LANES)),
                  const2((1, LANES)), const2((1, LANES))],
        out_specs=[blk, blk],
        out_shape=[jax.ShapeDtypeStruct((b, t, LANES), BF16)] * 2,
        scratch_shapes=[pltpu.VMEM((1, LANES), F32)],
        compiler_params=_cparams(("parallel", "arbitrary")),
        name="fox_aug",
    )(small, tril, jnp.asarray(eq, BF16), jnp.asarray(ek, BF16), jnp.asarray(oneq), jnp.asarray(onek))


def _fox_kernel(q_ref, aq_ref, k_ref, ak_ref, vt_ref, o_ref, st_ref, acc_ref, *, tq, tk):
    pair = pl.program_id(1)
    i = pl.program_id(2)
    lane = _lane_iota()
    q2 = q_ref[0]
    qa = aq_ref[0]
    qcats = []
    for hh in range(2):
        head = 2 * pair + hh
        in_head = (lane >= AUG_PER_HEAD * head) & (lane < AUG_PER_HEAD * (head + 1))
        qcats.append(jnp.concatenate([_half_mask(q2, hh), jnp.where(in_head, qa, jnp.zeros_like(qa))], axis=1))

    def scores(j):
        ks = pl.multiple_of(j * tk, tk)
        kcat = jnp.concatenate([k_ref[0, pl.ds(ks, tk), :], ak_ref[0, pl.ds(ks, tk), :]], axis=1)
        return tuple(_dot_nt(kcat, qcats[hh]) for hh in range(2))

    def mask(j, st):
        kpos, qpos = _positions_t(j, tk, i * tq, tq)
        return jnp.where(kpos <= qpos, st, NEG)

    def values(hh, j):
        return vt_ref[0, hh * HEAD_DIM:(hh + 1) * HEAD_DIM, pl.ds(pl.multiple_of(j * tk, tk), tk)]

    res = _stream(0, i * (tq // (2 * tk)), tq // tk, scores, mask, values, st_ref, acc_ref)
    ot = jnp.concatenate([acc / l for l, acc in res], axis=0)
    o_ref[0] = ot.T.astype(BF16)


def _fox_attention(main, aq, ak, vt, tq, tk):
    b, t, _ = main.shape
    return pl.pallas_call(
        partial(_fox_kernel, tq=tq, tk=tk),
        grid=(b, FOX_HEADS // 2, t // tq),
        in_specs=[pl.BlockSpec((1, tq, LANES), lambda bi, p, i: (bi, i, CH_FQ + p)),
                  pl.BlockSpec((1, tq, LANES), lambda bi, p, i: (bi, i, 0)),
                  pl.BlockSpec((1, t, LANES), lambda bi, p, i: (bi, 0, CH_FK + p)),
                  pl.BlockSpec((1, t, LANES), lambda bi, p, i: (bi, 0, 0)),
                  pl.BlockSpec((1, LANES, t), lambda bi, p, i: (bi, VT_FV + p, 0))],
        out_specs=pl.BlockSpec((1, tq, LANES), lambda bi, p, i: (bi, i, p)),
        out_shape=jax.ShapeDtypeStruct((b, t, FOX_W), BF16),
        scratch_shapes=_stream_scratch(2, HEAD_DIM, tq, tk),
        compiler_params=_cparams(("parallel", "parallel", "parallel")),
        name="fox_attention",
    )(main, aq, main, ak, vt)


def _compress_kernel(xk_ref, xv_ref, w1k_ref, w2k_ref, pk_ref, w1v_ref, w2v_ref, pv_ref, kc_ref, vct_ref):
    half = CMP_STRIDE * HEAD_DIM
    for x_ref, w1_ref, w2_ref, p_ref, o_ref in ((xk_ref, w1k_ref, w2k_ref, pk_ref, kc_ref),
                                                 (xv_ref, w1v_ref, w2v_ref, pv_ref, vct_ref)):
        w1 = w1_ref[...]
        pos_h = _dot(p_ref[...], w1)
        out = None
        for g in range(NSA_KV_GROUPS):
            x = x_ref[0, g]
            first = _dot(x, w1[:half])
            second = _dot(x, w1[half:])
            nrow = first.shape[0]
            hid = first + pltpu.roll(second, nrow - 1, 0) + pos_h
            a = jax.nn.gelu(hid, approximate=True).astype(BF16)
            y = _dot(a, w2_ref[g])
            out = y if out is None else out + y
        o_ref[0] = (out.T if o_ref is vct_ref else out).astype(BF16)


def _compress(xk, xv, w1k, w2k, pk, w1v, w2v, pv):
    b, g, nchunk, width = xk.shape
    xspec = pl.BlockSpec((1, g, nchunk, width), lambda bi: (bi, 0, 0, 0))
    const = lambda shape: pl.BlockSpec(shape, lambda bi: (0,) * len(shape))
    wspecs = [const((width * 2, CMP_HIDDEN)), const((g, CMP_HIDDEN, LANES)), const((1, width * 2))]
    return pl.pallas_call(
        _compress_kernel,
        grid=(b,),
        in_specs=[xspec, xspec] + wspecs + wspecs,
        out_specs=[pl.BlockSpec((1, nchunk, LANES), lambda bi: (bi, 0, 0)),
                   pl.BlockSpec((1, LANES, nchunk), lambda bi: (bi, 0, 0))],
        out_shape=[jax.ShapeDtypeStruct((b, nchunk, LANES), BF16), jax.ShapeDtypeStruct((b, LANES, nchunk), BF16)],
        compiler_params=_cparams(("parallel",)),
        name="nsa_compress",
    )(xk, xv, w1k, w2k, pk, w1v, w2v, pv)


def _gate_col(small, head, branch):
    idx = FOX_HEADS + 3 * head + branch
    return jnp.sum(jnp.where(_lane_iota() == idx, small, 0.0), axis=-1, keepdims=True)


def _nsa_select_kernel(q_ref, kc_ref, vct_ref, ovt_ref, small_ref, oc_ref, sel_ref, *, tq, ns, n_sel):
    q0 = pl.program_id(1) * tq
    lane = _lane_iota()
    kc = kc_ref[0]
    vct = vct_ref[0]
    ncp = kc.shape[0]
    small = small_ref[0]
    qpos = q0 + lax.broadcasted_iota(jnp.int32, (1, tq), 1)
    cmp_end = lax.broadcasted_iota(jnp.int32, (ncp, 1), 0) * CMP_STRIDE + (CMP_BLOCK - 1)
    cmask = cmp_end <= qpos
    psum = [jnp.zeros((ncp, tq), F32) for _ in range(NSA_KV_GROUPS)]
    logits = [[_dot_nt(kc, _half_mask(q_ref[0, :, n * LANES:(n + 1) * LANES], g)) for g in range(NSA_KV_GROUPS)]
              for n in range(NSA_HPG)]
    for n in range(NSA_HPG):
        ots = []
        for g in range(NSA_KV_GROUPS):
            z = jnp.where(cmask, logits[n][g], -jnp.inf)
            m = jnp.max(z, axis=0, keepdims=True)
            m = jnp.where(m == -jnp.inf, 0.0, m)
            p = jnp.exp2(z - m)
            p = p / jnp.maximum(jnp.sum(p, axis=0, keepdims=True), 1e-30)
            psum[g] = psum[g] + p
            ots.append(_dot(vct[g * HEAD_DIM:(g + 1) * HEAD_DIM], p.astype(BF16)))
        gate = jnp.where(lane < HEAD_DIM, _gate_col(small, n, 0), _gate_col(small, NSA_HPG + n, 0))
        oc_ref[0, :, n * LANES:(n + 1) * LANES] = gate * jnp.concatenate(ots, axis=0).T

    nsp = ovt_ref.shape[0]
    blk = lax.broadcasted_iota(jnp.int32, (nsp, 1), 0)
    cur = qpos // SLC_BLOCK
    valid = blk * SLC_BLOCK <= qpos
    forced = (blk == 0) | (blk == cur) | (blk == cur - 1)
    scores = []
    for g in range(NSA_KV_GROUPS):
        imp = jnp.zeros((nsp, tq), F32)
        for piece in _split3(psum[g]):
            imp = imp + _dot(ovt_ref[...], piece)
        scores.append(jnp.where(valid, jnp.where(forced, jnp.inf, imp), -jnp.inf))
    ranks = [jnp.zeros((nsp, tq), jnp.int32) for _ in range(NSA_KV_GROUPS)]
    for i in range(ns):
        for g in range(NSA_KV_GROUPS):
            row = scores[g][i:i + 1, :]
            ahead = (row > scores[g]) | ((row == scores[g]) & (blk > i))
            ranks[g] = ranks[g] + ahead.astype(jnp.int32)
    masks = [jnp.where((ranks[g] < n_sel) & (blk < ns), 0.0, NEG) for g in range(NSA_KV_GROUPS)]
    sel_ref[0] = jnp.concatenate(masks, axis=0).T.astype(BF16)


def _nsa_select(main, kc, vct, overlap_t, small, tq, ns, n_sel):
    b, t, _ = main.shape
    ncp = kc.shape[1]
    nsp = overlap_t.shape[0]
    return pl.pallas_call(
        partial(_nsa_select_kernel, tq=tq, ns=ns, n_sel=n_sel),
        grid=(b, t // tq),
        in_specs=[pl.BlockSpec((1, tq, NSA_W), lambda bi, i: (bi, i, CH_NQ * LANES // NSA_W)),
                  pl.BlockSpec((1, ncp, LANES), lambda bi, i: (bi, 0, 0)),
                  pl.BlockSpec((1, LANES, ncp), lambda bi, i: (bi, 0, 0)),
                  pl.BlockSpec((nsp, ncp), lambda bi, i: (0, 0)),
                  pl.BlockSpec((1, tq, LANES), lambda bi, i: (bi, i, 0))],
        out_specs=[pl.BlockSpec((1, tq, NSA_W), lambda bi, i: (bi, i, 0)),
                   pl.BlockSpec((1, tq, NSA_KV_GROUPS * nsp), lambda bi, i: (bi, i, 0))],
        out_shape=[jax.ShapeDtypeStruct((b, t, NSA_W), F32),
                   jax.ShapeDtypeStruct((b, t, NSA_KV_GROUPS * nsp), BF16)],
        compiler_params=_cparams(("parallel", "parallel")),
        name="nsa_select",
    )(main, kc, vct, overlap_t, small)


def _nsa_flash_kernel(q_ref, sel_ref, oc_ref, small_ref, ks_ref, kw_ref, hot_ref, vst_ref, vwt_ref, o_ref,
                      st_ref, acc_ref, *, tq, tk):
    n = pl.program_id(1)
    i = pl.program_id(2)
    q0 = i * tq
    q2 = q_ref[0]
    small = small_ref[0]
    qhs = [_half_mask(q2, g) for g in range(NSA_KV_GROUPS)]
    qcats = [jnp.concatenate([qhs[g], _half_mask(sel_ref[0], g)], axis=1) for g in range(NSA_KV_GROUPS)]
    rows = lambda g: slice(g * HEAD_DIM, (g + 1) * HEAD_DIM)

    def sel_scores(j):
        ks = pl.multiple_of(j * tk, tk)
        kcat = jnp.concatenate([ks_ref[0, pl.ds(ks, tk), :], hot_ref[pl.ds(ks, tk), :]], axis=1)
        return tuple(_dot_nt(kcat, qcats[g]) for g in range(NSA_KV_GROUPS))

    def sel_mask(j, st):
        kpos, qpos = _positions_t(j, tk, q0, tq)
        return jnp.where(kpos <= qpos, st, NEG)

    tile_start = lambda j: pl.multiple_of(jnp.maximum(j, 0) * tk, tk)

    def win_scores(j):
        kw = kw_ref[0, pl.ds(tile_start(j), tk), :]
        return tuple(_dot_nt(kw, qhs[g]) for g in range(NSA_KV_GROUPS))

    def win_mask(j, st):
        kpos, qpos = _positions_t(j, tk, q0, tq)
        return jnp.where((kpos <= qpos) & (kpos > qpos - WINDOW) & (kpos >= 0), st, NEG)

    values = lambda ref: lambda g, j: ref[0, rows(g), pl.ds(tile_start(j), tk)]
    lane = _lane_iota()
    n_win = (WINDOW + tq) // tk
    o = oc_ref[0]
    for branch, args in ((1, (0, i * (tq // (2 * tk)), tq // tk, sel_scores, sel_mask, values(vst_ref))),
                         (2, ((i + 1) * (tq // tk) - n_win, 0, n_win, win_scores, win_mask, values(vwt_ref)))):
        res = _stream(*args, st_ref, acc_ref)
        ot = jnp.concatenate([acc / l for l, acc in res], axis=0)
        gate = jnp.where(lane < HEAD_DIM, _gate_col(small, n, branch), _gate_col(small, NSA_HPG + n, branch))
        o = o + gate * ot.T
    o_ref[0] = o.astype(BF16)


def _nsa_flash(main, sel, oc, small, hot, vt, tq, tk):
    b, t, _ = main.shape
    tile = lambda ch: pl.BlockSpec((1, tq, LANES), lambda bi, n, i: (bi, i, ch + n))
    full = lambda ch: pl.BlockSpec((1, t, LANES), lambda bi, n, i: (bi, 0, ch))
    vfull = lambda ch: pl.BlockSpec((1, LANES, t), lambda bi, n, i: (bi, ch, 0))
    return pl.pallas_call(
        partial(_nsa_flash_kernel, tq=tq, tk=tk),
        grid=(b, NSA_HPG, t // tq),
        in_specs=[tile(CH_NQ),
                  pl.BlockSpec((1, tq, LANES), lambda bi, n, i: (bi, i, 0)),
                  tile(0),
                  pl.BlockSpec((1, tq, LANES), lambda bi, n, i: (bi, i, 0)),
                  full(CH_KS), full(CH_KW),
                  pl.BlockSpec((t, LANES), lambda bi, n, i: (0, 0)),
                  vfull(VT_VS), vfull(VT_VW)],
        out_specs=tile(0),
        out_shape=jax.ShapeDtypeStruct((b, t, NSA_W), BF16),
        scratch_shapes=_stream_scratch(NSA_KV_GROUPS, HEAD_DIM, tq, tk),
        compiler_params=_cparams(("parallel", "parallel", "parallel")),
        name="nsa_flash",
    )(main, sel, oc, small, main, main, hot, vt, vt)


def _diff_kernel(q_ref, k_ref, vt_ref, lam_ref, g_ref, o_ref, st_ref, acc_ref, *, tq, tk, lam_init):
    i = pl.program_id(2)
    q2 = q_ref[0]
    lp = lam_ref[...]
    lam = (jnp.exp(jnp.sum(lp[0:1] * lp[1:2], axis=-1, keepdims=True))
           - jnp.exp(jnp.sum(lp[2:3] * lp[3:4], axis=-1, keepdims=True)) + lam_init)
    qhs = [_half_mask(q2, comp) for comp in range(2)]

    def scores(j):
        k2 = k_ref[0, pl.ds(pl.multiple_of(j * tk, tk), tk), :]
        return tuple(_dot_nt(k2, qhs[comp]) for comp in range(2))

    def mask(j, st):
        kpos, qpos = _positions_t(j, tk, i * tq, tq)
        return jnp.where(kpos <= qpos, st, NEG)

    def values(comp, j):
        return vt_ref[0, :, pl.ds(pl.multiple_of(j * tk, tk), tk)]

    (l1, acc1), (l2, acc2) = _stream(0, i * (tq // (2 * tk)), tq // tk, scores, mask, values, st_ref, acc_ref)
    o = (acc1 / l1 - lam * (acc2 / l2)).T
    o_ref[0] = (_rms(o, g_ref[...]) * (1.0 - lam_init)).astype(BF16)


def _diff_attention(main, vt, lam_p, subln_g, tq, tk, lam_init):
    b, t, _ = main.shape
    nh = DIFF_HEADS
    return pl.pallas_call(
        partial(_diff_kernel, tq=tq, tk=tk, lam_init=lam_init),
        grid=(b, nh, t // tq),
        in_specs=[pl.BlockSpec((1, tq, LANES), lambda bi, h, i: (bi, i, h)),
                  pl.BlockSpec((1, t, LANES), lambda bi, h, i: (bi, 0, nh + h)),
                  pl.BlockSpec((1, LANES, t), lambda bi, h, i: (bi, h, 0)),
                  pl.BlockSpec((4, LANES), lambda bi, h, i: (0, 0)),
                  pl.BlockSpec((1, LANES), lambda bi, h, i: (0, 0))],
        out_specs=pl.BlockSpec((1, tq, LANES), lambda bi, h, i: (bi, i, h)),
        out_shape=jax.ShapeDtypeStruct((b, t, nh * LANES), BF16),
        scratch_shapes=_stream_scratch(2, LANES, tq, tk),
        compiler_params=_cparams(("parallel", "parallel", "parallel")),
        name="diff_attention",
    )(main, main, vt, lam_p, subln_g)


def _mem_kv_kernel(mem_ref, g_ref, wk_ref, wv_ref, k_ref, vt_ref):
    mn = _rms(mem_ref[0], g_ref[0]).astype(BF16)
    k_ref[0, 0] = _dot(mn, wk_ref[0]).astype(BF16)
    vt_ref[0, 0] = _dot(mn, wv_ref[0]).T.astype(BF16)


def _mem_kv(mem, mem_norm_g, wk, wv):
    depth = wk.shape[0]
    b, m, d = mem.shape
    wspec = pl.BlockSpec((1, d, MEM_W), lambda l, bi: (l, 0, 0))
    return pl.pallas_call(
        _mem_kv_kernel,
        grid=(depth, b),
        in_specs=[pl.BlockSpec((1, m, d), lambda l, bi: (bi, 0, 0)),
                  pl.BlockSpec((1, 1, d), lambda l, bi: (l, 0, 0)), wspec, wspec],
        out_specs=[pl.BlockSpec((1, 1, m, MEM_W), lambda l, bi: (l, bi, 0, 0)),
                   pl.BlockSpec((1, 1, MEM_W, m), lambda l, bi: (l, bi, 0, 0))],
        out_shape=[jax.ShapeDtypeStruct((depth, b, m, MEM_W), BF16),
                   jax.ShapeDtypeStruct((depth, b, MEM_W, m), BF16)],
        compiler_params=_cparams(("parallel", "parallel")),
        name="mem_kv",
    )(mem, mem_norm_g.reshape(depth, 1, d), wk, wv)


def _post_mixer_kernel(*refs, n_in, n_part):
    a_refs, w_refs = refs[:n_in], refs[n_in:2 * n_in]
    x_ref, g1_ref, g2_ref, g3_ref, wq_ref, k_ref, vt_ref, wo_ref, o_ref = refs[2 * n_in:]
    rows = x_ref.shape[0] // n_part
    parts = [slice(r * rows, (r + 1) * rows) for r in range(n_part)]
    n_mem = k_ref.shape[2]
    heads = [(ch, hh) for ch in range(MEM_W // LANES) for hh in range(2)]
    ones = jnp.ones((ONES_ROWS, n_mem), BF16)

    def mixer_out(ps):
        y = None
        for a_ref, w_ref in zip(a_refs, w_refs):
            t = _dot(a_ref[ps, :], w_ref[...])
            y = t if y is None else y + t
        return y

    def scores(q):
        return [_dot_nt(k_ref[0, 0, :, ch * LANES:(ch + 1) * LANES], _half_mask(q[:, ch * LANES:(ch + 1) * LANES], hh))
                for ch, hh in heads]

    def attend(logits):
        outs = []
        for (ch, hh), st in zip(heads, logits):
            p = jnp.exp2(st - jnp.max(st, axis=0, keepdims=True)).astype(BF16)
            r0 = ch * LANES + hh * HEAD_DIM
            pv = _dot(jnp.concatenate([vt_ref[0, 0, r0:r0 + HEAD_DIM, :], ones], axis=0), p)
            outs.append(pv[:HEAD_DIM] / pv[HEAD_DIM:HEAD_DIM + 1])
        return jnp.concatenate(outs, axis=0).T.astype(BF16)

    ys = [mixer_out(ps) for ps in parts]
    xs = [x_ref[ps, :] + _rms(y, g1_ref[...]) for ps, y in zip(parts, ys)]
    qs = [(_dot(_rms(x, g2_ref[...]).astype(BF16), wq_ref[...]) * Q_SCALE_LOG2).astype(BF16) for x in xs]
    logits = [scores(q) for q in qs]
    os_ = [attend(lg) for lg in logits]
    for ps, x, o in zip(parts, xs, os_):
        o_ref[ps, :] = x + _rms(_dot(o, wo_ref[...]), g3_ref[...])


def _post_mixer(acts, weights, x2, g1, g2, g3, wq, mem_k, mem_vt, wo, layer, tm, rows_per_batch, n_part=2):
    n = x2.shape[0]
    m = mem_k.shape[2]
    const = lambda shape: pl.BlockSpec(shape, lambda i: (0, 0))
    row = pl.BlockSpec((tm, D_MODEL), lambda i: (i, 0))
    gspec = const((1, D_MODEL))
    return pl.pallas_call(
        partial(_post_mixer_kernel, n_in=len(acts), n_part=n_part),
        grid=(n // tm,),
        in_specs=[pl.BlockSpec((tm, a.shape[1]), lambda i: (i, 0)) for a in acts]
                 + [const(w.shape) for w in weights]
                 + [row, gspec, gspec, gspec, const((D_MODEL, MEM_W)),
                    pl.BlockSpec((1, 1, m, MEM_W), lambda i: (layer, i // rows_per_batch, 0, 0)),
                    pl.BlockSpec((1, 1, MEM_W, m), lambda i: (layer, i // rows_per_batch, 0, 0)),
                    const((MEM_W, D_MODEL))],
        out_specs=row,
        out_shape=jax.ShapeDtypeStruct((n, D_MODEL), F32),
        compiler_params=_cparams(("parallel",)),
        name="post_mixer",
    )(*acts, *weights, x2, g1, g2, g3, wq, mem_k, mem_vt, wo)


def _ffn_kernel(x_ref, gin_ref, gout_ref, wg_ref, wu_ref, wd_ref, o_ref, *, chunk):
    x = x_ref[...]
    h = _rms(x, gin_ref[...]).astype(BF16)
    d_ff = wg_ref.shape[1]
    y = None
    for c0 in range(0, d_ff, chunk):
        gate = _dot(h, wg_ref[:, c0:c0 + chunk])
        up = _dot(h, wu_ref[:, c0:c0 + chunk])
        a = (gate * (1.0 / (1.0 + jnp.exp(-gate))) * up).astype(BF16)
        t = _dot(a, wd_ref[c0:c0 + chunk, :])
        y = t if y is None else y + t
    o_ref[...] = x + _rms(y, gout_ref[...])


def _ffn(x2, gin, gout, wg, wu, wd, tm, chunk):
    n = x2.shape[0]
    d_ff = wg.shape[1]
    const = lambda shape: pl.BlockSpec(shape, lambda i: (0, 0), pipeline_mode=pl.Buffered(1))
    return pl.pallas_call(
        partial(_ffn_kernel, chunk=chunk),
        grid=(n // tm,),
        in_specs=[pl.BlockSpec((tm, D_MODEL), lambda i: (i, 0)), const((1, D_MODEL)), const((1, D_MODEL)),
                  const((D_MODEL, d_ff)), const((D_MODEL, d_ff)), const((d_ff, D_MODEL))],
        out_specs=pl.BlockSpec((tm, D_MODEL), lambda i: (i, 0)),
        out_shape=jax.ShapeDtypeStruct((n, D_MODEL), F32),
        compiler_params=_cparams(("parallel",)),
        name="ffn",
    )(x2, gin, gout, wg, wu, wd)


def _even_weights(w_in, w_out):
    kvw = NSA_KV_GROUPS * HEAD_DIM
    offs = [int(o) for o in np.cumsum((FOX_W, FOX_W, FOX_W, FOX_HEADS, NSA_W) + (kvw,) * 6 + (3 * NSA_HEADS,))]
    fk0, fv0, fl0, nq0, kc0, vc0, ks0, vs0, kw0, vw0, gl0, end = offs
    perm = np.concatenate([np.arange(HEAD_DIM) + HEAD_DIM * (g * NSA_HPG + n)
                           for n in range(NSA_HPG) for g in range(NSA_KV_GROUPS)])
    cols = lambda a, b: w_in[:, a:b]
    w = jnp.concatenate([cols(0, fv0), w_in[:, nq0 + perm], cols(kc0, vc0), cols(vc0, ks0), cols(ks0, vs0),
                         cols(kw0, vw0), cols(fv0, fl0), cols(vs0, kw0), cols(vw0, gl0)], axis=1).astype(BF16)
    w_small = jnp.concatenate([cols(fl0, nq0), cols(gl0, end),
                               jnp.zeros((D_MODEL, LANES - FOX_HEADS - 3 * NSA_HEADS), w_in.dtype)],
                              axis=1).astype(BF16)
    w_out_fox = w_out[:FOX_W].astype(BF16)
    w_out_nsa = w_out[FOX_W + perm].astype(BF16)
    return w, w_small, w_out_fox, w_out_nsa


def _overlap_matrix_t(t, ncp):
    nc = (t - CMP_BLOCK) // CMP_STRIDE + 1
    ns = t // SLC_BLOCK
    cs = np.arange(nc) * CMP_STRIDE
    ss = np.arange(ns) * SLC_BLOCK
    ov = np.clip(np.minimum(cs[:, None] + CMP_BLOCK, ss[None, :] + SLC_BLOCK)
                 - np.maximum(cs[:, None], ss[None, :]), 0, None) / CMP_BLOCK
    full = np.zeros((LANES // NSA_KV_GROUPS, ncp), np.float32)
    full[:ns, :nc] = ov.T
    return jnp.asarray(full, BF16), ns


def _stride_chunks(main, ch):
    b, t, _ = main.shape
    x = main[:, :, ch * LANES:(ch + 1) * LANES].reshape(b, t // CMP_STRIDE, CMP_STRIDE, NSA_KV_GROUPS, HEAD_DIM)
    return jnp.transpose(x, (0, 3, 1, 2, 4)).reshape(b, NSA_KV_GROUPS, t // CMP_STRIDE, CMP_STRIDE * HEAD_DIM)


def _pad_w2(w2):
    out = jnp.zeros((NSA_KV_GROUPS, CMP_HIDDEN, LANES), BF16)
    for g in range(NSA_KV_GROUPS):
        out = out.at[g, :, g * HEAD_DIM:(g + 1) * HEAD_DIM].set(w2.astype(BF16))
    return out


def kernel(x, mem, positions, sandwich_g, mem_norm_g, ev_w_in, ev_fox_fbias, ev_cmp_pos_k, ev_cmp_w1_k, ev_cmp_w2_k, ev_cmp_pos_v, ev_cmp_w1_v, ev_cmp_w2_v, ev_w_out, od_w_in, od_lambda, od_subln_g, od_w_out, ca_wq, ca_wk, ca_wv, ca_wo, ffn_wg, ffn_wu, ffn_wd):
    b, t, d = x.shape
    depth = sandwich_g.shape[0]
    n = b * t
    tm = 256
    tm_ffn = 512
    tq, tk = 512, 256
    assert d == D_MODEL and t % tq == 0 and tq % (2 * tk) == 0 and WINDOW % tk == 0 and t % tm == 0 and n % tm_ffn == 0

    tabs = tuple(a.reshape(b, t, LANES) for a in _rope_tables(positions, 512))
    mem_k, mem_vt = _mem_kv(mem, mem_norm_g, ca_wk.astype(BF16), ca_wv.astype(BF16))
    ncp = t // CMP_STRIDE
    overlap_t, ns = _overlap_matrix_t(t, ncp)
    assert ns <= overlap_t.shape[0]
    n_sel = min(SLC_TOPK, ns)
    hot = jnp.asarray((np.arange(t)[:, None] // SLC_BLOCK == np.arange(LANES)[None, :] % overlap_t.shape[0])
                      .astype(np.float32), BF16)
    gain = lambda l, j: sandwich_g[l, j].reshape(1, d)

    x2 = x.reshape(n, d)
    for layer in range(depth):
        x3 = x2.reshape(b, t, d)
        if layer % 2 == 0:
            e = layer // 2
            w, w_small, w_out_fox, w_out_nsa = _even_weights(ev_w_in[e], ev_w_out[e])
            fb_row = jnp.zeros((1, LANES), F32).at[0, :FOX_HEADS].set(ev_fox_fbias[e].astype(F32))
            main, vt, small = _even_proj(x3, gain(layer, 0), w, w_small, fb_row, tabs, tm)
            aq, ak = _fox_aug(small, 256)
            o_fox = _fox_attention(main, aq, ak, vt, tq, tk)
            kc, vct = _compress(
                _stride_chunks(main, CH_KC), _stride_chunks(main, CH_VC),
                ev_cmp_w1_k[e].astype(BF16), _pad_w2(ev_cmp_w2_k[e]), ev_cmp_pos_k[e].reshape(1, -1).astype(BF16),
                ev_cmp_w1_v[e].astype(BF16), _pad_w2(ev_cmp_w2_v[e]), ev_cmp_pos_v[e].reshape(1, -1).astype(BF16))
            oc, sel = _nsa_select(main, kc, vct, overlap_t, small, 256, ns, n_sel)
            o_nsa = _nsa_flash(main, sel, oc, small, hot, vt, tq, tk)
            acts, w_outs = [o_fox.reshape(n, FOX_W), o_nsa.reshape(n, NSA_W)], [w_out_fox, w_out_nsa]
        else:
            o = layer // 2
            w_in = od_w_in[o].astype(BF16)
            main, vt = _odd_proj(x3, gain(layer, 0), w_in, tabs, tm)
            lam_init = 0.8 - 0.6 * math.exp(-0.3 * layer)
            lam_p = jnp.pad(od_lambda[o].astype(F32), ((0, 0), (0, LANES - HEAD_DIM)))
            attn = _diff_attention(main, vt, lam_p, od_subln_g[o].reshape(1, LANES).astype(F32), tq, tk, lam_init)
            acts, w_outs = [attn.reshape(n, D_MODEL)], [od_w_out[o].astype(BF16)]
        x2 = _post_mixer(acts, w_outs, x2, gain(layer, 1), gain(layer, 2), gain(layer, 3),
                         ca_wq[layer].astype(BF16), mem_k, mem_vt, ca_wo[layer].astype(BF16), layer,
                         tm_ffn, t // tm_ffn)
        x2 = _ffn(x2, gain(layer, 4), gain(layer, 5), ffn_wg[layer].astype(BF16), ffn_wu[layer].astype(BF16),
                  ffn_wd[layer].astype(BF16), tm_ffn, 256)
    return x2.reshape(b, t, d)
```

```python
import math
from functools import partial

import numpy as np
import jax
import jax.numpy as jnp
from jax import lax
from jax.experimental import pallas as pl
from jax.experimental.pallas import tpu as pltpu

F32 = jnp.float32
BF16 = jnp.bfloat16

D_MODEL = 1024
HEAD_DIM = 64
LANES = 128
ROPE_DIM = HEAD_DIM // 4
ROPE_THETA = 500000.0
FOX_HEADS = 8
NSA_HEADS = 8
NSA_KV_GROUPS = 2
NSA_HPG = NSA_HEADS // NSA_KV_GROUPS
CMP_BLOCK = 32
CMP_STRIDE = 16
CMP_HIDDEN = 2 * HEAD_DIM
SLC_BLOCK = 64
SLC_TOPK = 16
WINDOW = 512
DIFF_HEADS = 8
MEM_HEADS = 4
MEM_W = MEM_HEADS * HEAD_DIM
RMS_EPS = 1e-6
Q_SCALE = HEAD_DIM ** -0.5
LOG2E = math.log2(math.e)
Q_SCALE_LOG2 = Q_SCALE * LOG2E
NEG = -1e30
AUG_PER_HEAD = 6
ONES_ROWS = 16

FOX_W = FOX_HEADS * HEAD_DIM
NSA_W = NSA_HEADS * HEAD_DIM
CH_FQ, CH_FK, CH_NQ, CH_KC, CH_VC, CH_KS, CH_KW = 0, 4, 8, 12, 13, 14, 15
EV_MAIN = 16 * LANES
EV_ROPE_CHUNKS = tuple(range(CH_NQ, CH_NQ + 4)) + (CH_KC, CH_KS, CH_KW)
EV_QSCALE_CHUNKS = tuple(range(CH_FQ, CH_FQ + 4)) + tuple(range(CH_NQ, CH_NQ + 4))
VT_FV, VT_VS, VT_VW = 0, 4, 5
EV_VT = 6 * LANES
OD_MAIN = 2 * D_MODEL
OD_VT = D_MODEL

VMEM_LIMIT = 56 * 1024 * 1024


def _cparams(sem):
    return pltpu.CompilerParams(dimension_semantics=sem, vmem_limit_bytes=VMEM_LIMIT)


def _rms(x, g):
    return x * lax.rsqrt(jnp.mean(x * x, axis=-1, keepdims=True) + RMS_EPS) * g


def _split3(x):
    hi = x.astype(BF16)
    r1 = x - hi.astype(F32)
    mid = r1.astype(BF16)
    lo = (r1 - mid.astype(F32)).astype(BF16)
    return hi, mid, lo


def _dot(a, b):
    return jnp.dot(a, b, preferred_element_type=F32)


def _dot_nt(a, b):
    return lax.dot_general(a, b, (((1,), (1,)), ((), ())), preferred_element_type=F32)


def _lane_iota(n=LANES):
    return lax.broadcasted_iota(jnp.int32, (1, n), 1)


def _half_mask(q2, half):
    return jnp.where(_lane_iota() // HEAD_DIM == half, q2, jnp.zeros_like(q2))


def _positions_t(j, tk, q0, tq):
    kpos = j * tk + lax.broadcasted_iota(jnp.int32, (tk, 1), 0)
    qpos = q0 + lax.broadcasted_iota(jnp.int32, (1, tq), 1)
    return kpos, qpos


def _stream(base, n_pairs, n_tail, scores, mask, values, st_ref, acc_ref):
    n_chain, dv, tq = acc_ref.shape

    def park(j, slot):
        for c, st in enumerate(scores(j)):
            st_ref[slot, c] = st

    def step(j, slot, stats, masked, prefetch):
        if prefetch:
            park(j + 1, 1 - slot)
        parts = []
        for c in range(n_chain):
            st = st_ref[slot, c]
            if masked:
                st = mask(j, st)
            m, _ = stats[c]
            m_new = jnp.maximum(m, jnp.max(st, axis=0, keepdims=True))
            p = jnp.exp2(st - m_new).astype(BF16)
            vt1 = jnp.concatenate([values(c, j), jnp.ones((ONES_ROWS, p.shape[0]), BF16)], axis=0)
            parts.append((m_new, jnp.exp2(m - m_new), _dot(vt1, p)))
        out = []
        for c, (m_new, alpha, pv) in enumerate(parts):
            acc_ref[c] = alpha * acc_ref[c] + pv[:dv]
            out.append((m_new, alpha * stats[c][1] + pv[dv:dv + 1]))
        return tuple(out)

    def run(j, n_steps, stats):
        for s_ in range(n_steps):
            stats = step(j + s_, s_ % 2, stats, False, True)
        return stats

    acc_ref[...] = jnp.zeros_like(acc_ref)
    park(base, 0)
    stats = ((jnp.full((1, tq), NEG, F32), jnp.zeros((1, tq), F32)),) * n_chain
    stats = run(base, 2 * n_pairs, stats)
    for jj in range(n_tail):
        stats = step(base + 2 * n_pairs + jj, jj % 2, stats, True, jj + 1 < n_tail)
    return [(l, acc_ref[c]) for c, (_, l) in enumerate(stats)]


def _per_query_tile(kernel_fn, n_q, grid, in_specs_fn, out_spec_fn, out_shape, scratch, name, args):
    out = None
    for i in range(n_q):
        in_specs, call_args, aliases = list(in_specs_fn(i)), list(args), {}
        if out is not None:
            in_specs.append(pl.BlockSpec(memory_space=pl.ANY))
            call_args.append(out)
            aliases = {len(call_args) - 1: 0}
        out = pl.pallas_call(
            partial(kernel_fn, i=i), grid=grid, in_specs=in_specs, out_specs=out_spec_fn(i), out_shape=out_shape,
            scratch_shapes=scratch, input_output_aliases=aliases,
            compiler_params=_cparams(("parallel",) * len(grid)), name=f"{name}_q{i}",
        )(*call_args)
    return out


def _stream_scratch(n_chain, dv, tq, tk):
    return [pltpu.VMEM((2, n_chain, tk, tq), F32), pltpu.VMEM((n_chain, dv, tq), F32)]


def _rope_kernel(pos_ref, inv_ref, m1_ref, m2_ref, c_ref, s1_ref, s2_ref):
    ang = pos_ref[...].astype(F32) * inv_ref[...]
    c_ref[...] = jnp.cos(ang)
    sn = jnp.sin(ang)
    s1_ref[...] = -sn * m1_ref[...]
    s2_ref[...] = sn * m2_ref[...]


def _rope_tables(positions, tm):
    n = positions.size
    inv = ROPE_THETA ** (-jnp.arange(0, ROPE_DIM, 2, dtype=F32) / ROPE_DIM)
    lane = np.arange(LANES) % HEAD_DIM
    half = ROPE_DIM // 2
    inv_l = jnp.where(lane < ROPE_DIM, inv[lane % half], 0.0).reshape(1, LANES).astype(F32)
    m1 = jnp.asarray((lane < half).astype(np.float32)).reshape(1, LANES)
    m2 = jnp.asarray(((lane >= half) & (lane < ROPE_DIM)).astype(np.float32)).reshape(1, LANES)
    row = pl.BlockSpec((1, LANES), lambda i: (0, 0))
    tab = pl.BlockSpec((tm, LANES), lambda i: (i, 0))
    return pl.pallas_call(
        _rope_kernel,
        grid=(n // tm,),
        in_specs=[pl.BlockSpec((tm, 1), lambda i: (i, 0)), row, row, row],
        out_specs=[tab, tab, tab],
        out_shape=[jax.ShapeDtypeStruct((n, LANES), F32)] * 3,
        compiler_params=_cparams(("parallel",)),
        name="rope_tables",
    )(positions.reshape(n, 1), inv_l, m1, m2)


def _apply_rope(y, c, s1, s2):
    half = ROPE_DIM // 2
    return y * c + pltpu.roll(y, LANES - half, 1) * s1 + pltpu.roll(y, half, 1) * s2


def _project_chunks(h, w_ref, tabs, main_ref, vt_ref, n_main, n_vt, rope_chunks, qscale_chunks):
    c, s1, s2 = tabs
    wide = 2 * LANES
    for ch2 in range((n_main + n_vt) // 2):
        y2 = _dot(h, w_ref[:, ch2 * wide:(ch2 + 1) * wide])
        for ch in (2 * ch2, 2 * ch2 + 1):
            y = y2[:, (ch % 2) * LANES:(ch % 2 + 1) * LANES]
            if ch >= n_main:
                vt_ref[0, (ch - n_main) * LANES:(ch - n_main + 1) * LANES, :] = y.T.astype(BF16)
                continue
            if ch in rope_chunks:
                y = _apply_rope(y, c, s1, s2)
            if ch in qscale_chunks:
                y = y * Q_SCALE_LOG2
            main_ref[0, :, ch * LANES:(ch + 1) * LANES] = y.astype(BF16)


def _even_proj_kernel(x_ref, g_ref, w_ref, ws_ref, fb_ref, c_ref, s1_ref, s2_ref, main_ref, vt_ref, small_ref):
    h = _rms(x_ref[0], g_ref[...]).astype(BF16)
    _project_chunks(h, w_ref, (c_ref[0], s1_ref[0], s2_ref[0]), main_ref, vt_ref,
                    EV_MAIN // LANES, EV_VT // LANES, EV_ROPE_CHUNKS, EV_QSCALE_CHUNKS)
    ys = _dot(h, ws_ref[...])
    z = ys + fb_ref[...]
    log_f = jnp.minimum(z, 0.0) - jnp.log(1.0 + jnp.exp(-jnp.abs(z)))
    gate = 1.0 / (1.0 + jnp.exp(-ys))
    small_ref[0] = jnp.where(_lane_iota() < FOX_HEADS, log_f, gate)


def _even_proj(x3, g, w, w_small, fb_row, tabs, tm):
    b, t, _ = x3.shape
    const = lambda shape: pl.BlockSpec(shape, lambda bi, i: (0, 0))
    tab = pl.BlockSpec((1, tm, LANES), lambda bi, i: (bi, i, 0))
    return pl.pallas_call(
        _even_proj_kernel,
        grid=(b, t // tm),
        in_specs=[pl.BlockSpec((1, tm, D_MODEL), lambda bi, i: (bi, i, 0)), const((1, D_MODEL)),
                  const((D_MODEL, EV_MAIN + EV_VT)), const((D_MODEL, LANES)), const((1, LANES)), tab, tab, tab],
        out_specs=[pl.BlockSpec((1, tm, EV_MAIN), lambda bi, i: (bi, i, 0)),
                   pl.BlockSpec((1, EV_VT, tm), lambda bi, i: (bi, 0, i)), tab],
        out_shape=[jax.ShapeDtypeStruct((b, t, EV_MAIN), BF16), jax.ShapeDtypeStruct((b, EV_VT, t), BF16),
                   jax.ShapeDtypeStruct((b, t, LANES), F32)],
        compiler_params=_cparams(("parallel", "parallel")),
        name="even_proj",
    )(x3, g, w, w_small, fb_row, *tabs)


def _odd_proj_kernel(x_ref, g_ref, w_ref, c_ref, s1_ref, s2_ref, main_ref, vt_ref):
    h = _rms(x_ref[0], g_ref[...]).astype(BF16)
    n_main = OD_MAIN // LANES
    _project_chunks(h, w_ref, (c_ref[0], s1_ref[0], s2_ref[0]), main_ref, vt_ref,
                    n_main, OD_VT // LANES, tuple(range(n_main)), tuple(range(n_main // 2)))


def _odd_proj(x3, g, w, tabs, tm):
    b, t, _ = x3.shape
    const = lambda shape: pl.BlockSpec(shape, lambda bi, i: (0, 0))
    tab = pl.BlockSpec((1, tm, LANES), lambda bi, i: (bi, i, 0))
    return pl.pallas_call(
        _odd_proj_kernel,
        grid=(b, t // tm),
        in_specs=[pl.BlockSpec((1, tm, D_MODEL), lambda bi, i: (bi, i, 0)), const((1, D_MODEL)),
                  const((D_MODEL, OD_MAIN + OD_VT)), tab, tab, tab],
        out_specs=[pl.BlockSpec((1, tm, OD_MAIN), lambda bi, i: (bi, i, 0)),
                   pl.BlockSpec((1, OD_VT, tm), lambda bi, i: (bi, 0, i))],
        out_shape=[jax.ShapeDtypeStruct((b, t, OD_MAIN), BF16), jax.ShapeDtypeStruct((b, OD_VT, t), BF16)],
        compiler_params=_cparams(("parallel", "parallel")),
        name="odd_proj",
    )(x3, g, w, *tabs)


def _fox_aug_kernel(lf_ref, tril_ref, eq_ref, ek_ref, oneq_ref, onek_ref, aq_ref, ak_ref, carry_ref):
    @pl.when(pl.program_id(1) == 0)
    def _():
        carry_ref[...] = jnp.zeros_like(carry_ref)

    tril = tril_ref[...]
    c = carry_ref[...]
    for piece in _split3(lf_ref[0]):
        c = c + _dot(tril, piece)
    carry_ref[...] = c[-1:, :]
    aq = oneq_ref[...]
    ak = onek_ref[...]
    for r, piece in enumerate(_split3(c * LOG2E)):
        aq = aq + _dot(piece, eq_ref[r])
        ak = ak - _dot(piece, ek_ref[r])
    aq_ref[0] = aq.astype(BF16)
    ak_ref[0] = ak.astype(BF16)


def _fox_aug(small, tc):
    b, t, _ = small.shape
    tril = jnp.asarray(np.tril(np.ones((tc, tc), np.float32)), BF16)
    eq = np.zeros((3, LANES, LANES), np.float32)
    ek = np.zeros((3, LANES, LANES), np.float32)
    oneq = np.zeros((1, LANES), np.float32)
    onek = np.zeros((1, LANES), np.float32)
    for h in range(FOX_HEADS):
        for r in range(3):
            eq[r, h, AUG_PER_HEAD * h + r] = 1.0
            ek[r, h, AUG_PER_HEAD * h + 3 + r] = 1.0
            oneq[0, AUG_PER_HEAD * h + 3 + r] = 1.0
            onek[0, AUG_PER_HEAD * h + r] = 1.0
    const2 = lambda shape: pl.BlockSpec(shape, lambda bi, i: (0,) * len(shape))
    blk = pl.BlockSpec((1, tc, LANES), lambda bi, i: (bi, i, 0))
    return pl.pallas_call(
        _fox_aug_kernel,
        grid=(b, t // tc),
        in_specs=[blk, const2((tc, tc)), const2((3, LANES, LANES)), const2((3, LANES, LANES)),
                  const2((1, LANES)), const2((1, LANES))],
        out_specs=[blk, blk],
        out_shape=[jax.ShapeDtypeStruct((b, t, LANES), BF16)] * 2,
        scratch_shapes=[pltpu.VMEM((1, LANES), F32)],
        compiler_params=_cparams(("parallel", "arbitrary")),
        name="fox_aug",
    )(small, tril, jnp.asarray(eq, BF16), jnp.asarray(ek, BF16), jnp.asarray(oneq), jnp.asarray(onek))


def _fox_kernel(q_ref, aq_ref, k_ref, ak_ref, vt_ref, *rest, i, tq, tk):
    o_ref, st_ref, acc_ref = rest[-3:]
    pair = pl.program_id(1)
    lane = _lane_iota()
    q2 = q_ref[0]
    qa = aq_ref[0]
    qcats = []
    for hh in range(2):
        head = 2 * pair + hh
        in_head = (lane >= AUG_PER_HEAD * head) & (lane < AUG_PER_HEAD * (head + 1))
        qcats.append(jnp.concatenate([_half_mask(q2, hh), jnp.where(in_head, qa, jnp.zeros_like(qa))], axis=1))

    def scores(j):
        ks = j * tk
        kcat = jnp.concatenate([k_ref[0, pl.ds(ks, tk), :], ak_ref[0, pl.ds(ks, tk), :]], axis=1)
        return tuple(_dot_nt(kcat, qcats[hh]) for hh in range(2))

    def mask(j, st):
        kpos, qpos = _positions_t(j, tk, i * tq, tq)
        return jnp.where(kpos <= qpos, st, NEG)

    def values(hh, j):
        return vt_ref[0, hh * HEAD_DIM:(hh + 1) * HEAD_DIM, pl.ds(j * tk, tk)]

    res = _stream(0, i * (tq // (2 * tk)), tq // tk, scores, mask, values, st_ref, acc_ref)
    ot = jnp.concatenate([acc / l for l, acc in res], axis=0)
    o_ref[0] = ot.T.astype(BF16)


def _fox_attention(main, aq, ak, vt, tq, tk):
    b, t, _ = main.shape
    in_specs = lambda i: [pl.BlockSpec((1, tq, LANES), lambda bi, p: (bi, i, CH_FQ + p)),
                          pl.BlockSpec((1, tq, LANES), lambda bi, p: (bi, i, 0)),
                          pl.BlockSpec((1, t, LANES), lambda bi, p: (bi, 0, CH_FK + p)),
                          pl.BlockSpec((1, t, LANES), lambda bi, p: (bi, 0, 0)),
                          pl.BlockSpec((1, LANES, t), lambda bi, p: (bi, VT_FV + p, 0))]
    return _per_query_tile(
        partial(_fox_kernel, tq=tq, tk=tk), t // tq, (b, FOX_HEADS // 2), in_specs,
        lambda i: pl.BlockSpec((1, tq, LANES), lambda bi, p: (bi, i, p)),
        jax.ShapeDtypeStruct((b, t, FOX_W), BF16), _stream_scratch(2, HEAD_DIM, tq, tk), "fox_attention",
        (main, aq, main, ak, vt))


def _compress_kernel(xk_ref, xv_ref, w1k_ref, w2k_ref, pk_ref, w1v_ref, w2v_ref, pv_ref, kc_ref, vct_ref):
    half = CMP_STRIDE * HEAD_DIM
    for x_ref, w1_ref, w2_ref, p_ref, o_ref in ((xk_ref, w1k_ref, w2k_ref, pk_ref, kc_ref),
                                                 (xv_ref, w1v_ref, w2v_ref, pv_ref, vct_ref)):
        w1 = w1_ref[...]
        pos_h = _dot(p_ref[...], w1)
        out = None
        for g in range(NSA_KV_GROUPS):
            x = x_ref[0, g]
            first = _dot(x, w1[:half])
            second = _dot(x, w1[half:])
            nrow = first.shape[0]
            hid = first + pltpu.roll(second, nrow - 1, 0) + pos_h
            a = jax.nn.gelu(hid, approximate=True).astype(BF16)
            y = _dot(a, w2_ref[g])
            out = y if out is None else out + y
        o_ref[0] = (out.T if o_ref is vct_ref else out).astype(BF16)


def _compress(xk, xv, w1k, w2k, pk, w1v, w2v, pv):
    b, g, nchunk, width = xk.shape
    xspec = pl.BlockSpec((1, g, nchunk, width), lambda bi: (bi, 0, 0, 0))
    const = lambda shape: pl.BlockSpec(shape, lambda bi: (0,) * len(shape))
    wspecs = [const((width * 2, CMP_HIDDEN)), const((g, CMP_HIDDEN, LANES)), const((1, width * 2))]
    return pl.pallas_call(
        _compress_kernel,
        grid=(b,),
        in_specs=[xspec, xspec] + wspecs + wspecs,
        out_specs=[pl.BlockSpec((1, nchunk, LANES), lambda bi: (bi, 0, 0)),
                   pl.BlockSpec((1, LANES, nchunk), lambda bi: (bi, 0, 0))],
        out_shape=[jax.ShapeDtypeStruct((b, nchunk, LANES), BF16), jax.ShapeDtypeStruct((b, LANES, nchunk), BF16)],
        compiler_params=_cparams(("parallel",)),
        name="nsa_compress",
    )(xk, xv, w1k, w2k, pk, w1v, w2v, pv)


def _gate_col(small, head, branch):
    idx = FOX_HEADS + 3 * head + branch
    return jnp.sum(jnp.where(_lane_iota() == idx, small, 0.0), axis=-1, keepdims=True)


def _nsa_select_kernel(q_ref, kc_ref, vct_ref, ovt_ref, small_ref, oc_ref, sel_ref, *, tq, ns, n_sel):
    q0 = pl.program_id(1) * tq
    lane = _lane_iota()
    kc = kc_ref[0]
    vct = vct_ref[0]
    ncp = kc.shape[0]
    small = small_ref[0]
    qpos = q0 + lax.broadcasted_iota(jnp.int32, (1, tq), 1)
    cmp_end = lax.broadcasted_iota(jnp.int32, (ncp, 1), 0) * CMP_STRIDE + (CMP_BLOCK - 1)
    cmask = cmp_end <= qpos
    psum = [jnp.zeros((ncp, tq), F32) for _ in range(NSA_KV_GROUPS)]
    logits = [[_dot_nt(kc, _half_mask(q_ref[0, :, n * LANES:(n + 1) * LANES], g)) for g in range(NSA_KV_GROUPS)]
              for n in range(NSA_HPG)]
    for n in range(NSA_HPG):
        ots = []
        for g in range(NSA_KV_GROUPS):
            z = jnp.where(cmask, logits[n][g], -jnp.inf)
            m = jnp.max(z, axis=0, keepdims=True)
            m = jnp.where(m == -jnp.inf, 0.0, m)
            p = jnp.exp2(z - m)
            p = p / jnp.maximum(jnp.sum(p, axis=0, keepdims=True), 1e-30)
            psum[g] = psum[g] + p
            ots.append(_dot(vct[g * HEAD_DIM:(g + 1) * HEAD_DIM], p.astype(BF16)))
        gate = jnp.where(lane < HEAD_DIM, _gate_col(small, n, 0), _gate_col(small, NSA_HPG + n, 0))
        oc_ref[0, :, n * LANES:(n + 1) * LANES] = gate * jnp.concatenate(ots, axis=0).T

    nsp = ovt_ref.shape[0]
    blk = lax.broadcasted_iota(jnp.int32, (nsp, 1), 0)
    cur = qpos // SLC_BLOCK
    valid = blk * SLC_BLOCK <= qpos
    forced = (blk == 0) | (blk == cur) | (blk == cur - 1)
    scores = []
    for g in range(NSA_KV_GROUPS):
        imp = jnp.zeros((nsp, tq), F32)
        for piece in _split3(psum[g]):
            imp = imp + _dot(ovt_ref[...], piece)
        scores.append(jnp.where(valid, jnp.where(forced, jnp.inf, imp), -jnp.inf))
    ranks = [jnp.zeros((nsp, tq), jnp.int32) for _ in range(NSA_KV_GROUPS)]
    for i in range(ns):
        for g in range(NSA_KV_GROUPS):
            row = scores[g][i:i + 1, :]
            ahead = (row > scores[g]) | ((row == scores[g]) & (blk > i))
            ranks[g] = ranks[g] + ahead.astype(jnp.int32)
    masks = [jnp.where((ranks[g] < n_sel) & (blk < ns), 0.0, NEG) for g in range(NSA_KV_GROUPS)]
    sel_ref[0] = jnp.concatenate(masks, axis=0).T.astype(BF16)


def _nsa_select(main, kc, vct, overlap_t, small, tq, ns, n_sel):
    b, t, _ = main.shape
    ncp = kc.shape[1]
    nsp = overlap_t.shape[0]
    return pl.pallas_call(
        partial(_nsa_select_kernel, tq=tq, ns=ns, n_sel=n_sel),
        grid=(b, t // tq),
        in_specs=[pl.BlockSpec((1, tq, NSA_W), lambda bi, i: (bi, i, CH_NQ * LANES // NSA_W)),
                  pl.BlockSpec((1, ncp, LANES), lambda bi, i: (bi, 0, 0)),
                  pl.BlockSpec((1, LANES, ncp), lambda bi, i: (bi, 0, 0)),
                  pl.BlockSpec((nsp, ncp), lambda bi, i: (0, 0)),
                  pl.BlockSpec((1, tq, LANES), lambda bi, i: (bi, i, 0))],
        out_specs=[pl.BlockSpec((1, tq, NSA_W), lambda bi, i: (bi, i, 0)),
                   pl.BlockSpec((1, tq, NSA_KV_GROUPS * nsp), lambda bi, i: (bi, i, 0))],
        out_shape=[jax.ShapeDtypeStruct((b, t, NSA_W), F32),
                   jax.ShapeDtypeStruct((b, t, NSA_KV_GROUPS * nsp), BF16)],
        compiler_params=_cparams(("parallel", "parallel")),
        name="nsa_select",
    )(main, kc, vct, overlap_t, small)


def _nsa_flash_kernel(q_ref, sel_ref, oc_ref, small_ref, ks_ref, kw_ref, hot_ref, vst_ref, vwt_ref, *rest,
                      i, tq, tk):
    o_ref, st_ref, acc_ref = rest[-3:]
    n = pl.program_id(1)
    q0 = i * tq
    q2 = q_ref[0]
    small = small_ref[0]
    qhs = [_half_mask(q2, g) for g in range(NSA_KV_GROUPS)]
    qcats = [jnp.concatenate([qhs[g], _half_mask(sel_ref[0], g)], axis=1) for g in range(NSA_KV_GROUPS)]
    rows = lambda g: slice(g * HEAD_DIM, (g + 1) * HEAD_DIM)

    def sel_scores(j):
        ks = j * tk
        kcat = jnp.concatenate([ks_ref[0, pl.ds(ks, tk), :], hot_ref[pl.ds(ks, tk), :]], axis=1)
        return tuple(_dot_nt(kcat, qcats[g]) for g in range(NSA_KV_GROUPS))

    def sel_mask(j, st):
        kpos, qpos = _positions_t(j, tk, q0, tq)
        return jnp.where(kpos <= qpos, st, NEG)

    def win_scores(j):
        kw = kw_ref[0, pl.ds(j * tk, tk), :]
        return tuple(_dot_nt(kw, qhs[g]) for g in range(NSA_KV_GROUPS))

    def win_mask(j, st):
        kpos, qpos = _positions_t(j, tk, q0, tq)
        return jnp.where((kpos <= qpos) & (kpos > qpos - WINDOW), st, NEG)

    values = lambda ref: lambda g, j: ref[0, rows(g), pl.ds(j * tk, tk)]
    lane = _lane_iota()
    n_end = (i + 1) * (tq // tk)
    win_lo = max(n_end - (WINDOW + tq) // tk, 0)
    o = oc_ref[0]
    for branch, args in ((1, (0, i * (tq // (2 * tk)), tq // tk, sel_scores, sel_mask, values(vst_ref))),
                         (2, (win_lo, 0, n_end - win_lo, win_scores, win_mask, values(vwt_ref)))):
        res = _stream(*args, st_ref, acc_ref)
        ot = jnp.concatenate([acc / l for l, acc in res], axis=0)
        gate = jnp.where(lane < HEAD_DIM, _gate_col(small, n, branch), _gate_col(small, NSA_HPG + n, branch))
        o = o + gate * ot.T
    o_ref[0] = o.astype(BF16)


def _nsa_flash(main, sel, oc, small, hot, vt, tq, tk):
    b, t, _ = main.shape
    tile = lambda i, ch: pl.BlockSpec((1, tq, LANES), lambda bi, n: (bi, i, ch + n))
    shared = lambda i: pl.BlockSpec((1, tq, LANES), lambda bi, n: (bi, i, 0))
    full = lambda ch: pl.BlockSpec((1, t, LANES), lambda bi, n: (bi, 0, ch))
    vfull = lambda ch: pl.BlockSpec((1, LANES, t), lambda bi, n: (bi, ch, 0))
    in_specs = lambda i: [tile(i, CH_NQ), shared(i), tile(i, 0), shared(i), full(CH_KS), full(CH_KW),
                          pl.BlockSpec((t, LANES), lambda bi, n: (0, 0)), vfull(VT_VS), vfull(VT_VW)]
    return _per_query_tile(
        partial(_nsa_flash_kernel, tq=tq, tk=tk), t // tq, (b, NSA_HPG), in_specs, lambda i: tile(i, 0),
        jax.ShapeDtypeStruct((b, t, NSA_W), BF16), _stream_scratch(NSA_KV_GROUPS, HEAD_DIM, tq, tk), "nsa_flash",
        (main, sel, oc, small, main, main, hot, vt, vt))


def _diff_kernel(q_ref, k_ref, vt_ref, lam_ref, g_ref, *rest, i, tq, tk, lam_init):
    o_ref, st_ref, acc_ref = rest[-3:]
    q2 = q_ref[0]
    lp = lam_ref[...]
    lam = (jnp.exp(jnp.sum(lp[0:1] * lp[1:2], axis=-1, keepdims=True))
           - jnp.exp(jnp.sum(lp[2:3] * lp[3:4], axis=-1, keepdims=True)) + lam_init)
    qhs = [_half_mask(q2, comp) for comp in range(2)]

    def scores(j):
        k2 = k_ref[0, pl.ds(j * tk, tk), :]
        return tuple(_dot_nt(k2, qhs[comp]) for comp in range(2))

    def mask(j, st):
        kpos, qpos = _positions_t(j, tk, i * tq, tq)
        return jnp.where(kpos <= qpos, st, NEG)

    def values(comp, j):
        return vt_ref[0, :, pl.ds(j * tk, tk)]

    (l1, acc1), (l2, acc2) = _stream(0, i * (tq // (2 * tk)), tq // tk, scores, mask, values, st_ref, acc_ref)
    o = (acc1 / l1 - lam * (acc2 / l2)).T
    o_ref[0] = (_rms(o, g_ref[...]) * (1.0 - lam_init)).astype(BF16)


def _diff_attention(main, vt, lam_p, subln_g, tq, tk, lam_init):
    b, t, _ = main.shape
    nh = DIFF_HEADS
    in_specs = lambda i: [pl.BlockSpec((1, tq, LANES), lambda bi, h: (bi, i, h)),
                          pl.BlockSpec((1, t, LANES), lambda bi, h: (bi, 0, nh + h)),
                          pl.BlockSpec((1, LANES, t), lambda bi, h: (bi, h, 0)),
                          pl.BlockSpec((4, LANES), lambda bi, h: (0, 0)),
                          pl.BlockSpec((1, LANES), lambda bi, h: (0, 0))]
    return _per_query_tile(
        partial(_diff_kernel, tq=tq, tk=tk, lam_init=lam_init), t // tq, (b, nh), in_specs,
        lambda i: pl.BlockSpec((1, tq, LANES), lambda bi, h: (bi, i, h)),
        jax.ShapeDtypeStruct((b, t, nh * LANES), BF16), _stream_scratch(2, LANES, tq, tk), "diff_attention",
        (main, main, vt, lam_p, subln_g))


def _mem_kv_kernel(mem_ref, g_ref, wk_ref, wv_ref, k_ref, vt_ref):
    mn = _rms(mem_ref[0], g_ref[0]).astype(BF16)
    k_ref[0, 0] = _dot(mn, wk_ref[0]).astype(BF16)
    vt_ref[0, 0] = _dot(mn, wv_ref[0]).T.astype(BF16)


def _mem_kv(mem, mem_norm_g, wk, wv):
    depth = wk.shape[0]
    b, m, d = mem.shape
    wspec = pl.BlockSpec((1, d, MEM_W), lambda l, bi: (l, 0, 0))
    return pl.pallas_call(
        _mem_kv_kernel,
        grid=(depth, b),
        in_specs=[pl.BlockSpec((1, m, d), lambda l, bi: (bi, 0, 0)),
                  pl.BlockSpec((1, 1, d), lambda l, bi: (l, 0, 0)), wspec, wspec],
        out_specs=[pl.BlockSpec((1, 1, m, MEM_W), lambda l, bi: (l, bi, 0, 0)),
                   pl.BlockSpec((1, 1, MEM_W, m), lambda l, bi: (l, bi, 0, 0))],
        out_shape=[jax.ShapeDtypeStruct((depth, b, m, MEM_W), BF16),
                   jax.ShapeDtypeStruct((depth, b, MEM_W, m), BF16)],
        compiler_params=_cparams(("parallel", "parallel")),
        name="mem_kv",
    )(mem, mem_norm_g.reshape(depth, 1, d), wk, wv)


def _post_mixer_kernel(*refs, n_in, n_part):
    a_refs, w_refs = refs[:n_in], refs[n_in:2 * n_in]
    x_ref, g1_ref, g2_ref, g3_ref, wq_ref, k_ref, vt_ref, wo_ref, o_ref = refs[2 * n_in:]
    rows = x_ref.shape[0] // n_part
    parts = [slice(r * rows, (r + 1) * rows) for r in range(n_part)]
    n_mem = k_ref.shape[2]
    heads = [(ch, hh) for ch in range(MEM_W // LANES) for hh in range(2)]
    ones = jnp.ones((ONES_ROWS, n_mem), BF16)

    def mixer_out(ps):
        y = None
        for a_ref, w_ref in zip(a_refs, w_refs):
            t = _dot(a_ref[ps, :], w_ref[...])
            y = t if y is None else y + t
        return y

    def scores(q):
        return [_dot_nt(k_ref[0, 0, :, ch * LANES:(ch + 1) * LANES], _half_mask(q[:, ch * LANES:(ch + 1) * LANES], hh))
                for ch, hh in heads]

    def attend(logits):
        outs = []
        for (ch, hh), st in zip(heads, logits):
            p = jnp.exp2(st - jnp.max(st, axis=0, keepdims=True)).astype(BF16)
            r0 = ch * LANES + hh * HEAD_DIM
            pv = _dot(jnp.concatenate([vt_ref[0, 0, r0:r0 + HEAD_DIM, :], ones], axis=0), p)
            outs.append(pv[:HEAD_DIM] / pv[HEAD_DIM:HEAD_DIM + 1])
        return jnp.concatenate(outs, axis=0).T.astype(BF16)

    ys = [mixer_out(ps) for ps in parts]
    xs = [x_ref[ps, :] + _rms(y, g1_ref[...]) for ps, y in zip(parts, ys)]
    qs = [(_dot(_rms(x, g2_ref[...]).astype(BF16), wq_ref[...]) * Q_SCALE_LOG2).astype(BF16) for x in xs]
    logits = [scores(q) for q in qs]
    os_ = [attend(lg) for lg in logits]
    for ps, x, o in zip(parts, xs, os_):
        o_ref[ps, :] = x + _rms(_dot(o, wo_ref[...]), g3_ref[...])


def _post_mixer(acts, weights, x2, g1, g2, g3, wq, mem_k, mem_vt, wo, layer, tm, rows_per_batch, n_part=2):
    n = x2.shape[0]
    m = mem_k.shape[2]
    const = lambda shape: pl.BlockSpec(shape, lambda i: (0, 0))
    row = pl.BlockSpec((tm, D_MODEL), lambda i: (i, 0))
    gspec = const((1, D_MODEL))
    return pl.pallas_call(
        partial(_post_mixer_kernel, n_in=len(acts), n_part=n_part),
        grid=(n // tm,),
        in_specs=[pl.BlockSpec((tm, a.shape[1]), lambda i: (i, 0)) for a in acts]
                 + [const(w.shape) for w in weights]
                 + [row, gspec, gspec, gspec, const((D_MODEL, MEM_W)),
                    pl.BlockSpec((1, 1, m, MEM_W), lambda i: (layer, i // rows_per_batch, 0, 0)),
                    pl.BlockSpec((1, 1, MEM_W, m), lambda i: (layer, i // rows_per_batch, 0, 0)),
                    const((MEM_W, D_MODEL))],
        out_specs=row,
        out_shape=jax.ShapeDtypeStruct((n, D_MODEL), F32),
        compiler_params=_cparams(("parallel",)),
        name="post_mixer",
    )(*acts, *weights, x2, g1, g2, g3, wq, mem_k, mem_vt, wo)


def _ffn_kernel(x_ref, gin_ref, gout_ref, wg_ref, wu_ref, wd_ref, o_ref, *, chunk):
    x = x_ref[...]
    h = _rms(x, gin_ref[...]).astype(BF16)
    d_ff = wg_ref.shape[1]
    y = None
    for c0 in range(0, d_ff, chunk):
        gate = _dot(h, wg_ref[:, c0:c0 + chunk])
        up = _dot(h, wu_ref[:, c0:c0 + chunk])
        a = (gate * (1.0 / (1.0 + jnp.exp(-gate))) * up).astype(BF16)
        t = _dot(a, wd_ref[c0:c0 + chunk, :])
        y = t if y is None else y + t
    o_ref[...] = x + _rms(y, gout_ref[...])


def _ffn(x2, gin, gout, wg, wu, wd, tm, chunk):
    n = x2.shape[0]
    d_ff = wg.shape[1]
    const = lambda shape: pl.BlockSpec(shape, lambda i: (0, 0), pipeline_mode=pl.Buffered(1))
    return pl.pallas_call(
        partial(_ffn_kernel, chunk=chunk),
        grid=(n // tm,),
        in_specs=[pl.BlockSpec((tm, D_MODEL), lambda i: (i, 0)), const((1, D_MODEL)), const((1, D_MODEL)),
                  const((D_MODEL, d_ff)), const((D_MODEL, d_ff)), const((d_ff, D_MODEL))],
        out_specs=pl.BlockSpec((tm, D_MODEL), lambda i: (i, 0)),
        out_shape=jax.ShapeDtypeStruct((n, D_MODEL), F32),
        compiler_params=_cparams(("parallel",)),
        name="ffn",
    )(x2, gin, gout, wg, wu, wd)


def _even_weights(w_in, w_out):
    kvw = NSA_KV_GROUPS * HEAD_DIM
    offs = [int(o) for o in np.cumsum((FOX_W, FOX_W, FOX_W, FOX_HEADS, NSA_W) + (kvw,) * 6 + (3 * NSA_HEADS,))]
    fk0, fv0, fl0, nq0, kc0, vc0, ks0, vs0, kw0, vw0, gl0, end = offs
    perm = np.concatenate([np.arange(HEAD_DIM) + HEAD_DIM * (g * NSA_HPG + n)
                           for n in range(NSA_HPG) for g in range(NSA_KV_GROUPS)])
    cols = lambda a, b: w_in[:, a:b]
    w = jnp.concatenate([cols(0, fv0), w_in[:, nq0 + perm], cols(kc0, vc0), cols(vc0, ks0), cols(ks0, vs0),
                         cols(kw0, vw0), cols(fv0, fl0), cols(vs0, kw0), cols(vw0, gl0)], axis=1).astype(BF16)
    w_small = jnp.concatenate([cols(fl0, nq0), cols(gl0, end),
                               jnp.zeros((D_MODEL, LANES - FOX_HEADS - 3 * NSA_HEADS), w_in.dtype)],
                              axis=1).astype(BF16)
    w_out_fox = w_out[:FOX_W].astype(BF16)
    w_out_nsa = w_out[FOX_W + perm].astype(BF16)
    return w, w_small, w_out_fox, w_out_nsa


def _overlap_matrix_t(t, ncp):
    nc = (t - CMP_BLOCK) // CMP_STRIDE + 1
    ns = t // SLC_BLOCK
    cs = np.arange(nc) * CMP_STRIDE
    ss = np.arange(ns) * SLC_BLOCK
    ov = np.clip(np.minimum(cs[:, None] + CMP_BLOCK, ss[None, :] + SLC_BLOCK)
                 - np.maximum(cs[:, None], ss[None, :]), 0, None) / CMP_BLOCK
    full = np.zeros((LANES // NSA_KV_GROUPS, ncp), np.float32)
    full[:ns, :nc] = ov.T
    return jnp.asarray(full, BF16), ns


def _stride_chunks(main, ch):
    b, t, _ = main.shape
    x = main[:, :, ch * LANES:(ch + 1) * LANES].reshape(b, t // CMP_STRIDE, CMP_STRIDE, NSA_KV_GROUPS, HEAD_DIM)
    return jnp.transpose(x, (0, 3, 1, 2, 4)).reshape(b, NSA_KV_GROUPS, t // CMP_STRIDE, CMP_STRIDE * HEAD_DIM)


def _pad_w2(w2):
    out = jnp.zeros((NSA_KV_GROUPS, CMP_HIDDEN, LANES), BF16)
    for g in range(NSA_KV_GROUPS):
        out = out.at[g, :, g * HEAD_DIM:(g + 1) * HEAD_DIM].set(w2.astype(BF16))
    return out


def kernel(x, mem, positions, sandwich_g, mem_norm_g, ev_w_in, ev_fox_fbias, ev_cmp_pos_k, ev_cmp_w1_k, ev_cmp_w2_k, ev_cmp_pos_v, ev_cmp_w1_v, ev_cmp_w2_v, ev_w_out, od_w_in, od_lambda, od_subln_g, od_w_out, ca_wq, ca_wk, ca_wv, ca_wo, ffn_wg, ffn_wu, ffn_wd):
    b, t, d = x.shape
    depth = sandwich_g.shape[0]
    n = b * t
    tm = 256
    tm_ffn = 512
    tq, tk = 512, 256
    assert d == D_MODEL and t % tq == 0 and tq % (2 * tk) == 0 and WINDOW % tk == 0 and t % tm == 0 and n % tm_ffn == 0

    tabs = tuple(a.reshape(b, t, LANES) for a in _rope_tables(positions, 512))
    mem_k, mem_vt = _mem_kv(mem, mem_norm_g, ca_wk.astype(BF16), ca_wv.astype(BF16))
    ncp = t // CMP_STRIDE
    overlap_t, ns = _overlap_matrix_t(t, ncp)
    assert ns <= overlap_t.shape[0]
    n_sel = min(SLC_TOPK, ns)
    hot = jnp.asarray((np.arange(t)[:, None] // SLC_BLOCK == np.arange(LANES)[None, :] % overlap_t.shape[0])
                      .astype(np.float32), BF16)
    gain = lambda l, j: sandwich_g[l, j].reshape(1, d)

    x2 = x.reshape(n, d)
    for layer in range(depth):
        x3 = x2.reshape(b, t, d)
        if layer % 2 == 0:
            e = layer // 2
            w, w_small, w_out_fox, w_out_nsa = _even_weights(ev_w_in[e], ev_w_out[e])
            fb_row = jnp.zeros((1, LANES), F32).at[0, :FOX_HEADS].set(ev_fox_fbias[e].astype(F32))
            main, vt, small = _even_proj(x3, gain(layer, 0), w, w_small, fb_row, tabs, tm)
            aq, ak = _fox_aug(small, 256)
            o_fox = _fox_attention(main, aq, ak, vt, tq, tk)
            kc, vct = _compress(
                _stride_chunks(main, CH_KC), _stride_chunks(main, CH_VC),
                ev_cmp_w1_k[e].astype(BF16), _pad_w2(ev_cmp_w2_k[e]), ev_cmp_pos_k[e].reshape(1, -1).astype(BF16),
                ev_cmp_w1_v[e].astype(BF16), _pad_w2(ev_cmp_w2_v[e]), ev_cmp_pos_v[e].reshape(1, -1).astype(BF16))
            oc, sel = _nsa_select(main, kc, vct, overlap_t, small, 256, ns, n_sel)
            o_nsa = _nsa_flash(main, sel, oc, small, hot, vt, tq, tk)
            acts, w_outs = [o_fox.reshape(n, FOX_W), o_nsa.reshape(n, NSA_W)], [w_out_fox, w_out_nsa]
        else:
            o = layer // 2
            w_in = od_w_in[o].astype(BF16)
            main, vt = _odd_proj(x3, gain(layer, 0), w_in, tabs, tm)
            lam_init = 0.8 - 0.6 * math.exp(-0.3 * layer)
            lam_p = jnp.pad(od_lambda[o].astype(F32), ((0, 0), (0, LANES - HEAD_DIM)))
            attn = _diff_attention(main, vt, lam_p, od_subln_g[o].reshape(1, LANES).astype(F32), tq, tk, lam_init)
            acts, w_outs = [attn.reshape(n, D_MODEL)], [od_w_out[o].astype(BF16)]
        x2 = _post_mixer(acts, w_outs, x2, gain(layer, 1), gain(layer, 2), gain(layer, 3),
                         ca_wq[layer].astype(BF16), mem_k, mem_vt, ca_wo[layer].astype(BF16), layer,
                         tm_ffn, t // tm_ffn)
        x2 = _ffn(x2, gain(layer, 4), gain(layer, 5), ffn_wg[layer].astype(BF16), ffn_wu[layer].astype(BF16),
                  ffn_wd[layer].astype(BF16), tm_ffn, 256)
    return x2.reshape(b, t, d)
```

```python
import math
from functools import partial

import numpy as np
import jax
import jax.numpy as jnp
from jax import lax
from jax.experimental import pallas as pl
from jax.experimental.pallas import tpu as pltpu

F32 = jnp.float32
BF16 = jnp.bfloat16

D_MODEL = 1024
HEAD_DIM = 64
LANES = 128
ROPE_DIM = HEAD_DIM // 4
ROPE_THETA = 500000.0
FOX_HEADS = 8
NSA_HEADS = 8
NSA_KV_GROUPS = 2
NSA_HPG = NSA_HEADS // NSA_KV_GROUPS
CMP_BLOCK = 32
CMP_STRIDE = 16
CMP_HIDDEN = 2 * HEAD_DIM
SLC_BLOCK = 64
SLC_TOPK = 16
WINDOW = 512
DIFF_HEADS = 8
MEM_HEADS = 4
MEM_W = MEM_HEADS * HEAD_DIM
RMS_EPS = 1e-6
Q_SCALE = HEAD_DIM ** -0.5
LOG2E = math.log2(math.e)
Q_SCALE_LOG2 = Q_SCALE * LOG2E
NEG = -1e30
AUG_PER_HEAD = 6
ONES_ROWS = 16

FOX_W = FOX_HEADS * HEAD_DIM
NSA_W = NSA_HEADS * HEAD_DIM
CH_FQ, CH_FK, CH_NQ, CH_KC, CH_VC, CH_KS, CH_KW = 0, 4, 8, 12, 13, 14, 15
EV_MAIN = 16 * LANES
EV_ROPE_CHUNKS = tuple(range(CH_NQ, CH_NQ + 4)) + (CH_KC, CH_KS, CH_KW)
EV_QSCALE_CHUNKS = tuple(range(CH_FQ, CH_FQ + 4)) + tuple(range(CH_NQ, CH_NQ + 4))
VT_FV, VT_VS, VT_VW = 0, 4, 5
EV_VT = 6 * LANES
OD_MAIN = 2 * D_MODEL
OD_VT = D_MODEL

VMEM_LIMIT = 56 * 1024 * 1024


def _cparams(sem):
    return pltpu.CompilerParams(dimension_semantics=sem, vmem_limit_bytes=VMEM_LIMIT)


def _rms(x, g):
    return x * lax.rsqrt(jnp.mean(x * x, axis=-1, keepdims=True) + RMS_EPS) * g


def _split3(x):
    hi = x.astype(BF16)
    r1 = x - hi.astype(F32)
    mid = r1.astype(BF16)
    lo = (r1 - mid.astype(F32)).astype(BF16)
    return hi, mid, lo


def _dot(a, b):
    return jnp.dot(a, b, preferred_element_type=F32)


def _dot_nt(a, b):
    return lax.dot_general(a, b, (((1,), (1,)), ((), ())), preferred_element_type=F32)


def _lane_iota(n=LANES):
    return lax.broadcasted_iota(jnp.int32, (1, n), 1)


def _half_mask(q2, half):
    return jnp.where(_lane_iota() // HEAD_DIM == half, q2, jnp.zeros_like(q2))


def _transposed(x):
    return x.astype(F32).T.astype(BF16)


def _half_mask_t(qt, half):
    row = lax.broadcasted_iota(jnp.int32, (qt.shape[0], 1), 0)
    return jnp.where(row // HEAD_DIM == half, qt, jnp.zeros_like(qt))


def _positions_t(j, tk, q0, tq):
    kpos = j * tk + lax.broadcasted_iota(jnp.int32, (tk, 1), 0)
    qpos = q0 + lax.broadcasted_iota(jnp.int32, (1, tq), 1)
    return kpos, qpos


def _stream(base, n_pairs, n_tail, scores, mask, values, st_ref, acc_ref):
    n_chain, dv, tq = acc_ref.shape

    def park(j, slot):
        for c, st in enumerate(scores(j)):
            st_ref[slot, c] = st

    def step(j, slot, stats, masked, prefetch):
        if prefetch:
            park(j + 1, 1 - slot)
        parts = []
        for c in range(n_chain):
            st = st_ref[slot, c]
            if masked:
                st = mask(j, st)
            m, _ = stats[c]
            m_new = jnp.maximum(m, jnp.max(st, axis=0, keepdims=True))
            p = jnp.exp2(st - m_new).astype(BF16)
            vt1 = jnp.concatenate([values(c, j), jnp.ones((ONES_ROWS, p.shape[0]), BF16)], axis=0)
            parts.append((m_new, jnp.exp2(m - m_new), _dot(vt1, p)))
        out = []
        for c, (m_new, alpha, pv) in enumerate(parts):
            acc_ref[c] = alpha * acc_ref[c] + pv[:dv]
            out.append((m_new, alpha * stats[c][1] + pv[dv:dv + 1]))
        return tuple(out)

    def run(j, n_steps, stats):
        for s_ in range(n_steps):
            stats = step(j + s_, s_ % 2, stats, False, True)
        return stats

    acc_ref[...] = jnp.zeros_like(acc_ref)
    park(base, 0)
    stats = ((jnp.full((1, tq), NEG, F32), jnp.zeros((1, tq), F32)),) * n_chain
    stats = run(base, 2 * n_pairs, stats)
    for jj in range(n_tail):
        stats = step(base + 2 * n_pairs + jj, jj % 2, stats, True, jj + 1 < n_tail)
    return [(l, acc_ref[c]) for c, (_, l) in enumerate(stats)]


def _per_query_tile(kernel_fn, n_q, grid, in_specs_fn, out_spec_fn, out_shape, scratch, name, args):
    out = None
    for i in range(n_q):
        in_specs, call_args, aliases = list(in_specs_fn(i)), list(args), {}
        if out is not None:
            in_specs.append(pl.BlockSpec(memory_space=pl.ANY))
            call_args.append(out)
            aliases = {len(call_args) - 1: 0}
        out = pl.pallas_call(
            partial(kernel_fn, i=i), grid=grid, in_specs=in_specs, out_specs=out_spec_fn(i), out_shape=out_shape,
            scratch_shapes=scratch, input_output_aliases=aliases,
            compiler_params=_cparams(("parallel",) * len(grid)), name=f"{name}_q{i}",
        )(*call_args)
    return out


def _stream_scratch(n_chain, dv, tq, tk):
    return [pltpu.VMEM((2, n_chain, tk, tq), F32), pltpu.VMEM((n_chain, dv, tq), F32)]


def _rope_kernel(pos_ref, inv_ref, m1_ref, m2_ref, c_ref, s1_ref, s2_ref):
    ang = pos_ref[...].astype(F32) * inv_ref[...]
    c_ref[...] = jnp.cos(ang)
    sn = jnp.sin(ang)
    s1_ref[...] = -sn * m1_ref[...]
    s2_ref[...] = sn * m2_ref[...]


def _rope_tables(positions, tm):
    n = positions.size
    inv = ROPE_THETA ** (-jnp.arange(0, ROPE_DIM, 2, dtype=F32) / ROPE_DIM)
    lane = np.arange(LANES) % HEAD_DIM
    half = ROPE_DIM // 2
    inv_l = jnp.where(lane < ROPE_DIM, inv[lane % half], 0.0).reshape(1, LANES).astype(F32)
    m1 = jnp.asarray((lane < half).astype(np.float32)).reshape(1, LANES)
    m2 = jnp.asarray(((lane >= half) & (lane < ROPE_DIM)).astype(np.float32)).reshape(1, LANES)
    row = pl.BlockSpec((1, LANES), lambda i: (0, 0))
    tab = pl.BlockSpec((tm, LANES), lambda i: (i, 0))
    return pl.pallas_call(
        _rope_kernel,
        grid=(n // tm,),
        in_specs=[pl.BlockSpec((tm, 1), lambda i: (i, 0)), row, row, row],
        out_specs=[tab, tab, tab],
        out_shape=[jax.ShapeDtypeStruct((n, LANES), F32)] * 3,
        compiler_params=_cparams(("parallel",)),
        name="rope_tables",
    )(positions.reshape(n, 1), inv_l, m1, m2)


def _apply_rope(y, c, s1, s2):
    half = ROPE_DIM // 2
    return y * c + pltpu.roll(y, LANES - half, 1) * s1 + pltpu.roll(y, half, 1) * s2


def _project_chunks(h, w_ref, tabs, main_ref, vt_ref, n_main, n_vt, rope_chunks, qscale_chunks):
    c, s1, s2 = tabs
    wide = 2 * LANES
    for ch2 in range((n_main + n_vt) // 2):
        y2 = _dot(h, w_ref[:, ch2 * wide:(ch2 + 1) * wide])
        for ch in (2 * ch2, 2 * ch2 + 1):
            y = y2[:, (ch % 2) * LANES:(ch % 2 + 1) * LANES]
            if ch >= n_main:
                vt_ref[0, (ch - n_main) * LANES:(ch - n_main + 1) * LANES, :] = y.T.astype(BF16)
                continue
            if ch in rope_chunks:
                y = _apply_rope(y, c, s1, s2)
            if ch in qscale_chunks:
                y = y * Q_SCALE_LOG2
            main_ref[0, :, ch * LANES:(ch + 1) * LANES] = y.astype(BF16)


def _even_proj_kernel(x_ref, g_ref, w_ref, ws_ref, fb_ref, c_ref, s1_ref, s2_ref, main_ref, vt_ref, small_ref):
    h = _rms(x_ref[0], g_ref[...]).astype(BF16)
    _project_chunks(h, w_ref, (c_ref[0], s1_ref[0], s2_ref[0]), main_ref, vt_ref,
                    EV_MAIN // LANES, EV_VT // LANES, EV_ROPE_CHUNKS, EV_QSCALE_CHUNKS)
    ys = _dot(h, ws_ref[...])
    z = ys + fb_ref[...]
    log_f = jnp.minimum(z, 0.0) - jnp.log(1.0 + jnp.exp(-jnp.abs(z)))
    gate = 1.0 / (1.0 + jnp.exp(-ys))
    small_ref[0] = jnp.where(_lane_iota() < FOX_HEADS, log_f, gate)


def _even_proj(x3, g, w, w_small, fb_row, tabs, tm):
    b, t, _ = x3.shape
    const = lambda shape: pl.BlockSpec(shape, lambda bi, i: (0, 0))
    tab = pl.BlockSpec((1, tm, LANES), lambda bi, i: (bi, i, 0))
    return pl.pallas_call(
        _even_proj_kernel,
        grid=(b, t // tm),
        in_specs=[pl.BlockSpec((1, tm, D_MODEL), lambda bi, i: (bi, i, 0)), const((1, D_MODEL)),
                  const((D_MODEL, EV_MAIN + EV_VT)), const((D_MODEL, LANES)), const((1, LANES)), tab, tab, tab],
        out_specs=[pl.BlockSpec((1, tm, EV_MAIN), lambda bi, i: (bi, i, 0)),
                   pl.BlockSpec((1, EV_VT, tm), lambda bi, i: (bi, 0, i)), tab],
        out_shape=[jax.ShapeDtypeStruct((b, t, EV_MAIN), BF16), jax.ShapeDtypeStruct((b, EV_VT, t), BF16),
                   jax.ShapeDtypeStruct((b, t, LANES), F32)],
        compiler_params=_cparams(("parallel", "parallel")),
        name="even_proj",
    )(x3, g, w, w_small, fb_row, *tabs)


def _odd_proj_kernel(x_ref, g_ref, w_ref, c_ref, s1_ref, s2_ref, main_ref, vt_ref):
    h = _rms(x_ref[0], g_ref[...]).astype(BF16)
    n_main = OD_MAIN // LANES
    _project_chunks(h, w_ref, (c_ref[0], s1_ref[0], s2_ref[0]), main_ref, vt_ref,
                    n_main, OD_VT // LANES, tuple(range(n_main)), tuple(range(n_main // 2)))


def _odd_proj(x3, g, w, tabs, tm):
    b, t, _ = x3.shape
    const = lambda shape: pl.BlockSpec(shape, lambda bi, i: (0, 0))
    tab = pl.BlockSpec((1, tm, LANES), lambda bi, i: (bi, i, 0))
    return pl.pallas_call(
        _odd_proj_kernel,
        grid=(b, t // tm),
        in_specs=[pl.BlockSpec((1, tm, D_MODEL), lambda bi, i: (bi, i, 0)), const((1, D_MODEL)),
                  const((D_MODEL, OD_MAIN + OD_VT)), tab, tab, tab],
        out_specs=[pl.BlockSpec((1, tm, OD_MAIN), lambda bi, i: (bi, i, 0)),
                   pl.BlockSpec((1, OD_VT, tm), lambda bi, i: (bi, 0, i))],
        out_shape=[jax.ShapeDtypeStruct((b, t, OD_MAIN), BF16), jax.ShapeDtypeStruct((b, OD_VT, t), BF16)],
        compiler_params=_cparams(("parallel", "parallel")),
        name="odd_proj",
    )(x3, g, w, *tabs)


def _fox_aug_kernel(lf_ref, tril_ref, eq_ref, ek_ref, oneq_ref, onek_ref, aq_ref, ak_ref, carry_ref):
    @pl.when(pl.program_id(1) == 0)
    def _():
        carry_ref[...] = jnp.zeros_like(carry_ref)

    tril = tril_ref[...]
    c = carry_ref[...]
    for piece in _split3(lf_ref[0]):
        c = c + _dot(tril, piece)
    carry_ref[...] = c[-1:, :]
    aq = oneq_ref[...]
    ak = onek_ref[...]
    for r, piece in enumerate(_split3(c * LOG2E)):
        aq = aq + _dot(piece, eq_ref[r])
        ak = ak - _dot(piece, ek_ref[r])
    aq_ref[0] = aq.astype(BF16)
    ak_ref[0] = ak.astype(BF16)


def _fox_aug(small, tc):
    b, t, _ = small.shape
    tril = jnp.asarray(np.tril(np.ones((tc, tc), np.float32)), BF16)
    eq = np.zeros((3, LANES, LANES), np.float32)
    ek = np.zeros((3, LANES, LANES), np.float32)
    oneq = np.zeros((1, LANES), np.float32)
    onek = np.zeros((1, LANES), np.float32)
    for h in range(FOX_HEADS):
        for r in range(3):
            eq[r, h, AUG_PER_HEAD * h + r] = 1.0
            ek[r, h, AUG_PER_HEAD * h + 3 + r] = 1.0
            oneq[0, AUG_PER_HEAD * h + 3 + r] = 1.0
            onek[0, AUG_PER_HEAD * h + r] = 1.0
    const2 = lambda shape: pl.BlockSpec(shape, lambda bi, i: (0,) * len(shape))
    blk = pl.BlockSpec((1, tc, LANES), lambda bi, i: (bi, i, 0))
    return pl.pallas_call(
        _fox_aug_kernel,
        grid=(b, t // tc),
        in_specs=[blk, const2((tc, tc)), const2((3, LANES, LANES)), const2((3, LANES, LANES)),
                  const2((1, LANES)), const2((1, LANES))],
        out_specs=[blk, blk],
        out_shape=[jax.ShapeDtypeStruct((b, t, LANES), BF16)] * 2,
        scratch_shapes=[pltpu.VMEM((1, LANES), F32)],
        compiler_params=_cparams(("parallel", "arbitrary")),
        name="fox_aug",
    )(small, tril, jnp.asarray(eq, BF16), jnp.asarray(ek, BF16), jnp.asarray(oneq), jnp.asarray(onek))


def _fox_kernel(q_ref, aq_ref, k_ref, ak_ref, vt_ref, *rest, i, tq, tk):
    o_ref, st_ref, acc_ref = rest[-3:]
    pair = pl.program_id(1)
    row = lax.broadcasted_iota(jnp.int32, (LANES, 1), 0)
    qt = _transposed(q_ref[0])
    qat = _transposed(aq_ref[0])
    qcats = []
    for hh in range(2):
        head = 2 * pair + hh
        in_head = (row >= AUG_PER_HEAD * head) & (row < AUG_PER_HEAD * (head + 1))
        qcats.append(jnp.concatenate([_half_mask_t(qt, hh), jnp.where(in_head, qat, jnp.zeros_like(qat))], axis=0))

    def scores(j):
        ks = j * tk
        kcat = jnp.concatenate([k_ref[0, pl.ds(ks, tk), :], ak_ref[0, pl.ds(ks, tk), :]], axis=1)
        return tuple(_dot(kcat, qcats[hh]) for hh in range(2))

    def mask(j, st):
        kpos, qpos = _positions_t(j, tk, i * tq, tq)
        return jnp.where(kpos <= qpos, st, NEG)

    def values(hh, j):
        return vt_ref[0, hh * HEAD_DIM:(hh + 1) * HEAD_DIM, pl.ds(j * tk, tk)]

    res = _stream(0, i * (tq // (2 * tk)), tq // tk, scores, mask, values, st_ref, acc_ref)
    ot = jnp.concatenate([acc / l for l, acc in res], axis=0)
    o_ref[0] = ot.T.astype(BF16)


def _fox_attention(main, aq, ak, vt, tq, tk):
    b, t, _ = main.shape
    in_specs = lambda i: [pl.BlockSpec((1, tq, LANES), lambda bi, p: (bi, i, CH_FQ + p)),
                          pl.BlockSpec((1, tq, LANES), lambda bi, p: (bi, i, 0)),
                          pl.BlockSpec((1, t, LANES), lambda bi, p: (bi, 0, CH_FK + p)),
                          pl.BlockSpec((1, t, LANES), lambda bi, p: (bi, 0, 0)),
                          pl.BlockSpec((1, LANES, t), lambda bi, p: (bi, VT_FV + p, 0))]
    return _per_query_tile(
        partial(_fox_kernel, tq=tq, tk=tk), t // tq, (b, FOX_HEADS // 2), in_specs,
        lambda i: pl.BlockSpec((1, tq, LANES), lambda bi, p: (bi, i, p)),
        jax.ShapeDtypeStruct((b, t, FOX_W), BF16), _stream_scratch(2, HEAD_DIM, tq, tk), "fox_attention",
        (main, aq, main, ak, vt))


def _compress_kernel(xk_ref, xv_ref, w1k_ref, w2k_ref, pk_ref, w1v_ref, w2v_ref, pv_ref, kc_ref, vct_ref):
    half = CMP_STRIDE * HEAD_DIM
    for x_ref, w1_ref, w2_ref, p_ref, o_ref in ((xk_ref, w1k_ref, w2k_ref, pk_ref, kc_ref),
                                                 (xv_ref, w1v_ref, w2v_ref, pv_ref, vct_ref)):
        w1 = w1_ref[...]
        pos_h = _dot(p_ref[...], w1)
        out = None
        for g in range(NSA_KV_GROUPS):
            x = x_ref[0, g]
            first = _dot(x, w1[:half])
            second = _dot(x, w1[half:])
            nrow = first.shape[0]
            hid = first + pltpu.roll(second, nrow - 1, 0) + pos_h
            a = jax.nn.gelu(hid, approximate=True).astype(BF16)
            y = _dot(a, w2_ref[g])
            out = y if out is None else out + y
        o_ref[0] = (out.T if o_ref is vct_ref else out).astype(BF16)


def _compress(xk, xv, w1k, w2k, pk, w1v, w2v, pv):
    b, g, nchunk, width = xk.shape
    xspec = pl.BlockSpec((1, g, nchunk, width), lambda bi: (bi, 0, 0, 0))
    const = lambda shape: pl.BlockSpec(shape, lambda bi: (0,) * len(shape))
    wspecs = [const((width * 2, CMP_HIDDEN)), const((g, CMP_HIDDEN, LANES)), const((1, width * 2))]
    return pl.pallas_call(
        _compress_kernel,
        grid=(b,),
        in_specs=[xspec, xspec] + wspecs + wspecs,
        out_specs=[pl.BlockSpec((1, nchunk, LANES), lambda bi: (bi, 0, 0)),
                   pl.BlockSpec((1, LANES, nchunk), lambda bi: (bi, 0, 0))],
        out_shape=[jax.ShapeDtypeStruct((b, nchunk, LANES), BF16), jax.ShapeDtypeStruct((b, LANES, nchunk), BF16)],
        compiler_params=_cparams(("parallel",)),
        name="nsa_compress",
    )(xk, xv, w1k, w2k, pk, w1v, w2v, pv)


def _gate_col(small, head, branch):
    idx = FOX_HEADS + 3 * head + branch
    return jnp.sum(jnp.where(_lane_iota() == idx, small, 0.0), axis=-1, keepdims=True)


def _nsa_select_kernel(q_ref, kc_ref, vct_ref, ovt_ref, small_ref, oc_ref, sel_ref, *, tq, ns, n_sel):
    q0 = pl.program_id(1) * tq
    lane = _lane_iota()
    kc = kc_ref[0]
    vct = vct_ref[0]
    ncp = kc.shape[0]
    small = small_ref[0]
    qpos = q0 + lax.broadcasted_iota(jnp.int32, (1, tq), 1)
    cmp_end = lax.broadcasted_iota(jnp.int32, (ncp, 1), 0) * CMP_STRIDE + (CMP_BLOCK - 1)
    cmask = cmp_end <= qpos
    psum = [jnp.zeros((ncp, tq), F32) for _ in range(NSA_KV_GROUPS)]
    logits = [[_dot_nt(kc, _half_mask(q_ref[0, :, n * LANES:(n + 1) * LANES], g)) for g in range(NSA_KV_GROUPS)]
              for n in range(NSA_HPG)]
    for n in range(NSA_HPG):
        ots = []
        for g in range(NSA_KV_GROUPS):
            z = jnp.where(cmask, logits[n][g], -jnp.inf)
            m = jnp.max(z, axis=0, keepdims=True)
            m = jnp.where(m == -jnp.inf, 0.0, m)
            p = jnp.exp2(z - m)
            p = p / jnp.maximum(jnp.sum(p, axis=0, keepdims=True), 1e-30)
            psum[g] = psum[g] + p
            ots.append(_dot(vct[g * HEAD_DIM:(g + 1) * HEAD_DIM], p.astype(BF16)))
        gate = jnp.where(lane < HEAD_DIM, _gate_col(small, n, 0), _gate_col(small, NSA_HPG + n, 0))
        oc_ref[0, :, n * LANES:(n + 1) * LANES] = gate * jnp.concatenate(ots, axis=0).T

    nsp = ovt_ref.shape[0]
    blk = lax.broadcasted_iota(jnp.int32, (nsp, 1), 0)
    cur = qpos // SLC_BLOCK
    valid = blk * SLC_BLOCK <= qpos
    forced = (blk == 0) | (blk == cur) | (blk == cur - 1)
    scores = []
    for g in range(NSA_KV_GROUPS):
        imp = jnp.zeros((nsp, tq), F32)
        for piece in _split3(psum[g]):
            imp = imp + _dot(ovt_ref[...], piece)
        scores.append(jnp.where(valid, jnp.where(forced, jnp.inf, imp), -jnp.inf))
    ranks = [jnp.zeros((nsp, tq), jnp.int32) for _ in range(NSA_KV_GROUPS)]
    for i in range(ns):
        for g in range(NSA_KV_GROUPS):
            row = scores[g][i:i + 1, :]
            ahead = (row > scores[g]) | ((row == scores[g]) & (blk > i))
            ranks[g] = ranks[g] + ahead.astype(jnp.int32)
    masks = [jnp.where((ranks[g] < n_sel) & (blk < ns), 0.0, NEG) for g in range(NSA_KV_GROUPS)]
    sel_ref[0] = jnp.concatenate(masks, axis=0).T.astype(BF16)


def _nsa_select(main, kc, vct, overlap_t, small, tq, ns, n_sel):
    b, t, _ = main.shape
    ncp = kc.shape[1]
    nsp = overlap_t.shape[0]
    return pl.pallas_call(
        partial(_nsa_select_kernel, tq=tq, ns=ns, n_sel=n_sel),
        grid=(b, t // tq),
        in_specs=[pl.BlockSpec((1, tq, NSA_W), lambda bi, i: (bi, i, CH_NQ * LANES // NSA_W)),
                  pl.BlockSpec((1, ncp, LANES), lambda bi, i: (bi, 0, 0)),
                  pl.BlockSpec((1, LANES, ncp), lambda bi, i: (bi, 0, 0)),
                  pl.BlockSpec((nsp, ncp), lambda bi, i: (0, 0)),
                  pl.BlockSpec((1, tq, LANES), lambda bi, i: (bi, i, 0))],
        out_specs=[pl.BlockSpec((1, tq, NSA_W), lambda bi, i: (bi, i, 0)),
                   pl.BlockSpec((1, tq, NSA_KV_GROUPS * nsp), lambda bi, i: (bi, i, 0))],
        out_shape=[jax.ShapeDtypeStruct((b, t, NSA_W), F32),
                   jax.ShapeDtypeStruct((b, t, NSA_KV_GROUPS * nsp), BF16)],
        compiler_params=_cparams(("parallel", "parallel")),
        name="nsa_select",
    )(main, kc, vct, overlap_t, small)


def _nsa_flash_kernel(q_ref, sel_ref, oc_ref, small_ref, ks_ref, kw_ref, hot_ref, vst_ref, vwt_ref, *rest,
                      i, tq, tk):
    o_ref, st_ref, acc_ref = rest[-3:]
    n = pl.program_id(1)
    q0 = i * tq
    q2 = q_ref[0]
    small = small_ref[0]
    qt = _transposed(q2)
    selt = _transposed(sel_ref[0])
    qhs = [_half_mask_t(qt, g) for g in range(NSA_KV_GROUPS)]
    qcats = [jnp.concatenate([qhs[g], _half_mask_t(selt, g)], axis=0) for g in range(NSA_KV_GROUPS)]
    rows = lambda g: slice(g * HEAD_DIM, (g + 1) * HEAD_DIM)

    def sel_scores(j):
        ks = j * tk
        kcat = jnp.concatenate([ks_ref[0, pl.ds(ks, tk), :], hot_ref[pl.ds(ks, tk), :]], axis=1)
        return tuple(_dot(kcat, qcats[g]) for g in range(NSA_KV_GROUPS))

    def sel_mask(j, st):
        kpos, qpos = _positions_t(j, tk, q0, tq)
        return jnp.where(kpos <= qpos, st, NEG)

    def win_scores(j):
        kw = kw_ref[0, pl.ds(j * tk, tk), :]
        return tuple(_dot(kw, qhs[g]) for g in range(NSA_KV_GROUPS))

    def win_mask(j, st):
        kpos, qpos = _positions_t(j, tk, q0, tq)
        return jnp.where((kpos <= qpos) & (kpos > qpos - WINDOW), st, NEG)

    values = lambda ref: lambda g, j: ref[0, rows(g), pl.ds(j * tk, tk)]
    lane = _lane_iota()
    n_end = (i + 1) * (tq // tk)
    win_lo = max(n_end - (WINDOW + tq) // tk, 0)
    o = oc_ref[0]
    for branch, args in ((1, (0, i * (tq // (2 * tk)), tq // tk, sel_scores, sel_mask, values(vst_ref))),
                         (2, (win_lo, 0, n_end - win_lo, win_scores, win_mask, values(vwt_ref)))):
        res = _stream(*args, st_ref, acc_ref)
        ot = jnp.concatenate([acc / l for l, acc in res], axis=0)
        gate = jnp.where(lane < HEAD_DIM, _gate_col(small, n, branch), _gate_col(small, NSA_HPG + n, branch))
        o = o + gate * ot.T
    o_ref[0] = o.astype(BF16)


def _nsa_flash(main, sel, oc, small, hot, vt, tq, tk):
    b, t, _ = main.shape
    tile = lambda i, ch: pl.BlockSpec((1, tq, LANES), lambda bi, n: (bi, i, ch + n))
    shared = lambda i: pl.BlockSpec((1, tq, LANES), lambda bi, n: (bi, i, 0))
    full = lambda ch: pl.BlockSpec((1, t, LANES), lambda bi, n: (bi, 0, ch))
    vfull = lambda ch: pl.BlockSpec((1, LANES, t), lambda bi, n: (bi, ch, 0))
    in_specs = lambda i: [tile(i, CH_NQ), shared(i), tile(i, 0), shared(i), full(CH_KS), full(CH_KW),
                          pl.BlockSpec((t, LANES), lambda bi, n: (0, 0)), vfull(VT_VS), vfull(VT_VW)]
    return _per_query_tile(
        partial(_nsa_flash_kernel, tq=tq, tk=tk), t // tq, (b, NSA_HPG), in_specs, lambda i: tile(i, 0),
        jax.ShapeDtypeStruct((b, t, NSA_W), BF16), _stream_scratch(NSA_KV_GROUPS, HEAD_DIM, tq, tk), "nsa_flash",
        (main, sel, oc, small, main, main, hot, vt, vt))


def _diff_kernel(q_ref, k_ref, vt_ref, lam_ref, g_ref, *rest, i, tq, tk, lam_init):
    o_ref, st_ref, acc_ref = rest[-3:]
    q2 = q_ref[0]
    lp = lam_ref[...]
    lam = (jnp.exp(jnp.sum(lp[0:1] * lp[1:2], axis=-1, keepdims=True))
           - jnp.exp(jnp.sum(lp[2:3] * lp[3:4], axis=-1, keepdims=True)) + lam_init)
    qt = _transposed(q2)
    qhs = [_half_mask_t(qt, comp) for comp in range(2)]

    def scores(j):
        k2 = k_ref[0, pl.ds(j * tk, tk), :]
        return tuple(_dot(k2, qhs[comp]) for comp in range(2))

    def mask(j, st):
        kpos, qpos = _positions_t(j, tk, i * tq, tq)
        return jnp.where(kpos <= qpos, st, NEG)

    def values(comp, j):
        return vt_ref[0, :, pl.ds(j * tk, tk)]

    (l1, acc1), (l2, acc2) = _stream(0, i * (tq // (2 * tk)), tq // tk, scores, mask, values, st_ref, acc_ref)
    o = (acc1 / l1 - lam * (acc2 / l2)).T
    o_ref[0] = (_rms(o, g_ref[...]) * (1.0 - lam_init)).astype(BF16)


def _diff_attention(main, vt, lam_p, subln_g, tq, tk, lam_init):
    b, t, _ = main.shape
    nh = DIFF_HEADS
    in_specs = lambda i: [pl.BlockSpec((1, tq, LANES), lambda bi, h: (bi, i, h)),
                          pl.BlockSpec((1, t, LANES), lambda bi, h: (bi, 0, nh + h)),
                          pl.BlockSpec((1, LANES, t), lambda bi, h: (bi, h, 0)),
                          pl.BlockSpec((4, LANES), lambda bi, h: (0, 0)),
                          pl.BlockSpec((1, LANES), lambda bi, h: (0, 0))]
    return _per_query_tile(
        partial(_diff_kernel, tq=tq, tk=tk, lam_init=lam_init), t // tq, (b, nh), in_specs,
        lambda i: pl.BlockSpec((1, tq, LANES), lambda bi, h: (bi, i, h)),
        jax.ShapeDtypeStruct((b, t, nh * LANES), BF16), _stream_scratch(2, LANES, tq, tk), "diff_attention",
        (main, main, vt, lam_p, subln_g))


def _mem_kv_kernel(mem_ref, g_ref, wk_ref, wv_ref, k_ref, vt_ref):
    mn = _rms(mem_ref[0], g_ref[0]).astype(BF16)
    k_ref[0, 0] = _dot(mn, wk_ref[0]).astype(BF16)
    vt_ref[0, 0] = _dot(mn, wv_ref[0]).T.astype(BF16)


def _mem_kv(mem, mem_norm_g, wk, wv):
    depth = wk.shape[0]
    b, m, d = mem.shape
    wspec = pl.BlockSpec((1, d, MEM_W), lambda l, bi: (l, 0, 0))
    return pl.pallas_call(
        _mem_kv_kernel,
        grid=(depth, b),
        in_specs=[pl.BlockSpec((1, m, d), lambda l, bi: (bi, 0, 0)),
                  pl.BlockSpec((1, 1, d), lambda l, bi: (l, 0, 0)), wspec, wspec],
        out_specs=[pl.BlockSpec((1, 1, m, MEM_W), lambda l, bi: (l, bi, 0, 0)),
                   pl.BlockSpec((1, 1, MEM_W, m), lambda l, bi: (l, bi, 0, 0))],
        out_shape=[jax.ShapeDtypeStruct((depth, b, m, MEM_W), BF16),
                   jax.ShapeDtypeStruct((depth, b, MEM_W, m), BF16)],
        compiler_params=_cparams(("parallel", "parallel")),
        name="mem_kv",
    )(mem, mem_norm_g.reshape(depth, 1, d), wk, wv)


def _post_mixer_kernel(*refs, n_in, n_part):
    a_refs, w_refs = refs[:n_in], refs[n_in:2 * n_in]
    x_ref, g1_ref, g2_ref, g3_ref, wq_ref, k_ref, vt_ref, wo_ref, o_ref = refs[2 * n_in:]
    rows = x_ref.shape[0] // n_part
    parts = [slice(r * rows, (r + 1) * rows) for r in range(n_part)]
    n_mem = k_ref.shape[2]
    heads = [(ch, hh) for ch in range(MEM_W // LANES) for hh in range(2)]
    ones = jnp.ones((ONES_ROWS, n_mem), BF16)

    def mixer_out(ps):
        y = None
        for a_ref, w_ref in zip(a_refs, w_refs):
            t = _dot(a_ref[ps, :], w_ref[...])
            y = t if y is None else y + t
        return y

    def scores(q):
        return [_dot_nt(k_ref[0, 0, :, ch * LANES:(ch + 1) * LANES], _half_mask(q[:, ch * LANES:(ch + 1) * LANES], hh))
                for ch, hh in heads]

    def attend(logits):
        outs = []
        for (ch, hh), st in zip(heads, logits):
            p = jnp.exp2(st - jnp.max(st, axis=0, keepdims=True)).astype(BF16)
            r0 = ch * LANES + hh * HEAD_DIM
            pv = _dot(jnp.concatenate([vt_ref[0, 0, r0:r0 + HEAD_DIM, :], ones], axis=0), p)
            outs.append(pv[:HEAD_DIM] / pv[HEAD_DIM:HEAD_DIM + 1])
        return jnp.concatenate(outs, axis=0).T.astype(BF16)

    ys = [mixer_out(ps) for ps in parts]
    xs = [x_ref[ps, :] + _rms(y, g1_ref[...]) for ps, y in zip(parts, ys)]
    qs = [(_dot(_rms(x, g2_ref[...]).astype(BF16), wq_ref[...]) * Q_SCALE_LOG2).astype(BF16) for x in xs]
    logits = [scores(q) for q in qs]
    os_ = [attend(lg) for lg in logits]
    for ps, x, o in zip(parts, xs, os_):
        o_ref[ps, :] = x + _rms(_dot(o, wo_ref[...]), g3_ref[...])


def _post_mixer(acts, weights, x2, g1, g2, g3, wq, mem_k, mem_vt, wo, layer, tm, rows_per_batch, n_part=2):
    n = x2.shape[0]
    m = mem_k.shape[2]
    const = lambda shape: pl.BlockSpec(shape, lambda i: (0, 0))
    row = pl.BlockSpec((tm, D_MODEL), lambda i: (i, 0))
    gspec = const((1, D_MODEL))
    return pl.pallas_call(
        partial(_post_mixer_kernel, n_in=len(acts), n_part=n_part),
        grid=(n // tm,),
        in_specs=[pl.BlockSpec((tm, a.shape[1]), lambda i: (i, 0)) for a in acts]
                 + [const(w.shape) for w in weights]
                 + [row, gspec, gspec, gspec, const((D_MODEL, MEM_W)),
                    pl.BlockSpec((1, 1, m, MEM_W), lambda i: (layer, i // rows_per_batch, 0, 0)),
                    pl.BlockSpec((1, 1, MEM_W, m), lambda i: (layer, i // rows_per_batch, 0, 0)),
                    const((MEM_W, D_MODEL))],
        out_specs=row,
        out_shape=jax.ShapeDtypeStruct((n, D_MODEL), F32),
        compiler_params=_cparams(("parallel",)),
        name="post_mixer",
    )(*acts, *weights, x2, g1, g2, g3, wq, mem_k, mem_vt, wo)


def _ffn_kernel(x_ref, gin_ref, gout_ref, wg_ref, wu_ref, wd_ref, o_ref, *, chunk):
    x = x_ref[...]
    h = _rms(x, gin_ref[...]).astype(BF16)
    d_ff = wg_ref.shape[1]
    y = None
    for c0 in range(0, d_ff, chunk):
        gate = _dot(h, wg_ref[:, c0:c0 + chunk])
        up = _dot(h, wu_ref[:, c0:c0 + chunk])
        a = (gate * (1.0 / (1.0 + jnp.exp(-gate))) * up).astype(BF16)
        t = _dot(a, wd_ref[c0:c0 + chunk, :])
        y = t if y is None else y + t
    o_ref[...] = x + _rms(y, gout_ref[...])


def _ffn(x2, gin, gout, wg, wu, wd, tm, chunk):
    n = x2.shape[0]
    d_ff = wg.shape[1]
    const = lambda shape: pl.BlockSpec(shape, lambda i: (0, 0), pipeline_mode=pl.Buffered(1))
    return pl.pallas_call(
        partial(_ffn_kernel, chunk=chunk),
        grid=(n // tm,),
        in_specs=[pl.BlockSpec((tm, D_MODEL), lambda i: (i, 0)), const((1, D_MODEL)), const((1, D_MODEL)),
                  const((D_MODEL, d_ff)), const((D_MODEL, d_ff)), const((d_ff, D_MODEL))],
        out_specs=pl.BlockSpec((tm, D_MODEL), lambda i: (i, 0)),
        out_shape=jax.ShapeDtypeStruct((n, D_MODEL), F32),
        compiler_params=_cparams(("parallel",)),
        name="ffn",
    )(x2, gin, gout, wg, wu, wd)


def _even_weights(w_in, w_out):
    kvw = NSA_KV_GROUPS * HEAD_DIM
    offs = [int(o) for o in np.cumsum((FOX_W, FOX_W, FOX_W, FOX_HEADS, NSA_W) + (kvw,) * 6 + (3 * NSA_HEADS,))]
    fk0, fv0, fl0, nq0, kc0, vc0, ks0, vs0, kw0, vw0, gl0, end = offs
    perm = np.concatenate([np.arange(HEAD_DIM) + HEAD_DIM * (g * NSA_HPG + n)
                           for n in range(NSA_HPG) for g in range(NSA_KV_GROUPS)])
    cols = lambda a, b: w_in[:, a:b]
    w = jnp.concatenate([cols(0, fv0), w_in[:, nq0 + perm], cols(kc0, vc0), cols(vc0, ks0), cols(ks0, vs0),
                         cols(kw0, vw0), cols(fv0, fl0), cols(vs0, kw0), cols(vw0, gl0)], axis=1).astype(BF16)
    w_small = jnp.concatenate([cols(fl0, nq0), cols(gl0, end),
                               jnp.zeros((D_MODEL, LANES - FOX_HEADS - 3 * NSA_HEADS), w_in.dtype)],
                              axis=1).astype(BF16)
    w_out_fox = w_out[:FOX_W].astype(BF16)
    w_out_nsa = w_out[FOX_W + perm].astype(BF16)
    return w, w_small, w_out_fox, w_out_nsa


def _overlap_matrix_t(t, ncp):
    nc = (t - CMP_BLOCK) // CMP_STRIDE + 1
    ns = t // SLC_BLOCK
    cs = np.arange(nc) * CMP_STRIDE
    ss = np.arange(ns) * SLC_BLOCK
    ov = np.clip(np.minimum(cs[:, None] + CMP_BLOCK, ss[None, :] + SLC_BLOCK)
                 - np.maximum(cs[:, None], ss[None, :]), 0, None) / CMP_BLOCK
    full = np.zeros((LANES // NSA_KV_GROUPS, ncp), np.float32)
    full[:ns, :nc] = ov.T
    return jnp.asarray(full, BF16), ns


def _stride_chunks(main, ch):
    b, t, _ = main.shape
    x = main[:, :, ch * LANES:(ch + 1) * LANES].reshape(b, t // CMP_STRIDE, CMP_STRIDE, NSA_KV_GROUPS, HEAD_DIM)
    return jnp.transpose(x, (0, 3, 1, 2, 4)).reshape(b, NSA_KV_GROUPS, t // CMP_STRIDE, CMP_STRIDE * HEAD_DIM)


def _pad_w2(w2):
    out = jnp.zeros((NSA_KV_GROUPS, CMP_HIDDEN, LANES), BF16)
    for g in range(NSA_KV_GROUPS):
        out = out.at[g, :, g * HEAD_DIM:(g + 1) * HEAD_DIM].set(w2.astype(BF16))
    return out


def kernel(x, mem, positions, sandwich_g, mem_norm_g, ev_w_in, ev_fox_fbias, ev_cmp_pos_k, ev_cmp_w1_k, ev_cmp_w2_k, ev_cmp_pos_v, ev_cmp_w1_v, ev_cmp_w2_v, ev_w_out, od_w_in, od_lambda, od_subln_g, od_w_out, ca_wq, ca_wk, ca_wv, ca_wo, ffn_wg, ffn_wu, ffn_wd):
    b, t, d = x.shape
    depth = sandwich_g.shape[0]
    n = b * t
    tm = 256
    tm_ffn = 512
    tq, tk = 512, 256
    assert d == D_MODEL and t % tq == 0 and tq % (2 * tk) == 0 and WINDOW % tk == 0 and t % tm == 0 and n % tm_ffn == 0

    tabs = tuple(a.reshape(b, t, LANES) for a in _rope_tables(positions, 512))
    mem_k, mem_vt = _mem_kv(mem, mem_norm_g, ca_wk.astype(BF16), ca_wv.astype(BF16))
    ncp = t // CMP_STRIDE
    overlap_t, ns = _overlap_matrix_t(t, ncp)
    assert ns <= overlap_t.shape[0]
    n_sel = min(SLC_TOPK, ns)
    hot = jnp.asarray((np.arange(t)[:, None] // SLC_BLOCK == np.arange(LANES)[None, :] % overlap_t.shape[0])
                      .astype(np.float32), BF16)
    gain = lambda l, j: sandwich_g[l, j].reshape(1, d)

    x2 = x.reshape(n, d)
    for layer in range(depth):
        x3 = x2.reshape(b, t, d)
        if layer % 2 == 0:
            e = layer // 2
            w, w_small, w_out_fox, w_out_nsa = _even_weights(ev_w_in[e], ev_w_out[e])
            fb_row = jnp.zeros((1, LANES), F32).at[0, :FOX_HEADS].set(ev_fox_fbias[e].astype(F32))
            main, vt, small = _even_proj(x3, gain(layer, 0), w, w_small, fb_row, tabs, tm)
            aq, ak = _fox_aug(small, 256)
            o_fox = _fox_attention(main, aq, ak, vt, tq, tk)
            kc, vct = _compress(
                _stride_chunks(main, CH_KC), _stride_chunks(main, CH_VC),
                ev_cmp_w1_k[e].astype(BF16), _pad_w2(ev_cmp_w2_k[e]), ev_cmp_pos_k[e].reshape(1, -1).astype(BF16),
                ev_cmp_w1_v[e].astype(BF16), _pad_w2(ev_cmp_w2_v[e]), ev_cmp_pos_v[e].reshape(1, -1).astype(BF16))
            oc, sel = _nsa_select(main, kc, vct, overlap_t, small, 256, ns, n_sel)
            o_nsa = _nsa_flash(main, sel, oc, small, hot, vt, tq, tk)
            acts, w_outs = [o_fox.reshape(n, FOX_W), o_nsa.reshape(n, NSA_W)], [w_out_fox, w_out_nsa]
        else:
            o = layer // 2
            w_in = od_w_in[o].astype(BF16)
            main, vt = _odd_proj(x3, gain(layer, 0), w_in, tabs, tm)
            lam_init = 0.8 - 0.6 * math.exp(-0.3 * layer)
            lam_p = jnp.pad(od_lambda[o].astype(F32), ((0, 0), (0, LANES - HEAD_DIM)))
            attn = _diff_attention(main, vt, lam_p, od_subln_g[o].reshape(1, LANES).astype(F32), tq, tk, lam_init)
            acts, w_outs = [attn.reshape(n, D_MODEL)], [od_w_out[o].astype(BF16)]
        x2 = _post_mixer(acts, w_outs, x2, gain(layer, 1), gain(layer, 2), gain(layer, 3),
                         ca_wq[layer].astype(BF16), mem_k, mem_vt, ca_wo[layer].astype(BF16), layer,
                         tm_ffn, t // tm_ffn)
        x2 = _ffn(x2, gain(layer, 4), gain(layer, 5), ffn_wg[layer].astype(BF16), ffn_wu[layer].astype(BF16),
                  ffn_wd[layer].astype(BF16), tm_ffn, 256)
    return x2.reshape(b, t, d)
```

```python
import math
from functools import partial

import numpy as np
import jax
import jax.numpy as jnp
from jax import lax
from jax.experimental import pallas as pl
from jax.experimental.pallas import tpu as pltpu

F32 = jnp.float32
BF16 = jnp.bfloat16

D_MODEL = 1024
HEAD_DIM = 64
LANES = 128
ROPE_DIM = HEAD_DIM // 4
ROPE_THETA = 500000.0
FOX_HEADS = 8
NSA_HEADS = 8
NSA_KV_GROUPS = 2
NSA_HPG = NSA_HEADS // NSA_KV_GROUPS
CMP_BLOCK = 32
CMP_STRIDE = 16
CMP_HIDDEN = 2 * HEAD_DIM
SLC_BLOCK = 64
SLC_TOPK = 16
WINDOW = 512
DIFF_HEADS = 8
MEM_HEADS = 4
MEM_W = MEM_HEADS * HEAD_DIM
RMS_EPS = 1e-6
Q_SCALE = HEAD_DIM ** -0.5
LOG2E = math.log2(math.e)
Q_SCALE_LOG2 = Q_SCALE * LOG2E
NEG = -1e30
AUG_PER_HEAD = 6
ONES_ROWS = 16

FOX_W = FOX_HEADS * HEAD_DIM
NSA_W = NSA_HEADS * HEAD_DIM
CH_FQ, CH_FK, CH_NQ, CH_KC, CH_VC, CH_KS, CH_KW = 0, 4, 8, 12, 13, 14, 15
EV_MAIN = 16 * LANES
EV_ROPE_CHUNKS = tuple(range(CH_NQ, CH_NQ + 4)) + (CH_KC, CH_KS, CH_KW)
EV_QSCALE_CHUNKS = tuple(range(CH_FQ, CH_FQ + 4)) + tuple(range(CH_NQ, CH_NQ + 4))
VT_FV, VT_VS, VT_VW = 0, 4, 5
EV_VT = 6 * LANES
OD_MAIN = 2 * D_MODEL
OD_VT = D_MODEL

VMEM_LIMIT = 56 * 1024 * 1024


def _cparams(sem):
    return pltpu.CompilerParams(dimension_semantics=sem, vmem_limit_bytes=VMEM_LIMIT)


def _rms(x, g):
    return x * lax.rsqrt(jnp.mean(x * x, axis=-1, keepdims=True) + RMS_EPS) * g


def _split3(x):
    hi = x.astype(BF16)
    r1 = x - hi.astype(F32)
    mid = r1.astype(BF16)
    lo = (r1 - mid.astype(F32)).astype(BF16)
    return hi, mid, lo


def _dot(a, b):
    return jnp.dot(a, b, preferred_element_type=F32)


def _dot_nt(a, b):
    return lax.dot_general(a, b, (((1,), (1,)), ((), ())), preferred_element_type=F32)


def _lane_iota(n=LANES):
    return lax.broadcasted_iota(jnp.int32, (1, n), 1)


def _half_mask(q2, half):
    return jnp.where(_lane_iota() // HEAD_DIM == half, q2, jnp.zeros_like(q2))


def _transposed(x):
    return x.astype(F32).T.astype(BF16)


def _half_mask_t(qt, half):
    row = lax.broadcasted_iota(jnp.int32, (qt.shape[0], 1), 0)
    return jnp.where(row // HEAD_DIM == half, qt, jnp.zeros_like(qt))


def _positions_t(j, tk, q0, tq):
    kpos = j * tk + lax.broadcasted_iota(jnp.int32, (tk, 1), 0)
    qpos = q0 + lax.broadcasted_iota(jnp.int32, (1, tq), 1)
    return kpos, qpos


def _stream(base, n_full, n_tail, scores, mask, values, st_ref, acc_ref):
    n_chain, dv, tq = acc_ref.shape

    def park(j, slot):
        for c, st in enumerate(scores(j)):
            st_ref[slot, c] = st

    def step(j, slot, stats, masked, prefetch):
        if prefetch:
            park(j + 1, 1 - slot)
        parts = []
        for c in range(n_chain):
            st = st_ref[slot, c]
            if masked:
                st = mask(j, st)
            m, _ = stats[c]
            m_new = jnp.maximum(m, jnp.max(st, axis=0, keepdims=True))
            p = jnp.exp2(st - m_new).astype(BF16)
            vt1 = jnp.concatenate([values(c, j), jnp.ones((ONES_ROWS, p.shape[0]), BF16)], axis=0)
            parts.append((m_new, jnp.exp2(m - m_new), _dot(vt1, p)))
        out = []
        for c, (m_new, alpha, pv) in enumerate(parts):
            acc_ref[c] = alpha * acc_ref[c] + pv[:dv]
            out.append((m_new, alpha * stats[c][1] + pv[dv:dv + 1]))
        return tuple(out)

    acc_ref[...] = jnp.zeros_like(acc_ref)
    park(base, 0)
    stats = ((jnp.full((1, tq), NEG, F32), jnp.zeros((1, tq), F32)),) * n_chain
    n_tiles = n_full + n_tail
    for s_ in range(n_tiles):
        stats = step(base + s_, s_ % 2, stats, s_ >= n_full, s_ + 1 < n_tiles)
    return [(l, acc_ref[c]) for c, (_, l) in enumerate(stats)]


def _per_query_tile(kernel_fn, n_q, grid, in_specs_fn, out_spec_fn, out_shape, scratch, name, args):
    out = None
    for i in range(n_q):
        in_specs, call_args, aliases = list(in_specs_fn(i)), list(args), {}
        if out is not None:
            in_specs.append(pl.BlockSpec(memory_space=pl.ANY))
            call_args.append(out)
            aliases = {len(call_args) - 1: 0}
        out = pl.pallas_call(
            partial(kernel_fn, i=i), grid=grid, in_specs=in_specs, out_specs=out_spec_fn(i), out_shape=out_shape,
            scratch_shapes=scratch, input_output_aliases=aliases,
            compiler_params=_cparams(("parallel",) * len(grid)), name=f"{name}_q{i}",
        )(*call_args)
    return out


def _stream_scratch(n_chain, dv, tq, tk):
    return [pltpu.VMEM((2, n_chain, tk, tq), F32), pltpu.VMEM((n_chain, dv, tq), F32)]


def _rope_kernel(pos_ref, inv_ref, m1_ref, m2_ref, c_ref, s1_ref, s2_ref):
    ang = pos_ref[...].astype(F32) * inv_ref[...]
    c_ref[...] = jnp.cos(ang)
    sn = jnp.sin(ang)
    s1_ref[...] = -sn * m1_ref[...]
    s2_ref[...] = sn * m2_ref[...]


def _rope_tables(positions, tm):
    n = positions.size
    inv = ROPE_THETA ** (-jnp.arange(0, ROPE_DIM, 2, dtype=F32) / ROPE_DIM)
    lane = np.arange(LANES) % HEAD_DIM
    half = ROPE_DIM // 2
    inv_l = jnp.where(lane < ROPE_DIM, inv[lane % half], 0.0).reshape(1, LANES).astype(F32)
    m1 = jnp.asarray((lane < half).astype(np.float32)).reshape(1, LANES)
    m2 = jnp.asarray(((lane >= half) & (lane < ROPE_DIM)).astype(np.float32)).reshape(1, LANES)
    row = pl.BlockSpec((1, LANES), lambda i: (0, 0))
    tab = pl.BlockSpec((tm, LANES), lambda i: (i, 0))
    return pl.pallas_call(
        _rope_kernel,
        grid=(n // tm,),
        in_specs=[pl.BlockSpec((tm, 1), lambda i: (i, 0)), row, row, row],
        out_specs=[tab, tab, tab],
        out_shape=[jax.ShapeDtypeStruct((n, LANES), F32)] * 3,
        compiler_params=_cparams(("parallel",)),
        name="rope_tables",
    )(positions.reshape(n, 1), inv_l, m1, m2)


def _apply_rope(y, c, s1, s2):
    half = ROPE_DIM // 2
    return y * c + pltpu.roll(y, LANES - half, 1) * s1 + pltpu.roll(y, half, 1) * s2


def _project_chunks(h, w_ref, tabs, main_ref, vt_ref, n_main, n_vt, rope_chunks, qscale_chunks):
    c, s1, s2 = tabs
    wide = 2 * LANES
    for ch2 in range((n_main + n_vt) // 2):
        y2 = _dot(h, w_ref[:, ch2 * wide:(ch2 + 1) * wide])
        for ch in (2 * ch2, 2 * ch2 + 1):
            y = y2[:, (ch % 2) * LANES:(ch % 2 + 1) * LANES]
            if ch >= n_main:
                vt_ref[0, (ch - n_main) * LANES:(ch - n_main + 1) * LANES, :] = y.T.astype(BF16)
                continue
            if ch in rope_chunks:
                y = _apply_rope(y, c, s1, s2)
            if ch in qscale_chunks:
                y = y * Q_SCALE_LOG2
            main_ref[0, :, ch * LANES:(ch + 1) * LANES] = y.astype(BF16)


def _even_proj_kernel(x_ref, g_ref, w_ref, ws_ref, fb_ref, c_ref, s1_ref, s2_ref, main_ref, vt_ref, small_ref):
    h = _rms(x_ref[0], g_ref[...]).astype(BF16)
    _project_chunks(h, w_ref, (c_ref[0], s1_ref[0], s2_ref[0]), main_ref, vt_ref,
                    EV_MAIN // LANES, EV_VT // LANES, EV_ROPE_CHUNKS, EV_QSCALE_CHUNKS)
    ys = _dot(h, ws_ref[...])
    z = ys + fb_ref[...]
    log_f = jnp.minimum(z, 0.0) - jnp.log(1.0 + jnp.exp(-jnp.abs(z)))
    gate = 1.0 / (1.0 + jnp.exp(-ys))
    small_ref[0] = jnp.where(_lane_iota() < FOX_HEADS, log_f, gate)


def _even_proj(x3, g, w, w_small, fb_row, tabs, tm):
    b, t, _ = x3.shape
    const = lambda shape: pl.BlockSpec(shape, lambda bi, i: (0, 0))
    tab = pl.BlockSpec((1, tm, LANES), lambda bi, i: (bi, i, 0))
    return pl.pallas_call(
        _even_proj_kernel,
        grid=(b, t // tm),
        in_specs=[pl.BlockSpec((1, tm, D_MODEL), lambda bi, i: (bi, i, 0)), const((1, D_MODEL)),
                  const((D_MODEL, EV_MAIN + EV_VT)), const((D_MODEL, LANES)), const((1, LANES)), tab, tab, tab],
        out_specs=[pl.BlockSpec((1, tm, EV_MAIN), lambda bi, i: (bi, i, 0)),
                   pl.BlockSpec((1, EV_VT, tm), lambda bi, i: (bi, 0, i)), tab],
        out_shape=[jax.ShapeDtypeStruct((b, t, EV_MAIN), BF16), jax.ShapeDtypeStruct((b, EV_VT, t), BF16),
                   jax.ShapeDtypeStruct((b, t, LANES), F32)],
        compiler_params=_cparams(("parallel", "parallel")),
        name="even_proj",
    )(x3, g, w, w_small, fb_row, *tabs)


def _odd_proj_kernel(x_ref, g_ref, w_ref, c_ref, s1_ref, s2_ref, main_ref, vt_ref):
    h = _rms(x_ref[0], g_ref[...]).astype(BF16)
    n_main = OD_MAIN // LANES
    _project_chunks(h, w_ref, (c_ref[0], s1_ref[0], s2_ref[0]), main_ref, vt_ref,
                    n_main, OD_VT // LANES, tuple(range(n_main)), tuple(range(n_main // 2)))


def _odd_proj(x3, g, w, tabs, tm):
    b, t, _ = x3.shape
    const = lambda shape: pl.BlockSpec(shape, lambda bi, i: (0, 0))
    tab = pl.BlockSpec((1, tm, LANES), lambda bi, i: (bi, i, 0))
    return pl.pallas_call(
        _odd_proj_kernel,
        grid=(b, t // tm),
        in_specs=[pl.BlockSpec((1, tm, D_MODEL), lambda bi, i: (bi, i, 0)), const((1, D_MODEL)),
                  const((D_MODEL, OD_MAIN + OD_VT)), tab, tab, tab],
        out_specs=[pl.BlockSpec((1, tm, OD_MAIN), lambda bi, i: (bi, i, 0)),
                   pl.BlockSpec((1, OD_VT, tm), lambda bi, i: (bi, 0, i))],
        out_shape=[jax.ShapeDtypeStruct((b, t, OD_MAIN), BF16), jax.ShapeDtypeStruct((b, OD_VT, t), BF16)],
        compiler_params=_cparams(("parallel", "parallel")),
        name="odd_proj",
    )(x3, g, w, *tabs)


def _fox_aug_kernel(lf_ref, tril_ref, eq_ref, ek_ref, oneq_ref, onek_ref, aq_ref, ak_ref, carry_ref):
    @pl.when(pl.program_id(1) == 0)
    def _():
        carry_ref[...] = jnp.zeros_like(carry_ref)

    tril = tril_ref[...]
    c = carry_ref[...]
    for piece in _split3(lf_ref[0]):
        c = c + _dot(tril, piece)
    carry_ref[...] = c[-1:, :]
    aq = oneq_ref[...]
    ak = onek_ref[...]
    for r, piece in enumerate(_split3(c * LOG2E)):
        aq = aq + _dot(piece, eq_ref[r])
        ak = ak - _dot(piece, ek_ref[r])
    aq_ref[0] = aq.astype(BF16)
    ak_ref[0] = ak.astype(BF16)


def _fox_aug(small, tc):
    b, t, _ = small.shape
    tril = jnp.asarray(np.tril(np.ones((tc, tc), np.float32)), BF16)
    eq = np.zeros((3, LANES, LANES), np.float32)
    ek = np.zeros((3, LANES, LANES), np.float32)
    oneq = np.zeros((1, LANES), np.float32)
    onek = np.zeros((1, LANES), np.float32)
    for h in range(FOX_HEADS):
        for r in range(3):
            eq[r, h, AUG_PER_HEAD * h + r] = 1.0
            ek[r, h, AUG_PER_HEAD * h + 3 + r] = 1.0
            oneq[0, AUG_PER_HEAD * h + 3 + r] = 1.0
            onek[0, AUG_PER_HEAD * h + r] = 1.0
    const2 = lambda shape: pl.BlockSpec(shape, lambda bi, i: (0,) * len(shape))
    blk = pl.BlockSpec((1, tc, LANES), lambda bi, i: (bi, i, 0))
    return pl.pallas_call(
        _fox_aug_kernel,
        grid=(b, t // tc),
        in_specs=[blk, const2((tc, tc)), const2((3, LANES, LANES)), const2((3, LANES, LANES)),
                  const2((1, LANES)), const2((1, LANES))],
        out_specs=[blk, blk],
        out_shape=[jax.ShapeDtypeStruct((b, t, LANES), BF16)] * 2,
        scratch_shapes=[pltpu.VMEM((1, LANES), F32)],
        compiler_params=_cparams(("parallel", "arbitrary")),
        name="fox_aug",
    )(small, tril, jnp.asarray(eq, BF16), jnp.asarray(ek, BF16), jnp.asarray(oneq), jnp.asarray(onek))


def _fox_kernel(q_ref, aq_ref, k_ref, ak_ref, vt_ref, *rest, i, tq, tk):
    o_ref, st_ref, acc_ref = rest[-3:]
    pair = pl.program_id(1)
    row = lax.broadcasted_iota(jnp.int32, (LANES, 1), 0)
    qt = _transposed(q_ref[0])
    qat = _transposed(aq_ref[0])
    qcats = []
    for hh in range(2):
        head = 2 * pair + hh
        in_head = (row >= AUG_PER_HEAD * head) & (row < AUG_PER_HEAD * (head + 1))
        qcats.append(jnp.concatenate([_half_mask_t(qt, hh), jnp.where(in_head, qat, jnp.zeros_like(qat))], axis=0))

    def scores(j):
        ks = j * tk
        kcat = jnp.concatenate([k_ref[0, pl.ds(ks, tk), :], ak_ref[0, pl.ds(ks, tk), :]], axis=1)
        return tuple(_dot(kcat, qcats[hh]) for hh in range(2))

    def mask(j, st):
        kpos, qpos = _positions_t(j, tk, i * tq, tq)
        return jnp.where(kpos <= qpos, st, NEG)

    def values(hh, j):
        return vt_ref[0, hh * HEAD_DIM:(hh + 1) * HEAD_DIM, pl.ds(j * tk, tk)]

    res = _stream(0, i * (tq // tk), tq // tk, scores, mask, values, st_ref, acc_ref)
    ot = jnp.concatenate([acc / l for l, acc in res], axis=0)
    o_ref[0] = ot.T.astype(BF16)


def _fox_attention(main, aq, ak, vt, tq, tk):
    b, t, _ = main.shape
    in_specs = lambda i: [pl.BlockSpec((1, tq, LANES), lambda bi, p: (bi, i, CH_FQ + p)),
                          pl.BlockSpec((1, tq, LANES), lambda bi, p: (bi, i, 0)),
                          pl.BlockSpec((1, t, LANES), lambda bi, p: (bi, 0, CH_FK + p)),
                          pl.BlockSpec((1, t, LANES), lambda bi, p: (bi, 0, 0)),
                          pl.BlockSpec((1, LANES, t), lambda bi, p: (bi, VT_FV + p, 0))]
    return _per_query_tile(
        partial(_fox_kernel, tq=tq, tk=tk), t // tq, (b, FOX_HEADS // 2), in_specs,
        lambda i: pl.BlockSpec((1, tq, LANES), lambda bi, p: (bi, i, p)),
        jax.ShapeDtypeStruct((b, t, FOX_W), BF16), _stream_scratch(2, HEAD_DIM, tq, tk), "fox_attention",
        (main, aq, main, ak, vt))


def _compress_kernel(xk_ref, xv_ref, w1k_ref, w2k_ref, pk_ref, w1v_ref, w2v_ref, pv_ref, kc_ref, vct_ref):
    half = CMP_STRIDE * HEAD_DIM
    for x_ref, w1_ref, w2_ref, p_ref, o_ref in ((xk_ref, w1k_ref, w2k_ref, pk_ref, kc_ref),
                                                 (xv_ref, w1v_ref, w2v_ref, pv_ref, vct_ref)):
        w1 = w1_ref[...]
        pos_h = _dot(p_ref[...], w1)
        out = None
        for g in range(NSA_KV_GROUPS):
            x = x_ref[0, g]
            first = _dot(x, w1[:half])
            second = _dot(x, w1[half:])
            nrow = first.shape[0]
            hid = first + pltpu.roll(second, nrow - 1, 0) + pos_h
            a = jax.nn.gelu(hid, approximate=True).astype(BF16)
            y = _dot(a, w2_ref[g])
            out = y if out is None else out + y
        o_ref[0] = (out.T if o_ref is vct_ref else out).astype(BF16)


def _compress(xk, xv, w1k, w2k, pk, w1v, w2v, pv):
    b, g, nchunk, width = xk.shape
    xspec = pl.BlockSpec((1, g, nchunk, width), lambda bi: (bi, 0, 0, 0))
    const = lambda shape: pl.BlockSpec(shape, lambda bi: (0,) * len(shape))
    wspecs = [const((width * 2, CMP_HIDDEN)), const((g, CMP_HIDDEN, LANES)), const((1, width * 2))]
    return pl.pallas_call(
        _compress_kernel,
        grid=(b,),
        in_specs=[xspec, xspec] + wspecs + wspecs,
        out_specs=[pl.BlockSpec((1, nchunk, LANES), lambda bi: (bi, 0, 0)),
                   pl.BlockSpec((1, LANES, nchunk), lambda bi: (bi, 0, 0))],
        out_shape=[jax.ShapeDtypeStruct((b, nchunk, LANES), BF16), jax.ShapeDtypeStruct((b, LANES, nchunk), BF16)],
        compiler_params=_cparams(("parallel",)),
        name="nsa_compress",
    )(xk, xv, w1k, w2k, pk, w1v, w2v, pv)


def _gate_col(small, head, branch):
    idx = FOX_HEADS + 3 * head + branch
    return jnp.sum(jnp.where(_lane_iota() == idx, small, 0.0), axis=-1, keepdims=True)


def _nsa_select_kernel(q_ref, kc_ref, vct_ref, ovt_ref, small_ref, oc_ref, sel_ref, *, tq, ns, n_sel):
    q0 = pl.program_id(1) * tq
    lane = _lane_iota()
    kc = kc_ref[0]
    vct = vct_ref[0]
    ncp = kc.shape[0]
    small = small_ref[0]
    qpos = q0 + lax.broadcasted_iota(jnp.int32, (1, tq), 1)
    cmp_end = lax.broadcasted_iota(jnp.int32, (ncp, 1), 0) * CMP_STRIDE + (CMP_BLOCK - 1)
    cmask = cmp_end <= qpos
    psum = [jnp.zeros((ncp, tq), F32) for _ in range(NSA_KV_GROUPS)]
    logits = [[_dot_nt(kc, _half_mask(q_ref[0, :, n * LANES:(n + 1) * LANES], g)) for g in range(NSA_KV_GROUPS)]
              for n in range(NSA_HPG)]
    for n in range(NSA_HPG):
        ots = []
        for g in range(NSA_KV_GROUPS):
            z = jnp.where(cmask, logits[n][g], -jnp.inf)
            m = jnp.max(z, axis=0, keepdims=True)
            m = jnp.where(m == -jnp.inf, 0.0, m)
            p = jnp.exp2(z - m)
            p = p / jnp.maximum(jnp.sum(p, axis=0, keepdims=True), 1e-30)
            psum[g] = psum[g] + p
            ots.append(_dot(vct[g * HEAD_DIM:(g + 1) * HEAD_DIM], p.astype(BF16)))
        gate = jnp.where(lane < HEAD_DIM, _gate_col(small, n, 0), _gate_col(small, NSA_HPG + n, 0))
        oc_ref[0, :, n * LANES:(n + 1) * LANES] = gate * jnp.concatenate(ots, axis=0).T

    nsp = ovt_ref.shape[0]
    blk = lax.broadcasted_iota(jnp.int32, (nsp, 1), 0)
    cur = qpos // SLC_BLOCK
    valid = blk * SLC_BLOCK <= qpos
    forced = (blk == 0) | (blk == cur) | (blk == cur - 1)
    scores = []
    for g in range(NSA_KV_GROUPS):
        imp = jnp.zeros((nsp, tq), F32)
        for piece in _split3(psum[g]):
            imp = imp + _dot(ovt_ref[...], piece)
        scores.append(jnp.where(valid, jnp.where(forced, jnp.inf, imp), -jnp.inf))
    ranks = [jnp.zeros((nsp, tq), jnp.int32) for _ in range(NSA_KV_GROUPS)]
    for i in range(ns):
        for g in range(NSA_KV_GROUPS):
            row = scores[g][i:i + 1, :]
            ahead = (row > scores[g]) | ((row == scores[g]) & (blk > i))
            ranks[g] = ranks[g] + ahead.astype(jnp.int32)
    masks = [jnp.where((ranks[g] < n_sel) & (blk < ns), 0.0, NEG) for g in range(NSA_KV_GROUPS)]
    sel_ref[0] = jnp.concatenate(masks, axis=0).T.astype(BF16)


def _nsa_select(main, kc, vct, overlap_t, small, tq, ns, n_sel):
    b, t, _ = main.shape
    ncp = kc.shape[1]
    nsp = overlap_t.shape[0]
    return pl.pallas_call(
        partial(_nsa_select_kernel, tq=tq, ns=ns, n_sel=n_sel),
        grid=(b, t // tq),
        in_specs=[pl.BlockSpec((1, tq, NSA_W), lambda bi, i: (bi, i, CH_NQ * LANES // NSA_W)),
                  pl.BlockSpec((1, ncp, LANES), lambda bi, i: (bi, 0, 0)),
                  pl.BlockSpec((1, LANES, ncp), lambda bi, i: (bi, 0, 0)),
                  pl.BlockSpec((nsp, ncp), lambda bi, i: (0, 0)),
                  pl.BlockSpec((1, tq, LANES), lambda bi, i: (bi, i, 0))],
        out_specs=[pl.BlockSpec((1, tq, NSA_W), lambda bi, i: (bi, i, 0)),
                   pl.BlockSpec((1, tq, NSA_KV_GROUPS * nsp), lambda bi, i: (bi, i, 0))],
        out_shape=[jax.ShapeDtypeStruct((b, t, NSA_W), F32),
                   jax.ShapeDtypeStruct((b, t, NSA_KV_GROUPS * nsp), BF16)],
        compiler_params=_cparams(("parallel", "parallel")),
        name="nsa_select",
    )(main, kc, vct, overlap_t, small)


def _nsa_flash_kernel(q_ref, sel_ref, oc_ref, small_ref, ks_ref, kw_ref, hot_ref, vst_ref, vwt_ref, *rest,
                      i, tq, tk):
    o_ref, st_ref, acc_ref = rest[-3:]
    n = pl.program_id(1)
    q0 = i * tq
    q2 = q_ref[0]
    small = small_ref[0]
    qt = _transposed(q2)
    selt = _transposed(sel_ref[0])
    qhs = [_half_mask_t(qt, g) for g in range(NSA_KV_GROUPS)]
    qcats = [jnp.concatenate([qhs[g], _half_mask_t(selt, g)], axis=0) for g in range(NSA_KV_GROUPS)]
    rows = lambda g: slice(g * HEAD_DIM, (g + 1) * HEAD_DIM)

    def sel_scores(j):
        ks = j * tk
        kcat = jnp.concatenate([ks_ref[0, pl.ds(ks, tk), :], hot_ref[pl.ds(ks, tk), :]], axis=1)
        return tuple(_dot(kcat, qcats[g]) for g in range(NSA_KV_GROUPS))

    def sel_mask(j, st):
        kpos, qpos = _positions_t(j, tk, q0, tq)
        return jnp.where(kpos <= qpos, st, NEG)

    def win_scores(j):
        kw = kw_ref[0, pl.ds(j * tk, tk), :]
        return tuple(_dot(kw, qhs[g]) for g in range(NSA_KV_GROUPS))

    def win_mask(j, st):
        kpos, qpos = _positions_t(j, tk, q0, tq)
        return jnp.where((kpos <= qpos) & (kpos > qpos - WINDOW), st, NEG)

    values = lambda ref: lambda g, j: ref[0, rows(g), pl.ds(j * tk, tk)]
    lane = _lane_iota()
    n_end = (i + 1) * (tq // tk)
    win_lo = max(n_end - (WINDOW + tq) // tk, 0)
    o = oc_ref[0]
    for branch, args in ((1, (0, i * (tq // tk), tq // tk, sel_scores, sel_mask, values(vst_ref))),
                         (2, (win_lo, 0, n_end - win_lo, win_scores, win_mask, values(vwt_ref)))):
        res = _stream(*args, st_ref, acc_ref)
        ot = jnp.concatenate([acc / l for l, acc in res], axis=0)
        gate = jnp.where(lane < HEAD_DIM, _gate_col(small, n, branch), _gate_col(small, NSA_HPG + n, branch))
        o = o + gate * ot.T
    o_ref[0] = o.astype(BF16)


def _nsa_flash(main, sel, oc, small, hot, vt, tq, tk):
    b, t, _ = main.shape
    tile = lambda i, ch: pl.BlockSpec((1, tq, LANES), lambda bi, n: (bi, i, ch + n))
    shared = lambda i: pl.BlockSpec((1, tq, LANES), lambda bi, n: (bi, i, 0))
    full = lambda ch: pl.BlockSpec((1, t, LANES), lambda bi, n: (bi, 0, ch))
    vfull = lambda ch: pl.BlockSpec((1, LANES, t), lambda bi, n: (bi, ch, 0))
    in_specs = lambda i: [tile(i, CH_NQ), shared(i), tile(i, 0), shared(i), full(CH_KS), full(CH_KW),
                          pl.BlockSpec((t, LANES), lambda bi, n: (0, 0)), vfull(VT_VS), vfull(VT_VW)]
    return _per_query_tile(
        partial(_nsa_flash_kernel, tq=tq, tk=tk), t // tq, (b, NSA_HPG), in_specs, lambda i: tile(i, 0),
        jax.ShapeDtypeStruct((b, t, NSA_W), BF16), _stream_scratch(NSA_KV_GROUPS, HEAD_DIM, tq, tk), "nsa_flash",
        (main, sel, oc, small, main, main, hot, vt, vt))


def _diff_kernel(q_ref, k_ref, vt_ref, lam_ref, g_ref, *rest, i, tq, tk, lam_init):
    o_ref, st_ref, acc_ref = rest[-3:]
    q2 = q_ref[0]
    lp = lam_ref[...]
    lam = (jnp.exp(jnp.sum(lp[0:1] * lp[1:2], axis=-1, keepdims=True))
           - jnp.exp(jnp.sum(lp[2:3] * lp[3:4], axis=-1, keepdims=True)) + lam_init)
    qhs = [_half_mask(q2, comp) for comp in range(2)]

    def scores(j):
        k2 = k_ref[0, pl.ds(j * tk, tk), :]
        return tuple(_dot_nt(k2, qhs[comp]) for comp in range(2))

    def mask(j, st):
        kpos, qpos = _positions_t(j, tk, i * tq, tq)
        return jnp.where(kpos <= qpos, st, NEG)

    def values(comp, j):
        return vt_ref[0, :, pl.ds(j * tk, tk)]

    (l1, acc1), (l2, acc2) = _stream(0, i * (tq // tk), tq // tk, scores, mask, values, st_ref, acc_ref)
    o = (acc1 / l1 - lam * (acc2 / l2)).T
    o_ref[0] = (_rms(o, g_ref[...]) * (1.0 - lam_init)).astype(BF16)


def _diff_attention(main, vt, lam_p, subln_g, tq, tk, lam_init):
    b, t, _ = main.shape
    nh = DIFF_HEADS
    in_specs = lambda i: [pl.BlockSpec((1, tq, LANES), lambda bi, h: (bi, i, h)),
                          pl.BlockSpec((1, t, LANES), lambda bi, h: (bi, 0, nh + h)),
                          pl.BlockSpec((1, LANES, t), lambda bi, h: (bi, h, 0)),
                          pl.BlockSpec((4, LANES), lambda bi, h: (0, 0)),
                          pl.BlockSpec((1, LANES), lambda bi, h: (0, 0))]
    return _per_query_tile(
        partial(_diff_kernel, tq=tq, tk=tk, lam_init=lam_init), t // tq, (b, nh), in_specs,
        lambda i: pl.BlockSpec((1, tq, LANES), lambda bi, h: (bi, i, h)),
        jax.ShapeDtypeStruct((b, t, nh * LANES), BF16), _stream_scratch(2, LANES, tq, tk), "diff_attention",
        (main, main, vt, lam_p, subln_g))


def _mem_kv_kernel(mem_ref, g_ref, wk_ref, wv_ref, k_ref, vt_ref):
    mn = _rms(mem_ref[0], g_ref[0]).astype(BF16)
    k_ref[0, 0] = _dot(mn, wk_ref[0]).astype(BF16)
    vt_ref[0, 0] = _dot(mn, wv_ref[0]).T.astype(BF16)


def _mem_kv(mem, mem_norm_g, wk, wv):
    depth = wk.shape[0]
    b, m, d = mem.shape
    wspec = pl.BlockSpec((1, d, MEM_W), lambda l, bi: (l, 0, 0))
    return pl.pallas_call(
        _mem_kv_kernel,
        grid=(depth, b),
        in_specs=[pl.BlockSpec((1, m, d), lambda l, bi: (bi, 0, 0)),
                  pl.BlockSpec((1, 1, d), lambda l, bi: (l, 0, 0)), wspec, wspec],
        out_specs=[pl.BlockSpec((1, 1, m, MEM_W), lambda l, bi: (l, bi, 0, 0)),
                   pl.BlockSpec((1, 1, MEM_W, m), lambda l, bi: (l, bi, 0, 0))],
        out_shape=[jax.ShapeDtypeStruct((depth, b, m, MEM_W), BF16),
                   jax.ShapeDtypeStruct((depth, b, MEM_W, m), BF16)],
        compiler_params=_cparams(("parallel", "parallel")),
        name="mem_kv",
    )(mem, mem_norm_g.reshape(depth, 1, d), wk, wv)


def _post_mixer_kernel(*refs, n_in, n_part):
    a_refs, w_refs = refs[:n_in], refs[n_in:2 * n_in]
    x_ref, g1_ref, g2_ref, g3_ref, wq_ref, k_ref, vt_ref, wo_ref, o_ref = refs[2 * n_in:]
    rows = x_ref.shape[0] // n_part
    parts = [slice(r * rows, (r + 1) * rows) for r in range(n_part)]
    n_mem = k_ref.shape[2]
    heads = [(ch, hh) for ch in range(MEM_W // LANES) for hh in range(2)]
    ones = jnp.ones((ONES_ROWS, n_mem), BF16)

    def mixer_out(ps):
        y = None
        for a_ref, w_ref in zip(a_refs, w_refs):
            t = _dot(a_ref[ps, :], w_ref[...])
            y = t if y is None else y + t
        return y

    def scores(q):
        return [_dot_nt(k_ref[0, 0, :, ch * LANES:(ch + 1) * LANES], _half_mask(q[:, ch * LANES:(ch + 1) * LANES], hh))
                for ch, hh in heads]

    def attend(logits):
        outs = []
        for (ch, hh), st in zip(heads, logits):
            p = jnp.exp2(st - jnp.max(st, axis=0, keepdims=True)).astype(BF16)
            r0 = ch * LANES + hh * HEAD_DIM
            pv = _dot(jnp.concatenate([vt_ref[0, 0, r0:r0 + HEAD_DIM, :], ones], axis=0), p)
            outs.append(pv[:HEAD_DIM] / pv[HEAD_DIM:HEAD_DIM + 1])
        return jnp.concatenate(outs, axis=0).T.astype(BF16)

    ys = [mixer_out(ps) for ps in parts]
    xs = [x_ref[ps, :] + _rms(y, g1_ref[...]) for ps, y in zip(parts, ys)]
    qs = [(_dot(_rms(x, g2_ref[...]).astype(BF16), wq_ref[...]) * Q_SCALE_LOG2).astype(BF16) for x in xs]
    logits = [scores(q) for q in qs]
    os_ = [attend(lg) for lg in logits]
    for ps, x, o in zip(parts, xs, os_):
        o_ref[ps, :] = x + _rms(_dot(o, wo_ref[...]), g3_ref[...])


def _post_mixer(acts, weights, x2, g1, g2, g3, wq, mem_k, mem_vt, wo, layer, tm, rows_per_batch, n_part=2):
    n = x2.shape[0]
    m = mem_k.shape[2]
    const = lambda shape: pl.BlockSpec(shape, lambda i: (0, 0))
    row = pl.BlockSpec((tm, D_MODEL), lambda i: (i, 0))
    gspec = const((1, D_MODEL))
    return pl.pallas_call(
        partial(_post_mixer_kernel, n_in=len(acts), n_part=n_part),
        grid=(n // tm,),
        in_specs=[pl.BlockSpec((tm, a.shape[1]), lambda i: (i, 0)) for a in acts]
                 + [const(w.shape) for w in weights]
                 + [row, gspec, gspec, gspec, const((D_MODEL, MEM_W)),
                    pl.BlockSpec((1, 1, m, MEM_W), lambda i: (layer, i // rows_per_batch, 0, 0)),
                    pl.BlockSpec((1, 1, MEM_W, m), lambda i: (layer, i // rows_per_batch, 0, 0)),
                    const((MEM_W, D_MODEL))],
        out_specs=row,
        out_shape=jax.ShapeDtypeStruct((n, D_MODEL), F32),
        compiler_params=_cparams(("parallel",)),
        name="post_mixer",
    )(*acts, *weights, x2, g1, g2, g3, wq, mem_k, mem_vt, wo)


def _ffn_kernel(x_ref, gin_ref, gout_ref, wg_ref, wu_ref, wd_ref, o_ref, *, chunk):
    x = x_ref[...]
    h = _rms(x, gin_ref[...]).astype(BF16)
    d_ff = wg_ref.shape[1]
    y = None
    for c0 in range(0, d_ff, chunk):
        gate = _dot(h, wg_ref[:, c0:c0 + chunk])
        up = _dot(h, wu_ref[:, c0:c0 + chunk])
        a = (gate * (1.0 / (1.0 + jnp.exp(-gate))) * up).astype(BF16)
        t = _dot(a, wd_ref[c0:c0 + chunk, :])
        y = t if y is None else y + t
    o_ref[...] = x + _rms(y, gout_ref[...])


def _ffn(x2, gin, gout, wg, wu, wd, tm, chunk):
    n = x2.shape[0]
    d_ff = wg.shape[1]
    const = lambda shape: pl.BlockSpec(shape, lambda i: (0, 0), pipeline_mode=pl.Buffered(1))
    return pl.pallas_call(
        partial(_ffn_kernel, chunk=chunk),
        grid=(n // tm,),
        in_specs=[pl.BlockSpec((tm, D_MODEL), lambda i: (i, 0)), const((1, D_MODEL)), const((1, D_MODEL)),
                  const((D_MODEL, d_ff)), const((D_MODEL, d_ff)), const((d_ff, D_MODEL))],
        out_specs=pl.BlockSpec((tm, D_MODEL), lambda i: (i, 0)),
        out_shape=jax.ShapeDtypeStruct((n, D_MODEL), F32),
        compiler_params=_cparams(("parallel",)),
        name="ffn",
    )(x2, gin, gout, wg, wu, wd)


def _even_weights(w_in, w_out):
    kvw = NSA_KV_GROUPS * HEAD_DIM
    offs = [int(o) for o in np.cumsum((FOX_W, FOX_W, FOX_W, FOX_HEADS, NSA_W) + (kvw,) * 6 + (3 * NSA_HEADS,))]
    fk0, fv0, fl0, nq0, kc0, vc0, ks0, vs0, kw0, vw0, gl0, end = offs
    perm = np.concatenate([np.arange(HEAD_DIM) + HEAD_DIM * (g * NSA_HPG + n)
                           for n in range(NSA_HPG) for g in range(NSA_KV_GROUPS)])
    cols = lambda a, b: w_in[:, a:b]
    w = jnp.concatenate([cols(0, fv0), w_in[:, nq0 + perm], cols(kc0, vc0), cols(vc0, ks0), cols(ks0, vs0),
                         cols(kw0, vw0), cols(fv0, fl0), cols(vs0, kw0), cols(vw0, gl0)], axis=1).astype(BF16)
    w_small = jnp.concatenate([cols(fl0, nq0), cols(gl0, end),
                               jnp.zeros((D_MODEL, LANES - FOX_HEADS - 3 * NSA_HEADS), w_in.dtype)],
                              axis=1).astype(BF16)
    w_out_fox = w_out[:FOX_W].astype(BF16)
    w_out_nsa = w_out[FOX_W + perm].astype(BF16)
    return w, w_small, w_out_fox, w_out_nsa


def _overlap_matrix_t(t, ncp):
    nc = (t - CMP_BLOCK) // CMP_STRIDE + 1
    ns = t // SLC_BLOCK
    cs = np.arange(nc) * CMP_STRIDE
    ss = np.arange(ns) * SLC_BLOCK
    ov = np.clip(np.minimum(cs[:, None] + CMP_BLOCK, ss[None, :] + SLC_BLOCK)
                 - np.maximum(cs[:, None], ss[None, :]), 0, None) / CMP_BLOCK
    full = np.zeros((LANES // NSA_KV_GROUPS, ncp), np.float32)
    full[:ns, :nc] = ov.T
    return jnp.asarray(full, BF16), ns


def _stride_chunks(main, ch):
    b, t, _ = main.shape
    x = main[:, :, ch * LANES:(ch + 1) * LANES].reshape(b, t // CMP_STRIDE, CMP_STRIDE, NSA_KV_GROUPS, HEAD_DIM)
    return jnp.transpose(x, (0, 3, 1, 2, 4)).reshape(b, NSA_KV_GROUPS, t // CMP_STRIDE, CMP_STRIDE * HEAD_DIM)


def _pad_w2(w2):
    out = jnp.zeros((NSA_KV_GROUPS, CMP_HIDDEN, LANES), BF16)
    for g in range(NSA_KV_GROUPS):
        out = out.at[g, :, g * HEAD_DIM:(g + 1) * HEAD_DIM].set(w2.astype(BF16))
    return out


def kernel(x, mem, positions, sandwich_g, mem_norm_g, ev_w_in, ev_fox_fbias, ev_cmp_pos_k, ev_cmp_w1_k, ev_cmp_w2_k, ev_cmp_pos_v, ev_cmp_w1_v, ev_cmp_w2_v, ev_w_out, od_w_in, od_lambda, od_subln_g, od_w_out, ca_wq, ca_wk, ca_wv, ca_wo, ffn_wg, ffn_wu, ffn_wd):
    b, t, d = x.shape
    depth = sandwich_g.shape[0]
    n = b * t
    tm = 512
    tm_ffn = 512
    tq, tk = 512, 256
    tk_diff = 512
    assert d == D_MODEL and t % tq == 0 and tq % tk == 0 and WINDOW % tk == 0 and t % tm == 0 and n % tm_ffn == 0

    tabs = tuple(a.reshape(b, t, LANES) for a in _rope_tables(positions, 512))
    mem_k, mem_vt = _mem_kv(mem, mem_norm_g, ca_wk.astype(BF16), ca_wv.astype(BF16))
    ncp = t // CMP_STRIDE
    overlap_t, ns = _overlap_matrix_t(t, ncp)
    assert ns <= overlap_t.shape[0]
    n_sel = min(SLC_TOPK, ns)
    hot = jnp.asarray((np.arange(t)[:, None] // SLC_BLOCK == np.arange(LANES)[None, :] % overlap_t.shape[0])
                      .astype(np.float32), BF16)
    gain = lambda l, j: sandwich_g[l, j].reshape(1, d)

    x2 = x.reshape(n, d)
    for layer in range(depth):
        x3 = x2.reshape(b, t, d)
        if layer % 2 == 0:
            e = layer // 2
            w, w_small, w_out_fox, w_out_nsa = _even_weights(ev_w_in[e], ev_w_out[e])
            fb_row = jnp.zeros((1, LANES), F32).at[0, :FOX_HEADS].set(ev_fox_fbias[e].astype(F32))
            main, vt, small = _even_proj(x3, gain(layer, 0), w, w_small, fb_row, tabs, tm)
            aq, ak = _fox_aug(small, 512)
            o_fox = _fox_attention(main, aq, ak, vt, tq, tk)
            kc, vct = _compress(
                _stride_chunks(main, CH_KC), _stride_chunks(main, CH_VC),
                ev_cmp_w1_k[e].astype(BF16), _pad_w2(ev_cmp_w2_k[e]), ev_cmp_pos_k[e].reshape(1, -1).astype(BF16),
                ev_cmp_w1_v[e].astype(BF16), _pad_w2(ev_cmp_w2_v[e]), ev_cmp_pos_v[e].reshape(1, -1).astype(BF16))
            oc, sel = _nsa_select(main, kc, vct, overlap_t, small, 256, ns, n_sel)
            o_nsa = _nsa_flash(main, sel, oc, small, hot, vt, tq, tk)
            acts, w_outs = [o_fox.reshape(n, FOX_W), o_nsa.reshape(n, NSA_W)], [w_out_fox, w_out_nsa]
        else:
            o = layer // 2
            w_in = od_w_in[o].astype(BF16)
            main, vt = _odd_proj(x3, gain(layer, 0), w_in, tabs, tm)
            lam_init = 0.8 - 0.6 * math.exp(-0.3 * layer)
            lam_p = jnp.pad(od_lambda[o].astype(F32), ((0, 0), (0, LANES - HEAD_DIM)))
            attn = _diff_attention(main, vt, lam_p, od_subln_g[o].reshape(1, LANES).astype(F32), tq, tk_diff,
                                   lam_init)
            acts, w_outs = [attn.reshape(n, D_MODEL)], [od_w_out[o].astype(BF16)]
        x2 = _post_mixer(acts, w_outs, x2, gain(layer, 1), gain(layer, 2), gain(layer, 3),
                         ca_wq[layer].astype(BF16), mem_k, mem_vt, ca_wo[layer].astype(BF16), layer,
                         tm_ffn, t // tm_ffn)
        x2 = _ffn(x2, gain(layer, 4), gain(layer, 5), ffn_wg[layer].astype(BF16), ffn_wu[layer].astype(BF16),
                  ffn_wd[layer].astype(BF16), tm_ffn, 256)
    return x2.reshape(b, t, d)
```

```python
import math
from functools import partial

import numpy as np
import jax
import jax.numpy as jnp
from jax import lax
from jax.experimental import pallas as pl
from jax.experimental.pallas import tpu as pltpu

F32 = jnp.float32
BF16 = jnp.bfloat16

D_MODEL = 1024
HEAD_DIM = 64
LANES = 128
ROPE_DIM = HEAD_DIM // 4
ROPE_THETA = 500000.0
FOX_HEADS = 8
NSA_HEADS = 8
NSA_KV_GROUPS = 2
NSA_HPG = NSA_HEADS // NSA_KV_GROUPS
CMP_BLOCK = 32
CMP_STRIDE = 16
CMP_HIDDEN = 2 * HEAD_DIM
SLC_BLOCK = 64
SLC_TOPK = 16
WINDOW = 512
DIFF_HEADS = 8
MEM_HEADS = 4
MEM_W = MEM_HEADS * HEAD_DIM
RMS_EPS = 1e-6
Q_SCALE = HEAD_DIM ** -0.5
LOG2E = math.log2(math.e)
Q_SCALE_LOG2 = Q_SCALE * LOG2E
NEG = -1e30
AUG_PER_HEAD = 6
ONES_ROWS = 16

FOX_W = FOX_HEADS * HEAD_DIM
NSA_W = NSA_HEADS * HEAD_DIM
CH_FQ, CH_FK, CH_NQ, CH_KS, CH_KW = 0, 4, 8, 12, 13
EV_MAIN = 14 * LANES
EV_CMP = 2 * LANES
VT_FV, VT_VS, VT_VW = 0, 4, 5
EV_VT = 6 * LANES
EV_ROPE_CHUNKS = tuple(range(CH_NQ, CH_NQ + 4)) + (CH_KS, CH_KW, EV_MAIN // LANES)
EV_QSCALE_CHUNKS = tuple(range(CH_FQ, CH_FQ + 4)) + tuple(range(CH_NQ, CH_NQ + 4))
OD_MAIN = 2 * D_MODEL
OD_VT = D_MODEL

VMEM_LIMIT = 56 * 1024 * 1024


def _cparams(sem):
    return pltpu.CompilerParams(dimension_semantics=sem, vmem_limit_bytes=VMEM_LIMIT)


def _rms(x, g):
    return x * lax.rsqrt(jnp.mean(x * x, axis=-1, keepdims=True) + RMS_EPS) * g


def _split3(x):
    hi = x.astype(BF16)
    r1 = x - hi.astype(F32)
    mid = r1.astype(BF16)
    lo = (r1 - mid.astype(F32)).astype(BF16)
    return hi, mid, lo


def _dot(a, b):
    return jnp.dot(a, b, preferred_element_type=F32)


def _dot_nt(a, b):
    return lax.dot_general(a, b, (((1,), (1,)), ((), ())), preferred_element_type=F32)


def _lane_iota(n=LANES):
    return lax.broadcasted_iota(jnp.int32, (1, n), 1)


def _half_mask(q2, half):
    return jnp.where(_lane_iota() // HEAD_DIM == half, q2, jnp.zeros_like(q2))


def _transposed(x):
    return x.astype(F32).T.astype(BF16)


def _half_mask_t(qt, half):
    row = lax.broadcasted_iota(jnp.int32, (qt.shape[0], 1), 0)
    return jnp.where(row // HEAD_DIM == half, qt, jnp.zeros_like(qt))


def _positions_t(j, tk, q0, tq):
    kpos = j * tk + lax.broadcasted_iota(jnp.int32, (tk, 1), 0)
    qpos = q0 + lax.broadcasted_iota(jnp.int32, (1, tq), 1)
    return kpos, qpos


def _stream(base, n_full, n_tail, scores, mask, values, st_ref, acc_ref):
    n_chain, dv, tq = acc_ref.shape

    def park(j, slot):
        for c, st in enumerate(scores(j)):
            st_ref[slot, c] = st

    def step(j, slot, stats, masked, prefetch):
        if prefetch:
            park(j + 1, 1 - slot)
        parts = []
        for c in range(n_chain):
            st = st_ref[slot, c]
            if masked:
                st = mask(j, st)
            m, _ = stats[c]
            m_new = jnp.maximum(m, jnp.max(st, axis=0, keepdims=True))
            p = jnp.exp2(st - m_new).astype(BF16)
            vt1 = jnp.concatenate([values(c, j), jnp.ones((ONES_ROWS, p.shape[0]), BF16)], axis=0)
            parts.append((m_new, jnp.exp2(m - m_new), _dot(vt1, p)))
        out = []
        for c, (m_new, alpha, pv) in enumerate(parts):
            acc_ref[c] = alpha * acc_ref[c] + pv[:dv]
            out.append((m_new, alpha * stats[c][1] + pv[dv:dv + 1]))
        return tuple(out)

    acc_ref[...] = jnp.zeros_like(acc_ref)
    park(base, 0)
    stats = ((jnp.full((1, tq), NEG, F32), jnp.zeros((1, tq), F32)),) * n_chain
    n_tiles = n_full + n_tail
    for s_ in range(n_tiles):
        stats = step(base + s_, s_ % 2, stats, s_ >= n_full, s_ + 1 < n_tiles)
    return [(l, acc_ref[c]) for c, (_, l) in enumerate(stats)]


def _per_query_tile(kernel_fn, n_q, grid, in_specs_fn, out_spec_fn, out_shape, scratch, name, args):
    out = None
    for i in range(n_q):
        in_specs, call_args, aliases = list(in_specs_fn(i)), list(args), {}
        if out is not None:
            in_specs.append(pl.BlockSpec(memory_space=pl.ANY))
            call_args.append(out)
            aliases = {len(call_args) - 1: 0}
        out = pl.pallas_call(
            partial(kernel_fn, i=i), grid=grid, in_specs=in_specs, out_specs=out_spec_fn(i), out_shape=out_shape,
            scratch_shapes=scratch, input_output_aliases=aliases,
            compiler_params=_cparams(("parallel",) * len(grid)), name=f"{name}_q{i}",
        )(*call_args)
    return out


def _stream_scratch(n_chain, dv, tq, tk):
    return [pltpu.VMEM((2, n_chain, tk, tq), F32), pltpu.VMEM((n_chain, dv, tq), F32)]


def _rope_kernel(pos_ref, inv_ref, m1_ref, m2_ref, c_ref, s1_ref, s2_ref):
    ang = pos_ref[...].astype(F32) * inv_ref[...]
    c_ref[...] = jnp.cos(ang)
    sn = jnp.sin(ang)
    s1_ref[...] = -sn * m1_ref[...]
    s2_ref[...] = sn * m2_ref[...]


def _rope_tables(positions, tm):
    n = positions.size
    inv = ROPE_THETA ** (-jnp.arange(0, ROPE_DIM, 2, dtype=F32) / ROPE_DIM)
    lane = np.arange(LANES) % HEAD_DIM
    half = ROPE_DIM // 2
    inv_l = jnp.where(lane < ROPE_DIM, inv[lane % half], 0.0).reshape(1, LANES).astype(F32)
    m1 = jnp.asarray((lane < half).astype(np.float32)).reshape(1, LANES)
    m2 = jnp.asarray(((lane >= half) & (lane < ROPE_DIM)).astype(np.float32)).reshape(1, LANES)
    row = pl.BlockSpec((1, LANES), lambda i: (0, 0))
    tab = pl.BlockSpec((tm, LANES), lambda i: (i, 0))
    return pl.pallas_call(
        _rope_kernel,
        grid=(n // tm,),
        in_specs=[pl.BlockSpec((tm, 1), lambda i: (i, 0)), row, row, row],
        out_specs=[tab, tab, tab],
        out_shape=[jax.ShapeDtypeStruct((n, LANES), F32)] * 3,
        compiler_params=_cparams(("parallel",)),
        name="rope_tables",
    )(positions.reshape(n, 1), inv_l, m1, m2)


def _apply_rope(y, c, s1, s2):
    half = ROPE_DIM // 2
    return y * c + pltpu.roll(y, LANES - half, 1) * s1 + pltpu.roll(y, half, 1) * s2


def _project_chunks(h, w_ref, tabs, dests, rope_chunks, qscale_chunks):
    c, s1, s2 = tabs
    wide = 2 * LANES
    where = [(kind, ref, k) for kind, ref, n in dests for k in range(n)]
    for ch2 in range(len(where) // 2):
        y2 = _dot(h, w_ref[:, ch2 * wide:(ch2 + 1) * wide])
        for ch in (2 * ch2, 2 * ch2 + 1):
            y = y2[:, (ch % 2) * LANES:(ch % 2 + 1) * LANES]
            if ch in rope_chunks:
                y = _apply_rope(y, c, s1, s2)
            if ch in qscale_chunks:
                y = y * Q_SCALE_LOG2
            kind, ref, k = where[ch]
            if kind == "cols":
                ref[0, k * LANES:(k + 1) * LANES, :] = y.T.astype(BF16)
            elif kind == "rows_f32":
                ref[0, k] = y
            else:
                ref[0, :, k * LANES:(k + 1) * LANES] = y.astype(BF16)


def _even_proj_kernel(x_ref, g_ref, w_ref, ws_ref, fb_ref, c_ref, s1_ref, s2_ref,
                      main_ref, cmp_ref, vt_ref, small_ref):
    h = _rms(x_ref[0], g_ref[...]).astype(BF16)
    dests = [("rows", main_ref, EV_MAIN // LANES), ("rows_f32", cmp_ref, EV_CMP // LANES),
             ("cols", vt_ref, EV_VT // LANES)]
    _project_chunks(h, w_ref, (c_ref[0], s1_ref[0], s2_ref[0]), dests, EV_ROPE_CHUNKS, EV_QSCALE_CHUNKS)
    ys = _dot(h, ws_ref[...])
    z = ys + fb_ref[...]
    log_f = jnp.minimum(z, 0.0) - jnp.log(1.0 + jnp.exp(-jnp.abs(z)))
    gate = 1.0 / (1.0 + jnp.exp(-ys))
    small_ref[0] = jnp.where(_lane_iota() < FOX_HEADS, log_f, gate)


def _even_proj(x3, g, w, w_small, fb_row, tabs, tm):
    b, t, _ = x3.shape
    const = lambda shape: pl.BlockSpec(shape, lambda bi, i: (0, 0))
    tab = pl.BlockSpec((1, tm, LANES), lambda bi, i: (bi, i, 0))
    return pl.pallas_call(
        _even_proj_kernel,
        grid=(b, t // tm),
        in_specs=[pl.BlockSpec((1, tm, D_MODEL), lambda bi, i: (bi, i, 0)), const((1, D_MODEL)),
                  const((D_MODEL, EV_MAIN + EV_CMP + EV_VT)), const((D_MODEL, LANES)), const((1, LANES)),
                  tab, tab, tab],
        out_specs=[pl.BlockSpec((1, tm, EV_MAIN), lambda bi, i: (bi, i, 0)),
                   pl.BlockSpec((1, EV_CMP // LANES, tm, LANES), lambda bi, i: (bi, 0, i, 0)),
                   pl.BlockSpec((1, EV_VT, tm), lambda bi, i: (bi, 0, i)), tab],
        out_shape=[jax.ShapeDtypeStruct((b, t, EV_MAIN), BF16),
                   jax.ShapeDtypeStruct((b, EV_CMP // LANES, t, LANES), F32),
                   jax.ShapeDtypeStruct((b, EV_VT, t), BF16), jax.ShapeDtypeStruct((b, t, LANES), F32)],
        compiler_params=_cparams(("parallel", "parallel")),
        name="even_proj",
    )(x3, g, w, w_small, fb_row, *tabs)


def _odd_proj_kernel(x_ref, g_ref, w_ref, c_ref, s1_ref, s2_ref, main_ref, vt_ref):
    h = _rms(x_ref[0], g_ref[...]).astype(BF16)
    n_main = OD_MAIN // LANES
    dests = [("rows", main_ref, n_main), ("cols", vt_ref, OD_VT // LANES)]
    _project_chunks(h, w_ref, (c_ref[0], s1_ref[0], s2_ref[0]), dests, tuple(range(n_main)),
                    tuple(range(n_main // 2)))


def _odd_proj(x3, g, w, tabs, tm):
    b, t, _ = x3.shape
    const = lambda shape: pl.BlockSpec(shape, lambda bi, i: (0, 0))
    tab = pl.BlockSpec((1, tm, LANES), lambda bi, i: (bi, i, 0))
    return pl.pallas_call(
        _odd_proj_kernel,
        grid=(b, t // tm),
        in_specs=[pl.BlockSpec((1, tm, D_MODEL), lambda bi, i: (bi, i, 0)), const((1, D_MODEL)),
                  const((D_MODEL, OD_MAIN + OD_VT)), tab, tab, tab],
        out_specs=[pl.BlockSpec((1, tm, OD_MAIN), lambda bi, i: (bi, i, 0)),
                   pl.BlockSpec((1, OD_VT, tm), lambda bi, i: (bi, 0, i))],
        out_shape=[jax.ShapeDtypeStruct((b, t, OD_MAIN), BF16), jax.ShapeDtypeStruct((b, OD_VT, t), BF16)],
        compiler_params=_cparams(("parallel", "parallel")),
        name="odd_proj",
    )(x3, g, w, *tabs)


def _fox_aug_kernel(lf_ref, tril_ref, eq_ref, ek_ref, oneq_ref, onek_ref, aq_ref, ak_ref, carry_ref):
    @pl.when(pl.program_id(1) == 0)
    def _():
        carry_ref[...] = jnp.zeros_like(carry_ref)

    tril = tril_ref[...]
    c = carry_ref[...]
    for piece in _split3(lf_ref[0]):
        c = c + _dot(tril, piece)
    carry_ref[...] = c[-1:, :]
    aq = oneq_ref[...]
    ak = onek_ref[...]
    for r, piece in enumerate(_split3(c * LOG2E)):
        aq = aq + _dot(piece, eq_ref[r])
        ak = ak - _dot(piece, ek_ref[r])
    aq_ref[0] = aq.astype(BF16)
    ak_ref[0] = ak.astype(BF16)


def _fox_aug(small, tc):
    b, t, _ = small.shape
    tril = jnp.asarray(np.tril(np.ones((tc, tc), np.float32)), BF16)
    eq = np.zeros((3, LANES, LANES), np.float32)
    ek = np.zeros((3, LANES, LANES), np.float32)
    oneq = np.zeros((1, LANES), np.float32)
    onek = np.zeros((1, LANES), np.float32)
    for h in range(FOX_HEADS):
        for r in range(3):
            eq[r, h, AUG_PER_HEAD * h + r] = 1.0
            ek[r, h, AUG_PER_HEAD * h + 3 + r] = 1.0
            oneq[0, AUG_PER_HEAD * h + 3 + r] = 1.0
            onek[0, AUG_PER_HEAD * h + r] = 1.0
    const2 = lambda shape: pl.BlockSpec(shape, lambda bi, i: (0,) * len(shape))
    blk = pl.BlockSpec((1, tc, LANES), lambda bi, i: (bi, i, 0))
    return pl.pallas_call(
        _fox_aug_kernel,
        grid=(b, t // tc),
        in_specs=[blk, const2((tc, tc)), const2((3, LANES, LANES)), const2((3, LANES, LANES)),
                  const2((1, LANES)), const2((1, LANES))],
        out_specs=[blk, blk],
        out_shape=[jax.ShapeDtypeStruct((b, t, LANES), BF16)] * 2,
        scratch_shapes=[pltpu.VMEM((1, LANES), F32)],
        compiler_params=_cparams(("parallel", "arbitrary")),
        name="fox_aug",
    )(small, tril, jnp.asarray(eq, BF16), jnp.asarray(ek, BF16), jnp.asarray(oneq), jnp.asarray(onek))


def _fox_kernel(q_ref, aq_ref, k_ref, ak_ref, vt_ref, *rest, i, tq, tk):
    o_ref, st_ref, acc_ref = rest[-3:]
    pair = pl.program_id(1)
    row = lax.broadcasted_iota(jnp.int32, (LANES, 1), 0)
    qt = _transposed(q_ref[0])
    qat = _transposed(aq_ref[0])
    qcats = []
    for hh in range(2):
        head = 2 * pair + hh
        in_head = (row >= AUG_PER_HEAD * head) & (row < AUG_PER_HEAD * (head + 1))
        qcats.append(jnp.concatenate([_half_mask_t(qt, hh), jnp.where(in_head, qat, jnp.zeros_like(qat))], axis=0))

    def scores(j):
        ks = j * tk
        kcat = jnp.concatenate([k_ref[0, pl.ds(ks, tk), :], ak_ref[0, pl.ds(ks, tk), :]], axis=1)
        return tuple(_dot(kcat, qcats[hh]) for hh in range(2))

    def mask(j, st):
        kpos, qpos = _positions_t(j, tk, i * tq, tq)
        return jnp.where(kpos <= qpos, st, NEG)

    def values(hh, j):
        return vt_ref[0, hh * HEAD_DIM:(hh + 1) * HEAD_DIM, pl.ds(j * tk, tk)]

    res = _stream(0, i * (tq // tk), tq // tk, scores, mask, values, st_ref, acc_ref)
    ot = jnp.concatenate([acc / l for l, acc in res], axis=0)
    o_ref[0] = ot.T.astype(BF16)


def _fox_attention(main, aq, ak, vt, tq, tk):
    b, t, _ = main.shape
    in_specs = lambda i: [pl.BlockSpec((1, tq, LANES), lambda bi, p: (bi, i, CH_FQ + p)),
                          pl.BlockSpec((1, tq, LANES), lambda bi, p: (bi, i, 0)),
                          pl.BlockSpec((1, t, LANES), lambda bi, p: (bi, 0, CH_FK + p)),
                          pl.BlockSpec((1, t, LANES), lambda bi, p: (bi, 0, 0)),
                          pl.BlockSpec((1, LANES, t), lambda bi, p: (bi, VT_FV + p, 0))]
    return _per_query_tile(
        partial(_fox_kernel, tq=tq, tk=tk), t // tq, (b, FOX_HEADS // 2), in_specs,
        lambda i: pl.BlockSpec((1, tq, LANES), lambda bi, p: (bi, i, p)),
        jax.ShapeDtypeStruct((b, t, FOX_W), BF16), _stream_scratch(2, HEAD_DIM, tq, tk), "fox_attention",
        (main, aq, main, ak, vt))


def _compress_kernel(x_ref, pos_ref, w1_ref, w2_ref, kc_ref, vct_ref):
    t = x_ref.shape[2]
    nchunk = t // CMP_STRIDE
    for kv, o_ref in enumerate((kc_ref, vct_ref)):
        first = second = None
        for l in range(CMP_STRIDE):
            xl = x_ref[0, kv, pl.ds(l, nchunk, stride=CMP_STRIDE), :]
            a = _dot((xl + pos_ref[kv, l:l + 1, :]).astype(BF16), w1_ref[kv, l])
            b = _dot((xl + pos_ref[kv, CMP_STRIDE + l:CMP_STRIDE + l + 1, :]).astype(BF16),
                     w1_ref[kv, CMP_STRIDE + l])
            first = a if first is None else first + a
            second = b if second is None else second + b
        hid = first + pltpu.roll(second, nchunk - 1, 0)
        out = _dot(jax.nn.gelu(hid, approximate=True).astype(BF16), w2_ref[kv])
        o_ref[0] = (out.T if o_ref is vct_ref else out).astype(BF16)


def _compress(cmp_in, pos2, w1bd, w2bd):
    b, _, t, _ = cmp_in.shape
    nchunk = t // CMP_STRIDE
    const = lambda a: pl.BlockSpec(a.shape, lambda bi: (0,) * a.ndim)
    return pl.pallas_call(
        _compress_kernel,
        grid=(b,),
        in_specs=[pl.BlockSpec((1, EV_CMP // LANES, t, LANES), lambda bi: (bi, 0, 0, 0)),
                  const(pos2), const(w1bd), const(w2bd)],
        out_specs=[pl.BlockSpec((1, nchunk, LANES), lambda bi: (bi, 0, 0)),
                   pl.BlockSpec((1, LANES, nchunk), lambda bi: (bi, 0, 0))],
        out_shape=[jax.ShapeDtypeStruct((b, nchunk, LANES), BF16), jax.ShapeDtypeStruct((b, LANES, nchunk), BF16)],
        compiler_params=_cparams(("parallel",)),
        name="nsa_compress",
    )(cmp_in, pos2, w1bd, w2bd)


def _gate_col(small, head, branch):
    idx = FOX_HEADS + 3 * head + branch
    return jnp.sum(jnp.where(_lane_iota() == idx, small, 0.0), axis=-1, keepdims=True)


def _nsa_select_kernel(q_ref, kc_ref, vct_ref, ovt_ref, small_ref, oc_ref, sel_ref, *, tq, ns, n_sel):
    q0 = pl.program_id(1) * tq
    lane = _lane_iota()
    kc = kc_ref[0]
    vct = vct_ref[0]
    ncp = kc.shape[0]
    small = small_ref[0]
    qpos = q0 + lax.broadcasted_iota(jnp.int32, (1, tq), 1)
    cmp_end = lax.broadcasted_iota(jnp.int32, (ncp, 1), 0) * CMP_STRIDE + (CMP_BLOCK - 1)
    cmask = cmp_end <= qpos
    psum = [jnp.zeros((ncp, tq), F32) for _ in range(NSA_KV_GROUPS)]
    logits = [[_dot_nt(kc, _half_mask(q_ref[0, :, n * LANES:(n + 1) * LANES], g)) for g in range(NSA_KV_GROUPS)]
              for n in range(NSA_HPG)]
    for n in range(NSA_HPG):
        ots = []
        for g in range(NSA_KV_GROUPS):
            z = jnp.where(cmask, logits[n][g], -jnp.inf)
            m = jnp.max(z, axis=0, keepdims=True)
            m = jnp.where(m == -jnp.inf, 0.0, m)
            p = jnp.exp2(z - m)
            p = p / jnp.maximum(jnp.sum(p, axis=0, keepdims=True), 1e-30)
            psum[g] = psum[g] + p
            ots.append(_dot(vct[g * HEAD_DIM:(g + 1) * HEAD_DIM], p.astype(BF16)))
        gate = jnp.where(lane < HEAD_DIM, _gate_col(small, n, 0), _gate_col(small, NSA_HPG + n, 0))
        oc_ref[0, :, n * LANES:(n + 1) * LANES] = gate * jnp.concatenate(ots, axis=0).T

    nsp = ovt_ref.shape[0]
    blk = lax.broadcasted_iota(jnp.int32, (nsp, 1), 0)
    cur = qpos // SLC_BLOCK
    valid = blk * SLC_BLOCK <= qpos
    forced = (blk == 0) | (blk == cur) | (blk == cur - 1)
    scores = []
    for g in range(NSA_KV_GROUPS):
        imp = jnp.zeros((nsp, tq), F32)
        for piece in _split3(psum[g]):
            imp = imp + _dot(ovt_ref[...], piece)
        scores.append(jnp.where(valid, jnp.where(forced, jnp.inf, imp), -jnp.inf))
    slab = 8
    masks = []
    for g in range(NSA_KV_GROUPS):
        slabs = [scores[g][r:r + slab] for r in range(0, nsp, slab)]
        ranks = [jnp.zeros((slab, tq), jnp.int32) for _ in slabs]
        for i in range(ns):
            row = scores[g][i:i + 1, :]
            for r, sl in enumerate(slabs):
                if slab * r > i:
                    ahead = row >= sl
                elif slab * (r + 1) - 1 <= i:
                    ahead = row > sl
                else:
                    ahead = (row > sl) | ((row == sl) & (blk[slab * r:slab * (r + 1)] > i))
                ranks[r] = jnp.where(ahead, ranks[r] + 1, ranks[r])
        rank = jnp.concatenate(ranks, axis=0)
        masks.append(jnp.where((rank < n_sel) & (blk < ns), 0.0, NEG))
    sel_ref[0] = jnp.concatenate(masks, axis=0).T.astype(BF16)


def _nsa_select(main, kc, vct, overlap_t, small, tq, ns, n_sel):
    b, t, _ = main.shape
    ncp = kc.shape[1]
    nsp = overlap_t.shape[0]
    return pl.pallas_call(
        partial(_nsa_select_kernel, tq=tq, ns=ns, n_sel=n_sel),
        grid=(b, t // tq),
        in_specs=[pl.BlockSpec((1, tq, NSA_W), lambda bi, i: (bi, i, CH_NQ * LANES // NSA_W)),
                  pl.BlockSpec((1, ncp, LANES), lambda bi, i: (bi, 0, 0)),
                  pl.BlockSpec((1, LANES, ncp), lambda bi, i: (bi, 0, 0)),
                  pl.BlockSpec((nsp, ncp), lambda bi, i: (0, 0)),
                  pl.BlockSpec((1, tq, LANES), lambda bi, i: (bi, i, 0))],
        out_specs=[pl.BlockSpec((1, tq, NSA_W), lambda bi, i: (bi, i, 0)),
                   pl.BlockSpec((1, tq, NSA_KV_GROUPS * nsp), lambda bi, i: (bi, i, 0))],
        out_shape=[jax.ShapeDtypeStruct((b, t, NSA_W), F32),
                   jax.ShapeDtypeStruct((b, t, NSA_KV_GROUPS * nsp), BF16)],
        compiler_params=_cparams(("parallel", "parallel")),
        name="nsa_select",
    )(main, kc, vct, overlap_t, small)


def _nsa_flash_kernel(q_ref, sel_ref, oc_ref, small_ref, ks_ref, kw_ref, hot_ref, vst_ref, vwt_ref, *rest,
                      i, tq, tk):
    o_ref, st_ref, acc_ref = rest[-3:]
    n = pl.program_id(1)
    q0 = i * tq
    q2 = q_ref[0]
    small = small_ref[0]
    qt = _transposed(q2)
    selt = _transposed(sel_ref[0])
    qhs = [_half_mask_t(qt, g) for g in range(NSA_KV_GROUPS)]
    qcats = [jnp.concatenate([qhs[g], _half_mask_t(selt, g)], axis=0) for g in range(NSA_KV_GROUPS)]
    rows = lambda g: slice(g * HEAD_DIM, (g + 1) * HEAD_DIM)

    def sel_scores(j):
        ks = j * tk
        kcat = jnp.concatenate([ks_ref[0, pl.ds(ks, tk), :], hot_ref[pl.ds(ks, tk), :]], axis=1)
        return tuple(_dot(kcat, qcats[g]) for g in range(NSA_KV_GROUPS))

    def sel_mask(j, st):
        kpos, qpos = _positions_t(j, tk, q0, tq)
        return jnp.where(kpos <= qpos, st, NEG)

    def win_scores(j):
        kw = kw_ref[0, pl.ds(j * tk, tk), :]
        return tuple(_dot(kw, qhs[g]) for g in range(NSA_KV_GROUPS))

    def win_mask(j, st):
        kpos, qpos = _positions_t(j, tk, q0, tq)
        return jnp.where((kpos <= qpos) & (kpos > qpos - WINDOW), st, NEG)

    values = lambda ref: lambda g, j: ref[0, rows(g), pl.ds(j * tk, tk)]
    lane = _lane_iota()
    n_end = (i + 1) * (tq // tk)
    win_lo = max(n_end - (WINDOW + tq) // tk, 0)
    o = oc_ref[0]
    for branch, args in ((1, (0, i * (tq // tk), tq // tk, sel_scores, sel_mask, values(vst_ref))),
                         (2, (win_lo, 0, n_end - win_lo, win_scores, win_mask, values(vwt_ref)))):
        res = _stream(*args, st_ref, acc_ref)
        ot = jnp.concatenate([acc / l for l, acc in res], axis=0)
        gate = jnp.where(lane < HEAD_DIM, _gate_col(small, n, branch), _gate_col(small, NSA_HPG + n, branch))
        o = o + gate * ot.T
    o_ref[0] = o.astype(BF16)


def _nsa_flash(main, sel, oc, small, hot, vt, tq, tk):
    b, t, _ = main.shape
    tile = lambda i, ch: pl.BlockSpec((1, tq, LANES), lambda bi, n: (bi, i, ch + n))
    shared = lambda i: pl.BlockSpec((1, tq, LANES), lambda bi, n: (bi, i, 0))
    full = lambda ch: pl.BlockSpec((1, t, LANES), lambda bi, n: (bi, 0, ch))
    vfull = lambda ch: pl.BlockSpec((1, LANES, t), lambda bi, n: (bi, ch, 0))
    in_specs = lambda i: [tile(i, CH_NQ), shared(i), tile(i, 0), shared(i), full(CH_KS), full(CH_KW),
                          pl.BlockSpec((t, LANES), lambda bi, n: (0, 0)), vfull(VT_VS), vfull(VT_VW)]
    return _per_query_tile(
        partial(_nsa_flash_kernel, tq=tq, tk=tk), t // tq, (b, NSA_HPG), in_specs, lambda i: tile(i, 0),
        jax.ShapeDtypeStruct((b, t, NSA_W), BF16), _stream_scratch(NSA_KV_GROUPS, HEAD_DIM, tq, tk), "nsa_flash",
        (main, sel, oc, small, main, main, hot, vt, vt))


def _diff_kernel(q_ref, k_ref, vt_ref, lam_ref, g_ref, *rest, i, tq, tk, lam_init):
    o_ref, st_ref, acc_ref = rest[-3:]
    q2 = q_ref[0]
    lp = lam_ref[...]
    lam = (jnp.exp(jnp.sum(lp[0:1] * lp[1:2], axis=-1, keepdims=True))
           - jnp.exp(jnp.sum(lp[2:3] * lp[3:4], axis=-1, keepdims=True)) + lam_init)
    qhs = [_half_mask(q2, comp) for comp in range(2)]

    def scores(j):
        k2 = k_ref[0, pl.ds(j * tk, tk), :]
        return tuple(_dot_nt(k2, qhs[comp]) for comp in range(2))

    def mask(j, st):
        kpos, qpos = _positions_t(j, tk, i * tq, tq)
        return jnp.where(kpos <= qpos, st, NEG)

    def values(comp, j):
        return vt_ref[0, :, pl.ds(j * tk, tk)]

    (l1, acc1), (l2, acc2) = _stream(0, i * (tq // tk), tq // tk, scores, mask, values, st_ref, acc_ref)
    o = (acc1 / l1 - lam * (acc2 / l2)).T
    o_ref[0] = (_rms(o, g_ref[...]) * (1.0 - lam_init)).astype(BF16)


def _diff_attention(main, vt, lam_p, subln_g, tq, tk, lam_init):
    b, t, _ = main.shape
    nh = DIFF_HEADS
    in_specs = lambda i: [pl.BlockSpec((1, tq, LANES), lambda bi, h: (bi, i, h)),
                          pl.BlockSpec((1, t, LANES), lambda bi, h: (bi, 0, nh + h)),
                          pl.BlockSpec((1, LANES, t), lambda bi, h: (bi, h, 0)),
                          pl.BlockSpec((4, LANES), lambda bi, h: (0, 0)),
                          pl.BlockSpec((1, LANES), lambda bi, h: (0, 0))]
    return _per_query_tile(
        partial(_diff_kernel, tq=tq, tk=tk, lam_init=lam_init), t // tq, (b, nh), in_specs,
        lambda i: pl.BlockSpec((1, tq, LANES), lambda bi, h: (bi, i, h)),
        jax.ShapeDtypeStruct((b, t, nh * LANES), BF16), _stream_scratch(2, LANES, tq, tk), "diff_attention",
        (main, main, vt, lam_p, subln_g))


def _mem_kv_kernel(mem_ref, g_ref, wk_ref, wv_ref, k_ref, vt_ref):
    mn = _rms(mem_ref[0], g_ref[0]).astype(BF16)
    k_ref[0, 0] = _dot(mn, wk_ref[0]).astype(BF16)
    vt_ref[0, 0] = _dot(mn, wv_ref[0]).T.astype(BF16)


def _mem_kv(mem, mem_norm_g, wk, wv):
    depth = wk.shape[0]
    b, m, d = mem.shape
    wspec = pl.BlockSpec((1, d, MEM_W), lambda l, bi: (l, 0, 0))
    return pl.pallas_call(
        _mem_kv_kernel,
        grid=(depth, b),
        in_specs=[pl.BlockSpec((1, m, d), lambda l, bi: (bi, 0, 0)),
                  pl.BlockSpec((1, 1, d), lambda l, bi: (l, 0, 0)), wspec, wspec],
        out_specs=[pl.BlockSpec((1, 1, m, MEM_W), lambda l, bi: (l, bi, 0, 0)),
                   pl.BlockSpec((1, 1, MEM_W, m), lambda l, bi: (l, bi, 0, 0))],
        out_shape=[jax.ShapeDtypeStruct((depth, b, m, MEM_W), BF16),
                   jax.ShapeDtypeStruct((depth, b, MEM_W, m), BF16)],
        compiler_params=_cparams(("parallel", "parallel")),
        name="mem_kv",
    )(mem, mem_norm_g.reshape(depth, 1, d), wk, wv)


def _post_mixer_kernel(*refs, n_in, n_part):
    a_refs, w_refs = refs[:n_in], refs[n_in:2 * n_in]
    x_ref, g1_ref, g2_ref, g3_ref, wq_ref, k_ref, vt_ref, wo_ref, o_ref = refs[2 * n_in:]
    rows = x_ref.shape[0] // n_part
    parts = [slice(r * rows, (r + 1) * rows) for r in range(n_part)]
    n_mem = k_ref.shape[2]
    heads = [(ch, hh) for ch in range(MEM_W // LANES) for hh in range(2)]
    ones = jnp.ones((ONES_ROWS, n_mem), BF16)

    def mixer_out(ps):
        y = None
        for a_ref, w_ref in zip(a_refs, w_refs):
            t = _dot(a_ref[ps, :], w_ref[...])
            y = t if y is None else y + t
        return y

    def scores(q):
        return [_dot_nt(k_ref[0, 0, :, ch * LANES:(ch + 1) * LANES], _half_mask(q[:, ch * LANES:(ch + 1) * LANES], hh))
                for ch, hh in heads]

    def attend(logits):
        outs = []
        for (ch, hh), st in zip(heads, logits):
            p = jnp.exp2(st - jnp.max(st, axis=0, keepdims=True)).astype(BF16)
            r0 = ch * LANES + hh * HEAD_DIM
            pv = _dot(jnp.concatenate([vt_ref[0, 0, r0:r0 + HEAD_DIM, :], ones], axis=0), p)
            outs.append(pv[:HEAD_DIM] / pv[HEAD_DIM:HEAD_DIM + 1])
        return jnp.concatenate(outs, axis=0).T.astype(BF16)

    ys = [mixer_out(ps) for ps in parts]
    xs = [x_ref[ps, :] + _rms(y, g1_ref[...]) for ps, y in zip(parts, ys)]
    qs = [(_dot(_rms(x, g2_ref[...]).astype(BF16), wq_ref[...]) * Q_SCALE_LOG2).astype(BF16) for x in xs]
    logits = [scores(q) for q in qs]
    os_ = [attend(lg) for lg in logits]
    for ps, x, o in zip(parts, xs, os_):
        o_ref[ps, :] = x + _rms(_dot(o, wo_ref[...]), g3_ref[...])


def _post_mixer(acts, weights, x2, g1, g2, g3, wq, mem_k, mem_vt, wo, layer, tm, rows_per_batch, n_part=2):
    n = x2.shape[0]
    m = mem_k.shape[2]
    const = lambda shape: pl.BlockSpec(shape, lambda i: (0, 0))
    row = pl.BlockSpec((tm, D_MODEL), lambda i: (i, 0))
    gspec = const((1, D_MODEL))
    return pl.pallas_call(
        partial(_post_mixer_kernel, n_in=len(acts), n_part=n_part),
        grid=(n // tm,),
        in_specs=[pl.BlockSpec((tm, a.shape[1]), lambda i: (i, 0)) for a in acts]
                 + [const(w.shape) for w in weights]
                 + [row, gspec, gspec, gspec, const((D_MODEL, MEM_W)),
                    pl.BlockSpec((1, 1, m, MEM_W), lambda i: (layer, i // rows_per_batch, 0, 0)),
                    pl.BlockSpec((1, 1, MEM_W, m), lambda i: (layer, i // rows_per_batch, 0, 0)),
                    const((MEM_W, D_MODEL))],
        out_specs=row,
        out_shape=jax.ShapeDtypeStruct((n, D_MODEL), F32),
        compiler_params=_cparams(("parallel",)),
        name="post_mixer",
    )(*acts, *weights, x2, g1, g2, g3, wq, mem_k, mem_vt, wo)


def _ffn_kernel(x_ref, gin_ref, gout_ref, wg_ref, wu_ref, wd_ref, o_ref, *, chunk):
    x = x_ref[...]
    h = _rms(x, gin_ref[...]).astype(BF16)
    d_ff = wg_ref.shape[1]
    y = None
    for c0 in range(0, d_ff, chunk):
        gate = _dot(h, wg_ref[:, c0:c0 + chunk])
        up = _dot(h, wu_ref[:, c0:c0 + chunk])
        a = (gate * (1.0 / (1.0 + jnp.exp(-gate))) * up).astype(BF16)
        t = _dot(a, wd_ref[c0:c0 + chunk, :])
        y = t if y is None else y + t
    o_ref[...] = x + _rms(y, gout_ref[...])


def _ffn(x2, gin, gout, wg, wu, wd, tm, chunk):
    n = x2.shape[0]
    d_ff = wg.shape[1]
    const = lambda shape: pl.BlockSpec(shape, lambda i: (0, 0), pipeline_mode=pl.Buffered(1))
    return pl.pallas_call(
        partial(_ffn_kernel, chunk=chunk),
        grid=(n // tm,),
        in_specs=[pl.BlockSpec((tm, D_MODEL), lambda i: (i, 0)), const((1, D_MODEL)), const((1, D_MODEL)),
                  const((D_MODEL, d_ff)), const((D_MODEL, d_ff)), const((d_ff, D_MODEL))],
        out_specs=pl.BlockSpec((tm, D_MODEL), lambda i: (i, 0)),
        out_shape=jax.ShapeDtypeStruct((n, D_MODEL), F32),
        compiler_params=_cparams(("parallel",)),
        name="ffn",
    )(x2, gin, gout, wg, wu, wd)


def _even_weights(w_in, w_out):
    kvw = NSA_KV_GROUPS * HEAD_DIM
    offs = [int(o) for o in np.cumsum((FOX_W, FOX_W, FOX_W, FOX_HEADS, NSA_W) + (kvw,) * 6 + (3 * NSA_HEADS,))]
    fk0, fv0, fl0, nq0, kc0, vc0, ks0, vs0, kw0, vw0, gl0, end = offs
    perm = np.concatenate([np.arange(HEAD_DIM) + HEAD_DIM * (g * NSA_HPG + n)
                           for n in range(NSA_HPG) for g in range(NSA_KV_GROUPS)])
    cols = lambda a, b: w_in[:, a:b]
    w = jnp.concatenate([cols(0, fv0), w_in[:, nq0 + perm], cols(ks0, vs0), cols(kw0, vw0),
                         cols(kc0, vc0), cols(vc0, ks0),
                         cols(fv0, fl0), cols(vs0, kw0), cols(vw0, gl0)], axis=1).astype(BF16)
    w_small = jnp.concatenate([cols(fl0, nq0), cols(gl0, end),
                               jnp.zeros((D_MODEL, LANES - FOX_HEADS - 3 * NSA_HEADS), w_in.dtype)],
                              axis=1).astype(BF16)
    w_out_fox = w_out[:FOX_W].astype(BF16)
    w_out_nsa = w_out[FOX_W + perm].astype(BF16)
    return w, w_small, w_out_fox, w_out_nsa


def _overlap_matrix_t(t, ncp):
    nc = (t - CMP_BLOCK) // CMP_STRIDE + 1
    ns = t // SLC_BLOCK
    cs = np.arange(nc) * CMP_STRIDE
    ss = np.arange(ns) * SLC_BLOCK
    ov = np.clip(np.minimum(cs[:, None] + CMP_BLOCK, ss[None, :] + SLC_BLOCK)
                 - np.maximum(cs[:, None], ss[None, :]), 0, None) / CMP_BLOCK
    full = np.zeros((LANES // NSA_KV_GROUPS, ncp), np.float32)
    full[:ns, :nc] = ov.T
    return jnp.asarray(full, BF16), ns


def _compress_weights(pos_k, w1_k, w2_k, pos_v, w1_v, w2_v):
    g = NSA_KV_GROUPS
    pos2, w1bd, w2bd = [], [], []
    for pos, w1, w2 in ((pos_k, w1_k, w2_k), (pos_v, w1_v, w2_v)):
        pos2.append(jnp.tile(pos.astype(F32), (1, g)))
        w1l = w1.reshape(CMP_BLOCK, HEAD_DIM, CMP_HIDDEN).astype(BF16)
        bd = jnp.zeros((CMP_BLOCK, g * HEAD_DIM, g * CMP_HIDDEN), BF16)
        w2g = jnp.zeros((g * CMP_HIDDEN, g * HEAD_DIM), BF16)
        for gi in range(g):
            bd = bd.at[:, gi * HEAD_DIM:(gi + 1) * HEAD_DIM, gi * CMP_HIDDEN:(gi + 1) * CMP_HIDDEN].set(w1l)
            w2g = w2g.at[gi * CMP_HIDDEN:(gi + 1) * CMP_HIDDEN, gi * HEAD_DIM:(gi + 1) * HEAD_DIM].set(
                w2.astype(BF16))
        w1bd.append(bd)
        w2bd.append(w2g)
    return jnp.stack(pos2), jnp.stack(w1bd), jnp.stack(w2bd)


def kernel(x, mem, positions, sandwich_g, mem_norm_g, ev_w_in, ev_fox_fbias, ev_cmp_pos_k, ev_cmp_w1_k, ev_cmp_w2_k, ev_cmp_pos_v, ev_cmp_w1_v, ev_cmp_w2_v, ev_w_out, od_w_in, od_lambda, od_subln_g, od_w_out, ca_wq, ca_wk, ca_wv, ca_wo, ffn_wg, ffn_wu, ffn_wd):
    b, t, d = x.shape
    depth = sandwich_g.shape[0]
    n = b * t
    tm = 512
    tm_ffn = 512
    tm_mix = 1024
    tq, tk = 512, 256
    tk_diff = 512
    assert d == D_MODEL and t % tq == 0 and tq % tk == 0 and WINDOW % tk == 0 and t % tm == 0 and n % tm_ffn == 0

    tabs = tuple(a.reshape(b, t, LANES) for a in _rope_tables(positions, 512))
    mem_k, mem_vt = _mem_kv(mem, mem_norm_g, ca_wk.astype(BF16), ca_wv.astype(BF16))
    ncp = t // CMP_STRIDE
    overlap_t, ns = _overlap_matrix_t(t, ncp)
    assert ns <= overlap_t.shape[0]
    n_sel = min(SLC_TOPK, ns)
    hot = jnp.asarray((np.arange(t)[:, None] // SLC_BLOCK == np.arange(LANES)[None, :] % overlap_t.shape[0])
                      .astype(np.float32), BF16)
    gain = lambda l, j: sandwich_g[l, j].reshape(1, d)

    x2 = x.reshape(n, d)
    for layer in range(depth):
        x3 = x2.reshape(b, t, d)
        if layer % 2 == 0:
            e = layer // 2
            w, w_small, w_out_fox, w_out_nsa = _even_weights(ev_w_in[e], ev_w_out[e])
            fb_row = jnp.zeros((1, LANES), F32).at[0, :FOX_HEADS].set(ev_fox_fbias[e].astype(F32))
            main, cmp_in, vt, small = _even_proj(x3, gain(layer, 0), w, w_small, fb_row, tabs, tm)
            aq, ak = _fox_aug(small, 512)
            o_fox = _fox_attention(main, aq, ak, vt, tq, tk)
            kc, vct = _compress(cmp_in, *_compress_weights(
                ev_cmp_pos_k[e], ev_cmp_w1_k[e], ev_cmp_w2_k[e], ev_cmp_pos_v[e], ev_cmp_w1_v[e], ev_cmp_w2_v[e]))
            oc, sel = _nsa_select(main, kc, vct, overlap_t, small, 256, ns, n_sel)
            o_nsa = _nsa_flash(main, sel, oc, small, hot, vt, tq, tk)
            acts, w_outs = [o_fox.reshape(n, FOX_W), o_nsa.reshape(n, NSA_W)], [w_out_fox, w_out_nsa]
        else:
            o = layer // 2
            w_in = od_w_in[o].astype(BF16)
            main, vt = _odd_proj(x3, gain(layer, 0), w_in, tabs, tm)
            lam_init = 0.8 - 0.6 * math.exp(-0.3 * layer)
            lam_p = jnp.pad(od_lambda[o].astype(F32), ((0, 0), (0, LANES - HEAD_DIM)))
            attn = _diff_attention(main, vt, lam_p, od_subln_g[o].reshape(1, LANES).astype(F32), tq, tk_diff,
                                   lam_init)
            acts, w_outs = [attn.reshape(n, D_MODEL)], [od_w_out[o].astype(BF16)]
        x2 = _post_mixer(acts, w_outs, x2, gain(layer, 1), gain(layer, 2), gain(layer, 3),
                         ca_wq[layer].astype(BF16), mem_k, mem_vt, ca_wo[layer].astype(BF16), layer,
                         tm_mix, t // tm_mix, n_part=tm_mix // 256)
        x2 = _ffn(x2, gain(layer, 4), gain(layer, 5), ffn_wg[layer].astype(BF16), ffn_wu[layer].astype(BF16),
                  ffn_wd[layer].astype(BF16), tm_ffn, 256)
    return x2.reshape(b, t, d)
```

```python
import math
from functools import partial

import numpy as np
import jax
import jax.numpy as jnp
from jax import lax
from jax.experimental import pallas as pl
from jax.experimental.pallas import tpu as pltpu

F32 = jnp.float32
BF16 = jnp.bfloat16

D_MODEL = 1024
HEAD_DIM = 64
LANES = 128
ROPE_DIM = HEAD_DIM // 4
ROPE_THETA = 500000.0
FOX_HEADS = 8
NSA_HEADS = 8
NSA_KV_GROUPS = 2
NSA_HPG = NSA_HEADS // NSA_KV_GROUPS
CMP_BLOCK = 32
CMP_STRIDE = 16
CMP_HIDDEN = 2 * HEAD_DIM
SLC_BLOCK = 64
SLC_TOPK = 16
WINDOW = 512
DIFF_HEADS = 8
MEM_HEADS = 4
MEM_W = MEM_HEADS * HEAD_DIM
RMS_EPS = 1e-6
Q_SCALE = HEAD_DIM ** -0.5
LOG2E = math.log2(math.e)
Q_SCALE_LOG2 = Q_SCALE * LOG2E
NEG = -1e30
AUG_PER_HEAD = 6
ONES_ROWS = 16

FOX_W = FOX_HEADS * HEAD_DIM
NSA_W = NSA_HEADS * HEAD_DIM
CH_FQ, CH_FK, CH_NQ, CH_KS, CH_KW = 0, 4, 8, 12, 13
EV_MAIN = 14 * LANES
EV_CMP = 2 * LANES
VT_FV, VT_VS, VT_VW = 0, 4, 5
EV_VT = 6 * LANES
EV_ROPE_CHUNKS = tuple(range(CH_NQ, CH_NQ + 4)) + (CH_KS, CH_KW, EV_MAIN // LANES)
EV_QSCALE_CHUNKS = tuple(range(CH_FQ, CH_FQ + 4)) + tuple(range(CH_NQ, CH_NQ + 4))
OD_MAIN = 2 * D_MODEL
OD_VT = D_MODEL

VMEM_LIMIT = 56 * 1024 * 1024


def _cparams(sem):
    return pltpu.CompilerParams(dimension_semantics=sem, vmem_limit_bytes=VMEM_LIMIT)


def _rms(x, g):
    return x * lax.rsqrt(jnp.mean(x * x, axis=-1, keepdims=True) + RMS_EPS) * g


def _split3(x):
    hi = x.astype(BF16)
    r1 = x - hi.astype(F32)
    mid = r1.astype(BF16)
    lo = (r1 - mid.astype(F32)).astype(BF16)
    return hi, mid, lo


def _dot(a, b):
    return jnp.dot(a, b, preferred_element_type=F32)


def _dot_nt(a, b):
    return lax.dot_general(a, b, (((1,), (1,)), ((), ())), preferred_element_type=F32)


def _lane_iota(n=LANES):
    return lax.broadcasted_iota(jnp.int32, (1, n), 1)


def _half_mask(q2, half):
    return jnp.where(_lane_iota() // HEAD_DIM == half, q2, jnp.zeros_like(q2))


def _transposed(x):
    return x.astype(F32).T.astype(BF16)


def _half_mask_t(qt, half):
    row = lax.broadcasted_iota(jnp.int32, (qt.shape[0], 1), 0)
    return jnp.where(row // HEAD_DIM == half, qt, jnp.zeros_like(qt))


def _positions_t(j, tk, q0, tq):
    kpos = j * tk + lax.broadcasted_iota(jnp.int32, (tk, 1), 0)
    qpos = q0 + lax.broadcasted_iota(jnp.int32, (1, tq), 1)
    return kpos, qpos


def _stream(base, n_full, n_tail, scores, mask, values, st_ref, acc_ref):
    n_chain, dv, tq = acc_ref.shape

    def park(j, slot):
        for c, st in enumerate(scores(j)):
            st_ref[slot, c] = st

    def step(j, slot, stats, masked, prefetch):
        if prefetch:
            park(j + 1, 1 - slot)
        parts = []
        for c in range(n_chain):
            st = st_ref[slot, c]
            if masked:
                st = mask(j, st)
            m, _ = stats[c]
            m_new = jnp.maximum(m, jnp.max(st, axis=0, keepdims=True))
            p = jnp.exp2(st - m_new).astype(BF16)
            vt1 = jnp.concatenate([values(c, j), jnp.ones((ONES_ROWS, p.shape[0]), BF16)], axis=0)
            parts.append((m_new, jnp.exp2(m - m_new), _dot(vt1, p)))
        out = []
        for c, (m_new, alpha, pv) in enumerate(parts):
            acc_ref[c] = alpha * acc_ref[c] + pv[:dv]
            out.append((m_new, alpha * stats[c][1] + pv[dv:dv + 1]))
        return tuple(out)

    acc_ref[...] = jnp.zeros_like(acc_ref)
    park(base, 0)
    stats = ((jnp.full((1, tq), NEG, F32), jnp.zeros((1, tq), F32)),) * n_chain
    n_tiles = n_full + n_tail
    for s_ in range(n_tiles):
        stats = step(base + s_, s_ % 2, stats, s_ >= n_full, s_ + 1 < n_tiles)
    return [(l, acc_ref[c]) for c, (_, l) in enumerate(stats)]


def _per_query_tile(kernel_fn, n_q, grid, in_specs_fn, out_spec_fn, out_shape, scratch, name, args):
    out = jnp.zeros(out_shape.shape, out_shape.dtype)
    for i in range(n_q):
        in_specs = list(in_specs_fn(i)) + [pl.BlockSpec(memory_space=pl.ANY)]
        out = pl.pallas_call(
            partial(kernel_fn, i=i), grid=grid, in_specs=in_specs, out_specs=out_spec_fn(i), out_shape=out_shape,
            scratch_shapes=scratch, input_output_aliases={len(args): 0},
            compiler_params=_cparams(("parallel",) * len(grid)), name=f"{name}_q{i}",
        )(*args, out)
    return out


def _stream_scratch(n_chain, dv, tq, tk):
    return [pltpu.VMEM((2, n_chain, tk, tq), F32), pltpu.VMEM((n_chain, dv, tq), F32)]


def _rope_kernel(pos_ref, inv_ref, m1_ref, m2_ref, c_ref, s1_ref, s2_ref):
    ang = pos_ref[...].astype(F32) * inv_ref[...]
    c_ref[...] = jnp.cos(ang)
    sn = jnp.sin(ang)
    s1_ref[...] = -sn * m1_ref[...]
    s2_ref[...] = sn * m2_ref[...]


def _rope_tables(positions, tm):
    n = positions.size
    inv = ROPE_THETA ** (-jnp.arange(0, ROPE_DIM, 2, dtype=F32) / ROPE_DIM)
    lane = np.arange(LANES) % HEAD_DIM
    half = ROPE_DIM // 2
    inv_l = jnp.where(lane < ROPE_DIM, inv[lane % half], 0.0).reshape(1, LANES).astype(F32)
    m1 = jnp.asarray((lane < half).astype(np.float32)).reshape(1, LANES)
    m2 = jnp.asarray(((lane >= half) & (lane < ROPE_DIM)).astype(np.float32)).reshape(1, LANES)
    row = pl.BlockSpec((1, LANES), lambda i: (0, 0))
    tab = pl.BlockSpec((tm, LANES), lambda i: (i, 0))
    return pl.pallas_call(
        _rope_kernel,
        grid=(n // tm,),
        in_specs=[pl.BlockSpec((tm, 1), lambda i: (i, 0)), row, row, row],
        out_specs=[tab, tab, tab],
        out_shape=[jax.ShapeDtypeStruct((n, LANES), F32)] * 3,
        compiler_params=_cparams(("parallel",)),
        name="rope_tables",
    )(positions.reshape(n, 1), inv_l, m1, m2)


def _apply_rope(y, c, s1, s2):
    half = ROPE_DIM // 2
    return y * c + pltpu.roll(y, LANES - half, 1) * s1 + pltpu.roll(y, half, 1) * s2


def _project_chunks(h, w_ref, tabs, dests, rope_chunks, qscale_chunks):
    c, s1, s2 = tabs
    wide = 2 * LANES
    where = [(kind, ref, k) for kind, ref, n in dests for k in range(n)]
    for ch2 in range(len(where) // 2):
        y2 = _dot(h, w_ref[:, ch2 * wide:(ch2 + 1) * wide])
        for ch in (2 * ch2, 2 * ch2 + 1):
            y = y2[:, (ch % 2) * LANES:(ch % 2 + 1) * LANES]
            if ch in rope_chunks:
                y = _apply_rope(y, c, s1, s2)
            if ch in qscale_chunks:
                y = y * Q_SCALE_LOG2
            kind, ref, k = where[ch]
            if kind == "cols":
                ref[0, k * LANES:(k + 1) * LANES, :] = y.T.astype(BF16)
            elif kind == "rows_f32":
                ref[0, k] = y
            else:
                ref[0, :, k * LANES:(k + 1) * LANES] = y.astype(BF16)


def _even_proj_kernel(x_ref, g_ref, w_ref, ws_ref, fb_ref, c_ref, s1_ref, s2_ref,
                      main_ref, cmp_ref, vt_ref, small_ref):
    h = _rms(x_ref[0], g_ref[...]).astype(BF16)
    dests = [("rows", main_ref, EV_MAIN // LANES), ("rows_f32", cmp_ref, EV_CMP // LANES),
             ("cols", vt_ref, EV_VT // LANES)]
    _project_chunks(h, w_ref, (c_ref[0], s1_ref[0], s2_ref[0]), dests, EV_ROPE_CHUNKS, EV_QSCALE_CHUNKS)
    ys = _dot(h, ws_ref[...])
    z = ys + fb_ref[...]
    log_f = jnp.minimum(z, 0.0) - jnp.log(1.0 + jnp.exp(-jnp.abs(z)))
    gate = 1.0 / (1.0 + jnp.exp(-ys))
    small_ref[0] = jnp.where(_lane_iota() < FOX_HEADS, log_f, gate)


def _even_proj(x3, g, w, w_small, fb_row, tabs, tm):
    b, t, _ = x3.shape
    const = lambda shape: pl.BlockSpec(shape, lambda bi, i: (0, 0))
    tab = pl.BlockSpec((1, tm, LANES), lambda bi, i: (bi, i, 0))
    return pl.pallas_call(
        _even_proj_kernel,
        grid=(b, t // tm),
        in_specs=[pl.BlockSpec((1, tm, D_MODEL), lambda bi, i: (bi, i, 0)), const((1, D_MODEL)),
                  const((D_MODEL, EV_MAIN + EV_CMP + EV_VT)), const((D_MODEL, LANES)), const((1, LANES)),
                  tab, tab, tab],
        out_specs=[pl.BlockSpec((1, tm, EV_MAIN), lambda bi, i: (bi, i, 0)),
                   pl.BlockSpec((1, EV_CMP // LANES, tm, LANES), lambda bi, i: (bi, 0, i, 0)),
                   pl.BlockSpec((1, EV_VT, tm), lambda bi, i: (bi, 0, i)), tab],
        out_shape=[jax.ShapeDtypeStruct((b, t, EV_MAIN), BF16),
                   jax.ShapeDtypeStruct((b, EV_CMP // LANES, t, LANES), F32),
                   jax.ShapeDtypeStruct((b, EV_VT, t), BF16), jax.ShapeDtypeStruct((b, t, LANES), F32)],
        compiler_params=_cparams(("parallel", "parallel")),
        name="even_proj",
    )(x3, g, w, w_small, fb_row, *tabs)


def _odd_proj_kernel(x_ref, g_ref, w_ref, c_ref, s1_ref, s2_ref, main_ref, vt_ref):
    h = _rms(x_ref[0], g_ref[...]).astype(BF16)
    n_main = OD_MAIN // LANES
    dests = [("rows", main_ref, n_main), ("cols", vt_ref, OD_VT // LANES)]
    _project_chunks(h, w_ref, (c_ref[0], s1_ref[0], s2_ref[0]), dests, tuple(range(n_main)),
                    tuple(range(n_main // 2)))


def _odd_proj(x3, g, w, tabs, tm):
    b, t, _ = x3.shape
    const = lambda shape: pl.BlockSpec(shape, lambda bi, i: (0, 0))
    tab = pl.BlockSpec((1, tm, LANES), lambda bi, i: (bi, i, 0))
    return pl.pallas_call(
        _odd_proj_kernel,
        grid=(b, t // tm),
        in_specs=[pl.BlockSpec((1, tm, D_MODEL), lambda bi, i: (bi, i, 0)), const((1, D_MODEL)),
                  const((D_MODEL, OD_MAIN + OD_VT)), tab, tab, tab],
        out_specs=[pl.BlockSpec((1, tm, OD_MAIN), lambda bi, i: (bi, i, 0)),
                   pl.BlockSpec((1, OD_VT, tm), lambda bi, i: (bi, 0, i))],
        out_shape=[jax.ShapeDtypeStruct((b, t, OD_MAIN), BF16), jax.ShapeDtypeStruct((b, OD_VT, t), BF16)],
        compiler_params=_cparams(("parallel", "parallel")),
        name="odd_proj",
    )(x3, g, w, *tabs)


def _fox_aug_kernel(lf_ref, tril_ref, eq_ref, ek_ref, oneq_ref, onek_ref, aq_ref, ak_ref, carry_ref):
    @pl.when(pl.program_id(1) == 0)
    def _():
        carry_ref[...] = jnp.zeros_like(carry_ref)

    tril = tril_ref[...]
    c = carry_ref[...]
    for piece in _split3(lf_ref[0]):
        c = c + _dot(tril, piece)
    carry_ref[...] = c[-1:, :]
    aq = oneq_ref[...]
    ak = onek_ref[...]
    for r, piece in enumerate(_split3(c * LOG2E)):
        aq = aq + _dot(piece, eq_ref[r])
        ak = ak - _dot(piece, ek_ref[r])
    aq_ref[0] = aq.astype(BF16)
    ak_ref[0] = ak.astype(BF16)


def _fox_aug(small, tc):
    b, t, _ = small.shape
    tril = jnp.asarray(np.tril(np.ones((tc, tc), np.float32)), BF16)
    eq = np.zeros((3, LANES, LANES), np.float32)
    ek = np.zeros((3, LANES, LANES), np.float32)
    oneq = np.zeros((1, LANES), np.float32)
    onek = np.zeros((1, LANES), np.float32)
    for h in range(FOX_HEADS):
        for r in range(3):
            eq[r, h, AUG_PER_HEAD * h + r] = 1.0
            ek[r, h, AUG_PER_HEAD * h + 3 + r] = 1.0
            oneq[0, AUG_PER_HEAD * h + 3 + r] = 1.0
            onek[0, AUG_PER_HEAD * h + r] = 1.0
    const2 = lambda shape: pl.BlockSpec(shape, lambda bi, i: (0,) * len(shape))
    blk = pl.BlockSpec((1, tc, LANES), lambda bi, i: (bi, i, 0))
    return pl.pallas_call(
        _fox_aug_kernel,
        grid=(b, t // tc),
        in_specs=[blk, const2((tc, tc)), const2((3, LANES, LANES)), const2((3, LANES, LANES)),
                  const2((1, LANES)), const2((1, LANES))],
        out_specs=[blk, blk],
        out_shape=[jax.ShapeDtypeStruct((b, t, LANES), BF16)] * 2,
        scratch_shapes=[pltpu.VMEM((1, LANES), F32)],
        compiler_params=_cparams(("parallel", "arbitrary")),
        name="fox_aug",
    )(small, tril, jnp.asarray(eq, BF16), jnp.asarray(ek, BF16), jnp.asarray(oneq), jnp.asarray(onek))


def _fox_kernel(q_ref, aq_ref, k_ref, ak_ref, vt_ref, *rest, i, tq, tk):
    o_ref, st_ref, acc_ref = rest[-3:]
    pair = pl.program_id(1)
    row = lax.broadcasted_iota(jnp.int32, (LANES, 1), 0)
    qt = _transposed(q_ref[0])
    qat = _transposed(aq_ref[0])
    qcats = []
    for hh in range(2):
        head = 2 * pair + hh
        in_head = (row >= AUG_PER_HEAD * head) & (row < AUG_PER_HEAD * (head + 1))
        qcats.append(jnp.concatenate([_half_mask_t(qt, hh), jnp.where(in_head, qat, jnp.zeros_like(qat))], axis=0))

    def scores(j):
        ks = j * tk
        kcat = jnp.concatenate([k_ref[0, pl.ds(ks, tk), :], ak_ref[0, pl.ds(ks, tk), :]], axis=1)
        return tuple(_dot(kcat, qcats[hh]) for hh in range(2))

    def mask(j, st):
        kpos, qpos = _positions_t(j, tk, i * tq, tq)
        return jnp.where(kpos <= qpos, st, NEG)

    def values(hh, j):
        return vt_ref[0, hh * HEAD_DIM:(hh + 1) * HEAD_DIM, pl.ds(j * tk, tk)]

    res = _stream(0, i * (tq // tk), tq // tk, scores, mask, values, st_ref, acc_ref)
    ot = jnp.concatenate([acc / l for l, acc in res], axis=0)
    o_ref[0] = ot.T.astype(BF16)


def _fox_attention(main, aq, ak, vt, tq, tk):
    b, t, _ = main.shape
    in_specs = lambda i: [pl.BlockSpec((1, tq, LANES), lambda bi, p: (bi, i, CH_FQ + p)),
                          pl.BlockSpec((1, tq, LANES), lambda bi, p: (bi, i, 0)),
                          pl.BlockSpec((1, t, LANES), lambda bi, p: (bi, 0, CH_FK + p)),
                          pl.BlockSpec((1, t, LANES), lambda bi, p: (bi, 0, 0)),
                          pl.BlockSpec((1, LANES, t), lambda bi, p: (bi, VT_FV + p, 0))]
    return _per_query_tile(
        partial(_fox_kernel, tq=tq, tk=tk), t // tq, (b, FOX_HEADS // 2), in_specs,
        lambda i: pl.BlockSpec((1, tq, LANES), lambda bi, p: (bi, i, p)),
        jax.ShapeDtypeStruct((b, t, FOX_W), BF16), _stream_scratch(2, HEAD_DIM, tq, tk), "fox_attention",
        (main, aq, main, ak, vt))


def _compress_kernel(x_ref, pos_ref, w1_ref, w2_ref, kc_ref, vct_ref):
    t = x_ref.shape[2]
    nchunk = t // CMP_STRIDE
    for kv, o_ref in enumerate((kc_ref, vct_ref)):
        first = second = None
        for l in range(CMP_STRIDE):
            xl = x_ref[0, kv, pl.ds(l, nchunk, stride=CMP_STRIDE), :]
            a = _dot((xl + pos_ref[kv, l:l + 1, :]).astype(BF16), w1_ref[kv, l])
            b = _dot((xl + pos_ref[kv, CMP_STRIDE + l:CMP_STRIDE + l + 1, :]).astype(BF16),
                     w1_ref[kv, CMP_STRIDE + l])
            first = a if first is None else first + a
            second = b if second is None else second + b
        hid = first + pltpu.roll(second, nchunk - 1, 0)
        out = _dot(jax.nn.gelu(hid, approximate=True).astype(BF16), w2_ref[kv])
        o_ref[0] = (out.T if o_ref is vct_ref else out).astype(BF16)


def _compress(cmp_in, pos2, w1bd, w2bd):
    b, _, t, _ = cmp_in.shape
    nchunk = t // CMP_STRIDE
    const = lambda a: pl.BlockSpec(a.shape, lambda bi: (0,) * a.ndim)
    return pl.pallas_call(
        _compress_kernel,
        grid=(b,),
        in_specs=[pl.BlockSpec((1, EV_CMP // LANES, t, LANES), lambda bi: (bi, 0, 0, 0)),
                  const(pos2), const(w1bd), const(w2bd)],
        out_specs=[pl.BlockSpec((1, nchunk, LANES), lambda bi: (bi, 0, 0)),
                   pl.BlockSpec((1, LANES, nchunk), lambda bi: (bi, 0, 0))],
        out_shape=[jax.ShapeDtypeStruct((b, nchunk, LANES), BF16), jax.ShapeDtypeStruct((b, LANES, nchunk), BF16)],
        compiler_params=_cparams(("parallel",)),
        name="nsa_compress",
    )(cmp_in, pos2, w1bd, w2bd)


def _gate_col(small, head, branch):
    idx = FOX_HEADS + 3 * head + branch
    return jnp.sum(jnp.where(_lane_iota() == idx, small, 0.0), axis=-1, keepdims=True)


def _nsa_select_kernel(q_ref, kc_ref, vct_ref, ovt_ref, small_ref, *rest, q_lo, tq, ns, n_sel):
    oc_ref, sel_ref = rest[-2:]
    q0 = q_lo + pl.program_id(1) * tq
    lane = _lane_iota()
    kc = kc_ref[0]
    vct = vct_ref[0]
    ncp = kc.shape[0]
    small = small_ref[0]
    qpos = q0 + lax.broadcasted_iota(jnp.int32, (1, tq), 1)
    cmp_end = lax.broadcasted_iota(jnp.int32, (ncp, 1), 0) * CMP_STRIDE + (CMP_BLOCK - 1)
    cmask = cmp_end <= qpos
    psum = [jnp.zeros((ncp, tq), F32) for _ in range(NSA_KV_GROUPS)]
    logits = [[_dot_nt(kc, _half_mask(q_ref[0, :, n * LANES:(n + 1) * LANES], g)) for g in range(NSA_KV_GROUPS)]
              for n in range(NSA_HPG)]
    for n in range(NSA_HPG):
        ots = []
        for g in range(NSA_KV_GROUPS):
            z = jnp.where(cmask, logits[n][g], -jnp.inf)
            m = jnp.max(z, axis=0, keepdims=True)
            m = jnp.where(m == -jnp.inf, 0.0, m)
            p = jnp.exp2(z - m)
            p = p / jnp.maximum(jnp.sum(p, axis=0, keepdims=True), 1e-30)
            psum[g] = psum[g] + p
            ots.append(_dot(vct[g * HEAD_DIM:(g + 1) * HEAD_DIM], p.astype(BF16)))
        gate = jnp.where(lane < HEAD_DIM, _gate_col(small, n, 0), _gate_col(small, NSA_HPG + n, 0))
        oc_ref[0, :, n * LANES:(n + 1) * LANES] = gate * jnp.concatenate(ots, axis=0).T

    nsp = ovt_ref.shape[0]
    blk = lax.broadcasted_iota(jnp.int32, (nsp, 1), 0)
    cur = qpos // SLC_BLOCK
    valid = blk * SLC_BLOCK <= qpos
    forced = (blk == 0) | (blk == cur) | (blk == cur - 1)
    scores = []
    for g in range(NSA_KV_GROUPS):
        imp = jnp.zeros((nsp, tq), F32)
        for piece in _split3(psum[g]):
            imp = imp + _dot(ovt_ref[...], piece)
        scores.append(jnp.where(valid, jnp.where(forced, jnp.inf, imp), -jnp.inf))
    slab = 8
    masks = []
    for g in range(NSA_KV_GROUPS):
        slabs = [scores[g][r:r + slab] for r in range(0, nsp, slab)]
        ranks = [jnp.zeros((slab, tq), jnp.int32) for _ in slabs]
        for i in range(ns):
            row = scores[g][i:i + 1, :]
            for r, sl in enumerate(slabs):
                if slab * r >= ns:
                    continue
                if slab * r > i:
                    ahead = row >= sl
                elif slab * (r + 1) - 1 <= i:
                    ahead = row > sl
                else:
                    ahead = (row > sl) | ((row == sl) & (blk[slab * r:slab * (r + 1)] > i))
                ranks[r] = jnp.where(ahead, ranks[r] + 1, ranks[r])
        rank = jnp.concatenate(ranks, axis=0)
        masks.append(jnp.where((rank < n_sel) & (blk < ns), 0.0, NEG))
    sel_ref[0] = jnp.concatenate(masks, axis=0).T.astype(BF16)


def _nsa_select(main, kc, vct, overlap_t, small, tq, ns, n_sel, n_span=4):
    b, t, _ = main.shape
    nsp = overlap_t.shape[0]
    span = t // n_span
    steps = span // tq
    shapes = [jax.ShapeDtypeStruct((b, t, NSA_W), F32), jax.ShapeDtypeStruct((b, t, NSA_KV_GROUPS * nsp), BF16)]
    outs = [jnp.zeros(s.shape, s.dtype) for s in shapes]
    for k in range(n_span):
        q_hi = (k + 1) * span
        ncl = min(-(-(q_hi // CMP_STRIDE) // LANES) * LANES, kc.shape[1])
        row = lambda bi, i, k=k: (bi, k * steps + i, 0)
        outs = pl.pallas_call(
            partial(_nsa_select_kernel, q_lo=k * span, tq=tq, ns=min(ns, q_hi // SLC_BLOCK), n_sel=n_sel),
            grid=(b, steps),
            in_specs=[pl.BlockSpec((1, tq, NSA_W), lambda bi, i, k=k: (bi, k * steps + i, CH_NQ * LANES // NSA_W)),
                      pl.BlockSpec((1, ncl, LANES), lambda bi, i: (bi, 0, 0)),
                      pl.BlockSpec((1, LANES, ncl), lambda bi, i: (bi, 0, 0)),
                      pl.BlockSpec((nsp, ncl), lambda bi, i: (0, 0)),
                      pl.BlockSpec((1, tq, LANES), row),
                      pl.BlockSpec(memory_space=pl.ANY), pl.BlockSpec(memory_space=pl.ANY)],
            out_specs=[pl.BlockSpec((1, tq, NSA_W), row), pl.BlockSpec((1, tq, NSA_KV_GROUPS * nsp), row)],
            out_shape=shapes,
            input_output_aliases={5: 0, 6: 1},
            compiler_params=_cparams(("parallel", "parallel")),
            name=f"nsa_select_s{k}",
        )(main, kc, vct, overlap_t, small, *outs)
    return outs


def _nsa_flash_kernel(q_ref, sel_ref, oc_ref, small_ref, ks_ref, kw_ref, hot_ref, vst_ref, vwt_ref, *rest,
                      i, tq, tk):
    o_ref, st_ref, acc_ref = rest[-3:]
    n = pl.program_id(1)
    q0 = i * tq
    q2 = q_ref[0]
    small = small_ref[0]
    qt = _transposed(q2)
    selt = _transposed(sel_ref[0])
    qhs = [_half_mask_t(qt, g) for g in range(NSA_KV_GROUPS)]
    qcats = [jnp.concatenate([qhs[g], _half_mask_t(selt, g)], axis=0) for g in range(NSA_KV_GROUPS)]
    rows = lambda g: slice(g * HEAD_DIM, (g + 1) * HEAD_DIM)

    def sel_scores(j):
        ks = j * tk
        kcat = jnp.concatenate([ks_ref[0, pl.ds(ks, tk), :], hot_ref[pl.ds(ks, tk), :]], axis=1)
        return tuple(_dot(kcat, qcats[g]) for g in range(NSA_KV_GROUPS))

    def sel_mask(j, st):
        kpos, qpos = _positions_t(j, tk, q0, tq)
        return jnp.where(kpos <= qpos, st, NEG)

    def win_scores(j):
        kw = kw_ref[0, pl.ds(j * tk, tk), :]
        return tuple(_dot(kw, qhs[g]) for g in range(NSA_KV_GROUPS))

    def win_mask(j, st):
        kpos, qpos = _positions_t(j, tk, q0, tq)
        return jnp.where((kpos <= qpos) & (kpos > qpos - WINDOW), st, NEG)

    values = lambda ref: lambda g, j: ref[0, rows(g), pl.ds(j * tk, tk)]
    lane = _lane_iota()
    n_end = (i + 1) * (tq // tk)
    win_lo = max(n_end - (WINDOW + tq) // tk, 0)
    o = oc_ref[0]
    for branch, args in ((1, (0, i * (tq // tk), tq // tk, sel_scores, sel_mask, values(vst_ref))),
                         (2, (win_lo, 0, n_end - win_lo, win_scores, win_mask, values(vwt_ref)))):
        res = _stream(*args, st_ref, acc_ref)
        ot = jnp.concatenate([acc / l for l, acc in res], axis=0)
        gate = jnp.where(lane < HEAD_DIM, _gate_col(small, n, branch), _gate_col(small, NSA_HPG + n, branch))
        o = o + gate * ot.T
    o_ref[0] = o.astype(BF16)


def _nsa_flash(main, sel, oc, small, hot, vt, tq, tk):
    b, t, _ = main.shape
    tile = lambda i, ch: pl.BlockSpec((1, tq, LANES), lambda bi, n: (bi, i, ch + n))
    shared = lambda i: pl.BlockSpec((1, tq, LANES), lambda bi, n: (bi, i, 0))
    full = lambda ch: pl.BlockSpec((1, t, LANES), lambda bi, n: (bi, 0, ch))
    vfull = lambda ch: pl.BlockSpec((1, LANES, t), lambda bi, n: (bi, ch, 0))
    in_specs = lambda i: [tile(i, CH_NQ), shared(i), tile(i, 0), shared(i), full(CH_KS), full(CH_KW),
                          pl.BlockSpec((t, LANES), lambda bi, n: (0, 0)), vfull(VT_VS), vfull(VT_VW)]
    return _per_query_tile(
        partial(_nsa_flash_kernel, tq=tq, tk=tk), t // tq, (b, NSA_HPG), in_specs, lambda i: tile(i, 0),
        jax.ShapeDtypeStruct((b, t, NSA_W), BF16), _stream_scratch(NSA_KV_GROUPS, HEAD_DIM, tq, tk), "nsa_flash",
        (main, sel, oc, small, main, main, hot, vt, vt))


def _diff_kernel(q_ref, k_ref, vt_ref, lam_ref, g_ref, *rest, i, tq, tk, lam_init):
    o_ref, st_ref, acc_ref = rest[-3:]
    q2 = q_ref[0]
    lp = lam_ref[...]
    lam = (jnp.exp(jnp.sum(lp[0:1] * lp[1:2], axis=-1, keepdims=True))
           - jnp.exp(jnp.sum(lp[2:3] * lp[3:4], axis=-1, keepdims=True)) + lam_init)
    qhs = [_half_mask(q2, comp) for comp in range(2)]

    def scores(j):
        k2 = k_ref[0, pl.ds(j * tk, tk), :]
        return tuple(_dot_nt(k2, qhs[comp]) for comp in range(2))

    def mask(j, st):
        kpos, qpos = _positions_t(j, tk, i * tq, tq)
        return jnp.where(kpos <= qpos, st, NEG)

    def values(comp, j):
        return vt_ref[0, :, pl.ds(j * tk, tk)]

    (l1, acc1), (l2, acc2) = _stream(0, i * (tq // tk), tq // tk, scores, mask, values, st_ref, acc_ref)
    o = (acc1 / l1 - lam * (acc2 / l2)).T
    o_ref[0] = (_rms(o, g_ref[...]) * (1.0 - lam_init)).astype(BF16)


def _diff_attention(main, vt, lam_p, subln_g, tq, tk, lam_init):
    b, t, _ = main.shape
    nh = DIFF_HEADS
    in_specs = lambda i: [pl.BlockSpec((1, tq, LANES), lambda bi, h: (bi, i, h)),
                          pl.BlockSpec((1, t, LANES), lambda bi, h: (bi, 0, nh + h)),
                          pl.BlockSpec((1, LANES, t), lambda bi, h: (bi, h, 0)),
                          pl.BlockSpec((4, LANES), lambda bi, h: (0, 0)),
                          pl.BlockSpec((1, LANES), lambda bi, h: (0, 0))]
    return _per_query_tile(
        partial(_diff_kernel, tq=tq, tk=tk, lam_init=lam_init), t // tq, (b, nh), in_specs,
        lambda i: pl.BlockSpec((1, tq, LANES), lambda bi, h: (bi, i, h)),
        jax.ShapeDtypeStruct((b, t, nh * LANES), BF16), _stream_scratch(2, LANES, tq, tk), "diff_attention",
        (main, main, vt, lam_p, subln_g))


def _mem_kv_kernel(mem_ref, g_ref, wk_ref, wv_ref, k_ref, vt_ref):
    mn = _rms(mem_ref[0], g_ref[0]).astype(BF16)
    k_ref[0, 0] = _dot(mn, wk_ref[0]).astype(BF16)
    vt_ref[0, 0] = _dot(mn, wv_ref[0]).T.astype(BF16)


def _mem_kv(mem, mem_norm_g, wk, wv):
    depth = wk.shape[0]
    b, m, d = mem.shape
    wspec = pl.BlockSpec((1, d, MEM_W), lambda l, bi: (l, 0, 0))
    return pl.pallas_call(
        _mem_kv_kernel,
        grid=(depth, b),
        in_specs=[pl.BlockSpec((1, m, d), lambda l, bi: (bi, 0, 0)),
                  pl.BlockSpec((1, 1, d), lambda l, bi: (l, 0, 0)), wspec, wspec],
        out_specs=[pl.BlockSpec((1, 1, m, MEM_W), lambda l, bi: (l, bi, 0, 0)),
                   pl.BlockSpec((1, 1, MEM_W, m), lambda l, bi: (l, bi, 0, 0))],
        out_shape=[jax.ShapeDtypeStruct((depth, b, m, MEM_W), BF16),
                   jax.ShapeDtypeStruct((depth, b, MEM_W, m), BF16)],
        compiler_params=_cparams(("parallel", "parallel")),
        name="mem_kv",
    )(mem, mem_norm_g.reshape(depth, 1, d), wk, wv)


def _post_mixer_kernel(*refs, n_in, n_part):
    a_refs, w_refs = refs[:n_in], refs[n_in:2 * n_in]
    x_ref, g1_ref, g2_ref, g3_ref, wq_ref, k_ref, vt_ref, wo_ref, o_ref = refs[2 * n_in:]
    rows = x_ref.shape[0] // n_part
    parts = [slice(r * rows, (r + 1) * rows) for r in range(n_part)]
    n_mem = k_ref.shape[2]
    heads = [(ch, hh) for ch in range(MEM_W // LANES) for hh in range(2)]
    ones = jnp.ones((ONES_ROWS, n_mem), BF16)

    def mixer_out(ps):
        y = None
        for a_ref, w_ref in zip(a_refs, w_refs):
            t = _dot(a_ref[ps, :], w_ref[...])
            y = t if y is None else y + t
        return y

    def scores(q):
        return [_dot_nt(k_ref[0, 0, :, ch * LANES:(ch + 1) * LANES], _half_mask(q[:, ch * LANES:(ch + 1) * LANES], hh))
                for ch, hh in heads]

    def attend(logits):
        outs = []
        for (ch, hh), st in zip(heads, logits):
            p = jnp.exp2(st - jnp.max(st, axis=0, keepdims=True)).astype(BF16)
            r0 = ch * LANES + hh * HEAD_DIM
            pv = _dot(jnp.concatenate([vt_ref[0, 0, r0:r0 + HEAD_DIM, :], ones], axis=0), p)
            outs.append(pv[:HEAD_DIM] / pv[HEAD_DIM:HEAD_DIM + 1])
        return jnp.concatenate(outs, axis=0).T.astype(BF16)

    ys = [mixer_out(ps) for ps in parts]
    xs = [x_ref[ps, :] + _rms(y, g1_ref[...]) for ps, y in zip(parts, ys)]
    qs = [(_dot(_rms(x, g2_ref[...]).astype(BF16), wq_ref[...]) * Q_SCALE_LOG2).astype(BF16) for x in xs]
    logits = [scores(q) for q in qs]
    os_ = [attend(lg) for lg in logits]
    for ps, x, o in zip(parts, xs, os_):
        o_ref[ps, :] = x + _rms(_dot(o, wo_ref[...]), g3_ref[...])


def _post_mixer(acts, weights, x2, g1, g2, g3, wq, mem_k, mem_vt, wo, layer, tm, rows_per_batch, n_part=2):
    n = x2.shape[0]
    m = mem_k.shape[2]
    const = lambda shape: pl.BlockSpec(shape, lambda i: (0, 0))
    row = pl.BlockSpec((tm, D_MODEL), lambda i: (i, 0))
    gspec = const((1, D_MODEL))
    return pl.pallas_call(
        partial(_post_mixer_kernel, n_in=len(acts), n_part=n_part),
        grid=(n // tm,),
        in_specs=[pl.BlockSpec((tm, a.shape[1]), lambda i: (i, 0)) for a in acts]
                 + [const(w.shape) for w in weights]
                 + [row, gspec, gspec, gspec, const((D_MODEL, MEM_W)),
                    pl.BlockSpec((1, 1, m, MEM_W), lambda i: (layer, i // rows_per_batch, 0, 0)),
                    pl.BlockSpec((1, 1, MEM_W, m), lambda i: (layer, i // rows_per_batch, 0, 0)),
                    const((MEM_W, D_MODEL))],
        out_specs=row,
        out_shape=jax.ShapeDtypeStruct((n, D_MODEL), F32),
        compiler_params=_cparams(("parallel",)),
        name="post_mixer",
    )(*acts, *weights, x2, g1, g2, g3, wq, mem_k, mem_vt, wo)


def _ffn_kernel(x_ref, gin_ref, gout_ref, wg_ref, wu_ref, wd_ref, o_ref, *, chunk):
    x = x_ref[...]
    h = _rms(x, gin_ref[...]).astype(BF16)
    d_ff = wg_ref.shape[1]
    y = None
    for c0 in range(0, d_ff, chunk):
        gate = _dot(h, wg_ref[:, c0:c0 + chunk])
        up = _dot(h, wu_ref[:, c0:c0 + chunk])
        a = (gate * (1.0 / (1.0 + jnp.exp(-gate))) * up).astype(BF16)
        t = _dot(a, wd_ref[c0:c0 + chunk, :])
        y = t if y is None else y + t
    o_ref[...] = x + _rms(y, gout_ref[...])


def _ffn(x2, gin, gout, wg, wu, wd, tm, chunk):
    n = x2.shape[0]
    d_ff = wg.shape[1]
    const = lambda shape: pl.BlockSpec(shape, lambda i: (0, 0), pipeline_mode=pl.Buffered(1))
    return pl.pallas_call(
        partial(_ffn_kernel, chunk=chunk),
        grid=(n // tm,),
        in_specs=[pl.BlockSpec((tm, D_MODEL), lambda i: (i, 0)), const((1, D_MODEL)), const((1, D_MODEL)),
                  const((D_MODEL, d_ff)), const((D_MODEL, d_ff)), const((d_ff, D_MODEL))],
        out_specs=pl.BlockSpec((tm, D_MODEL), lambda i: (i, 0)),
        out_shape=jax.ShapeDtypeStruct((n, D_MODEL), F32),
        compiler_params=_cparams(("parallel",)),
        name="ffn",
    )(x2, gin, gout, wg, wu, wd)


def _even_weights(w_in, w_out):
    kvw = NSA_KV_GROUPS * HEAD_DIM
    offs = [int(o) for o in np.cumsum((FOX_W, FOX_W, FOX_W, FOX_HEADS, NSA_W) + (kvw,) * 6 + (3 * NSA_HEADS,))]
    fk0, fv0, fl0, nq0, kc0, vc0, ks0, vs0, kw0, vw0, gl0, end = offs
    perm = np.concatenate([np.arange(HEAD_DIM) + HEAD_DIM * (g * NSA_HPG + n)
                           for n in range(NSA_HPG) for g in range(NSA_KV_GROUPS)])
    cols = lambda a, b: w_in[:, a:b]
    w = jnp.concatenate([cols(0, fv0), w_in[:, nq0 + perm], cols(ks0, vs0), cols(kw0, vw0),
                         cols(kc0, vc0), cols(vc0, ks0),
                         cols(fv0, fl0), cols(vs0, kw0), cols(vw0, gl0)], axis=1).astype(BF16)
    w_small = jnp.concatenate([cols(fl0, nq0), cols(gl0, end),
                               jnp.zeros((D_MODEL, LANES - FOX_HEADS - 3 * NSA_HEADS), w_in.dtype)],
                              axis=1).astype(BF16)
    w_out_fox = w_out[:FOX_W].astype(BF16)
    w_out_nsa = w_out[FOX_W + perm].astype(BF16)
    return w, w_small, w_out_fox, w_out_nsa


def _overlap_matrix_t(t, ncp):
    nc = (t - CMP_BLOCK) // CMP_STRIDE + 1
    ns = t // SLC_BLOCK
    cs = np.arange(nc) * CMP_STRIDE
    ss = np.arange(ns) * SLC_BLOCK
    ov = np.clip(np.minimum(cs[:, None] + CMP_BLOCK, ss[None, :] + SLC_BLOCK)
                 - np.maximum(cs[:, None], ss[None, :]), 0, None) / CMP_BLOCK
    full = np.zeros((LANES // NSA_KV_GROUPS, ncp), np.float32)
    full[:ns, :nc] = ov.T
    return jnp.asarray(full, BF16), ns


def _compress_weights(pos_k, w1_k, w2_k, pos_v, w1_v, w2_v):
    g = NSA_KV_GROUPS
    pos2, w1bd, w2bd = [], [], []
    for pos, w1, w2 in ((pos_k, w1_k, w2_k), (pos_v, w1_v, w2_v)):
        pos2.append(jnp.tile(pos.astype(F32), (1, g)))
        w1l = w1.reshape(CMP_BLOCK, HEAD_DIM, CMP_HIDDEN).astype(BF16)
        bd = jnp.zeros((CMP_BLOCK, g * HEAD_DIM, g * CMP_HIDDEN), BF16)
        w2g = jnp.zeros((g * CMP_HIDDEN, g * HEAD_DIM), BF16)
        for gi in range(g):
            bd = bd.at[:, gi * HEAD_DIM:(gi + 1) * HEAD_DIM, gi * CMP_HIDDEN:(gi + 1) * CMP_HIDDEN].set(w1l)
            w2g = w2g.at[gi * CMP_HIDDEN:(gi + 1) * CMP_HIDDEN, gi * HEAD_DIM:(gi + 1) * HEAD_DIM].set(
                w2.astype(BF16))
        w1bd.append(bd)
        w2bd.append(w2g)
    return jnp.stack(pos2), jnp.stack(w1bd), jnp.stack(w2bd)


def kernel(x, mem, positions, sandwich_g, mem_norm_g, ev_w_in, ev_fox_fbias, ev_cmp_pos_k, ev_cmp_w1_k, ev_cmp_w2_k, ev_cmp_pos_v, ev_cmp_w1_v, ev_cmp_w2_v, ev_w_out, od_w_in, od_lambda, od_subln_g, od_w_out, ca_wq, ca_wk, ca_wv, ca_wo, ffn_wg, ffn_wu, ffn_wd):
    b, t, d = x.shape
    depth = sandwich_g.shape[0]
    n = b * t
    tm = 512
    tm_ffn = 512
    tm_mix = 1024
    tq, tk = 512, 256
    tk_diff = 512
    assert d == D_MODEL and t % tq == 0 and tq % tk == 0 and WINDOW % tk == 0 and t % tm == 0 and n % tm_ffn == 0

    tabs = tuple(a.reshape(b, t, LANES) for a in _rope_tables(positions, 512))
    mem_k, mem_vt = _mem_kv(mem, mem_norm_g, ca_wk.astype(BF16), ca_wv.astype(BF16))
    ncp = t // CMP_STRIDE
    overlap_t, ns = _overlap_matrix_t(t, ncp)
    assert ns <= overlap_t.shape[0]
    n_sel = min(SLC_TOPK, ns)
    hot = jnp.asarray((np.arange(t)[:, None] // SLC_BLOCK == np.arange(LANES)[None, :] % overlap_t.shape[0])
                      .astype(np.float32), BF16)
    gain = lambda l, j: sandwich_g[l, j].reshape(1, d)

    x2 = x.reshape(n, d)
    for layer in range(depth):
        x3 = x2.reshape(b, t, d)
        if layer % 2 == 0:
            e = layer // 2
            w, w_small, w_out_fox, w_out_nsa = _even_weights(ev_w_in[e], ev_w_out[e])
            fb_row = jnp.zeros((1, LANES), F32).at[0, :FOX_HEADS].set(ev_fox_fbias[e].astype(F32))
            main, cmp_in, vt, small = _even_proj(x3, gain(layer, 0), w, w_small, fb_row, tabs, tm)
            aq, ak = _fox_aug(small, 512)
            o_fox = _fox_attention(main, aq, ak, vt, tq, tk)
            kc, vct = _compress(cmp_in, *_compress_weights(
                ev_cmp_pos_k[e], ev_cmp_w1_k[e], ev_cmp_w2_k[e], ev_cmp_pos_v[e], ev_cmp_w1_v[e], ev_cmp_w2_v[e]))
            oc, sel = _nsa_select(main, kc, vct, overlap_t, small, 256, ns, n_sel)
            o_nsa = _nsa_flash(main, sel, oc, small, hot, vt, tq, tk)
            acts, w_outs = [o_fox.reshape(n, FOX_W), o_nsa.reshape(n, NSA_W)], [w_out_fox, w_out_nsa]
        else:
            o = layer // 2
            w_in = od_w_in[o].astype(BF16)
            main, vt = _odd_proj(x3, gain(layer, 0), w_in, tabs, tm)
            lam_init = 0.8 - 0.6 * math.exp(-0.3 * layer)
            lam_p = jnp.pad(od_lambda[o].astype(F32), ((0, 0), (0, LANES - HEAD_DIM)))
            attn = _diff_attention(main, vt, lam_p, od_subln_g[o].reshape(1, LANES).astype(F32), tq, tk_diff,
                                   lam_init)
            acts, w_outs = [attn.reshape(n, D_MODEL)], [od_w_out[o].astype(BF16)]
        x2 = _post_mixer(acts, w_outs, x2, gain(layer, 1), gain(layer, 2), gain(layer, 3),
                         ca_wq[layer].astype(BF16), mem_k, mem_vt, ca_wo[layer].astype(BF16), layer,
                         tm_mix, t // tm_mix, n_part=tm_mix // 256)
        x2 = _ffn(x2, gain(layer, 4), gain(layer, 5), ffn_wg[layer].astype(BF16), ffn_wu[layer].astype(BF16),
                  ffn_wd[layer].astype(BF16), tm_ffn, 256)
    return x2.reshape(b, t, d)
```

```python
import math
from functools import partial

import numpy as np
import jax
import jax.numpy as jnp
from jax import lax
from jax.experimental import pallas as pl
from jax.experimental.pallas import tpu as pltpu

F32 = jnp.float32
BF16 = jnp.bfloat16

D_MODEL = 1024
HEAD_DIM = 64
LANES = 128
ROPE_DIM = HEAD_DIM // 4
ROPE_THETA = 500000.0
FOX_HEADS = 8
NSA_HEADS = 8
NSA_KV_GROUPS = 2
NSA_HPG = NSA_HEADS // NSA_KV_GROUPS
CMP_BLOCK = 32
CMP_STRIDE = 16
CMP_HIDDEN = 2 * HEAD_DIM
SLC_BLOCK = 64
SLC_TOPK = 16
WINDOW = 512
DIFF_HEADS = 8
MEM_HEADS = 4
MEM_W = MEM_HEADS * HEAD_DIM
RMS_EPS = 1e-6
Q_SCALE = HEAD_DIM ** -0.5
LOG2E = math.log2(math.e)
Q_SCALE_LOG2 = Q_SCALE * LOG2E
NEG = -1e30
AUG_PER_HEAD = 6
ONES_ROWS = 16

FOX_W = FOX_HEADS * HEAD_DIM
NSA_W = NSA_HEADS * HEAD_DIM
CH_FQ, CH_FK, CH_NQ, CH_KS, CH_KW = 0, 4, 8, 12, 13
EV_MAIN = 14 * LANES
EV_CMP = 2 * LANES
VT_FV, VT_VS, VT_VW = 0, 4, 5
EV_VT = 6 * LANES
EV_ROPE_CHUNKS = tuple(range(CH_NQ, CH_NQ + 4)) + (CH_KS, CH_KW, EV_MAIN // LANES)
EV_QSCALE_CHUNKS = tuple(range(CH_FQ, CH_FQ + 4)) + tuple(range(CH_NQ, CH_NQ + 4))
OD_MAIN = 2 * D_MODEL
OD_VT = D_MODEL

VMEM_LIMIT = 56 * 1024 * 1024


def _layer_spec(stacked, layer, **kw):
    zeros = (0,) * (stacked.ndim - 1)
    return pl.BlockSpec((None,) + stacked.shape[1:], lambda *idx: (layer,) + zeros, **kw)


def _cparams(sem):
    return pltpu.CompilerParams(dimension_semantics=sem, vmem_limit_bytes=VMEM_LIMIT)


def _rms(x, g):
    return x * lax.rsqrt(jnp.mean(x * x, axis=-1, keepdims=True) + RMS_EPS) * g


def _split3(x):
    hi = x.astype(BF16)
    r1 = x - hi.astype(F32)
    mid = r1.astype(BF16)
    lo = (r1 - mid.astype(F32)).astype(BF16)
    return hi, mid, lo


def _dot(a, b):
    return jnp.dot(a, b, preferred_element_type=F32)


def _dot_nt(a, b):
    return lax.dot_general(a, b, (((1,), (1,)), ((), ())), preferred_element_type=F32)


def _lane_iota(n=LANES):
    return lax.broadcasted_iota(jnp.int32, (1, n), 1)


def _half_mask(q2, half):
    return jnp.where(_lane_iota() // HEAD_DIM == half, q2, jnp.zeros_like(q2))


def _transposed(x):
    return x.astype(F32).T.astype(BF16)


def _half_mask_t(qt, half):
    row = lax.broadcasted_iota(jnp.int32, (qt.shape[0], 1), 0)
    return jnp.where(row // HEAD_DIM == half, qt, jnp.zeros_like(qt))


def _positions_t(j, tk, q0, tq):
    kpos = j * tk + lax.broadcasted_iota(jnp.int32, (tk, 1), 0)
    qpos = q0 + lax.broadcasted_iota(jnp.int32, (1, tq), 1)
    return kpos, qpos


def _stream(base, n_full, n_tail, scores, mask, values, st_ref, acc_ref):
    n_chain, dv, tq = acc_ref.shape

    def park(j, slot):
        for c, st in enumerate(scores(j)):
            st_ref[slot, c] = st

    def step(j, slot, stats, masked, prefetch):
        if prefetch:
            park(j + 1, 1 - slot)
        parts = []
        for c in range(n_chain):
            st = st_ref[slot, c]
            if masked:
                st = mask(j, st)
            m, _ = stats[c]
            m_new = jnp.maximum(m, jnp.max(st, axis=0, keepdims=True))
            p = jnp.exp2(st - m_new).astype(BF16)
            vt1 = jnp.concatenate([values(c, j), jnp.ones((ONES_ROWS, p.shape[0]), BF16)], axis=0)
            parts.append((m_new, jnp.exp2(m - m_new), _dot(vt1, p)))
        out = []
        for c, (m_new, alpha, pv) in enumerate(parts):
            acc_ref[c] = alpha * acc_ref[c] + pv[:dv]
            out.append((m_new, alpha * stats[c][1] + pv[dv:dv + 1]))
        return tuple(out)

    acc_ref[...] = jnp.zeros_like(acc_ref)
    park(base, 0)
    stats = ((jnp.full((1, tq), NEG, F32), jnp.zeros((1, tq), F32)),) * n_chain
    n_tiles = n_full + n_tail
    for s_ in range(n_tiles):
        stats = step(base + s_, s_ % 2, stats, s_ >= n_full, s_ + 1 < n_tiles)
    return [(l, acc_ref[c]) for c, (_, l) in enumerate(stats)]


def _per_query_tile(kernel_fn, n_q, grid, in_specs_fn, out_spec_fn, out_shape, scratch, name, args, init):
    out = jnp.zeros(out_shape.shape, out_shape.dtype) if init is None else init
    for i in range(n_q):
        in_specs = list(in_specs_fn(i)) + [pl.BlockSpec(memory_space=pl.ANY)]
        out = pl.pallas_call(
            partial(kernel_fn, i=i), grid=grid, in_specs=in_specs, out_specs=out_spec_fn(i), out_shape=out_shape,
            scratch_shapes=scratch, input_output_aliases={len(args): 0},
            compiler_params=_cparams(("parallel",) * len(grid)), name=f"{name}_q{i}",
        )(*args, out)
    return out


def _stream_scratch(n_chain, dv, tq, tk):
    return [pltpu.VMEM((2, n_chain, tk, tq), F32), pltpu.VMEM((n_chain, dv, tq), F32)]


def _rope_kernel(pos_ref, inv_ref, m1_ref, m2_ref, c_ref, s1_ref, s2_ref):
    ang = pos_ref[...].astype(F32) * inv_ref[...]
    c_ref[...] = jnp.cos(ang)
    sn = jnp.sin(ang)
    s1_ref[...] = -sn * m1_ref[...]
    s2_ref[...] = sn * m2_ref[...]


def _rope_tables(positions, tm):
    n = positions.size
    inv = ROPE_THETA ** (-jnp.arange(0, ROPE_DIM, 2, dtype=F32) / ROPE_DIM)
    lane = np.arange(LANES) % HEAD_DIM
    half = ROPE_DIM // 2
    inv_l = jnp.where(lane < ROPE_DIM, inv[lane % half], 0.0).reshape(1, LANES).astype(F32)
    m1 = jnp.asarray((lane < half).astype(np.float32)).reshape(1, LANES)
    m2 = jnp.asarray(((lane >= half) & (lane < ROPE_DIM)).astype(np.float32)).reshape(1, LANES)
    row = pl.BlockSpec((1, LANES), lambda i: (0, 0))
    tab = pl.BlockSpec((tm, LANES), lambda i: (i, 0))
    return pl.pallas_call(
        _rope_kernel,
        grid=(n // tm,),
        in_specs=[pl.BlockSpec((tm, 1), lambda i: (i, 0)), row, row, row],
        out_specs=[tab, tab, tab],
        out_shape=[jax.ShapeDtypeStruct((n, LANES), F32)] * 3,
        compiler_params=_cparams(("parallel",)),
        name="rope_tables",
    )(positions.reshape(n, 1), inv_l, m1, m2)


def _apply_rope(y, c, s1, s2):
    half = ROPE_DIM // 2
    return y * c + pltpu.roll(y, LANES - half, 1) * s1 + pltpu.roll(y, half, 1) * s2


def _project_chunks(h, w_ref, tabs, dests, rope_chunks, qscale_chunks):
    c, s1, s2 = tabs
    wide = 2 * LANES
    where = [(kind, ref, k) for kind, ref, n in dests for k in range(n)]
    for ch2 in range(len(where) // 2):
        y2 = _dot(h, w_ref[:, ch2 * wide:(ch2 + 1) * wide])
        for ch in (2 * ch2, 2 * ch2 + 1):
            y = y2[:, (ch % 2) * LANES:(ch % 2 + 1) * LANES]
            if ch in rope_chunks:
                y = _apply_rope(y, c, s1, s2)
            if ch in qscale_chunks:
                y = y * Q_SCALE_LOG2
            kind, ref, k = where[ch]
            if kind == "cols":
                ref[0, k * LANES:(k + 1) * LANES, :] = y.T.astype(BF16)
            elif kind == "rows_f32":
                ref[0, k] = y
            else:
                ref[0, :, k * LANES:(k + 1) * LANES] = y.astype(BF16)


def _even_proj_kernel(x_ref, g_ref, w_ref, ws_ref, fb_ref, c_ref, s1_ref, s2_ref,
                      main_ref, cmp_ref, vt_ref, small_ref):
    h = _rms(x_ref[0], g_ref[...]).astype(BF16)
    dests = [("rows", main_ref, EV_MAIN // LANES), ("rows_f32", cmp_ref, EV_CMP // LANES),
             ("cols", vt_ref, EV_VT // LANES)]
    _project_chunks(h, w_ref, (c_ref[0], s1_ref[0], s2_ref[0]), dests, EV_ROPE_CHUNKS, EV_QSCALE_CHUNKS)
    ys = _dot(h, ws_ref[...])
    z = ys + fb_ref[...]
    log_f = jnp.minimum(z, 0.0) - jnp.log(1.0 + jnp.exp(-jnp.abs(z)))
    gate = 1.0 / (1.0 + jnp.exp(-ys))
    small_ref[0] = jnp.where(_lane_iota() < FOX_HEADS, log_f, gate)


def _even_proj(x3, g, w, w_small, fb_row, tabs, tm):
    b, t, _ = x3.shape
    const = lambda shape: pl.BlockSpec(shape, lambda bi, i: (0, 0))
    tab = pl.BlockSpec((1, tm, LANES), lambda bi, i: (bi, i, 0))
    return pl.pallas_call(
        _even_proj_kernel,
        grid=(b, t // tm),
        in_specs=[pl.BlockSpec((1, tm, D_MODEL), lambda bi, i: (bi, i, 0)), const((1, D_MODEL)),
                  const((D_MODEL, EV_MAIN + EV_CMP + EV_VT)), const((D_MODEL, LANES)), const((1, LANES)),
                  tab, tab, tab],
        out_specs=[pl.BlockSpec((1, tm, EV_MAIN), lambda bi, i: (bi, i, 0)),
                   pl.BlockSpec((1, EV_CMP // LANES, tm, LANES), lambda bi, i: (bi, 0, i, 0)),
                   pl.BlockSpec((1, EV_VT, tm), lambda bi, i: (bi, 0, i)), tab],
        out_shape=[jax.ShapeDtypeStruct((b, t, EV_MAIN), BF16),
                   jax.ShapeDtypeStruct((b, EV_CMP // LANES, t, LANES), F32),
                   jax.ShapeDtypeStruct((b, EV_VT, t), BF16), jax.ShapeDtypeStruct((b, t, LANES), F32)],
        compiler_params=_cparams(("parallel", "parallel")),
        name="even_proj",
    )(x3, g, w, w_small, fb_row, *tabs)


def _odd_proj_kernel(x_ref, g_ref, w_ref, c_ref, s1_ref, s2_ref, main_ref, vt_ref):
    h = _rms(x_ref[0], g_ref[...]).astype(BF16)
    n_main = OD_MAIN // LANES
    dests = [("rows", main_ref, n_main), ("cols", vt_ref, OD_VT // LANES)]
    _project_chunks(h, w_ref, (c_ref[0], s1_ref[0], s2_ref[0]), dests, tuple(range(n_main)),
                    tuple(range(n_main // 2)))


def _odd_proj(x3, g, w_all, layer, tabs, tm):
    b, t, _ = x3.shape
    const = lambda shape: pl.BlockSpec(shape, lambda bi, i: (0, 0))
    tab = pl.BlockSpec((1, tm, LANES), lambda bi, i: (bi, i, 0))
    return pl.pallas_call(
        _odd_proj_kernel,
        grid=(b, t // tm),
        in_specs=[pl.BlockSpec((1, tm, D_MODEL), lambda bi, i: (bi, i, 0)), const((1, D_MODEL)),
                  _layer_spec(w_all, layer), tab, tab, tab],
        out_specs=[pl.BlockSpec((1, tm, OD_MAIN), lambda bi, i: (bi, i, 0)),
                   pl.BlockSpec((1, OD_VT, tm), lambda bi, i: (bi, 0, i))],
        out_shape=[jax.ShapeDtypeStruct((b, t, OD_MAIN), BF16), jax.ShapeDtypeStruct((b, OD_VT, t), BF16)],
        compiler_params=_cparams(("parallel", "parallel")),
        name="odd_proj",
    )(x3, g, w_all, *tabs)


def _fox_aug_kernel(lf_ref, tril_ref, eq_ref, ek_ref, oneq_ref, onek_ref, aq_ref, ak_ref, carry_ref):
    @pl.when(pl.program_id(1) == 0)
    def _():
        carry_ref[...] = jnp.zeros_like(carry_ref)

    tril = tril_ref[...]
    c = carry_ref[...]
    for piece in _split3(lf_ref[0]):
        c = c + _dot(tril, piece)
    carry_ref[...] = c[-1:, :]
    aq = oneq_ref[...]
    ak = onek_ref[...]
    for r, piece in enumerate(_split3(c * LOG2E)):
        aq = aq + _dot(piece, eq_ref[r])
        ak = ak - _dot(piece, ek_ref[r])
    aq_ref[0] = aq.astype(BF16)
    ak_ref[0] = ak.astype(BF16)


def _fox_aug(small, tc):
    b, t, _ = small.shape
    tril = jnp.asarray(np.tril(np.ones((tc, tc), np.float32)), BF16)
    eq = np.zeros((3, LANES, LANES), np.float32)
    ek = np.zeros((3, LANES, LANES), np.float32)
    oneq = np.zeros((1, LANES), np.float32)
    onek = np.zeros((1, LANES), np.float32)
    for h in range(FOX_HEADS):
        for r in range(3):
            eq[r, h, AUG_PER_HEAD * h + r] = 1.0
            ek[r, h, AUG_PER_HEAD * h + 3 + r] = 1.0
            oneq[0, AUG_PER_HEAD * h + 3 + r] = 1.0
            onek[0, AUG_PER_HEAD * h + r] = 1.0
    const2 = lambda shape: pl.BlockSpec(shape, lambda bi, i: (0,) * len(shape))
    blk = pl.BlockSpec((1, tc, LANES), lambda bi, i: (bi, i, 0))
    return pl.pallas_call(
        _fox_aug_kernel,
        grid=(b, t // tc),
        in_specs=[blk, const2((tc, tc)), const2((3, LANES, LANES)), const2((3, LANES, LANES)),
                  const2((1, LANES)), const2((1, LANES))],
        out_specs=[blk, blk],
        out_shape=[jax.ShapeDtypeStruct((b, t, LANES), BF16)] * 2,
        scratch_shapes=[pltpu.VMEM((1, LANES), F32)],
        compiler_params=_cparams(("parallel", "arbitrary")),
        name="fox_aug",
    )(small, tril, jnp.asarray(eq, BF16), jnp.asarray(ek, BF16), jnp.asarray(oneq), jnp.asarray(onek))


def _fox_kernel(q_ref, aq_ref, k_ref, ak_ref, vt_ref, *rest, i, tq, tk):
    o_ref, st_ref, acc_ref = rest[-3:]
    pair = pl.program_id(1)
    row = lax.broadcasted_iota(jnp.int32, (LANES, 1), 0)
    qt = _transposed(q_ref[0])
    qat = _transposed(aq_ref[0])
    qcats = []
    for hh in range(2):
        head = 2 * pair + hh
        in_head = (row >= AUG_PER_HEAD * head) & (row < AUG_PER_HEAD * (head + 1))
        qcats.append(jnp.concatenate([_half_mask_t(qt, hh), jnp.where(in_head, qat, jnp.zeros_like(qat))], axis=0))

    def scores(j):
        ks = j * tk
        kcat = jnp.concatenate([k_ref[0, pl.ds(ks, tk), :], ak_ref[0, pl.ds(ks, tk), :]], axis=1)
        return tuple(_dot(kcat, qcats[hh]) for hh in range(2))

    def mask(j, st):
        kpos, qpos = _positions_t(j, tk, i * tq, tq)
        return jnp.where(kpos <= qpos, st, NEG)

    def values(hh, j):
        return vt_ref[0, hh * HEAD_DIM:(hh + 1) * HEAD_DIM, pl.ds(j * tk, tk)]

    res = _stream(0, i * (tq // tk), tq // tk, scores, mask, values, st_ref, acc_ref)
    ot = jnp.concatenate([acc / l for l, acc in res], axis=0)
    o_ref[0] = ot.T.astype(BF16)


def _fox_attention(main, aq, ak, vt, tq, tk, init):
    b, t, _ = main.shape
    in_specs = lambda i: [pl.BlockSpec((1, tq, LANES), lambda bi, p: (bi, i, CH_FQ + p)),
                          pl.BlockSpec((1, tq, LANES), lambda bi, p: (bi, i, 0)),
                          pl.BlockSpec((1, t, LANES), lambda bi, p: (bi, 0, CH_FK + p)),
                          pl.BlockSpec((1, t, LANES), lambda bi, p: (bi, 0, 0)),
                          pl.BlockSpec((1, LANES, t), lambda bi, p: (bi, VT_FV + p, 0))]
    return _per_query_tile(
        partial(_fox_kernel, tq=tq, tk=tk), t // tq, (b, FOX_HEADS // 2), in_specs,
        lambda i: pl.BlockSpec((1, tq, LANES), lambda bi, p: (bi, i, p)),
        jax.ShapeDtypeStruct((b, t, FOX_W), BF16), _stream_scratch(2, HEAD_DIM, tq, tk), "fox_attention",
        (main, aq, main, ak, vt), init)


def _compress_kernel(x_ref, pos_ref, w1_ref, w2_ref, kc_ref, vct_ref):
    t = x_ref.shape[2]
    nchunk = t // CMP_STRIDE
    for kv, o_ref in enumerate((kc_ref, vct_ref)):
        first = second = None
        for l in range(CMP_STRIDE):
            xl = x_ref[0, kv, pl.ds(l, nchunk, stride=CMP_STRIDE), :]
            a = _dot((xl + pos_ref[kv, l:l + 1, :]).astype(BF16), w1_ref[kv, l])
            b = _dot((xl + pos_ref[kv, CMP_STRIDE + l:CMP_STRIDE + l + 1, :]).astype(BF16),
                     w1_ref[kv, CMP_STRIDE + l])
            first = a if first is None else first + a
            second = b if second is None else second + b
        hid = first + pltpu.roll(second, nchunk - 1, 0)
        out = _dot(jax.nn.gelu(hid, approximate=True).astype(BF16), w2_ref[kv])
        o_ref[0] = (out.T if o_ref is vct_ref else out).astype(BF16)


def _compress(cmp_in, pos2, w1bd, w2bd):
    b, _, t, _ = cmp_in.shape
    nchunk = t // CMP_STRIDE
    const = lambda a: pl.BlockSpec(a.shape, lambda bi: (0,) * a.ndim)
    return pl.pallas_call(
        _compress_kernel,
        grid=(b,),
        in_specs=[pl.BlockSpec((1, EV_CMP // LANES, t, LANES), lambda bi: (bi, 0, 0, 0)),
                  const(pos2), const(w1bd), const(w2bd)],
        out_specs=[pl.BlockSpec((1, nchunk, LANES), lambda bi: (bi, 0, 0)),
                   pl.BlockSpec((1, LANES, nchunk), lambda bi: (bi, 0, 0))],
        out_shape=[jax.ShapeDtypeStruct((b, nchunk, LANES), BF16), jax.ShapeDtypeStruct((b, LANES, nchunk), BF16)],
        compiler_params=_cparams(("parallel",)),
        name="nsa_compress",
    )(cmp_in, pos2, w1bd, w2bd)


def _gate_col(small, head, branch):
    idx = FOX_HEADS + 3 * head + branch
    return jnp.sum(jnp.where(_lane_iota() == idx, small, 0.0), axis=-1, keepdims=True)


def _nsa_select_kernel(q_ref, kc_ref, vct_ref, ovt_ref, small_ref, *rest, q_lo, tq, ns, n_sel):
    oc_ref, sel_ref = rest[-2:]
    q0 = q_lo + pl.program_id(1) * tq
    lane = _lane_iota()
    kc = kc_ref[0]
    vct = vct_ref[0]
    ncp = kc.shape[0]
    small = small_ref[0]
    qpos = q0 + lax.broadcasted_iota(jnp.int32, (1, tq), 1)
    cmp_end = lax.broadcasted_iota(jnp.int32, (ncp, 1), 0) * CMP_STRIDE + (CMP_BLOCK - 1)
    cmask = cmp_end <= qpos
    psum = [jnp.zeros((ncp, tq), F32) for _ in range(NSA_KV_GROUPS)]
    logits = [[_dot_nt(kc, _half_mask(q_ref[0, :, n * LANES:(n + 1) * LANES], g)) for g in range(NSA_KV_GROUPS)]
              for n in range(NSA_HPG)]
    for n in range(NSA_HPG):
        ots = []
        for g in range(NSA_KV_GROUPS):
            z = jnp.where(cmask, logits[n][g], -jnp.inf)
            m = jnp.max(z, axis=0, keepdims=True)
            m = jnp.where(m == -jnp.inf, 0.0, m)
            p = jnp.exp2(z - m)
            p = p / jnp.maximum(jnp.sum(p, axis=0, keepdims=True), 1e-30)
            psum[g] = psum[g] + p
            ots.append(_dot(vct[g * HEAD_DIM:(g + 1) * HEAD_DIM], p.astype(BF16)))
        gate = jnp.where(lane < HEAD_DIM, _gate_col(small, n, 0), _gate_col(small, NSA_HPG + n, 0))
        oc_ref[0, :, n * LANES:(n + 1) * LANES] = gate * jnp.concatenate(ots, axis=0).T

    nsp = ovt_ref.shape[0]
    blk = lax.broadcasted_iota(jnp.int32, (nsp, 1), 0)
    cur = qpos // SLC_BLOCK
    valid = blk * SLC_BLOCK <= qpos
    forced = (blk == 0) | (blk == cur) | (blk == cur - 1)
    scores = []
    for g in range(NSA_KV_GROUPS):
        imp = jnp.zeros((nsp, tq), F32)
        for piece in _split3(psum[g]):
            imp = imp + _dot(ovt_ref[...], piece)
        scores.append(jnp.where(valid, jnp.where(forced, jnp.inf, imp), -jnp.inf))
    slab = 8
    masks = []
    for g in range(NSA_KV_GROUPS):
        slabs = [scores[g][r:r + slab] for r in range(0, nsp, slab)]
        ranks = [jnp.zeros((slab, tq), jnp.int32) for _ in slabs]
        for i in range(ns):
            row = scores[g][i:i + 1, :]
            for r, sl in enumerate(slabs):
                if slab * r >= ns:
                    continue
                if slab * r > i:
                    ahead = row >= sl
                elif slab * (r + 1) - 1 <= i:
                    ahead = row > sl
                else:
                    ahead = (row > sl) | ((row == sl) & (blk[slab * r:slab * (r + 1)] > i))
                ranks[r] = jnp.where(ahead, ranks[r] + 1, ranks[r])
        rank = jnp.concatenate(ranks, axis=0)
        masks.append(jnp.where((rank < n_sel) & (blk < ns), 0.0, NEG))
    sel_ref[0] = jnp.concatenate(masks, axis=0).T.astype(BF16)


def _nsa_select(main, kc, vct, overlap_t, small, tq, ns, n_sel, init, n_span=4):
    b, t, _ = main.shape
    nsp = overlap_t.shape[0]
    span = t // n_span
    steps = span // tq
    shapes = [jax.ShapeDtypeStruct((b, t, NSA_W), F32), jax.ShapeDtypeStruct((b, t, NSA_KV_GROUPS * nsp), BF16)]
    outs = [jnp.zeros(s.shape, s.dtype) for s in shapes] if init is None else init
    for k in range(n_span):
        q_hi = (k + 1) * span
        ncl = min(-(-(q_hi // CMP_STRIDE) // LANES) * LANES, kc.shape[1])
        row = lambda bi, i, k=k: (bi, k * steps + i, 0)
        outs = pl.pallas_call(
            partial(_nsa_select_kernel, q_lo=k * span, tq=tq, ns=min(ns, q_hi // SLC_BLOCK), n_sel=n_sel),
            grid=(b, steps),
            in_specs=[pl.BlockSpec((1, tq, NSA_W), lambda bi, i, k=k: (bi, k * steps + i, CH_NQ * LANES // NSA_W)),
                      pl.BlockSpec((1, ncl, LANES), lambda bi, i: (bi, 0, 0)),
                      pl.BlockSpec((1, LANES, ncl), lambda bi, i: (bi, 0, 0)),
                      pl.BlockSpec((nsp, ncl), lambda bi, i: (0, 0)),
                      pl.BlockSpec((1, tq, LANES), row),
                      pl.BlockSpec(memory_space=pl.ANY), pl.BlockSpec(memory_space=pl.ANY)],
            out_specs=[pl.BlockSpec((1, tq, NSA_W), row), pl.BlockSpec((1, tq, NSA_KV_GROUPS * nsp), row)],
            out_shape=shapes,
            input_output_aliases={5: 0, 6: 1},
            compiler_params=_cparams(("parallel", "parallel")),
            name=f"nsa_select_s{k}",
        )(main, kc, vct, overlap_t, small, *outs)
    return outs


def _nsa_flash_kernel(q_ref, sel_ref, oc_ref, small_ref, ks_ref, kw_ref, hot_ref, vst_ref, vwt_ref, *rest,
                      i, tq, tk):
    o_ref, st_ref, acc_ref = rest[-3:]
    n = pl.program_id(1)
    q0 = i * tq
    q2 = q_ref[0]
    small = small_ref[0]
    qt = _transposed(q2)
    selt = _transposed(sel_ref[0])
    qhs = [_half_mask_t(qt, g) for g in range(NSA_KV_GROUPS)]
    qcats = [jnp.concatenate([qhs[g], _half_mask_t(selt, g)], axis=0) for g in range(NSA_KV_GROUPS)]
    rows = lambda g: slice(g * HEAD_DIM, (g + 1) * HEAD_DIM)

    def sel_scores(j):
        ks = j * tk
        kcat = jnp.concatenate([ks_ref[0, pl.ds(ks, tk), :], hot_ref[pl.ds(ks, tk), :]], axis=1)
        return tuple(_dot(kcat, qcats[g]) for g in range(NSA_KV_GROUPS))

    def sel_mask(j, st):
        kpos, qpos = _positions_t(j, tk, q0, tq)
        return jnp.where(kpos <= qpos, st, NEG)

    def win_scores(j):
        kw = kw_ref[0, pl.ds(j * tk, tk), :]
        return tuple(_dot(kw, qhs[g]) for g in range(NSA_KV_GROUPS))

    def win_mask(j, st):
        kpos, qpos = _positions_t(j, tk, q0, tq)
        return jnp.where((kpos <= qpos) & (kpos > qpos - WINDOW), st, NEG)

    values = lambda ref: lambda g, j: ref[0, rows(g), pl.ds(j * tk, tk)]
    lane = _lane_iota()
    n_end = (i + 1) * (tq // tk)
    win_lo = max(n_end - (WINDOW + tq) // tk, 0)
    o = oc_ref[0]
    for branch, args in ((1, (0, i * (tq // tk), tq // tk, sel_scores, sel_mask, values(vst_ref))),
                         (2, (win_lo, 0, n_end - win_lo, win_scores, win_mask, values(vwt_ref)))):
        res = _stream(*args, st_ref, acc_ref)
        ot = jnp.concatenate([acc / l for l, acc in res], axis=0)
        gate = jnp.where(lane < HEAD_DIM, _gate_col(small, n, branch), _gate_col(small, NSA_HPG + n, branch))
        o = o + gate * ot.T
    o_ref[0] = o.astype(BF16)


def _nsa_flash(main, sel, oc, small, hot, vt, tq, tk, init):
    b, t, _ = main.shape
    tile = lambda i, ch: pl.BlockSpec((1, tq, LANES), lambda bi, n: (bi, i, ch + n))
    shared = lambda i: pl.BlockSpec((1, tq, LANES), lambda bi, n: (bi, i, 0))
    full = lambda ch: pl.BlockSpec((1, t, LANES), lambda bi, n: (bi, 0, ch))
    vfull = lambda ch: pl.BlockSpec((1, LANES, t), lambda bi, n: (bi, ch, 0))
    in_specs = lambda i: [tile(i, CH_NQ), shared(i), tile(i, 0), shared(i), full(CH_KS), full(CH_KW),
                          pl.BlockSpec((t, LANES), lambda bi, n: (0, 0)), vfull(VT_VS), vfull(VT_VW)]
    return _per_query_tile(
        partial(_nsa_flash_kernel, tq=tq, tk=tk), t // tq, (b, NSA_HPG), in_specs, lambda i: tile(i, 0),
        jax.ShapeDtypeStruct((b, t, NSA_W), BF16), _stream_scratch(NSA_KV_GROUPS, HEAD_DIM, tq, tk), "nsa_flash",
        (main, sel, oc, small, main, main, hot, vt, vt), init)


def _diff_kernel(q_ref, k_ref, vt_ref, lam_ref, g_ref, *rest, i, tq, tk, lam_init):
    o_ref, st_ref, acc_ref = rest[-3:]
    q2 = q_ref[0]
    lp = lam_ref[...]
    lam = (jnp.exp(jnp.sum(lp[0:1] * lp[1:2], axis=-1, keepdims=True))
           - jnp.exp(jnp.sum(lp[2:3] * lp[3:4], axis=-1, keepdims=True)) + lam_init)
    qhs = [_half_mask(q2, comp) for comp in range(2)]

    def scores(j):
        k2 = k_ref[0, pl.ds(j * tk, tk), :]
        return tuple(_dot_nt(k2, qhs[comp]) for comp in range(2))

    def mask(j, st):
        kpos, qpos = _positions_t(j, tk, i * tq, tq)
        return jnp.where(kpos <= qpos, st, NEG)

    def values(comp, j):
        return vt_ref[0, :, pl.ds(j * tk, tk)]

    (l1, acc1), (l2, acc2) = _stream(0, i * (tq // tk), tq // tk, scores, mask, values, st_ref, acc_ref)
    o = (acc1 / l1 - lam * (acc2 / l2)).T
    o_ref[0] = (_rms(o, g_ref[...]) * (1.0 - lam_init)).astype(BF16)


def _diff_attention(main, vt, lam_p, subln_g, tq, tk, lam_init, init):
    b, t, _ = main.shape
    nh = DIFF_HEADS
    in_specs = lambda i: [pl.BlockSpec((1, tq, LANES), lambda bi, h: (bi, i, h)),
                          pl.BlockSpec((1, t, LANES), lambda bi, h: (bi, 0, nh + h)),
                          pl.BlockSpec((1, LANES, t), lambda bi, h: (bi, h, 0)),
                          pl.BlockSpec((4, LANES), lambda bi, h: (0, 0)),
                          pl.BlockSpec((1, LANES), lambda bi, h: (0, 0))]
    return _per_query_tile(
        partial(_diff_kernel, tq=tq, tk=tk, lam_init=lam_init), t // tq, (b, nh), in_specs,
        lambda i: pl.BlockSpec((1, tq, LANES), lambda bi, h: (bi, i, h)),
        jax.ShapeDtypeStruct((b, t, nh * LANES), BF16), _stream_scratch(2, LANES, tq, tk), "diff_attention",
        (main, main, vt, lam_p, subln_g), init)


def _mem_kv_kernel(mem_ref, g_ref, wk_ref, wv_ref, k_ref, vt_ref):
    mn = _rms(mem_ref[0], g_ref[0]).astype(BF16)
    k_ref[0, 0] = _dot(mn, wk_ref[0]).astype(BF16)
    vt_ref[0, 0] = _dot(mn, wv_ref[0]).T.astype(BF16)


def _mem_kv(mem, mem_norm_g, wk, wv):
    depth = wk.shape[0]
    b, m, d = mem.shape
    wspec = pl.BlockSpec((1, d, MEM_W), lambda l, bi: (l, 0, 0))
    return pl.pallas_call(
        _mem_kv_kernel,
        grid=(depth, b),
        in_specs=[pl.BlockSpec((1, m, d), lambda l, bi: (bi, 0, 0)),
                  pl.BlockSpec((1, 1, d), lambda l, bi: (l, 0, 0)), wspec, wspec],
        out_specs=[pl.BlockSpec((1, 1, m, MEM_W), lambda l, bi: (l, bi, 0, 0)),
                   pl.BlockSpec((1, 1, MEM_W, m), lambda l, bi: (l, bi, 0, 0))],
        out_shape=[jax.ShapeDtypeStruct((depth, b, m, MEM_W), BF16),
                   jax.ShapeDtypeStruct((depth, b, MEM_W, m), BF16)],
        compiler_params=_cparams(("parallel", "parallel")),
        name="mem_kv",
    )(mem, mem_norm_g.reshape(depth, 1, d), wk, wv)


def _post_mixer_kernel(*refs, n_in, n_part):
    a_refs, w_refs = refs[:n_in], refs[n_in:2 * n_in]
    x_ref, g1_ref, g2_ref, g3_ref, wq_ref, k_ref, vt_ref, wo_ref, o_ref = refs[2 * n_in:]
    rows = x_ref.shape[0] // n_part
    parts = [slice(r * rows, (r + 1) * rows) for r in range(n_part)]
    n_mem = k_ref.shape[2]
    heads = [(ch, hh) for ch in range(MEM_W // LANES) for hh in range(2)]
    ones = jnp.ones((ONES_ROWS, n_mem), BF16)

    def mixer_out(ps):
        y = None
        for a_ref, w_ref in zip(a_refs, w_refs):
            t = _dot(a_ref[ps, :], w_ref[...])
            y = t if y is None else y + t
        return y

    def scores(q):
        return [_dot_nt(k_ref[0, 0, :, ch * LANES:(ch + 1) * LANES], _half_mask(q[:, ch * LANES:(ch + 1) * LANES], hh))
                for ch, hh in heads]

    def attend(logits):
        outs = []
        for (ch, hh), st in zip(heads, logits):
            p = jnp.exp2(st - jnp.max(st, axis=0, keepdims=True)).astype(BF16)
            r0 = ch * LANES + hh * HEAD_DIM
            pv = _dot(jnp.concatenate([vt_ref[0, 0, r0:r0 + HEAD_DIM, :], ones], axis=0), p)
            outs.append(pv[:HEAD_DIM] / pv[HEAD_DIM:HEAD_DIM + 1])
        return jnp.concatenate(outs, axis=0).T.astype(BF16)

    ys = [mixer_out(ps) for ps in parts]
    xs = [x_ref[ps, :] + _rms(y, g1_ref[...]) for ps, y in zip(parts, ys)]
    qs = [(_dot(_rms(x, g2_ref[...]).astype(BF16), wq_ref[...]) * Q_SCALE_LOG2).astype(BF16) for x in xs]
    logits = [scores(q) for q in qs]
    os_ = [attend(lg) for lg in logits]
    for ps, x, o in zip(parts, xs, os_):
        o_ref[ps, :] = x + _rms(_dot(o, wo_ref[...]), g3_ref[...])


def _post_mixer(acts, weights, x2, g1, g2, g3, wq_all, mem_k, mem_vt, wo_all, layer, tm, rows_per_batch, n_part=2):
    n = x2.shape[0]
    m = mem_k.shape[2]
    const = lambda shape: pl.BlockSpec(shape, lambda i: (0, 0))
    row = pl.BlockSpec((tm, D_MODEL), lambda i: (i, 0))
    gspec = const((1, D_MODEL))
    return pl.pallas_call(
        partial(_post_mixer_kernel, n_in=len(acts), n_part=n_part),
        grid=(n // tm,),
        in_specs=[pl.BlockSpec((tm, a.shape[1]), lambda i: (i, 0)) for a in acts]
                 + [const(w.shape) if l is None else _layer_spec(w, l) for w, l in weights]
                 + [row, gspec, gspec, gspec, _layer_spec(wq_all, layer),
                    pl.BlockSpec((1, 1, m, MEM_W), lambda i: (layer, i // rows_per_batch, 0, 0)),
                    pl.BlockSpec((1, 1, MEM_W, m), lambda i: (layer, i // rows_per_batch, 0, 0)),
                    _layer_spec(wo_all, layer)],
        out_specs=row,
        out_shape=jax.ShapeDtypeStruct((n, D_MODEL), F32),
        compiler_params=_cparams(("parallel",)),
        name="post_mixer",
    )(*acts, *[w for w, _ in weights], x2, g1, g2, g3, wq_all, mem_k, mem_vt, wo_all)


def _ffn_kernel(x_ref, gin_ref, gout_ref, wg_ref, wu_ref, wd_ref, o_ref, *, chunk):
    x = x_ref[...]
    h = _rms(x, gin_ref[...]).astype(BF16)
    d_ff = wg_ref.shape[1]
    y = None
    for c0 in range(0, d_ff, chunk):
        gate = _dot(h, wg_ref[:, c0:c0 + chunk])
        up = _dot(h, wu_ref[:, c0:c0 + chunk])
        a = (gate * (1.0 / (1.0 + jnp.exp(-gate))) * up).astype(BF16)
        t = _dot(a, wd_ref[c0:c0 + chunk, :])
        y = t if y is None else y + t
    o_ref[...] = x + _rms(y, gout_ref[...])


def _ffn(x2, gin, gout, wg_all, wu_all, wd_all, layer, tm, chunk):
    n = x2.shape[0]
    const = lambda shape: pl.BlockSpec(shape, lambda i: (0, 0), pipeline_mode=pl.Buffered(1))
    wspec = lambda w: _layer_spec(w, layer, pipeline_mode=pl.Buffered(1))
    return pl.pallas_call(
        partial(_ffn_kernel, chunk=chunk),
        grid=(n // tm,),
        in_specs=[pl.BlockSpec((tm, D_MODEL), lambda i: (i, 0)), const((1, D_MODEL)), const((1, D_MODEL)),
                  wspec(wg_all), wspec(wu_all), wspec(wd_all)],
        out_specs=pl.BlockSpec((tm, D_MODEL), lambda i: (i, 0)),
        out_shape=jax.ShapeDtypeStruct((n, D_MODEL), F32),
        compiler_params=_cparams(("parallel",)),
        name="ffn",
    )(x2, gin, gout, wg_all, wu_all, wd_all)


def _even_weights(w_in, w_out):
    kvw = NSA_KV_GROUPS * HEAD_DIM
    offs = [int(o) for o in np.cumsum((FOX_W, FOX_W, FOX_W, FOX_HEADS, NSA_W) + (kvw,) * 6 + (3 * NSA_HEADS,))]
    fk0, fv0, fl0, nq0, kc0, vc0, ks0, vs0, kw0, vw0, gl0, end = offs
    paired = [HEAD_DIM * (g * NSA_HPG + n) for n in range(NSA_HPG) for g in range(NSA_KV_GROUPS)]
    cols = lambda a, b: w_in[:, a:b]
    w = jnp.concatenate([cols(0, fv0)] + [cols(nq0 + h, nq0 + h + HEAD_DIM) for h in paired]
                        + [cols(ks0, vs0), cols(kw0, vw0), cols(kc0, vc0), cols(vc0, ks0),
                           cols(fv0, fl0), cols(vs0, kw0), cols(vw0, gl0)], axis=1).astype(BF16)
    w_small = jnp.concatenate([cols(fl0, nq0), cols(gl0, end),
                               jnp.zeros((D_MODEL, LANES - FOX_HEADS - 3 * NSA_HEADS), w_in.dtype)],
                              axis=1).astype(BF16)
    w_out_fox = w_out[:FOX_W].astype(BF16)
    w_out_nsa = jnp.concatenate([w_out[FOX_W + h:FOX_W + h + HEAD_DIM] for h in paired], axis=0).astype(BF16)
    return w, w_small, w_out_fox, w_out_nsa


def _overlap_matrix_t(t, ncp):
    nc = (t - CMP_BLOCK) // CMP_STRIDE + 1
    ns = t // SLC_BLOCK
    cs = np.arange(nc) * CMP_STRIDE
    ss = np.arange(ns) * SLC_BLOCK
    ov = np.clip(np.minimum(cs[:, None] + CMP_BLOCK, ss[None, :] + SLC_BLOCK)
                 - np.maximum(cs[:, None], ss[None, :]), 0, None) / CMP_BLOCK
    full = np.zeros((LANES // NSA_KV_GROUPS, ncp), np.float32)
    full[:ns, :nc] = ov.T
    return jnp.asarray(full, BF16), ns


def _compress_weights(pos_k, w1_k, w2_k, pos_v, w1_v, w2_v):
    g = NSA_KV_GROUPS
    pos2, w1bd, w2bd = [], [], []
    for pos, w1, w2 in ((pos_k, w1_k, w2_k), (pos_v, w1_v, w2_v)):
        pos2.append(jnp.tile(pos.astype(F32), (1, g)))
        w1l = w1.reshape(CMP_BLOCK, HEAD_DIM, CMP_HIDDEN).astype(BF16)
        bd = jnp.zeros((CMP_BLOCK, g * HEAD_DIM, g * CMP_HIDDEN), BF16)
        w2g = jnp.zeros((g * CMP_HIDDEN, g * HEAD_DIM), BF16)
        for gi in range(g):
            bd = bd.at[:, gi * HEAD_DIM:(gi + 1) * HEAD_DIM, gi * CMP_HIDDEN:(gi + 1) * CMP_HIDDEN].set(w1l)
            w2g = w2g.at[gi * CMP_HIDDEN:(gi + 1) * CMP_HIDDEN, gi * HEAD_DIM:(gi + 1) * HEAD_DIM].set(
                w2.astype(BF16))
        w1bd.append(bd)
        w2bd.append(w2g)
    return jnp.stack(pos2), jnp.stack(w1bd), jnp.stack(w2bd)


def kernel(x, mem, positions, sandwich_g, mem_norm_g, ev_w_in, ev_fox_fbias, ev_cmp_pos_k, ev_cmp_w1_k, ev_cmp_w2_k, ev_cmp_pos_v, ev_cmp_w1_v, ev_cmp_w2_v, ev_w_out, od_w_in, od_lambda, od_subln_g, od_w_out, ca_wq, ca_wk, ca_wv, ca_wo, ffn_wg, ffn_wu, ffn_wd):
    b, t, d = x.shape
    depth = sandwich_g.shape[0]
    n = b * t
    tm = 512
    tm_ffn = 512
    tm_mix = 1024
    tq, tk = 512, 256
    tk_diff = 512
    assert d == D_MODEL and t % tq == 0 and tq % tk == 0 and WINDOW % tk == 0 and t % tm == 0 and n % tm_ffn == 0

    tabs = tuple(a.reshape(b, t, LANES) for a in _rope_tables(positions, 512))
    mem_k, mem_vt = _mem_kv(mem, mem_norm_g, ca_wk.astype(BF16), ca_wv.astype(BF16))
    ncp = t // CMP_STRIDE
    overlap_t, ns = _overlap_matrix_t(t, ncp)
    assert ns <= overlap_t.shape[0]
    n_sel = min(SLC_TOPK, ns)
    hot = jnp.asarray((np.arange(t)[:, None] // SLC_BLOCK == np.arange(LANES)[None, :] % overlap_t.shape[0])
                      .astype(np.float32), BF16)
    gain = lambda l, j: sandwich_g[l, j].reshape(1, d)

    bf = lambda w: w.astype(BF16)
    od_w_in_b, od_w_out_b, ca_wq_b, ca_wo_b = bf(od_w_in), bf(od_w_out), bf(ca_wq), bf(ca_wo)
    ffn_wg_b, ffn_wu_b, ffn_wd_b = bf(ffn_wg), bf(ffn_wu), bf(ffn_wd)
    x2 = x.reshape(n, d)
    dead = {}
    for layer in range(depth):
        x3 = x2.reshape(b, t, d)
        if layer % 2 == 0:
            e = layer // 2
            w, w_small, w_out_fox, w_out_nsa = _even_weights(ev_w_in[e], ev_w_out[e])
            fb_row = jnp.zeros((1, LANES), F32).at[0, :FOX_HEADS].set(ev_fox_fbias[e].astype(F32))
            main, cmp_in, vt, small = _even_proj(x3, gain(layer, 0), w, w_small, fb_row, tabs, tm)
            aq, ak = _fox_aug(small, 512)
            o_fox = dead["fox"] = _fox_attention(main, aq, ak, vt, tq, tk, dead.get("fox"))
            kc, vct = _compress(cmp_in, *_compress_weights(
                ev_cmp_pos_k[e], ev_cmp_w1_k[e], ev_cmp_w2_k[e], ev_cmp_pos_v[e], ev_cmp_w1_v[e], ev_cmp_w2_v[e]))
            oc, sel = dead["sel"] = _nsa_select(main, kc, vct, overlap_t, small, 256, ns, n_sel, dead.get("sel"))
            o_nsa = dead["nsa"] = _nsa_flash(main, sel, oc, small, hot, vt, tq, tk, dead.get("nsa"))
            acts, w_outs = [o_fox.reshape(n, FOX_W), o_nsa.reshape(n, NSA_W)], [(w_out_fox, None), (w_out_nsa, None)]
        else:
            o = layer // 2
            main, vt = _odd_proj(x3, gain(layer, 0), od_w_in_b, o, tabs, tm)
            lam_init = 0.8 - 0.6 * math.exp(-0.3 * layer)
            lam_p = jnp.pad(od_lambda[o].astype(F32), ((0, 0), (0, LANES - HEAD_DIM)))
            attn = dead["diff"] = _diff_attention(main, vt, lam_p, od_subln_g[o].reshape(1, LANES).astype(F32),
                                                  tq, tk_diff, lam_init, dead.get("diff"))
            acts, w_outs = [attn.reshape(n, D_MODEL)], [(od_w_out_b, o)]
        x2 = _post_mixer(acts, w_outs, x2, gain(layer, 1), gain(layer, 2), gain(layer, 3),
                         ca_wq_b, mem_k, mem_vt, ca_wo_b, layer, tm_mix, t // tm_mix, n_part=tm_mix // 256)
        x2 = _ffn(x2, gain(layer, 4), gain(layer, 5), ffn_wg_b, ffn_wu_b, ffn_wd_b, layer, tm_ffn, 256)
    return x2.reshape(b, t, d)
```

```python
import math
from functools import partial

import numpy as np
import jax
import jax.numpy as jnp
from jax import lax
from jax.experimental import pallas as pl
from jax.experimental.pallas import tpu as pltpu

F32 = jnp.float32
BF16 = jnp.bfloat16

D_MODEL = 1024
HEAD_DIM = 64
LANES = 128
ROPE_DIM = HEAD_DIM // 4
ROPE_THETA = 500000.0
FOX_HEADS = 8
NSA_HEADS = 8
NSA_KV_GROUPS = 2
NSA_HPG = NSA_HEADS // NSA_KV_GROUPS
CMP_BLOCK = 32
CMP_STRIDE = 16
CMP_HIDDEN = 2 * HEAD_DIM
SLC_BLOCK = 64
SLC_TOPK = 16
WINDOW = 512
DIFF_HEADS = 8
MEM_HEADS = 4
MEM_W = MEM_HEADS * HEAD_DIM
RMS_EPS = 1e-6
Q_SCALE = HEAD_DIM ** -0.5
LOG2E = math.log2(math.e)
Q_SCALE_LOG2 = Q_SCALE * LOG2E
NEG = -1e30
AUG_PER_HEAD = 6
ONES_ROWS = 16

FOX_W = FOX_HEADS * HEAD_DIM
NSA_W = NSA_HEADS * HEAD_DIM
CH_FQ, CH_FK, CH_NQ, CH_KS, CH_KW = 0, 4, 8, 12, 13
EV_MAIN = 14 * LANES
EV_CMP = 2 * LANES
VT_FV, VT_VS, VT_VW = 0, 4, 5
EV_VT = 6 * LANES
EV_ROPE_CHUNKS = tuple(range(CH_NQ, CH_NQ + 4)) + (CH_KS, CH_KW, EV_MAIN // LANES)
EV_QSCALE_CHUNKS = tuple(range(CH_FQ, CH_FQ + 4)) + tuple(range(CH_NQ, CH_NQ + 4))
OD_MAIN = 2 * D_MODEL
OD_VT = D_MODEL

VMEM_LIMIT = 56 * 1024 * 1024


def _layer_spec(stacked, layer, **kw):
    zeros = (0,) * (stacked.ndim - 1)
    return pl.BlockSpec((None,) + stacked.shape[1:], lambda *idx: (layer,) + zeros, **kw)


def _cparams(sem):
    return pltpu.CompilerParams(dimension_semantics=sem, vmem_limit_bytes=VMEM_LIMIT)


def _rms(x, g):
    return x * lax.rsqrt(jnp.mean(x * x, axis=-1, keepdims=True) + RMS_EPS) * g


def _split3(x):
    hi = x.astype(BF16)
    r1 = x - hi.astype(F32)
    mid = r1.astype(BF16)
    lo = (r1 - mid.astype(F32)).astype(BF16)
    return hi, mid, lo


def _dot(a, b):
    return jnp.dot(a, b, preferred_element_type=F32)


def _dot_nt(a, b):
    return lax.dot_general(a, b, (((1,), (1,)), ((), ())), preferred_element_type=F32)


def _lane_iota(n=LANES):
    return lax.broadcasted_iota(jnp.int32, (1, n), 1)


def _half_mask(q2, half):
    return jnp.where(_lane_iota() // HEAD_DIM == half, q2, jnp.zeros_like(q2))


def _transposed(x):
    return x.astype(F32).T.astype(BF16)


def _half_mask_t(qt, half):
    row = lax.broadcasted_iota(jnp.int32, (qt.shape[0], 1), 0)
    return jnp.where(row // HEAD_DIM == half, qt, jnp.zeros_like(qt))


def _positions_t(j, tk, q0, tq):
    kpos = j * tk + lax.broadcasted_iota(jnp.int32, (tk, 1), 0)
    qpos = q0 + lax.broadcasted_iota(jnp.int32, (1, tq), 1)
    return kpos, qpos


def _stream(base, n_full, n_tail, scores, mask, values, st_ref, acc_ref):
    n_chain, dv, tq = acc_ref.shape

    def park(j, slot):
        for c, st in enumerate(scores(j)):
            st_ref[slot, c] = st

    def step(j, slot, stats, masked, prefetch):
        if prefetch:
            park(j + 1, 1 - slot)
        parts = []
        for c in range(n_chain):
            st = st_ref[slot, c]
            if masked:
                st = mask(j, st)
            m, _ = stats[c]
            m_new = jnp.maximum(m, jnp.max(st, axis=0, keepdims=True))
            p = jnp.exp2(st - m_new).astype(BF16)
            vt1 = jnp.concatenate([values(c, j), jnp.ones((ONES_ROWS, p.shape[0]), BF16)], axis=0)
            parts.append((m_new, jnp.exp2(m - m_new), _dot(vt1, p)))
        out = []
        for c, (m_new, alpha, pv) in enumerate(parts):
            acc_ref[c] = alpha * acc_ref[c] + pv[:dv]
            out.append((m_new, alpha * stats[c][1] + pv[dv:dv + 1]))
        return tuple(out)

    acc_ref[...] = jnp.zeros_like(acc_ref)
    park(base, 0)
    stats = ((jnp.full((1, tq), NEG, F32), jnp.zeros((1, tq), F32)),) * n_chain
    n_tiles = n_full + n_tail
    for s_ in range(n_tiles):
        stats = step(base + s_, s_ % 2, stats, s_ >= n_full, s_ + 1 < n_tiles)
    return [(l, acc_ref[c]) for c, (_, l) in enumerate(stats)]


def _per_query_tile(kernel_fn, n_q, grid, in_specs_fn, out_spec_fn, out_shape, scratch, name, args, init):
    out = jnp.zeros(out_shape.shape, out_shape.dtype) if init is None else init
    for i in range(n_q):
        in_specs = list(in_specs_fn(i)) + [pl.BlockSpec(memory_space=pl.ANY)]
        out = pl.pallas_call(
            partial(kernel_fn, i=i), grid=grid, in_specs=in_specs, out_specs=out_spec_fn(i), out_shape=out_shape,
            scratch_shapes=scratch, input_output_aliases={len(args): 0},
            compiler_params=_cparams(("parallel",) * len(grid)), name=f"{name}_q{i}",
        )(*args, out)
    return out


def _stream_scratch(n_chain, dv, tq, tk):
    return [pltpu.VMEM((2, n_chain, tk, tq), F32), pltpu.VMEM((n_chain, dv, tq), F32)]


def _rope_kernel(pos_ref, inv_ref, m1_ref, m2_ref, c_ref, s1_ref, s2_ref):
    ang = pos_ref[...].astype(F32) * inv_ref[...]
    c_ref[...] = jnp.cos(ang)
    sn = jnp.sin(ang)
    s1_ref[...] = -sn * m1_ref[...]
    s2_ref[...] = sn * m2_ref[...]


def _rope_tables(positions, tm):
    n = positions.size
    inv = ROPE_THETA ** (-jnp.arange(0, ROPE_DIM, 2, dtype=F32) / ROPE_DIM)
    lane = np.arange(LANES) % HEAD_DIM
    half = ROPE_DIM // 2
    inv_l = jnp.where(lane < ROPE_DIM, inv[lane % half], 0.0).reshape(1, LANES).astype(F32)
    m1 = jnp.asarray((lane < half).astype(np.float32)).reshape(1, LANES)
    m2 = jnp.asarray(((lane >= half) & (lane < ROPE_DIM)).astype(np.float32)).reshape(1, LANES)
    row = pl.BlockSpec((1, LANES), lambda i: (0, 0))
    tab = pl.BlockSpec((tm, LANES), lambda i: (i, 0))
    return pl.pallas_call(
        _rope_kernel,
        grid=(n // tm,),
        in_specs=[pl.BlockSpec((tm, 1), lambda i: (i, 0)), row, row, row],
        out_specs=[tab, tab, tab],
        out_shape=[jax.ShapeDtypeStruct((n, LANES), F32)] * 3,
        compiler_params=_cparams(("parallel",)),
        name="rope_tables",
    )(positions.reshape(n, 1), inv_l, m1, m2)


def _apply_rope(y, c, s1, s2):
    half = ROPE_DIM // 2
    return y * c + pltpu.roll(y, LANES - half, 1) * s1 + pltpu.roll(y, half, 1) * s2


def _project_chunks(h, w_ref, tabs, dests, rope_chunks, qscale_chunks):
    c, s1, s2 = tabs
    wide = 2 * LANES
    where = [(kind, ref, k) for kind, ref, n in dests for k in range(n)]
    for ch2 in range(len(where) // 2):
        y2 = _dot(h, w_ref[:, ch2 * wide:(ch2 + 1) * wide])
        for ch in (2 * ch2, 2 * ch2 + 1):
            y = y2[:, (ch % 2) * LANES:(ch % 2 + 1) * LANES]
            if ch in rope_chunks:
                y = _apply_rope(y, c, s1, s2)
            if ch in qscale_chunks:
                y = y * Q_SCALE_LOG2
            kind, ref, k = where[ch]
            if kind == "cols":
                ref[0, k * LANES:(k + 1) * LANES, :] = y.T.astype(BF16)
            elif kind == "rows_f32":
                ref[0, k] = y
            else:
                ref[0, :, k * LANES:(k + 1) * LANES] = y.astype(BF16)


def _even_proj_kernel(x_ref, g_ref, w_ref, ws_ref, fb_ref, c_ref, s1_ref, s2_ref,
                      main_ref, cmp_ref, vt_ref, small_ref):
    h = _rms(x_ref[0], g_ref[...]).astype(BF16)
    dests = [("rows", main_ref, EV_MAIN // LANES), ("rows_f32", cmp_ref, EV_CMP // LANES),
             ("cols", vt_ref, EV_VT // LANES)]
    _project_chunks(h, w_ref, (c_ref[0], s1_ref[0], s2_ref[0]), dests, EV_ROPE_CHUNKS, EV_QSCALE_CHUNKS)
    ys = _dot(h, ws_ref[...])
    z = ys + fb_ref[...]
    log_f = jnp.minimum(z, 0.0) - jnp.log(1.0 + jnp.exp(-jnp.abs(z)))
    gate = 1.0 / (1.0 + jnp.exp(-ys))
    small_ref[0] = jnp.where(_lane_iota() < FOX_HEADS, log_f, gate)


def _even_proj(x3, g, w, w_small, fb_row, tabs, tm):
    b, t, _ = x3.shape
    const = lambda shape: pl.BlockSpec(shape, lambda bi, i: (0, 0))
    tab = pl.BlockSpec((1, tm, LANES), lambda bi, i: (bi, i, 0))
    return pl.pallas_call(
        _even_proj_kernel,
        grid=(b, t // tm),
        in_specs=[pl.BlockSpec((1, tm, D_MODEL), lambda bi, i: (bi, i, 0)), const((1, D_MODEL)),
                  const((D_MODEL, EV_MAIN + EV_CMP + EV_VT)), const((D_MODEL, LANES)), const((1, LANES)),
                  tab, tab, tab],
        out_specs=[pl.BlockSpec((1, tm, EV_MAIN), lambda bi, i: (bi, i, 0)),
                   pl.BlockSpec((1, EV_CMP // LANES, tm, LANES), lambda bi, i: (bi, 0, i, 0)),
                   pl.BlockSpec((1, EV_VT, tm), lambda bi, i: (bi, 0, i)), tab],
        out_shape=[jax.ShapeDtypeStruct((b, t, EV_MAIN), BF16),
                   jax.ShapeDtypeStruct((b, EV_CMP // LANES, t, LANES), F32),
                   jax.ShapeDtypeStruct((b, EV_VT, t), BF16), jax.ShapeDtypeStruct((b, t, LANES), F32)],
        compiler_params=_cparams(("parallel", "parallel")),
        name="even_proj",
    )(x3, g, w, w_small, fb_row, *tabs)


def _odd_proj_kernel(x_ref, g_ref, w_ref, c_ref, s1_ref, s2_ref, main_ref, vt_ref):
    h = _rms(x_ref[0], g_ref[...]).astype(BF16)
    n_main = OD_MAIN // LANES
    dests = [("rows", main_ref, n_main), ("cols", vt_ref, OD_VT // LANES)]
    _project_chunks(h, w_ref, (c_ref[0], s1_ref[0], s2_ref[0]), dests, tuple(range(n_main)),
                    tuple(range(n_main // 2)))


def _odd_proj(x3, g, w_all, layer, tabs, tm):
    b, t, _ = x3.shape
    const = lambda shape: pl.BlockSpec(shape, lambda bi, i: (0, 0))
    tab = pl.BlockSpec((1, tm, LANES), lambda bi, i: (bi, i, 0))
    return pl.pallas_call(
        _odd_proj_kernel,
        grid=(b, t // tm),
        in_specs=[pl.BlockSpec((1, tm, D_MODEL), lambda bi, i: (bi, i, 0)), const((1, D_MODEL)),
                  _layer_spec(w_all, layer), tab, tab, tab],
        out_specs=[pl.BlockSpec((1, tm, OD_MAIN), lambda bi, i: (bi, i, 0)),
                   pl.BlockSpec((1, OD_VT, tm), lambda bi, i: (bi, 0, i))],
        out_shape=[jax.ShapeDtypeStruct((b, t, OD_MAIN), BF16), jax.ShapeDtypeStruct((b, OD_VT, t), BF16)],
        compiler_params=_cparams(("parallel", "parallel")),
        name="odd_proj",
    )(x3, g, w_all, *tabs)


def _fox_aug_kernel(lf_ref, tril_ref, e_ref, one_ref, aq_ref, ak_ref, carry_ref):
    @pl.when(pl.program_id(1) == 0)
    def _():
        carry_ref[...] = jnp.zeros_like(carry_ref)

    tril = tril_ref[...]
    sub = tril.shape[0]
    carry = carry_ref[...]
    blocks = []
    for r0 in range(0, lf_ref.shape[1], sub):
        c = carry
        for piece in _split3(lf_ref[0, r0:r0 + sub, :]):
            c = c + _dot(tril, piece)
        blocks.append(c)
        carry = c[-1:, :]
    carry_ref[...] = carry
    aug = one_ref[...]
    for r, piece in enumerate(_split3(jnp.concatenate(blocks, axis=0) * LOG2E)):
        aug = aug + _dot(piece, e_ref[r])
    aq_ref[0] = aug[:, :LANES].astype(BF16)
    ak_ref[0] = aug[:, LANES:].astype(BF16)


def _fox_aug(small, tc, sub=128):
    b, t, _ = small.shape
    tril = jnp.asarray(np.tril(np.ones((sub, sub), np.float32)), BF16)
    spread = np.zeros((3, LANES, 2 * LANES), np.float32)
    ones = np.zeros((1, 2 * LANES), np.float32)
    for h in range(FOX_HEADS):
        for r in range(3):
            spread[r, h, AUG_PER_HEAD * h + r] = 1.0
            spread[r, h, LANES + AUG_PER_HEAD * h + 3 + r] = -1.0
            ones[0, AUG_PER_HEAD * h + 3 + r] = 1.0
            ones[0, LANES + AUG_PER_HEAD * h + r] = 1.0
    const2 = lambda shape: pl.BlockSpec(shape, lambda bi, i: (0,) * len(shape))
    blk = pl.BlockSpec((1, tc, LANES), lambda bi, i: (bi, i, 0))
    return pl.pallas_call(
        _fox_aug_kernel,
        grid=(b, t // tc),
        in_specs=[blk, const2((sub, sub)), const2((3, LANES, 2 * LANES)), const2((1, 2 * LANES))],
        out_specs=[blk, blk],
        out_shape=[jax.ShapeDtypeStruct((b, t, LANES), BF16)] * 2,
        scratch_shapes=[pltpu.VMEM((1, LANES), F32)],
        compiler_params=_cparams(("parallel", "arbitrary")),
        name="fox_aug",
    )(small, tril, jnp.asarray(spread, BF16), jnp.asarray(ones))


def _fox_kernel(q_ref, aq_ref, k_ref, ak_ref, vt_ref, *rest, i, tq, tk):
    o_ref, st_ref, acc_ref = rest[-3:]
    pair = pl.program_id(1)
    row = lax.broadcasted_iota(jnp.int32, (LANES, 1), 0)
    qt = _transposed(q_ref[0])
    qat = _transposed(aq_ref[0])
    qcats = []
    for hh in range(2):
        head = 2 * pair + hh
        in_head = (row >= AUG_PER_HEAD * head) & (row < AUG_PER_HEAD * (head + 1))
        qcats.append(jnp.concatenate([_half_mask_t(qt, hh), jnp.where(in_head, qat, jnp.zeros_like(qat))], axis=0))

    def scores(j):
        ks = j * tk
        kcat = jnp.concatenate([k_ref[0, pl.ds(ks, tk), :], ak_ref[0, pl.ds(ks, tk), :]], axis=1)
        return tuple(_dot(kcat, qcats[hh]) for hh in range(2))

    def mask(j, st):
        kpos, qpos = _positions_t(j, tk, i * tq, tq)
        return jnp.where(kpos <= qpos, st, NEG)

    def values(hh, j):
        return vt_ref[0, hh * HEAD_DIM:(hh + 1) * HEAD_DIM, pl.ds(j * tk, tk)]

    res = _stream(0, i * (tq // tk), tq // tk, scores, mask, values, st_ref, acc_ref)
    ot = jnp.concatenate([acc / l for l, acc in res], axis=0)
    o_ref[0] = ot.T.astype(BF16)


def _fox_attention(main, aq, ak, vt, tq, tk, init):
    b, t, _ = main.shape
    in_specs = lambda i: [pl.BlockSpec((1, tq, LANES), lambda bi, p: (bi, i, CH_FQ + p)),
                          pl.BlockSpec((1, tq, LANES), lambda bi, p: (bi, i, 0)),
                          pl.BlockSpec((1, t, LANES), lambda bi, p: (bi, 0, CH_FK + p)),
                          pl.BlockSpec((1, t, LANES), lambda bi, p: (bi, 0, 0)),
                          pl.BlockSpec((1, LANES, t), lambda bi, p: (bi, VT_FV + p, 0))]
    return _per_query_tile(
        partial(_fox_kernel, tq=tq, tk=tk), t // tq, (b, FOX_HEADS // 2), in_specs,
        lambda i: pl.BlockSpec((1, tq, LANES), lambda bi, p: (bi, i, p)),
        jax.ShapeDtypeStruct((b, t, FOX_W), BF16), _stream_scratch(2, HEAD_DIM, tq, tk), "fox_attention",
        (main, aq, main, ak, vt), init)


def _compress_kernel(x_ref, pos_ref, w1_ref, w2_ref, kc_ref, vct_ref):
    t = x_ref.shape[2]
    nchunk = t // CMP_STRIDE
    for kv, o_ref in enumerate((kc_ref, vct_ref)):
        first = second = None
        for l in range(CMP_STRIDE):
            xl = x_ref[0, kv, pl.ds(l, nchunk, stride=CMP_STRIDE), :]
            a = _dot((xl + pos_ref[kv, l:l + 1, :]).astype(BF16), w1_ref[kv, l])
            b = _dot((xl + pos_ref[kv, CMP_STRIDE + l:CMP_STRIDE + l + 1, :]).astype(BF16),
                     w1_ref[kv, CMP_STRIDE + l])
            first = a if first is None else first + a
            second = b if second is None else second + b
        hid = first + pltpu.roll(second, nchunk - 1, 0)
        out = _dot(jax.nn.gelu(hid, approximate=True).astype(BF16), w2_ref[kv])
        o_ref[0] = (out.T if o_ref is vct_ref else out).astype(BF16)


def _compress(cmp_in, pos2, w1bd, w2bd):
    b, _, t, _ = cmp_in.shape
    nchunk = t // CMP_STRIDE
    const = lambda a: pl.BlockSpec(a.shape, lambda bi: (0,) * a.ndim)
    return pl.pallas_call(
        _compress_kernel,
        grid=(b,),
        in_specs=[pl.BlockSpec((1, EV_CMP // LANES, t, LANES), lambda bi: (bi, 0, 0, 0)),
                  const(pos2), const(w1bd), const(w2bd)],
        out_specs=[pl.BlockSpec((1, nchunk, LANES), lambda bi: (bi, 0, 0)),
                   pl.BlockSpec((1, LANES, nchunk), lambda bi: (bi, 0, 0))],
        out_shape=[jax.ShapeDtypeStruct((b, nchunk, LANES), BF16), jax.ShapeDtypeStruct((b, LANES, nchunk), BF16)],
        compiler_params=_cparams(("parallel",)),
        name="nsa_compress",
    )(cmp_in, pos2, w1bd, w2bd)


def _gate_col(small, head, branch):
    idx = FOX_HEADS + 3 * head + branch
    return jnp.sum(jnp.where(_lane_iota() == idx, small, 0.0), axis=-1, keepdims=True)


def _nsa_select_kernel(q_ref, kc_ref, vct_ref, ovt_ref, small_ref, *rest, q_lo, tq, ns, n_sel):
    oc_ref, sel_ref = rest[-2:]
    q0 = q_lo + pl.program_id(1) * tq
    lane = _lane_iota()
    kc = kc_ref[0]
    vct = vct_ref[0]
    ncp = kc.shape[0]
    small = small_ref[0]
    qpos = q0 + lax.broadcasted_iota(jnp.int32, (1, tq), 1)
    cmp_end = lax.broadcasted_iota(jnp.int32, (ncp, 1), 0) * CMP_STRIDE + (CMP_BLOCK - 1)
    cmask = cmp_end <= qpos
    psum = [jnp.zeros((ncp, tq), F32) for _ in range(NSA_KV_GROUPS)]
    logits = [[_dot_nt(kc, _half_mask(q_ref[0, :, n * LANES:(n + 1) * LANES], g)) for g in range(NSA_KV_GROUPS)]
              for n in range(NSA_HPG)]
    for n in range(NSA_HPG):
        ots = []
        for g in range(NSA_KV_GROUPS):
            z = jnp.where(cmask, logits[n][g], -jnp.inf)
            m = jnp.max(z, axis=0, keepdims=True)
            m = jnp.where(m == -jnp.inf, 0.0, m)
            p = jnp.exp2(z - m)
            p = p / jnp.maximum(jnp.sum(p, axis=0, keepdims=True), 1e-30)
            psum[g] = psum[g] + p
            ots.append(_dot(vct[g * HEAD_DIM:(g + 1) * HEAD_DIM], p.astype(BF16)))
        gate = jnp.where(lane < HEAD_DIM, _gate_col(small, n, 0), _gate_col(small, NSA_HPG + n, 0))
        oc_ref[0, :, n * LANES:(n + 1) * LANES] = gate * jnp.concatenate(ots, axis=0).T

    nsp = ovt_ref.shape[0]
    blk = lax.broadcasted_iota(jnp.int32, (nsp, 1), 0)
    cur = qpos // SLC_BLOCK
    valid = blk * SLC_BLOCK <= qpos
    forced = (blk == 0) | (blk == cur) | (blk == cur - 1)
    scores = []
    for g in range(NSA_KV_GROUPS):
        imp = jnp.zeros((nsp, tq), F32)
        for piece in _split3(psum[g]):
            imp = imp + _dot(ovt_ref[...], piece)
        scores.append(jnp.where(valid, jnp.where(forced, jnp.inf, imp), -jnp.inf))
    slab = 8
    masks = []
    for g in range(NSA_KV_GROUPS):
        slabs = [scores[g][r:r + slab] for r in range(0, nsp, slab)]
        ranks = [jnp.zeros((slab, tq), jnp.int32) for _ in slabs]
        for i in range(ns):
            row = scores[g][i:i + 1, :]
            for r, sl in enumerate(slabs):
                if slab * r >= ns:
                    continue
                if slab * r > i:
                    ahead = row >= sl
                elif slab * (r + 1) - 1 <= i:
                    ahead = row > sl
                else:
                    ahead = (row > sl) | ((row == sl) & (blk[slab * r:slab * (r + 1)] > i))
                ranks[r] = jnp.where(ahead, ranks[r] + 1, ranks[r])
        rank = jnp.concatenate(ranks, axis=0)
        masks.append(jnp.where((rank < n_sel) & (blk < ns), 0.0, NEG))
    sel_ref[0] = jnp.concatenate(masks, axis=0).T.astype(BF16)


def _nsa_select(main, kc, vct, overlap_t, small, tq, ns, n_sel, init, n_span=4):
    b, t, _ = main.shape
    nsp = overlap_t.shape[0]
    span = t // n_span
    steps = span // tq
    shapes = [jax.ShapeDtypeStruct((b, t, NSA_W), F32), jax.ShapeDtypeStruct((b, t, NSA_KV_GROUPS * nsp), BF16)]
    outs = [jnp.zeros(s.shape, s.dtype) for s in shapes] if init is None else init
    for k in range(n_span):
        q_hi = (k + 1) * span
        ncl = min(-(-(q_hi // CMP_STRIDE) // LANES) * LANES, kc.shape[1])
        row = lambda bi, i, k=k: (bi, k * steps + i, 0)
        outs = pl.pallas_call(
            partial(_nsa_select_kernel, q_lo=k * span, tq=tq, ns=min(ns, q_hi // SLC_BLOCK), n_sel=n_sel),
            grid=(b, steps),
            in_specs=[pl.BlockSpec((1, tq, NSA_W), lambda bi, i, k=k: (bi, k * steps + i, CH_NQ * LANES // NSA_W)),
                      pl.BlockSpec((1, ncl, LANES), lambda bi, i: (bi, 0, 0)),
                      pl.BlockSpec((1, LANES, ncl), lambda bi, i: (bi, 0, 0)),
                      pl.BlockSpec((nsp, ncl), lambda bi, i: (0, 0)),
                      pl.BlockSpec((1, tq, LANES), row),
                      pl.BlockSpec(memory_space=pl.ANY), pl.BlockSpec(memory_space=pl.ANY)],
            out_specs=[pl.BlockSpec((1, tq, NSA_W), row), pl.BlockSpec((1, tq, NSA_KV_GROUPS * nsp), row)],
            out_shape=shapes,
            input_output_aliases={5: 0, 6: 1},
            compiler_params=_cparams(("parallel", "parallel")),
            name=f"nsa_select_s{k}",
        )(main, kc, vct, overlap_t, small, *outs)
    return outs


def _nsa_flash_kernel(q_ref, sel_ref, oc_ref, small_ref, ks_ref, kw_ref, hot_ref, vst_ref, vwt_ref, *rest,
                      i, tq, tk):
    o_ref, st_ref, acc_ref = rest[-3:]
    n = pl.program_id(1)
    q0 = i * tq
    q2 = q_ref[0]
    small = small_ref[0]
    qt = _transposed(q2)
    selt = _transposed(sel_ref[0])
    qhs = [_half_mask_t(qt, g) for g in range(NSA_KV_GROUPS)]
    qcats = [jnp.concatenate([qhs[g], _half_mask_t(selt, g)], axis=0) for g in range(NSA_KV_GROUPS)]
    rows = lambda g: slice(g * HEAD_DIM, (g + 1) * HEAD_DIM)

    def sel_scores(j):
        ks = j * tk
        kcat = jnp.concatenate([ks_ref[0, pl.ds(ks, tk), :], hot_ref[pl.ds(ks, tk), :]], axis=1)
        return tuple(_dot(kcat, qcats[g]) for g in range(NSA_KV_GROUPS))

    def sel_mask(j, st):
        kpos, qpos = _positions_t(j, tk, q0, tq)
        return jnp.where(kpos <= qpos, st, NEG)

    def win_scores(j):
        kw = kw_ref[0, pl.ds(j * tk, tk), :]
        return tuple(_dot(kw, qhs[g]) for g in range(NSA_KV_GROUPS))

    def win_mask(j, st):
        kpos, qpos = _positions_t(j, tk, q0, tq)
        return jnp.where((kpos <= qpos) & (kpos > qpos - WINDOW), st, NEG)

    values = lambda ref: lambda g, j: ref[0, rows(g), pl.ds(j * tk, tk)]
    lane = _lane_iota()
    n_end = (i + 1) * (tq // tk)
    win_lo = max(n_end - (WINDOW + tq) // tk, 0)
    o = oc_ref[0]
    for branch, args in ((1, (0, i * (tq // tk), tq // tk, sel_scores, sel_mask, values(vst_ref))),
                         (2, (win_lo, 0, n_end - win_lo, win_scores, win_mask, values(vwt_ref)))):
        res = _stream(*args, st_ref, acc_ref)
        ot = jnp.concatenate([acc / l for l, acc in res], axis=0)
        gate = jnp.where(lane < HEAD_DIM, _gate_col(small, n, branch), _gate_col(small, NSA_HPG + n, branch))
        o = o + gate * ot.T
    o_ref[0] = o.astype(BF16)


def _nsa_flash(main, sel, oc, small, hot, vt, tq, tk, init):
    b, t, _ = main.shape
    tile = lambda i, ch: pl.BlockSpec((1, tq, LANES), lambda bi, n: (bi, i, ch + n))
    shared = lambda i: pl.BlockSpec((1, tq, LANES), lambda bi, n: (bi, i, 0))
    full = lambda ch: pl.BlockSpec((1, t, LANES), lambda bi, n: (bi, 0, ch))
    vfull = lambda ch: pl.BlockSpec((1, LANES, t), lambda bi, n: (bi, ch, 0))
    in_specs = lambda i: [tile(i, CH_NQ), shared(i), tile(i, 0), shared(i), full(CH_KS), full(CH_KW),
                          pl.BlockSpec((t, LANES), lambda bi, n: (0, 0)), vfull(VT_VS), vfull(VT_VW)]
    return _per_query_tile(
        partial(_nsa_flash_kernel, tq=tq, tk=tk), t // tq, (b, NSA_HPG), in_specs, lambda i: tile(i, 0),
        jax.ShapeDtypeStruct((b, t, NSA_W), BF16), _stream_scratch(NSA_KV_GROUPS, HEAD_DIM, tq, tk), "nsa_flash",
        (main, sel, oc, small, main, main, hot, vt, vt), init)


def _diff_kernel(q_ref, k_ref, vt_ref, lam_ref, g_ref, *rest, i, tq, tk, lam_init):
    o_ref, st_ref, acc_ref = rest[-3:]
    q2 = q_ref[0]
    lp = lam_ref[...]
    lam = (jnp.exp(jnp.sum(lp[0:1] * lp[1:2], axis=-1, keepdims=True))
           - jnp.exp(jnp.sum(lp[2:3] * lp[3:4], axis=-1, keepdims=True)) + lam_init)
    qhs = [_half_mask(q2, comp) for comp in range(2)]

    def scores(j):
        k2 = k_ref[0, pl.ds(j * tk, tk), :]
        return tuple(_dot_nt(k2, qhs[comp]) for comp in range(2))

    def mask(j, st):
        kpos, qpos = _positions_t(j, tk, i * tq, tq)
        return jnp.where(kpos <= qpos, st, NEG)

    def values(comp, j):
        return vt_ref[0, :, pl.ds(j * tk, tk)]

    (l1, acc1), (l2, acc2) = _stream(0, i * (tq // tk), tq // tk, scores, mask, values, st_ref, acc_ref)
    o = (acc1 / l1 - lam * (acc2 / l2)).T
    o_ref[0] = (_rms(o, g_ref[...]) * (1.0 - lam_init)).astype(BF16)


def _diff_attention(main, vt, lam_p, subln_g, tq, tk, lam_init, init):
    b, t, _ = main.shape
    nh = DIFF_HEADS
    in_specs = lambda i: [pl.BlockSpec((1, tq, LANES), lambda bi, h: (bi, i, h)),
                          pl.BlockSpec((1, t, LANES), lambda bi, h: (bi, 0, nh + h)),
                          pl.BlockSpec((1, LANES, t), lambda bi, h: (bi, h, 0)),
                          pl.BlockSpec((4, LANES), lambda bi, h: (0, 0)),
                          pl.BlockSpec((1, LANES), lambda bi, h: (0, 0))]
    return _per_query_tile(
        partial(_diff_kernel, tq=tq, tk=tk, lam_init=lam_init), t // tq, (b, nh), in_specs,
        lambda i: pl.BlockSpec((1, tq, LANES), lambda bi, h: (bi, i, h)),
        jax.ShapeDtypeStruct((b, t, nh * LANES), BF16), _stream_scratch(2, LANES, tq, tk), "diff_attention",
        (main, main, vt, lam_p, subln_g), init)


def _mem_kv_kernel(mem_ref, g_ref, wk_ref, wv_ref, k_ref, vt_ref):
    mn = _rms(mem_ref[0], g_ref[0]).astype(BF16)
    k_ref[0, 0] = _dot(mn, wk_ref[0]).astype(BF16)
    vt_ref[0, 0] = _dot(mn, wv_ref[0]).T.astype(BF16)


def _mem_kv(mem, mem_norm_g, wk, wv):
    depth = wk.shape[0]
    b, m, d = mem.shape
    wspec = pl.BlockSpec((1, d, MEM_W), lambda l, bi: (l, 0, 0))
    return pl.pallas_call(
        _mem_kv_kernel,
        grid=(depth, b),
        in_specs=[pl.BlockSpec((1, m, d), lambda l, bi: (bi, 0, 0)),
                  pl.BlockSpec((1, 1, d), lambda l, bi: (l, 0, 0)), wspec, wspec],
        out_specs=[pl.BlockSpec((1, 1, m, MEM_W), lambda l, bi: (l, bi, 0, 0)),
                   pl.BlockSpec((1, 1, MEM_W, m), lambda l, bi: (l, bi, 0, 0))],
        out_shape=[jax.ShapeDtypeStruct((depth, b, m, MEM_W), BF16),
                   jax.ShapeDtypeStruct((depth, b, MEM_W, m), BF16)],
        compiler_params=_cparams(("parallel", "parallel")),
        name="mem_kv",
    )(mem, mem_norm_g.reshape(depth, 1, d), wk, wv)


def _post_mixer_kernel(*refs, n_in, n_part):
    a_refs, w_refs = refs[:n_in], refs[n_in:2 * n_in]
    x_ref, g1_ref, g2_ref, g3_ref, wq_ref, k_ref, vt_ref, wo_ref, o_ref = refs[2 * n_in:]
    rows = x_ref.shape[0] // n_part
    parts = [slice(r * rows, (r + 1) * rows) for r in range(n_part)]
    n_mem = k_ref.shape[2]
    heads = [(ch, hh) for ch in range(MEM_W // LANES) for hh in range(2)]
    ones = jnp.ones((ONES_ROWS, n_mem), BF16)

    def mixer_out(ps):
        y = None
        for a_ref, w_ref in zip(a_refs, w_refs):
            t = _dot(a_ref[ps, :], w_ref[...])
            y = t if y is None else y + t
        return y

    def scores(q):
        return [_dot_nt(k_ref[0, 0, :, ch * LANES:(ch + 1) * LANES], _half_mask(q[:, ch * LANES:(ch + 1) * LANES], hh))
                for ch, hh in heads]

    def attend(logits):
        outs = []
        for (ch, hh), st in zip(heads, logits):
            p = jnp.exp2(st - jnp.max(st, axis=0, keepdims=True)).astype(BF16)
            r0 = ch * LANES + hh * HEAD_DIM
            pv = _dot(jnp.concatenate([vt_ref[0, 0, r0:r0 + HEAD_DIM, :], ones], axis=0), p)
            outs.append(pv[:HEAD_DIM] / pv[HEAD_DIM:HEAD_DIM + 1])
        return jnp.concatenate(outs, axis=0).T.astype(BF16)

    ys = [mixer_out(ps) for ps in parts]
    xs = [x_ref[ps, :] + _rms(y, g1_ref[...]) for ps, y in zip(parts, ys)]
    qs = [(_dot(_rms(x, g2_ref[...]).astype(BF16), wq_ref[...]) * Q_SCALE_LOG2).astype(BF16) for x in xs]
    logits = [scores(q) for q in qs]
    os_ = [attend(lg) for lg in logits]
    for ps, x, o in zip(parts, xs, os_):
        o_ref[ps, :] = x + _rms(_dot(o, wo_ref[...]), g3_ref[...])


def _post_mixer(acts, weights, x2, g1, g2, g3, wq_all, mem_k, mem_vt, wo_all, layer, tm, rows_per_batch, n_part=2):
    n = x2.shape[0]
    m = mem_k.shape[2]
    const = lambda shape: pl.BlockSpec(shape, lambda i: (0, 0))
    row = pl.BlockSpec((tm, D_MODEL), lambda i: (i, 0))
    gspec = const((1, D_MODEL))
    return pl.pallas_call(
        partial(_post_mixer_kernel, n_in=len(acts), n_part=n_part),
        grid=(n // tm,),
        in_specs=[pl.BlockSpec((tm, a.shape[1]), lambda i: (i, 0)) for a in acts]
                 + [const(w.shape) if l is None else _layer_spec(w, l) for w, l in weights]
                 + [row, gspec, gspec, gspec, _layer_spec(wq_all, layer),
                    pl.BlockSpec((1, 1, m, MEM_W), lambda i: (layer, i // rows_per_batch, 0, 0)),
                    pl.BlockSpec((1, 1, MEM_W, m), lambda i: (layer, i // rows_per_batch, 0, 0)),
                    _layer_spec(wo_all, layer)],
        out_specs=row,
        out_shape=jax.ShapeDtypeStruct((n, D_MODEL), F32),
        compiler_params=_cparams(("parallel",)),
        name="post_mixer",
    )(*acts, *[w for w, _ in weights], x2, g1, g2, g3, wq_all, mem_k, mem_vt, wo_all)


def _ffn_kernel(x_ref, gin_ref, gout_ref, wg_ref, wu_ref, wd_ref, o_ref, *, chunk):
    x = x_ref[...]
    h = _rms(x, gin_ref[...]).astype(BF16)
    d_ff = wg_ref.shape[1]
    y = None
    for c0 in range(0, d_ff, chunk):
        gate = _dot(h, wg_ref[:, c0:c0 + chunk])
        up = _dot(h, wu_ref[:, c0:c0 + chunk])
        a = (gate * (1.0 / (1.0 + jnp.exp(-gate))) * up).astype(BF16)
        t = _dot(a, wd_ref[c0:c0 + chunk, :])
        y = t if y is None else y + t
    o_ref[...] = x + _rms(y, gout_ref[...])


def _ffn(x2, gin, gout, wg_all, wu_all, wd_all, layer, tm, chunk):
    n = x2.shape[0]
    const = lambda shape: pl.BlockSpec(shape, lambda i: (0, 0), pipeline_mode=pl.Buffered(1))
    wspec = lambda w: _layer_spec(w, layer, pipeline_mode=pl.Buffered(1))
    return pl.pallas_call(
        partial(_ffn_kernel, chunk=chunk),
        grid=(n // tm,),
        in_specs=[pl.BlockSpec((tm, D_MODEL), lambda i: (i, 0)), const((1, D_MODEL)), const((1, D_MODEL)),
                  wspec(wg_all), wspec(wu_all), wspec(wd_all)],
        out_specs=pl.BlockSpec((tm, D_MODEL), lambda i: (i, 0)),
        out_shape=jax.ShapeDtypeStruct((n, D_MODEL), F32),
        compiler_params=_cparams(("parallel",)),
        name="ffn",
    )(x2, gin, gout, wg_all, wu_all, wd_all)


def _even_weights(w_in, w_out):
    kvw = NSA_KV_GROUPS * HEAD_DIM
    offs = [int(o) for o in np.cumsum((FOX_W, FOX_W, FOX_W, FOX_HEADS, NSA_W) + (kvw,) * 6 + (3 * NSA_HEADS,))]
    fk0, fv0, fl0, nq0, kc0, vc0, ks0, vs0, kw0, vw0, gl0, end = offs
    paired = [HEAD_DIM * (g * NSA_HPG + n) for n in range(NSA_HPG) for g in range(NSA_KV_GROUPS)]
    cols = lambda a, b: w_in[:, a:b]
    w = jnp.concatenate([cols(0, fv0)] + [cols(nq0 + h, nq0 + h + HEAD_DIM) for h in paired]
                        + [cols(ks0, vs0), cols(kw0, vw0), cols(kc0, vc0), cols(vc0, ks0),
                           cols(fv0, fl0), cols(vs0, kw0), cols(vw0, gl0)], axis=1).astype(BF16)
    w_small = jnp.concatenate([cols(fl0, nq0), cols(gl0, end),
                               jnp.zeros((D_MODEL, LANES - FOX_HEADS - 3 * NSA_HEADS), w_in.dtype)],
                              axis=1).astype(BF16)
    w_out_fox = w_out[:FOX_W].astype(BF16)
    w_out_nsa = jnp.concatenate([w_out[FOX_W + h:FOX_W + h + HEAD_DIM] for h in paired], axis=0).astype(BF16)
    return w, w_small, w_out_fox, w_out_nsa


def _overlap_matrix_t(t, ncp):
    nc = (t - CMP_BLOCK) // CMP_STRIDE + 1
    ns = t // SLC_BLOCK
    cs = np.arange(nc) * CMP_STRIDE
    ss = np.arange(ns) * SLC_BLOCK
    ov = np.clip(np.minimum(cs[:, None] + CMP_BLOCK, ss[None, :] + SLC_BLOCK)
                 - np.maximum(cs[:, None], ss[None, :]), 0, None) / CMP_BLOCK
    full = np.zeros((LANES // NSA_KV_GROUPS, ncp), np.float32)
    full[:ns, :nc] = ov.T
    return jnp.asarray(full, BF16), ns


def _compress_weights(pos_k, w1_k, w2_k, pos_v, w1_v, w2_v):
    g = NSA_KV_GROUPS
    pos2, w1bd, w2bd = [], [], []
    for pos, w1, w2 in ((pos_k, w1_k, w2_k), (pos_v, w1_v, w2_v)):
        pos2.append(jnp.tile(pos.astype(F32), (1, g)))
        w1l = w1.reshape(CMP_BLOCK, HEAD_DIM, CMP_HIDDEN).astype(BF16)
        bd = jnp.zeros((CMP_BLOCK, g * HEAD_DIM, g * CMP_HIDDEN), BF16)
        w2g = jnp.zeros((g * CMP_HIDDEN, g * HEAD_DIM), BF16)
        for gi in range(g):
            bd = bd.at[:, gi * HEAD_DIM:(gi + 1) * HEAD_DIM, gi * CMP_HIDDEN:(gi + 1) * CMP_HIDDEN].set(w1l)
            w2g = w2g.at[gi * CMP_HIDDEN:(gi + 1) * CMP_HIDDEN, gi * HEAD_DIM:(gi + 1) * HEAD_DIM].set(
                w2.astype(BF16))
        w1bd.append(bd)
        w2bd.append(w2g)
    return jnp.stack(pos2), jnp.stack(w1bd), jnp.stack(w2bd)


def kernel(x, mem, positions, sandwich_g, mem_norm_g, ev_w_in, ev_fox_fbias, ev_cmp_pos_k, ev_cmp_w1_k, ev_cmp_w2_k, ev_cmp_pos_v, ev_cmp_w1_v, ev_cmp_w2_v, ev_w_out, od_w_in, od_lambda, od_subln_g, od_w_out, ca_wq, ca_wk, ca_wv, ca_wo, ffn_wg, ffn_wu, ffn_wd):
    b, t, d = x.shape
    depth = sandwich_g.shape[0]
    n = b * t
    tm = 512
    tm_ffn = 512
    tm_mix = 1024
    tq, tk = 512, 256
    tk_diff = 512
    assert d == D_MODEL and t % tq == 0 and tq % tk == 0 and WINDOW % tk == 0 and t % tm == 0 and n % tm_ffn == 0

    tabs = tuple(a.reshape(b, t, LANES) for a in _rope_tables(positions, 512))
    mem_k, mem_vt = _mem_kv(mem, mem_norm_g, ca_wk.astype(BF16), ca_wv.astype(BF16))
    ncp = t // CMP_STRIDE
    overlap_t, ns = _overlap_matrix_t(t, ncp)
    assert ns <= overlap_t.shape[0]
    n_sel = min(SLC_TOPK, ns)
    hot = jnp.asarray((np.arange(t)[:, None] // SLC_BLOCK == np.arange(LANES)[None, :] % overlap_t.shape[0])
                      .astype(np.float32), BF16)
    gain = lambda l, j: sandwich_g[l, j].reshape(1, d)

    bf = lambda w: w.astype(BF16)
    od_w_in_b, od_w_out_b, ca_wq_b, ca_wo_b = bf(od_w_in), bf(od_w_out), bf(ca_wq), bf(ca_wo)
    ffn_wg_b, ffn_wu_b, ffn_wd_b = bf(ffn_wg), bf(ffn_wu), bf(ffn_wd)
    x2 = x.reshape(n, d)
    dead = {}
    for layer in range(depth):
        x3 = x2.reshape(b, t, d)
        if layer % 2 == 0:
            e = layer // 2
            w, w_small, w_out_fox, w_out_nsa = _even_weights(ev_w_in[e], ev_w_out[e])
            fb_row = jnp.zeros((1, LANES), F32).at[0, :FOX_HEADS].set(ev_fox_fbias[e].astype(F32))
            main, cmp_in, vt, small = _even_proj(x3, gain(layer, 0), w, w_small, fb_row, tabs, tm)
            aq, ak = _fox_aug(small, 512)
            o_fox = dead["fox"] = _fox_attention(main, aq, ak, vt, tq, tk, dead.get("fox"))
            kc, vct = _compress(cmp_in, *_compress_weights(
                ev_cmp_pos_k[e], ev_cmp_w1_k[e], ev_cmp_w2_k[e], ev_cmp_pos_v[e], ev_cmp_w1_v[e], ev_cmp_w2_v[e]))
            oc, sel = dead["sel"] = _nsa_select(main, kc, vct, overlap_t, small, 256, ns, n_sel, dead.get("sel"))
            o_nsa = dead["nsa"] = _nsa_flash(main, sel, oc, small, hot, vt, tq, tk, dead.get("nsa"))
            acts, w_outs = [o_fox.reshape(n, FOX_W), o_nsa.reshape(n, NSA_W)], [(w_out_fox, None), (w_out_nsa, None)]
        else:
            o = layer // 2
            main, vt = _odd_proj(x3, gain(layer, 0), od_w_in_b, o, tabs, tm)
            lam_init = 0.8 - 0.6 * math.exp(-0.3 * layer)
            lam_p = jnp.pad(od_lambda[o].astype(F32), ((0, 0), (0, LANES - HEAD_DIM)))
            attn = dead["diff"] = _diff_attention(main, vt, lam_p, od_subln_g[o].reshape(1, LANES).astype(F32),
                                                  tq, tk_diff, lam_init, dead.get("diff"))
            acts, w_outs = [attn.reshape(n, D_MODEL)], [(od_w_out_b, o)]
        x2 = _post_mixer(acts, w_outs, x2, gain(layer, 1), gain(layer, 2), gain(layer, 3),
                         ca_wq_b, mem_k, mem_vt, ca_wo_b, layer, tm_mix, t // tm_mix, n_part=tm_mix // 256)
        x2 = _ffn(x2, gain(layer, 4), gain(layer, 5), ffn_wg_b, ffn_wu_b, ffn_wd_b, layer, tm_ffn, 256)
    return x2.reshape(b, t, d)
```

```python
import math
from functools import partial

import numpy as np
import jax
import jax.numpy as jnp
from jax import lax
from jax.experimental import pallas as pl
from jax.experimental.pallas import tpu as pltpu

F32 = jnp.float32
BF16 = jnp.bfloat16

D_MODEL = 1024
HEAD_DIM = 64
LANES = 128
ROPE_DIM = HEAD_DIM // 4
ROPE_THETA = 500000.0
FOX_HEADS = 8
NSA_HEADS = 8
NSA_KV_GROUPS = 2
NSA_HPG = NSA_HEADS // NSA_KV_GROUPS
CMP_BLOCK = 32
CMP_STRIDE = 16
CMP_HIDDEN = 2 * HEAD_DIM
SLC_BLOCK = 64
SLC_TOPK = 16
WINDOW = 512
DIFF_HEADS = 8
MEM_HEADS = 4
MEM_W = MEM_HEADS * HEAD_DIM
RMS_EPS = 1e-6
Q_SCALE = HEAD_DIM ** -0.5
LOG2E = math.log2(math.e)
Q_SCALE_LOG2 = Q_SCALE * LOG2E
NEG = -1e30
AUG_PER_HEAD = 6
ONES_ROWS = 16

FOX_W = FOX_HEADS * HEAD_DIM
NSA_W = NSA_HEADS * HEAD_DIM
CH_FQ, CH_FK, CH_NQ, CH_KS, CH_KW = 0, 4, 8, 12, 13
EV_MAIN = 14 * LANES
EV_CMP = 2 * LANES
VT_FV, VT_VS, VT_VW = 0, 4, 5
EV_VT = 6 * LANES
EV_ROPE_CHUNKS = tuple(range(CH_NQ, CH_NQ + 4)) + (CH_KS, CH_KW, EV_MAIN // LANES)
EV_QSCALE_CHUNKS = tuple(range(CH_FQ, CH_FQ + 4)) + tuple(range(CH_NQ, CH_NQ + 4))
OD_MAIN = 2 * D_MODEL
OD_VT = D_MODEL

VMEM_LIMIT = 56 * 1024 * 1024


def _layer_spec(stacked, layer, **kw):
    zeros = (0,) * (stacked.ndim - 1)
    return pl.BlockSpec((None,) + stacked.shape[1:], lambda *idx: (layer,) + zeros, **kw)


def _cparams(sem):
    return pltpu.CompilerParams(dimension_semantics=sem, vmem_limit_bytes=VMEM_LIMIT)


def _rms(x, g):
    return x * lax.rsqrt(jnp.mean(x * x, axis=-1, keepdims=True) + RMS_EPS) * g


def _split3(x):
    hi = x.astype(BF16)
    r1 = x - hi.astype(F32)
    mid = r1.astype(BF16)
    lo = (r1 - mid.astype(F32)).astype(BF16)
    return hi, mid, lo


def _dot(a, b):
    return jnp.dot(a, b, preferred_element_type=F32)


def _dot_nt(a, b):
    return lax.dot_general(a, b, (((1,), (1,)), ((), ())), preferred_element_type=F32)


def _lane_iota(n=LANES):
    return lax.broadcasted_iota(jnp.int32, (1, n), 1)


def _half_mask(q2, half):
    return jnp.where(_lane_iota() // HEAD_DIM == half, q2, jnp.zeros_like(q2))


def _transposed(x):
    return x.astype(F32).T.astype(BF16)


def _half_mask_t(qt, half):
    row = lax.broadcasted_iota(jnp.int32, (qt.shape[0], 1), 0)
    return jnp.where(row // HEAD_DIM == half, qt, jnp.zeros_like(qt))


def _positions_t(j, tk, q0, tq):
    kpos = j * tk + lax.broadcasted_iota(jnp.int32, (tk, 1), 0)
    qpos = q0 + lax.broadcasted_iota(jnp.int32, (1, tq), 1)
    return kpos, qpos


def _stream(base, n_full, n_tail, scores, mask, values, st_ref, acc_ref):
    n_chain, dv, tq = acc_ref.shape

    def park(j, slot):
        for c, st in enumerate(scores(j)):
            st_ref[slot, c] = st

    def step(j, slot, stats, masked, prefetch):
        if prefetch:
            park(j + 1, 1 - slot)
        parts = []
        for c in range(n_chain):
            st = st_ref[slot, c]
            if masked:
                st = mask(j, st)
            m, _ = stats[c]
            m_new = jnp.maximum(m, jnp.max(st, axis=0, keepdims=True))
            p = jnp.exp2(st - m_new).astype(BF16)
            vt1 = jnp.concatenate([values(c, j), jnp.ones((ONES_ROWS, p.shape[0]), BF16)], axis=0)
            parts.append((m_new, jnp.exp2(m - m_new), _dot(vt1, p)))
        out = []
        for c, (m_new, alpha, pv) in enumerate(parts):
            acc_ref[c] = alpha * acc_ref[c] + pv[:dv]
            out.append((m_new, alpha * stats[c][1] + pv[dv:dv + 1]))
        return tuple(out)

    acc_ref[...] = jnp.zeros_like(acc_ref)
    park(base, 0)
    stats = ((jnp.full((1, tq), NEG, F32), jnp.zeros((1, tq), F32)),) * n_chain
    n_tiles = n_full + n_tail
    for s_ in range(n_tiles):
        stats = step(base + s_, s_ % 2, stats, s_ >= n_full, s_ + 1 < n_tiles)
    return [(l, acc_ref[c]) for c, (_, l) in enumerate(stats)]


def _per_query_tile(kernel_fn, n_q, grid, in_specs_fn, out_spec_fn, out_shape, scratch, name, args, init):
    out = jnp.zeros(out_shape.shape, out_shape.dtype) if init is None else init
    for i in range(n_q):
        in_specs = list(in_specs_fn(i)) + [pl.BlockSpec(memory_space=pl.ANY)]
        out = pl.pallas_call(
            partial(kernel_fn, i=i), grid=grid, in_specs=in_specs, out_specs=out_spec_fn(i), out_shape=out_shape,
            scratch_shapes=scratch, input_output_aliases={len(args): 0},
            compiler_params=_cparams(("parallel",) * len(grid)), name=f"{name}_q{i}",
        )(*args, out)
    return out


def _stream_scratch(n_chain, dv, tq, tk):
    return [pltpu.VMEM((2, n_chain, tk, tq), F32), pltpu.VMEM((n_chain, dv, tq), F32)]


def _rope_kernel(pos_ref, inv_ref, m1_ref, m2_ref, c_ref, s1_ref, s2_ref):
    ang = pos_ref[...].astype(F32) * inv_ref[...]
    c_ref[...] = jnp.cos(ang)
    sn = jnp.sin(ang)
    s1_ref[...] = -sn * m1_ref[...]
    s2_ref[...] = sn * m2_ref[...]


def _rope_tables(positions, tm):
    n = positions.size
    inv = ROPE_THETA ** (-jnp.arange(0, ROPE_DIM, 2, dtype=F32) / ROPE_DIM)
    lane = np.arange(LANES) % HEAD_DIM
    half = ROPE_DIM // 2
    inv_l = jnp.where(lane < ROPE_DIM, inv[lane % half], 0.0).reshape(1, LANES).astype(F32)
    m1 = jnp.asarray((lane < half).astype(np.float32)).reshape(1, LANES)
    m2 = jnp.asarray(((lane >= half) & (lane < ROPE_DIM)).astype(np.float32)).reshape(1, LANES)
    row = pl.BlockSpec((1, LANES), lambda i: (0, 0))
    tab = pl.BlockSpec((tm, LANES), lambda i: (i, 0))
    return pl.pallas_call(
        _rope_kernel,
        grid=(n // tm,),
        in_specs=[pl.BlockSpec((tm, 1), lambda i: (i, 0)), row, row, row],
        out_specs=[tab, tab, tab],
        out_shape=[jax.ShapeDtypeStruct((n, LANES), F32)] * 3,
        compiler_params=_cparams(("parallel",)),
        name="rope_tables",
    )(positions.reshape(n, 1), inv_l, m1, m2)


def _apply_rope(y, c, s1, s2):
    half = ROPE_DIM // 2
    return y * c + pltpu.roll(y, LANES - half, 1) * s1 + pltpu.roll(y, half, 1) * s2


def _project_chunks(h, w_ref, tabs, dests, rope_chunks, qscale_chunks):
    c, s1, s2 = tabs
    wide = 2 * LANES
    where = [(kind, ref, k) for kind, ref, n in dests for k in range(n)]
    for ch2 in range(len(where) // 2):
        y2 = _dot(h, w_ref[:, ch2 * wide:(ch2 + 1) * wide])
        for ch in (2 * ch2, 2 * ch2 + 1):
            y = y2[:, (ch % 2) * LANES:(ch % 2 + 1) * LANES]
            if ch in rope_chunks:
                y = _apply_rope(y, c, s1, s2)
            if ch in qscale_chunks:
                y = y * Q_SCALE_LOG2
            kind, ref, k = where[ch]
            if kind == "cols":
                ref[0, k * LANES:(k + 1) * LANES, :] = y.T.astype(BF16)
            elif kind == "rows_f32":
                ref[0, k] = y
            else:
                ref[0, :, k * LANES:(k + 1) * LANES] = y.astype(BF16)


def _even_proj_kernel(x_ref, g_ref, w_ref, ws_ref, fb_ref, c_ref, s1_ref, s2_ref,
                      main_ref, cmp_ref, vt_ref, small_ref):
    h = _rms(x_ref[0], g_ref[...]).astype(BF16)
    dests = [("rows", main_ref, EV_MAIN // LANES), ("rows_f32", cmp_ref, EV_CMP // LANES),
             ("cols", vt_ref, EV_VT // LANES)]
    _project_chunks(h, w_ref, (c_ref[0], s1_ref[0], s2_ref[0]), dests, EV_ROPE_CHUNKS, EV_QSCALE_CHUNKS)
    ys = _dot(h, ws_ref[...])
    z = ys + fb_ref[...]
    log_f = jnp.minimum(z, 0.0) - jnp.log(1.0 + jnp.exp(-jnp.abs(z)))
    gate = 1.0 / (1.0 + jnp.exp(-ys))
    small_ref[0] = jnp.where(_lane_iota() < FOX_HEADS, log_f, gate)


def _even_proj(x3, g, w, w_small, fb_row, tabs, tm):
    b, t, _ = x3.shape
    const = lambda shape: pl.BlockSpec(shape, lambda bi, i: (0, 0))
    tab = pl.BlockSpec((1, tm, LANES), lambda bi, i: (bi, i, 0))
    return pl.pallas_call(
        _even_proj_kernel,
        grid=(b, t // tm),
        in_specs=[pl.BlockSpec((1, tm, D_MODEL), lambda bi, i: (bi, i, 0)), const((1, D_MODEL)),
                  const((D_MODEL, EV_MAIN + EV_CMP + EV_VT)), const((D_MODEL, LANES)), const((1, LANES)),
                  tab, tab, tab],
        out_specs=[pl.BlockSpec((1, tm, EV_MAIN), lambda bi, i: (bi, i, 0)),
                   pl.BlockSpec((1, EV_CMP // LANES, tm, LANES), lambda bi, i: (bi, 0, i, 0)),
                   pl.BlockSpec((1, EV_VT, tm), lambda bi, i: (bi, 0, i)), tab],
        out_shape=[jax.ShapeDtypeStruct((b, t, EV_MAIN), BF16),
                   jax.ShapeDtypeStruct((b, EV_CMP // LANES, t, LANES), F32),
                   jax.ShapeDtypeStruct((b, EV_VT, t), BF16), jax.ShapeDtypeStruct((b, t, LANES), F32)],
        compiler_params=_cparams(("parallel", "parallel")),
        name="even_proj",
    )(x3, g, w, w_small, fb_row, *tabs)


def _odd_proj_kernel(x_ref, g_ref, w_ref, c_ref, s1_ref, s2_ref, main_ref, vt_ref):
    h = _rms(x_ref[0], g_ref[...]).astype(BF16)
    n_main = OD_MAIN // LANES
    dests = [("rows", main_ref, n_main), ("cols", vt_ref, OD_VT // LANES)]
    _project_chunks(h, w_ref, (c_ref[0], s1_ref[0], s2_ref[0]), dests, tuple(range(n_main)),
                    tuple(range(n_main // 2)))


def _odd_proj(x3, g, w_all, layer, tabs, tm):
    b, t, _ = x3.shape
    const = lambda shape: pl.BlockSpec(shape, lambda bi, i: (0, 0))
    tab = pl.BlockSpec((1, tm, LANES), lambda bi, i: (bi, i, 0))
    return pl.pallas_call(
        _odd_proj_kernel,
        grid=(b, t // tm),
        in_specs=[pl.BlockSpec((1, tm, D_MODEL), lambda bi, i: (bi, i, 0)), const((1, D_MODEL)),
                  _layer_spec(w_all, layer), tab, tab, tab],
        out_specs=[pl.BlockSpec((1, tm, OD_MAIN), lambda bi, i: (bi, i, 0)),
                   pl.BlockSpec((1, OD_VT, tm), lambda bi, i: (bi, 0, i))],
        out_shape=[jax.ShapeDtypeStruct((b, t, OD_MAIN), BF16), jax.ShapeDtypeStruct((b, OD_VT, t), BF16)],
        compiler_params=_cparams(("parallel", "parallel")),
        name="odd_proj",
    )(x3, g, w_all, *tabs)


def _fox_aug_kernel(lf_ref, tril_ref, e_ref, one_ref, aq_ref, ak_ref, carry_ref):
    @pl.when(pl.program_id(1) == 0)
    def _():
        carry_ref[...] = jnp.zeros_like(carry_ref)

    tril = tril_ref[...]
    sub = tril.shape[0]
    carry = carry_ref[...]
    blocks = []
    for r0 in range(0, lf_ref.shape[1], sub):
        c = carry
        for piece in _split3(lf_ref[0, r0:r0 + sub, :]):
            c = c + _dot(tril, piece)
        blocks.append(c)
        carry = c[-1:, :]
    carry_ref[...] = carry
    aug = one_ref[...]
    for r, piece in enumerate(_split3(jnp.concatenate(blocks, axis=0) * LOG2E)):
        aug = aug + _dot(piece, e_ref[r])
    aq_ref[0] = aug[:, :LANES].astype(BF16)
    ak_ref[0] = aug[:, LANES:].astype(BF16)


def _fox_aug(small, tc, sub=128):
    b, t, _ = small.shape
    tril = jnp.asarray(np.tril(np.ones((sub, sub), np.float32)), BF16)
    spread = np.zeros((3, LANES, 2 * LANES), np.float32)
    ones = np.zeros((1, 2 * LANES), np.float32)
    for h in range(FOX_HEADS):
        for r in range(3):
            spread[r, h, AUG_PER_HEAD * h + r] = 1.0
            spread[r, h, LANES + AUG_PER_HEAD * h + 3 + r] = -1.0
            ones[0, AUG_PER_HEAD * h + 3 + r] = 1.0
            ones[0, LANES + AUG_PER_HEAD * h + r] = 1.0
    const2 = lambda shape: pl.BlockSpec(shape, lambda bi, i: (0,) * len(shape))
    blk = pl.BlockSpec((1, tc, LANES), lambda bi, i: (bi, i, 0))
    return pl.pallas_call(
        _fox_aug_kernel,
        grid=(b, t // tc),
        in_specs=[blk, const2((sub, sub)), const2((3, LANES, 2 * LANES)), const2((1, 2 * LANES))],
        out_specs=[blk, blk],
        out_shape=[jax.ShapeDtypeStruct((b, t, LANES), BF16)] * 2,
        scratch_shapes=[pltpu.VMEM((1, LANES), F32)],
        compiler_params=_cparams(("parallel", "arbitrary")),
        name="fox_aug",
    )(small, tril, jnp.asarray(spread, BF16), jnp.asarray(ones))


def _fox_kernel(q_ref, aq_ref, k_ref, ak_ref, vt_ref, *rest, i, tq, tk):
    o_ref, st_ref, acc_ref = rest[-3:]
    pair = pl.program_id(1)
    row = lax.broadcasted_iota(jnp.int32, (LANES, 1), 0)
    qt = _transposed(q_ref[0])
    qat = _transposed(aq_ref[0])
    qcats = []
    for hh in range(2):
        head = 2 * pair + hh
        in_head = (row >= AUG_PER_HEAD * head) & (row < AUG_PER_HEAD * (head + 1))
        qcats.append(jnp.concatenate([_half_mask_t(qt, hh), jnp.where(in_head, qat, jnp.zeros_like(qat))], axis=0))

    def scores(j):
        ks = j * tk
        kcat = jnp.concatenate([k_ref[0, pl.ds(ks, tk), :], ak_ref[0, pl.ds(ks, tk), :]], axis=1)
        return tuple(_dot(kcat, qcats[hh]) for hh in range(2))

    def mask(j, st):
        kpos, qpos = _positions_t(j, tk, i * tq, tq)
        return jnp.where(kpos <= qpos, st, NEG)

    def values(hh, j):
        return vt_ref[0, hh * HEAD_DIM:(hh + 1) * HEAD_DIM, pl.ds(j * tk, tk)]

    res = _stream(0, i * (tq // tk), tq // tk, scores, mask, values, st_ref, acc_ref)
    ot = jnp.concatenate([acc / l for l, acc in res], axis=0)
    o_ref[0] = ot.T.astype(BF16)


def _fox_attention(main, aq, ak, vt, tq, tk, init):
    b, t, _ = main.shape
    in_specs = lambda i: [pl.BlockSpec((1, tq, LANES), lambda bi, p: (bi, i, CH_FQ + p)),
                          pl.BlockSpec((1, tq, LANES), lambda bi, p: (bi, i, 0)),
                          pl.BlockSpec((1, (i + 1) * tq, LANES), lambda bi, p: (bi, 0, CH_FK + p)),
                          pl.BlockSpec((1, (i + 1) * tq, LANES), lambda bi, p: (bi, 0, 0)),
                          pl.BlockSpec((1, LANES, (i + 1) * tq), lambda bi, p: (bi, VT_FV + p, 0))]
    return _per_query_tile(
        partial(_fox_kernel, tq=tq, tk=tk), t // tq, (b, FOX_HEADS // 2), in_specs,
        lambda i: pl.BlockSpec((1, tq, LANES), lambda bi, p: (bi, i, p)),
        jax.ShapeDtypeStruct((b, t, FOX_W), BF16), _stream_scratch(2, HEAD_DIM, tq, tk), "fox_attention",
        (main, aq, main, ak, vt), init)


def _compress_kernel(x_ref, pos_ref, w1_ref, w2_ref, kc_ref, vct_ref):
    t = x_ref.shape[2]
    nchunk = t // CMP_STRIDE
    for kv, o_ref in enumerate((kc_ref, vct_ref)):
        first = second = None
        for l in range(CMP_STRIDE):
            xl = x_ref[0, kv, pl.ds(l, nchunk, stride=CMP_STRIDE), :]
            a = _dot((xl + pos_ref[kv, l:l + 1, :]).astype(BF16), w1_ref[kv, l])
            b = _dot((xl + pos_ref[kv, CMP_STRIDE + l:CMP_STRIDE + l + 1, :]).astype(BF16),
                     w1_ref[kv, CMP_STRIDE + l])
            first = a if first is None else first + a
            second = b if second is None else second + b
        hid = first + pltpu.roll(second, nchunk - 1, 0)
        out = _dot(jax.nn.gelu(hid, approximate=True).astype(BF16), w2_ref[kv])
        o_ref[0] = (out.T if o_ref is vct_ref else out).astype(BF16)


def _compress(cmp_in, pos2, w1bd, w2bd):
    b, _, t, _ = cmp_in.shape
    nchunk = t // CMP_STRIDE
    const = lambda a: pl.BlockSpec(a.shape, lambda bi: (0,) * a.ndim)
    return pl.pallas_call(
        _compress_kernel,
        grid=(b,),
        in_specs=[pl.BlockSpec((1, EV_CMP // LANES, t, LANES), lambda bi: (bi, 0, 0, 0)),
                  const(pos2), const(w1bd), const(w2bd)],
        out_specs=[pl.BlockSpec((1, nchunk, LANES), lambda bi: (bi, 0, 0)),
                   pl.BlockSpec((1, LANES, nchunk), lambda bi: (bi, 0, 0))],
        out_shape=[jax.ShapeDtypeStruct((b, nchunk, LANES), BF16), jax.ShapeDtypeStruct((b, LANES, nchunk), BF16)],
        compiler_params=_cparams(("parallel",)),
        name="nsa_compress",
    )(cmp_in, pos2, w1bd, w2bd)


def _gate_col(small, head, branch):
    idx = FOX_HEADS + 3 * head + branch
    return jnp.sum(jnp.where(_lane_iota() == idx, small, 0.0), axis=-1, keepdims=True)


def _nsa_select_kernel(q_ref, kc_ref, vct_ref, ovt_ref, small_ref, *rest, q_lo, tq, ns, n_sel):
    oc_ref, sel_ref = rest[-2:]
    q0 = q_lo + pl.program_id(1) * tq
    lane = _lane_iota()
    kc = kc_ref[0]
    vct = vct_ref[0]
    ncp = kc.shape[0]
    small = small_ref[0]
    qpos = q0 + lax.broadcasted_iota(jnp.int32, (1, tq), 1)
    cmp_end = lax.broadcasted_iota(jnp.int32, (ncp, 1), 0) * CMP_STRIDE + (CMP_BLOCK - 1)
    cmask = cmp_end <= qpos
    psum = [jnp.zeros((ncp, tq), F32) for _ in range(NSA_KV_GROUPS)]
    logits = [[_dot_nt(kc, _half_mask(q_ref[0, :, n * LANES:(n + 1) * LANES], g)) for g in range(NSA_KV_GROUPS)]
              for n in range(NSA_HPG)]
    for n in range(NSA_HPG):
        ots = []
        for g in range(NSA_KV_GROUPS):
            z = jnp.where(cmask, logits[n][g], -jnp.inf)
            m = jnp.max(z, axis=0, keepdims=True)
            m = jnp.where(m == -jnp.inf, 0.0, m)
            p = jnp.exp2(z - m)
            p = p / jnp.maximum(jnp.sum(p, axis=0, keepdims=True), 1e-30)
            psum[g] = psum[g] + p
            ots.append(_dot(vct[g * HEAD_DIM:(g + 1) * HEAD_DIM], p.astype(BF16)))
        gate = jnp.where(lane < HEAD_DIM, _gate_col(small, n, 0), _gate_col(small, NSA_HPG + n, 0))
        oc_ref[0, :, n * LANES:(n + 1) * LANES] = gate * jnp.concatenate(ots, axis=0).T

    nsp = ovt_ref.shape[0]
    blk = lax.broadcasted_iota(jnp.int32, (nsp, 1), 0)
    cur = qpos // SLC_BLOCK
    valid = blk * SLC_BLOCK <= qpos
    forced = (blk == 0) | (blk == cur) | (blk == cur - 1)
    scores = []
    for g in range(NSA_KV_GROUPS):
        imp = jnp.zeros((nsp, tq), F32)
        for piece in _split3(psum[g]):
            imp = imp + _dot(ovt_ref[...], piece)
        scores.append(jnp.where(valid, jnp.where(forced, jnp.inf, imp), -jnp.inf))
    slab = 8
    masks = []
    for g in range(NSA_KV_GROUPS):
        slabs = [scores[g][r:r + slab] for r in range(0, nsp, slab)]
        ranks = [jnp.zeros((slab, tq), jnp.int32) for _ in slabs]
        for i in range(ns):
            row = scores[g][i:i + 1, :]
            for r, sl in enumerate(slabs):
                if slab * r >= ns:
                    continue
                if slab * r > i:
                    ahead = row >= sl
                elif slab * (r + 1) - 1 <= i:
                    ahead = row > sl
                else:
                    ahead = (row > sl) | ((row == sl) & (blk[slab * r:slab * (r + 1)] > i))
                ranks[r] = jnp.where(ahead, ranks[r] + 1, ranks[r])
        rank = jnp.concatenate(ranks, axis=0)
        masks.append(jnp.where((rank < n_sel) & (blk < ns), 0.0, NEG))
    sel_ref[0] = jnp.concatenate(masks, axis=0).T.astype(BF16)


def _nsa_select(main, kc, vct, overlap_t, small, tq, ns, n_sel, init, n_span=4):
    b, t, _ = main.shape
    nsp = overlap_t.shape[0]
    span = t // n_span
    steps = span // tq
    shapes = [jax.ShapeDtypeStruct((b, t, NSA_W), F32), jax.ShapeDtypeStruct((b, t, NSA_KV_GROUPS * nsp), BF16)]
    outs = [jnp.zeros(s.shape, s.dtype) for s in shapes] if init is None else init
    for k in range(n_span):
        q_hi = (k + 1) * span
        ncl = min(-(-(q_hi // CMP_STRIDE) // LANES) * LANES, kc.shape[1])
        row = lambda bi, i, k=k: (bi, k * steps + i, 0)
        outs = pl.pallas_call(
            partial(_nsa_select_kernel, q_lo=k * span, tq=tq, ns=min(ns, q_hi // SLC_BLOCK), n_sel=n_sel),
            grid=(b, steps),
            in_specs=[pl.BlockSpec((1, tq, NSA_W), lambda bi, i, k=k: (bi, k * steps + i, CH_NQ * LANES // NSA_W)),
                      pl.BlockSpec((1, ncl, LANES), lambda bi, i: (bi, 0, 0)),
                      pl.BlockSpec((1, LANES, ncl), lambda bi, i: (bi, 0, 0)),
                      pl.BlockSpec((nsp, ncl), lambda bi, i: (0, 0)),
                      pl.BlockSpec((1, tq, LANES), row),
                      pl.BlockSpec(memory_space=pl.ANY), pl.BlockSpec(memory_space=pl.ANY)],
            out_specs=[pl.BlockSpec((1, tq, NSA_W), row), pl.BlockSpec((1, tq, NSA_KV_GROUPS * nsp), row)],
            out_shape=shapes,
            input_output_aliases={5: 0, 6: 1},
            compiler_params=_cparams(("parallel", "parallel")),
            name=f"nsa_select_s{k}",
        )(main, kc, vct, overlap_t, small, *outs)
    return outs


def _nsa_flash_kernel(q_ref, sel_ref, oc_ref, small_ref, ks_ref, kw_ref, hot_ref, vst_ref, vwt_ref, *rest,
                      i, tq, tk):
    o_ref, st_ref, acc_ref = rest[-3:]
    n = pl.program_id(1)
    q0 = i * tq
    q2 = q_ref[0]
    small = small_ref[0]
    qt = _transposed(q2)
    selt = _transposed(sel_ref[0])
    qhs = [_half_mask_t(qt, g) for g in range(NSA_KV_GROUPS)]
    qcats = [jnp.concatenate([qhs[g], _half_mask_t(selt, g)], axis=0) for g in range(NSA_KV_GROUPS)]
    rows = lambda g: slice(g * HEAD_DIM, (g + 1) * HEAD_DIM)

    def sel_scores(j):
        ks = j * tk
        kcat = jnp.concatenate([ks_ref[0, pl.ds(ks, tk), :], hot_ref[pl.ds(ks, tk), :]], axis=1)
        return tuple(_dot(kcat, qcats[g]) for g in range(NSA_KV_GROUPS))

    def sel_mask(j, st):
        kpos, qpos = _positions_t(j, tk, q0, tq)
        return jnp.where(kpos <= qpos, st, NEG)

    def win_scores(j):
        kw = kw_ref[0, pl.ds(j * tk, tk), :]
        return tuple(_dot(kw, qhs[g]) for g in range(NSA_KV_GROUPS))

    def win_mask(j, st):
        kpos, qpos = _positions_t(j, tk, q0, tq)
        return jnp.where((kpos <= qpos) & (kpos > qpos - WINDOW), st, NEG)

    values = lambda ref: lambda g, j: ref[0, rows(g), pl.ds(j * tk, tk)]
    lane = _lane_iota()
    n_end = (i + 1) * (tq // tk)
    win_lo = max(n_end - (WINDOW + tq) // tk, 0)
    o = oc_ref[0]
    for branch, args in ((1, (0, i * (tq // tk), tq // tk, sel_scores, sel_mask, values(vst_ref))),
                         (2, (win_lo, 0, n_end - win_lo, win_scores, win_mask, values(vwt_ref)))):
        res = _stream(*args, st_ref, acc_ref)
        ot = jnp.concatenate([acc / l for l, acc in res], axis=0)
        gate = jnp.where(lane < HEAD_DIM, _gate_col(small, n, branch), _gate_col(small, NSA_HPG + n, branch))
        o = o + gate * ot.T
    o_ref[0] = o.astype(BF16)


def _nsa_flash(main, sel, oc, small, hot, vt, tq, tk, init):
    b, t, _ = main.shape
    tile = lambda i, ch: pl.BlockSpec((1, tq, LANES), lambda bi, n: (bi, i, ch + n))
    shared = lambda i: pl.BlockSpec((1, tq, LANES), lambda bi, n: (bi, i, 0))
    full = lambda i, ch: pl.BlockSpec((1, (i + 1) * tq, LANES), lambda bi, n: (bi, 0, ch))
    vfull = lambda i, ch: pl.BlockSpec((1, LANES, (i + 1) * tq), lambda bi, n: (bi, ch, 0))
    in_specs = lambda i: [tile(i, CH_NQ), shared(i), tile(i, 0), shared(i), full(i, CH_KS), full(i, CH_KW),
                          pl.BlockSpec(((i + 1) * tq, LANES), lambda bi, n: (0, 0)),
                          vfull(i, VT_VS), vfull(i, VT_VW)]
    return _per_query_tile(
        partial(_nsa_flash_kernel, tq=tq, tk=tk), t // tq, (b, NSA_HPG), in_specs, lambda i: tile(i, 0),
        jax.ShapeDtypeStruct((b, t, NSA_W), BF16), _stream_scratch(NSA_KV_GROUPS, HEAD_DIM, tq, tk), "nsa_flash",
        (main, sel, oc, small, main, main, hot, vt, vt), init)


def _diff_kernel(q_ref, k_ref, vt_ref, lam_ref, g_ref, *rest, i, tq, tk, lam_init):
    o_ref, st_ref, acc_ref = rest[-3:]
    q2 = q_ref[0]
    lp = lam_ref[...]
    lam = (jnp.exp(jnp.sum(lp[0:1] * lp[1:2], axis=-1, keepdims=True))
           - jnp.exp(jnp.sum(lp[2:3] * lp[3:4], axis=-1, keepdims=True)) + lam_init)
    qhs = [_half_mask(q2, comp) for comp in range(2)]

    def scores(j):
        k2 = k_ref[0, pl.ds(j * tk, tk), :]
        return tuple(_dot_nt(k2, qhs[comp]) for comp in range(2))

    def mask(j, st):
        kpos, qpos = _positions_t(j, tk, i * tq, tq)
        return jnp.where(kpos <= qpos, st, NEG)

    def values(comp, j):
        return vt_ref[0, :, pl.ds(j * tk, tk)]

    (l1, acc1), (l2, acc2) = _stream(0, i * (tq // tk), tq // tk, scores, mask, values, st_ref, acc_ref)
    o = (acc1 / l1 - lam * (acc2 / l2)).T
    o_ref[0] = (_rms(o, g_ref[...]) * (1.0 - lam_init)).astype(BF16)


def _diff_attention(main, vt, lam_p, subln_g, tq, tk, lam_init, init):
    b, t, _ = main.shape
    nh = DIFF_HEADS
    in_specs = lambda i: [pl.BlockSpec((1, tq, LANES), lambda bi, h: (bi, i, h)),
                          pl.BlockSpec((1, (i + 1) * tq, LANES), lambda bi, h: (bi, 0, nh + h)),
                          pl.BlockSpec((1, LANES, (i + 1) * tq), lambda bi, h: (bi, h, 0)),
                          pl.BlockSpec((4, LANES), lambda bi, h: (0, 0)),
                          pl.BlockSpec((1, LANES), lambda bi, h: (0, 0))]
    return _per_query_tile(
        partial(_diff_kernel, tq=tq, tk=tk, lam_init=lam_init), t // tq, (b, nh), in_specs,
        lambda i: pl.BlockSpec((1, tq, LANES), lambda bi, h: (bi, i, h)),
        jax.ShapeDtypeStruct((b, t, nh * LANES), BF16), _stream_scratch(2, LANES, tq, tk), "diff_attention",
        (main, main, vt, lam_p, subln_g), init)


def _mem_kv_kernel(mem_ref, g_ref, wk_ref, wv_ref, k_ref, vt_ref):
    mn = _rms(mem_ref[0], g_ref[0]).astype(BF16)
    k_ref[0, 0] = _dot(mn, wk_ref[0]).astype(BF16)
    vt_ref[0, 0] = _dot(mn, wv_ref[0]).T.astype(BF16)


def _mem_kv(mem, mem_norm_g, wk, wv):
    depth = wk.shape[0]
    b, m, d = mem.shape
    wspec = pl.BlockSpec((1, d, MEM_W), lambda l, bi: (l, 0, 0))
    return pl.pallas_call(
        _mem_kv_kernel,
        grid=(depth, b),
        in_specs=[pl.BlockSpec((1, m, d), lambda l, bi: (bi, 0, 0)),
                  pl.BlockSpec((1, 1, d), lambda l, bi: (l, 0, 0)), wspec, wspec],
        out_specs=[pl.BlockSpec((1, 1, m, MEM_W), lambda l, bi: (l, bi, 0, 0)),
                   pl.BlockSpec((1, 1, MEM_W, m), lambda l, bi: (l, bi, 0, 0))],
        out_shape=[jax.ShapeDtypeStruct((depth, b, m, MEM_W), BF16),
                   jax.ShapeDtypeStruct((depth, b, MEM_W, m), BF16)],
        compiler_params=_cparams(("parallel", "parallel")),
        name="mem_kv",
    )(mem, mem_norm_g.reshape(depth, 1, d), wk, wv)


def _post_mixer_kernel(*refs, n_in, n_part):
    a_refs, w_refs = refs[:n_in], refs[n_in:2 * n_in]
    x_ref, g1_ref, g2_ref, g3_ref, wq_ref, k_ref, vt_ref, wo_ref, o_ref = refs[2 * n_in:]
    rows = x_ref.shape[0] // n_part
    parts = [slice(r * rows, (r + 1) * rows) for r in range(n_part)]
    n_mem = k_ref.shape[2]
    heads = [(ch, hh) for ch in range(MEM_W // LANES) for hh in range(2)]
    ones = jnp.ones((ONES_ROWS, n_mem), BF16)

    def mixer_out(ps):
        y = None
        for a_ref, w_ref in zip(a_refs, w_refs):
            t = _dot(a_ref[ps, :], w_ref[...])
            y = t if y is None else y + t
        return y

    def scores(q):
        return [_dot_nt(k_ref[0, 0, :, ch * LANES:(ch + 1) * LANES], _half_mask(q[:, ch * LANES:(ch + 1) * LANES], hh))
                for ch, hh in heads]

    def attend(logits):
        outs = []
        for (ch, hh), st in zip(heads, logits):
            p = jnp.exp2(st - jnp.max(st, axis=0, keepdims=True)).astype(BF16)
            r0 = ch * LANES + hh * HEAD_DIM
            pv = _dot(jnp.concatenate([vt_ref[0, 0, r0:r0 + HEAD_DIM, :], ones], axis=0), p)
            outs.append(pv[:HEAD_DIM] / pv[HEAD_DIM:HEAD_DIM + 1])
        return jnp.concatenate(outs, axis=0).T.astype(BF16)

    ys = [mixer_out(ps) for ps in parts]
    xs = [x_ref[ps, :] + _rms(y, g1_ref[...]) for ps, y in zip(parts, ys)]
    qs = [(_dot(_rms(x, g2_ref[...]).astype(BF16), wq_ref[...]) * Q_SCALE_LOG2).astype(BF16) for x in xs]
    logits = [scores(q) for q in qs]
    os_ = [attend(lg) for lg in logits]
    for ps, x, o in zip(parts, xs, os_):
        o_ref[ps, :] = x + _rms(_dot(o, wo_ref[...]), g3_ref[...])


def _post_mixer(acts, weights, x2, g1, g2, g3, wq_all, mem_k, mem_vt, wo_all, layer, tm, rows_per_batch, n_part=2):
    n = x2.shape[0]
    m = mem_k.shape[2]
    const = lambda shape: pl.BlockSpec(shape, lambda i: (0, 0))
    row = pl.BlockSpec((tm, D_MODEL), lambda i: (i, 0))
    gspec = const((1, D_MODEL))
    return pl.pallas_call(
        partial(_post_mixer_kernel, n_in=len(acts), n_part=n_part),
        grid=(n // tm,),
        in_specs=[pl.BlockSpec((tm, a.shape[1]), lambda i: (i, 0)) for a in acts]
                 + [const(w.shape) if l is None else _layer_spec(w, l) for w, l in weights]
                 + [row, gspec, gspec, gspec, _layer_spec(wq_all, layer),
                    pl.BlockSpec((1, 1, m, MEM_W), lambda i: (layer, i // rows_per_batch, 0, 0)),
                    pl.BlockSpec((1, 1, MEM_W, m), lambda i: (layer, i // rows_per_batch, 0, 0)),
                    _layer_spec(wo_all, layer)],
        out_specs=row,
        out_shape=jax.ShapeDtypeStruct((n, D_MODEL), F32),
        compiler_params=_cparams(("parallel",)),
        name="post_mixer",
    )(*acts, *[w for w, _ in weights], x2, g1, g2, g3, wq_all, mem_k, mem_vt, wo_all)


def _ffn_kernel(x_ref, gin_ref, gout_ref, wg_ref, wu_ref, wd_ref, o_ref, *, chunk):
    x = x_ref[...]
    h = _rms(x, gin_ref[...]).astype(BF16)
    d_ff = wg_ref.shape[1]
    y = None
    for c0 in range(0, d_ff, chunk):
        gate = _dot(h, wg_ref[:, c0:c0 + chunk])
        up = _dot(h, wu_ref[:, c0:c0 + chunk])
        a = (gate * (1.0 / (1.0 + jnp.exp(-gate))) * up).astype(BF16)
        t = _dot(a, wd_ref[c0:c0 + chunk, :])
        y = t if y is None else y + t
    o_ref[...] = x + _rms(y, gout_ref[...])


def _ffn(x2, gin, gout, wg_all, wu_all, wd_all, layer, tm, chunk):
    n = x2.shape[0]
    const = lambda shape: pl.BlockSpec(shape, lambda i: (0, 0), pipeline_mode=pl.Buffered(1))
    wspec = lambda w: _layer_spec(w, layer, pipeline_mode=pl.Buffered(1))
    return pl.pallas_call(
        partial(_ffn_kernel, chunk=chunk),
        grid=(n // tm,),
        in_specs=[pl.BlockSpec((tm, D_MODEL), lambda i: (i, 0)), const((1, D_MODEL)), const((1, D_MODEL)),
                  wspec(wg_all), wspec(wu_all), wspec(wd_all)],
        out_specs=pl.BlockSpec((tm, D_MODEL), lambda i: (i, 0)),
        out_shape=jax.ShapeDtypeStruct((n, D_MODEL), F32),
        compiler_params=_cparams(("parallel",)),
        name="ffn",
    )(x2, gin, gout, wg_all, wu_all, wd_all)


def _even_weights(w_in, w_out):
    kvw = NSA_KV_GROUPS * HEAD_DIM
    offs = [int(o) for o in np.cumsum((FOX_W, FOX_W, FOX_W, FOX_HEADS, NSA_W) + (kvw,) * 6 + (3 * NSA_HEADS,))]
    fk0, fv0, fl0, nq0, kc0, vc0, ks0, vs0, kw0, vw0, gl0, end = offs
    paired = [HEAD_DIM * (g * NSA_HPG + n) for n in range(NSA_HPG) for g in range(NSA_KV_GROUPS)]
    cols = lambda a, b: w_in[:, a:b]
    w = jnp.concatenate([cols(0, fv0)] + [cols(nq0 + h, nq0 + h + HEAD_DIM) for h in paired]
                        + [cols(ks0, vs0), cols(kw0, vw0), cols(kc0, vc0), cols(vc0, ks0),
                           cols(fv0, fl0), cols(vs0, kw0), cols(vw0, gl0)], axis=1).astype(BF16)
    w_small = jnp.concatenate([cols(fl0, nq0), cols(gl0, end),
                               jnp.zeros((D_MODEL, LANES - FOX_HEADS - 3 * NSA_HEADS), w_in.dtype)],
                              axis=1).astype(BF16)
    w_out_fox = w_out[:FOX_W].astype(BF16)
    w_out_nsa = jnp.concatenate([w_out[FOX_W + h:FOX_W + h + HEAD_DIM] for h in paired], axis=0).astype(BF16)
    return w, w_small, w_out_fox, w_out_nsa


def _overlap_matrix_t(t, ncp):
    nc = (t - CMP_BLOCK) // CMP_STRIDE + 1
    ns = t // SLC_BLOCK
    cs = np.arange(nc) * CMP_STRIDE
    ss = np.arange(ns) * SLC_BLOCK
    ov = np.clip(np.minimum(cs[:, None] + CMP_BLOCK, ss[None, :] + SLC_BLOCK)
                 - np.maximum(cs[:, None], ss[None, :]), 0, None) / CMP_BLOCK
    full = np.zeros((LANES // NSA_KV_GROUPS, ncp), np.float32)
    full[:ns, :nc] = ov.T
    return jnp.asarray(full, BF16), ns


def _compress_weights(pos_k, w1_k, w2_k, pos_v, w1_v, w2_v):
    g = NSA_KV_GROUPS
    pos2, w1bd, w2bd = [], [], []
    for pos, w1, w2 in ((pos_k, w1_k, w2_k), (pos_v, w1_v, w2_v)):
        pos2.append(jnp.tile(pos.astype(F32), (1, g)))
        w1l = w1.reshape(CMP_BLOCK, HEAD_DIM, CMP_HIDDEN).astype(BF16)
        bd = jnp.zeros((CMP_BLOCK, g * HEAD_DIM, g * CMP_HIDDEN), BF16)
        w2g = jnp.zeros((g * CMP_HIDDEN, g * HEAD_DIM), BF16)
        for gi in range(g):
            bd = bd.at[:, gi * HEAD_DIM:(gi + 1) * HEAD_DIM, gi * CMP_HIDDEN:(gi + 1) * CMP_HIDDEN].set(w1l)
            w2g = w2g.at[gi * CMP_HIDDEN:(gi + 1) * CMP_HIDDEN, gi * HEAD_DIM:(gi + 1) * HEAD_DIM].set(
                w2.astype(BF16))
        w1bd.append(bd)
        w2bd.append(w2g)
    return jnp.stack(pos2), jnp.stack(w1bd), jnp.stack(w2bd)


def kernel(x, mem, positions, sandwich_g, mem_norm_g, ev_w_in, ev_fox_fbias, ev_cmp_pos_k, ev_cmp_w1_k, ev_cmp_w2_k, ev_cmp_pos_v, ev_cmp_w1_v, ev_cmp_w2_v, ev_w_out, od_w_in, od_lambda, od_subln_g, od_w_out, ca_wq, ca_wk, ca_wv, ca_wo, ffn_wg, ffn_wu, ffn_wd):
    b, t, d = x.shape
    depth = sandwich_g.shape[0]
    n = b * t
    tm = 512
    tm_ffn = 512
    tm_mix = 1024
    tq, tk = 512, 256
    tk_diff = 512
    assert d == D_MODEL and t % tq == 0 and tq % tk == 0 and WINDOW % tk == 0 and t % tm == 0 and n % tm_ffn == 0

    tabs = tuple(a.reshape(b, t, LANES) for a in _rope_tables(positions, 512))
    mem_k, mem_vt = _mem_kv(mem, mem_norm_g, ca_wk.astype(BF16), ca_wv.astype(BF16))
    ncp = t // CMP_STRIDE
    overlap_t, ns = _overlap_matrix_t(t, ncp)
    assert ns <= overlap_t.shape[0]
    n_sel = min(SLC_TOPK, ns)
    hot = jnp.asarray((np.arange(t)[:, None] // SLC_BLOCK == np.arange(LANES)[None, :] % overlap_t.shape[0])
                      .astype(np.float32), BF16)
    gain = lambda l, j: sandwich_g[l, j].reshape(1, d)

    bf = lambda w: w.astype(BF16)
    od_w_in_b, od_w_out_b, ca_wq_b, ca_wo_b = bf(od_w_in), bf(od_w_out), bf(ca_wq), bf(ca_wo)
    ffn_wg_b, ffn_wu_b, ffn_wd_b = bf(ffn_wg), bf(ffn_wu), bf(ffn_wd)
    x2 = x.reshape(n, d)
    dead = {}
    for layer in range(depth):
        x3 = x2.reshape(b, t, d)
        if layer % 2 == 0:
            e = layer // 2
            w, w_small, w_out_fox, w_out_nsa = _even_weights(ev_w_in[e], ev_w_out[e])
            fb_row = jnp.zeros((1, LANES), F32).at[0, :FOX_HEADS].set(ev_fox_fbias[e].astype(F32))
            main, cmp_in, vt, small = _even_proj(x3, gain(layer, 0), w, w_small, fb_row, tabs, tm)
            aq, ak = _fox_aug(small, 512)
            o_fox = dead["fox"] = _fox_attention(main, aq, ak, vt, tq, tk, dead.get("fox"))
            kc, vct = _compress(cmp_in, *_compress_weights(
                ev_cmp_pos_k[e], ev_cmp_w1_k[e], ev_cmp_w2_k[e], ev_cmp_pos_v[e], ev_cmp_w1_v[e], ev_cmp_w2_v[e]))
            oc, sel = dead["sel"] = _nsa_select(main, kc, vct, overlap_t, small, 256, ns, n_sel, dead.get("sel"))
            o_nsa = dead["nsa"] = _nsa_flash(main, sel, oc, small, hot, vt, tq, tk, dead.get("nsa"))
            acts, w_outs = [o_fox.reshape(n, FOX_W), o_nsa.reshape(n, NSA_W)], [(w_out_fox, None), (w_out_nsa, None)]
        else:
            o = layer // 2
            main, vt = _odd_proj(x3, gain(layer, 0), od_w_in_b, o, tabs, tm)
            lam_init = 0.8 - 0.6 * math.exp(-0.3 * layer)
            lam_p = jnp.pad(od_lambda[o].astype(F32), ((0, 0), (0, LANES - HEAD_DIM)))
            attn = dead["diff"] = _diff_attention(main, vt, lam_p, od_subln_g[o].reshape(1, LANES).astype(F32),
                                                  tq, tk_diff, lam_init, dead.get("diff"))
            acts, w_outs = [attn.reshape(n, D_MODEL)], [(od_w_out_b, o)]
        x2 = _post_mixer(acts, w_outs, x2, gain(layer, 1), gain(layer, 2), gain(layer, 3),
                         ca_wq_b, mem_k, mem_vt, ca_wo_b, layer, tm_mix, t // tm_mix, n_part=tm_mix // 256)
        x2 = _ffn(x2, gain(layer, 4), gain(layer, 5), ffn_wg_b, ffn_wu_b, ffn_wd_b, layer, tm_ffn, 256)
    return x2.reshape(b, t, d)
```

```python
import math
from functools import partial

import numpy as np
import jax
import jax.numpy as jnp
from jax import lax
from jax.experimental import pallas as pl
from jax.experimental.pallas import tpu as pltpu

F32 = jnp.float32
BF16 = jnp.bfloat16

D_MODEL = 1024
HEAD_DIM = 64
LANES = 128
ROPE_DIM = HEAD_DIM // 4
ROPE_THETA = 500000.0
FOX_HEADS = 8
NSA_HEADS = 8
NSA_KV_GROUPS = 2
NSA_HPG = NSA_HEADS // NSA_KV_GROUPS
CMP_BLOCK = 32
CMP_STRIDE = 16
CMP_HIDDEN = 2 * HEAD_DIM
SLC_BLOCK = 64
SLC_TOPK = 16
WINDOW = 512
DIFF_HEADS = 8
MEM_HEADS = 4
MEM_W = MEM_HEADS * HEAD_DIM
RMS_EPS = 1e-6
Q_SCALE = HEAD_DIM ** -0.5
LOG2E = math.log2(math.e)
Q_SCALE_LOG2 = Q_SCALE * LOG2E
NEG = -1e30
AUG_PER_HEAD = 6
ONES_ROWS = 16

FOX_W = FOX_HEADS * HEAD_DIM
NSA_W = NSA_HEADS * HEAD_DIM
CH_FQ, CH_FK, CH_NQ, CH_KS, CH_KW = 0, 4, 8, 12, 13
EV_MAIN = 14 * LANES
EV_CMP = 2 * LANES
VT_FV, VT_VS, VT_VW = 0, 4, 5
EV_VT = 6 * LANES
EV_ROPE_CHUNKS = tuple(range(CH_NQ, CH_NQ + 4)) + (CH_KS, CH_KW, EV_MAIN // LANES)
EV_QSCALE_CHUNKS = tuple(range(CH_FQ, CH_FQ + 4)) + tuple(range(CH_NQ, CH_NQ + 4))
OD_MAIN = 2 * D_MODEL
OD_VT = D_MODEL

VMEM_LIMIT = 56 * 1024 * 1024


def _layer_spec(stacked, layer, **kw):
    zeros = (0,) * (stacked.ndim - 1)
    return pl.BlockSpec((None,) + stacked.shape[1:], lambda *idx: (layer,) + zeros, **kw)


def _cparams(sem):
    return pltpu.CompilerParams(dimension_semantics=sem, vmem_limit_bytes=VMEM_LIMIT)


def _rms(x, g):
    return x * lax.rsqrt(jnp.mean(x * x, axis=-1, keepdims=True) + RMS_EPS) * g


def _split3(x):
    hi = x.astype(BF16)
    r1 = x - hi.astype(F32)
    mid = r1.astype(BF16)
    lo = (r1 - mid.astype(F32)).astype(BF16)
    return hi, mid, lo


def _dot(a, b):
    return jnp.dot(a, b, preferred_element_type=F32)


def _dot_nt(a, b):
    return lax.dot_general(a, b, (((1,), (1,)), ((), ())), preferred_element_type=F32)


def _lane_iota(n=LANES):
    return lax.broadcasted_iota(jnp.int32, (1, n), 1)


def _half_mask(q2, half):
    return jnp.where(_lane_iota() // HEAD_DIM == half, q2, jnp.zeros_like(q2))


def _transposed(x):
    return x.astype(F32).T.astype(BF16)


def _half_mask_t(qt, half):
    row = lax.broadcasted_iota(jnp.int32, (qt.shape[0], 1), 0)
    return jnp.where(row // HEAD_DIM == half, qt, jnp.zeros_like(qt))


def _positions_t(j, tk, q0, tq):
    kpos = j * tk + lax.broadcasted_iota(jnp.int32, (tk, 1), 0)
    qpos = q0 + lax.broadcasted_iota(jnp.int32, (1, tq), 1)
    return kpos, qpos


def _stream(base, n_full, n_tail, scores, mask, values, st_ref, acc_ref):
    n_chain, dv, tq = acc_ref.shape

    def park(j, slot):
        for c, st in enumerate(scores(j)):
            st_ref[slot, c] = st

    def step(j, slot, stats, masked, prefetch):
        if prefetch:
            park(j + 1, 1 - slot)
        parts = []
        for c in range(n_chain):
            st = st_ref[slot, c]
            if masked:
                st = mask(j, st)
            m, _ = stats[c]
            m_new = jnp.maximum(m, jnp.max(st, axis=0, keepdims=True))
            p = jnp.exp2(st - m_new).astype(BF16)
            vt1 = jnp.concatenate([values(c, j), jnp.ones((ONES_ROWS, p.shape[0]), BF16)], axis=0)
            parts.append((m_new, jnp.exp2(m - m_new), _dot(vt1, p)))
        out = []
        for c, (m_new, alpha, pv) in enumerate(parts):
            acc_ref[c] = alpha * acc_ref[c] + pv[:dv]
            out.append((m_new, alpha * stats[c][1] + pv[dv:dv + 1]))
        return tuple(out)

    acc_ref[...] = jnp.zeros_like(acc_ref)
    park(base, 0)
    stats = ((jnp.full((1, tq), NEG, F32), jnp.zeros((1, tq), F32)),) * n_chain
    n_tiles = n_full + n_tail
    for s_ in range(n_tiles):
        stats = step(base + s_, s_ % 2, stats, s_ >= n_full, s_ + 1 < n_tiles)
    return [(l, acc_ref[c]) for c, (_, l) in enumerate(stats)]


TILES_PER_CALL = 2


def _per_query_tile(kernel_fn, n_q, grid, in_specs_fn, out_spec_fn, out_shape, scratch, name, args, init):
    out = jnp.zeros(out_shape.shape, out_shape.dtype) if init is None else init
    for i in range(n_q):
        in_specs = list(in_specs_fn(i)) + [pl.BlockSpec(memory_space=pl.ANY)]
        out = pl.pallas_call(
            partial(kernel_fn, i=i), grid=grid, in_specs=in_specs, out_specs=out_spec_fn(i), out_shape=out_shape,
            scratch_shapes=scratch, input_output_aliases={len(args): 0},
            compiler_params=_cparams(("parallel",) * len(grid)), name=f"{name}_q{i}",
        )(*args, out)
    return out


def _stream_scratch(n_chain, dv, tq, tk):
    return [pltpu.VMEM((2, n_chain, tk, tq), F32), pltpu.VMEM((n_chain, dv, tq), F32)]


def _rope_kernel(pos_ref, inv_ref, m1_ref, m2_ref, c_ref, s1_ref, s2_ref):
    ang = pos_ref[...].astype(F32) * inv_ref[...]
    c_ref[...] = jnp.cos(ang)
    sn = jnp.sin(ang)
    s1_ref[...] = -sn * m1_ref[...]
    s2_ref[...] = sn * m2_ref[...]


def _rope_tables(positions, tm):
    n = positions.size
    inv = ROPE_THETA ** (-jnp.arange(0, ROPE_DIM, 2, dtype=F32) / ROPE_DIM)
    lane = np.arange(LANES) % HEAD_DIM
    half = ROPE_DIM // 2
    inv_l = jnp.where(lane < ROPE_DIM, inv[lane % half], 0.0).reshape(1, LANES).astype(F32)
    m1 = jnp.asarray((lane < half).astype(np.float32)).reshape(1, LANES)
    m2 = jnp.asarray(((lane >= half) & (lane < ROPE_DIM)).astype(np.float32)).reshape(1, LANES)
    row = pl.BlockSpec((1, LANES), lambda i: (0, 0))
    tab = pl.BlockSpec((tm, LANES), lambda i: (i, 0))
    return pl.pallas_call(
        _rope_kernel,
        grid=(n // tm,),
        in_specs=[pl.BlockSpec((tm, 1), lambda i: (i, 0)), row, row, row],
        out_specs=[tab, tab, tab],
        out_shape=[jax.ShapeDtypeStruct((n, LANES), F32)] * 3,
        compiler_params=_cparams(("parallel",)),
        name="rope_tables",
    )(positions.reshape(n, 1), inv_l, m1, m2)


def _apply_rope(y, c, s1, s2):
    half = ROPE_DIM // 2
    return y * c + pltpu.roll(y, LANES - half, 1) * s1 + pltpu.roll(y, half, 1) * s2


def _project_chunks(h, w_ref, tabs, dests, rope_chunks, qscale_chunks):
    c, s1, s2 = tabs
    wide = 2 * LANES
    where = [(kind, ref, k) for kind, ref, n in dests for k in range(n)]
    for ch2 in range(len(where) // 2):
        y2 = _dot(h, w_ref[:, ch2 * wide:(ch2 + 1) * wide])
        for ch in (2 * ch2, 2 * ch2 + 1):
            y = y2[:, (ch % 2) * LANES:(ch % 2 + 1) * LANES]
            if ch in rope_chunks:
                y = _apply_rope(y, c, s1, s2)
            if ch in qscale_chunks:
                y = y * Q_SCALE_LOG2
            kind, ref, k = where[ch]
            if kind == "cols":
                ref[0, k * LANES:(k + 1) * LANES, :] = y.T.astype(BF16)
            elif kind == "rows_f32":
                ref[0, k] = y
            else:
                ref[0, :, k * LANES:(k + 1) * LANES] = y.astype(BF16)


def _even_proj_kernel(x_ref, g_ref, w_ref, ws_ref, fb_ref, c_ref, s1_ref, s2_ref,
                      main_ref, cmp_ref, vt_ref, small_ref):
    h = _rms(x_ref[0], g_ref[...]).astype(BF16)
    dests = [("rows", main_ref, EV_MAIN // LANES), ("rows_f32", cmp_ref, EV_CMP // LANES),
             ("cols", vt_ref, EV_VT // LANES)]
    _project_chunks(h, w_ref, (c_ref[0], s1_ref[0], s2_ref[0]), dests, EV_ROPE_CHUNKS, EV_QSCALE_CHUNKS)
    ys = _dot(h, ws_ref[...])
    z = ys + fb_ref[...]
    log_f = jnp.minimum(z, 0.0) - jnp.log(1.0 + jnp.exp(-jnp.abs(z)))
    gate = 1.0 / (1.0 + jnp.exp(-ys))
    small_ref[0] = jnp.where(_lane_iota() < FOX_HEADS, log_f, gate)


def _even_proj(x3, g, w, w_small, fb_row, tabs, tm):
    b, t, _ = x3.shape
    const = lambda shape: pl.BlockSpec(shape, lambda bi, i: (0, 0))
    tab = pl.BlockSpec((1, tm, LANES), lambda bi, i: (bi, i, 0))
    return pl.pallas_call(
        _even_proj_kernel,
        grid=(b, t // tm),
        in_specs=[pl.BlockSpec((1, tm, D_MODEL), lambda bi, i: (bi, i, 0)), const((1, D_MODEL)),
                  const((D_MODEL, EV_MAIN + EV_CMP + EV_VT)), const((D_MODEL, LANES)), const((1, LANES)),
                  tab, tab, tab],
        out_specs=[pl.BlockSpec((1, tm, EV_MAIN), lambda bi, i: (bi, i, 0)),
                   pl.BlockSpec((1, EV_CMP // LANES, tm, LANES), lambda bi, i: (bi, 0, i, 0)),
                   pl.BlockSpec((1, EV_VT, tm), lambda bi, i: (bi, 0, i)), tab],
        out_shape=[jax.ShapeDtypeStruct((b, t, EV_MAIN), BF16),
                   jax.ShapeDtypeStruct((b, EV_CMP // LANES, t, LANES), F32),
                   jax.ShapeDtypeStruct((b, EV_VT, t), BF16), jax.ShapeDtypeStruct((b, t, LANES), F32)],
        compiler_params=_cparams(("parallel", "parallel")),
        name="even_proj",
    )(x3, g, w, w_small, fb_row, *tabs)


def _odd_proj_kernel(x_ref, g_ref, w_ref, c_ref, s1_ref, s2_ref, main_ref, vt_ref):
    h = _rms(x_ref[0], g_ref[...]).astype(BF16)
    n_main = OD_MAIN // LANES
    dests = [("rows", main_ref, n_main), ("cols", vt_ref, OD_VT // LANES)]
    _project_chunks(h, w_ref, (c_ref[0], s1_ref[0], s2_ref[0]), dests, tuple(range(n_main)),
                    tuple(range(n_main // 2)))


def _odd_proj(x3, g, w_all, layer, tabs, tm):
    b, t, _ = x3.shape
    const = lambda shape: pl.BlockSpec(shape, lambda bi, i: (0, 0))
    tab = pl.BlockSpec((1, tm, LANES), lambda bi, i: (bi, i, 0))
    return pl.pallas_call(
        _odd_proj_kernel,
        grid=(b, t // tm),
        in_specs=[pl.BlockSpec((1, tm, D_MODEL), lambda bi, i: (bi, i, 0)), const((1, D_MODEL)),
                  _layer_spec(w_all, layer), tab, tab, tab],
        out_specs=[pl.BlockSpec((1, tm, OD_MAIN), lambda bi, i: (bi, i, 0)),
                   pl.BlockSpec((1, OD_VT, tm), lambda bi, i: (bi, 0, i))],
        out_shape=[jax.ShapeDtypeStruct((b, t, OD_MAIN), BF16), jax.ShapeDtypeStruct((b, OD_VT, t), BF16)],
        compiler_params=_cparams(("parallel", "parallel")),
        name="odd_proj",
    )(x3, g, w_all, *tabs)


def _fox_aug_kernel(lf_ref, tril_ref, e_ref, one_ref, aq_ref, ak_ref, carry_ref):
    @pl.when(pl.program_id(1) == 0)
    def _():
        carry_ref[...] = jnp.zeros_like(carry_ref)

    tril = tril_ref[...]
    sub = tril.shape[0]
    carry = carry_ref[...]
    blocks = []
    for r0 in range(0, lf_ref.shape[1], sub):
        c = carry
        for piece in _split3(lf_ref[0, r0:r0 + sub, :]):
            c = c + _dot(tril, piece)
        blocks.append(c)
        carry = c[-1:, :]
    carry_ref[...] = carry
    aug = one_ref[...]
    for r, piece in enumerate(_split3(jnp.concatenate(blocks, axis=0) * LOG2E)):
        aug = aug + _dot(piece, e_ref[r])
    aq_ref[0] = aug[:, :LANES].astype(BF16)
    ak_ref[0] = aug[:, LANES:].astype(BF16)


def _fox_aug(small, tc, sub=128):
    b, t, _ = small.shape
    tril = jnp.asarray(np.tril(np.ones((sub, sub), np.float32)), BF16)
    spread = np.zeros((3, LANES, 2 * LANES), np.float32)
    ones = np.zeros((1, 2 * LANES), np.float32)
    for h in range(FOX_HEADS):
        for r in range(3):
            spread[r, h, AUG_PER_HEAD * h + r] = 1.0
            spread[r, h, LANES + AUG_PER_HEAD * h + 3 + r] = -1.0
            ones[0, AUG_PER_HEAD * h + 3 + r] = 1.0
            ones[0, LANES + AUG_PER_HEAD * h + r] = 1.0
    const2 = lambda shape: pl.BlockSpec(shape, lambda bi, i: (0,) * len(shape))
    blk = pl.BlockSpec((1, tc, LANES), lambda bi, i: (bi, i, 0))
    return pl.pallas_call(
        _fox_aug_kernel,
        grid=(b, t // tc),
        in_specs=[blk, const2((sub, sub)), const2((3, LANES, 2 * LANES)), const2((1, 2 * LANES))],
        out_specs=[blk, blk],
        out_shape=[jax.ShapeDtypeStruct((b, t, LANES), BF16)] * 2,
        scratch_shapes=[pltpu.VMEM((1, LANES), F32)],
        compiler_params=_cparams(("parallel", "arbitrary")),
        name="fox_aug",
    )(small, tril, jnp.asarray(spread, BF16), jnp.asarray(ones))


def _fox_kernel(q_ref, aq_ref, k_ref, ak_ref, vt_ref, *rest, i, n_sub, tq, tk):
    o_ref, st_ref, acc_ref = rest[-3:]
    pair = pl.program_id(1)
    row = lax.broadcasted_iota(jnp.int32, (LANES, 1), 0)
    for s in range(n_sub):
        qi, rows = i * n_sub + s, slice(s * tq, (s + 1) * tq)
        qt = _transposed(q_ref[0, rows, :])
        qat = _transposed(aq_ref[0, rows, :])
        qcats = []
        for hh in range(2):
            head = 2 * pair + hh
            in_head = (row >= AUG_PER_HEAD * head) & (row < AUG_PER_HEAD * (head + 1))
            qcats.append(jnp.concatenate([_half_mask_t(qt, hh), jnp.where(in_head, qat, jnp.zeros_like(qat))],
                                         axis=0))

        def scores(j, qcats=qcats):
            ks = j * tk
            kcat = jnp.concatenate([k_ref[0, pl.ds(ks, tk), :], ak_ref[0, pl.ds(ks, tk), :]], axis=1)
            return tuple(_dot(kcat, qcats[hh]) for hh in range(2))

        def mask(j, st, qi=qi):
            kpos, qpos = _positions_t(j, tk, qi * tq, tq)
            return jnp.where(kpos <= qpos, st, NEG)

        def values(hh, j):
            return vt_ref[0, hh * HEAD_DIM:(hh + 1) * HEAD_DIM, pl.ds(j * tk, tk)]

        res = _stream(0, qi * (tq // tk), tq // tk, scores, mask, values, st_ref, acc_ref)
        ot = jnp.concatenate([acc / l for l, acc in res], axis=0)
        o_ref[0, rows, :] = ot.T.astype(BF16)


def _fox_attention(main, aq, ak, vt, tq, tk, init):
    b, t, _ = main.shape
    tc = TILES_PER_CALL * tq
    in_specs = lambda i: [pl.BlockSpec((1, tc, LANES), lambda bi, p: (bi, i, CH_FQ + p)),
                          pl.BlockSpec((1, tc, LANES), lambda bi, p: (bi, i, 0)),
                          pl.BlockSpec((1, (i + 1) * tc, LANES), lambda bi, p: (bi, 0, CH_FK + p)),
                          pl.BlockSpec((1, (i + 1) * tc, LANES), lambda bi, p: (bi, 0, 0)),
                          pl.BlockSpec((1, LANES, (i + 1) * tc), lambda bi, p: (bi, VT_FV + p, 0))]
    return _per_query_tile(
        partial(_fox_kernel, n_sub=TILES_PER_CALL, tq=tq, tk=tk), t // tc, (b, FOX_HEADS // 2), in_specs,
        lambda i: pl.BlockSpec((1, tc, LANES), lambda bi, p: (bi, i, p)),
        jax.ShapeDtypeStruct((b, t, FOX_W), BF16), _stream_scratch(2, HEAD_DIM, tq, tk), "fox_attention",
        (main, aq, main, ak, vt), init)


def _compress_kernel(x_ref, pos_ref, w1_ref, w2_ref, kc_ref, vct_ref):
    t = x_ref.shape[2]
    nchunk = t // CMP_STRIDE
    for kv, o_ref in enumerate((kc_ref, vct_ref)):
        first = second = None
        for l in range(CMP_STRIDE):
            xl = x_ref[0, kv, pl.ds(l, nchunk, stride=CMP_STRIDE), :]
            a = _dot((xl + pos_ref[kv, l:l + 1, :]).astype(BF16), w1_ref[kv, l])
            b = _dot((xl + pos_ref[kv, CMP_STRIDE + l:CMP_STRIDE + l + 1, :]).astype(BF16),
                     w1_ref[kv, CMP_STRIDE + l])
            first = a if first is None else first + a
            second = b if second is None else second + b
        hid = first + pltpu.roll(second, nchunk - 1, 0)
        out = _dot(jax.nn.gelu(hid, approximate=True).astype(BF16), w2_ref[kv])
        o_ref[0] = (out.T if o_ref is vct_ref else out).astype(BF16)


def _compress(cmp_in, pos2, w1bd, w2bd):
    b, _, t, _ = cmp_in.shape
    nchunk = t // CMP_STRIDE
    const = lambda a: pl.BlockSpec(a.shape, lambda bi: (0,) * a.ndim)
    return pl.pallas_call(
        _compress_kernel,
        grid=(b,),
        in_specs=[pl.BlockSpec((1, EV_CMP // LANES, t, LANES), lambda bi: (bi, 0, 0, 0)),
                  const(pos2), const(w1bd), const(w2bd)],
        out_specs=[pl.BlockSpec((1, nchunk, LANES), lambda bi: (bi, 0, 0)),
                   pl.BlockSpec((1, LANES, nchunk), lambda bi: (bi, 0, 0))],
        out_shape=[jax.ShapeDtypeStruct((b, nchunk, LANES), BF16), jax.ShapeDtypeStruct((b, LANES, nchunk), BF16)],
        compiler_params=_cparams(("parallel",)),
        name="nsa_compress",
    )(cmp_in, pos2, w1bd, w2bd)


def _gate_col(small, head, branch):
    idx = FOX_HEADS + 3 * head + branch
    return jnp.sum(jnp.where(_lane_iota() == idx, small, 0.0), axis=-1, keepdims=True)


def _nsa_select_kernel(q_ref, kc_ref, vct_ref, ovt_ref, small_ref, *rest, q_lo, tq, ns, n_sel):
    oc_ref, sel_ref = rest[-2:]
    q0 = q_lo + pl.program_id(1) * tq
    lane = _lane_iota()
    kc = kc_ref[0]
    vct = vct_ref[0]
    ncp = kc.shape[0]
    small = small_ref[0]
    qpos = q0 + lax.broadcasted_iota(jnp.int32, (1, tq), 1)
    cmp_end = lax.broadcasted_iota(jnp.int32, (ncp, 1), 0) * CMP_STRIDE + (CMP_BLOCK - 1)
    cmask = cmp_end <= qpos
    psum = [jnp.zeros((ncp, tq), F32) for _ in range(NSA_KV_GROUPS)]
    logits = [[_dot_nt(kc, _half_mask(q_ref[0, :, n * LANES:(n + 1) * LANES], g)) for g in range(NSA_KV_GROUPS)]
              for n in range(NSA_HPG)]
    for n in range(NSA_HPG):
        ots = []
        for g in range(NSA_KV_GROUPS):
            z = jnp.where(cmask, logits[n][g], -jnp.inf)
            m = jnp.max(z, axis=0, keepdims=True)
            m = jnp.where(m == -jnp.inf, 0.0, m)
            p = jnp.exp2(z - m)
            p = p / jnp.maximum(jnp.sum(p, axis=0, keepdims=True), 1e-30)
            psum[g] = psum[g] + p
            ots.append(_dot(vct[g * HEAD_DIM:(g + 1) * HEAD_DIM], p.astype(BF16)))
        gate = jnp.where(lane < HEAD_DIM, _gate_col(small, n, 0), _gate_col(small, NSA_HPG + n, 0))
        oc_ref[0, :, n * LANES:(n + 1) * LANES] = gate * jnp.concatenate(ots, axis=0).T

    nsp = ovt_ref.shape[0]
    blk = lax.broadcasted_iota(jnp.int32, (nsp, 1), 0)
    cur = qpos // SLC_BLOCK
    valid = blk * SLC_BLOCK <= qpos
    forced = (blk == 0) | (blk == cur) | (blk == cur - 1)
    scores = []
    for g in range(NSA_KV_GROUPS):
        imp = jnp.zeros((nsp, tq), F32)
        for piece in _split3(psum[g]):
            imp = imp + _dot(ovt_ref[...], piece)
        scores.append(jnp.where(valid, jnp.where(forced, jnp.inf, imp), -jnp.inf))
    slab = 8
    masks = []
    for g in range(NSA_KV_GROUPS):
        slabs = [scores[g][r:r + slab] for r in range(0, nsp, slab)]
        ranks = [jnp.zeros((slab, tq), jnp.int32) for _ in slabs]
        for i in range(ns):
            row = scores[g][i:i + 1, :]
            for r, sl in enumerate(slabs):
                if slab * r >= ns:
                    continue
                if slab * r > i:
                    ahead = row >= sl
                elif slab * (r + 1) - 1 <= i:
                    ahead = row > sl
                else:
                    ahead = (row > sl) | ((row == sl) & (blk[slab * r:slab * (r + 1)] > i))
                ranks[r] = jnp.where(ahead, ranks[r] + 1, ranks[r])
        rank = jnp.concatenate(ranks, axis=0)
        masks.append(jnp.where((rank < n_sel) & (blk < ns), 0.0, NEG))
    sel_ref[0] = jnp.concatenate(masks, axis=0).T.astype(BF16)


def _nsa_select(main, kc, vct, overlap_t, small, tq, ns, n_sel, init, n_span=4):
    b, t, _ = main.shape
    nsp = overlap_t.shape[0]
    span = t // n_span
    steps = span // tq
    shapes = [jax.ShapeDtypeStruct((b, t, NSA_W), F32), jax.ShapeDtypeStruct((b, t, NSA_KV_GROUPS * nsp), BF16)]
    outs = [jnp.zeros(s.shape, s.dtype) for s in shapes] if init is None else init
    for k in range(n_span):
        q_hi = (k + 1) * span
        ncl = min(-(-(q_hi // CMP_STRIDE) // LANES) * LANES, kc.shape[1])
        row = lambda bi, i, k=k: (bi, k * steps + i, 0)
        outs = pl.pallas_call(
            partial(_nsa_select_kernel, q_lo=k * span, tq=tq, ns=min(ns, q_hi // SLC_BLOCK), n_sel=n_sel),
            grid=(b, steps),
            in_specs=[pl.BlockSpec((1, tq, NSA_W), lambda bi, i, k=k: (bi, k * steps + i, CH_NQ * LANES // NSA_W)),
                      pl.BlockSpec((1, ncl, LANES), lambda bi, i: (bi, 0, 0)),
                      pl.BlockSpec((1, LANES, ncl), lambda bi, i: (bi, 0, 0)),
                      pl.BlockSpec((nsp, ncl), lambda bi, i: (0, 0)),
                      pl.BlockSpec((1, tq, LANES), row),
                      pl.BlockSpec(memory_space=pl.ANY), pl.BlockSpec(memory_space=pl.ANY)],
            out_specs=[pl.BlockSpec((1, tq, NSA_W), row), pl.BlockSpec((1, tq, NSA_KV_GROUPS * nsp), row)],
            out_shape=shapes,
            input_output_aliases={5: 0, 6: 1},
            compiler_params=_cparams(("parallel", "parallel")),
            name=f"nsa_select_s{k}",
        )(main, kc, vct, overlap_t, small, *outs)
    return outs


def _nsa_flash_kernel(q_ref, sel_ref, oc_ref, small_ref, ks_ref, kw_ref, hot_ref, vst_ref, vwt_ref, *rest,
                      i, n_sub, tq, tk):
    o_ref, st_ref, acc_ref = rest[-3:]
    n = pl.program_id(1)
    lane = _lane_iota()
    rows_of = lambda g: slice(g * HEAD_DIM, (g + 1) * HEAD_DIM)
    for s in range(n_sub):
        qi, rows = i * n_sub + s, slice(s * tq, (s + 1) * tq)
        q0 = qi * tq
        small = small_ref[0, rows, :]
        qt = _transposed(q_ref[0, rows, :])
        selt = _transposed(sel_ref[0, rows, :])
        qhs = [_half_mask_t(qt, g) for g in range(NSA_KV_GROUPS)]
        qcats = [jnp.concatenate([qhs[g], _half_mask_t(selt, g)], axis=0) for g in range(NSA_KV_GROUPS)]

        def sel_scores(j, qcats=qcats):
            ks = j * tk
            kcat = jnp.concatenate([ks_ref[0, pl.ds(ks, tk), :], hot_ref[pl.ds(ks, tk), :]], axis=1)
            return tuple(_dot(kcat, qcats[g]) for g in range(NSA_KV_GROUPS))

        def sel_mask(j, st, q0=q0):
            kpos, qpos = _positions_t(j, tk, q0, tq)
            return jnp.where(kpos <= qpos, st, NEG)

        def win_scores(j, qhs=qhs):
            kw = kw_ref[0, pl.ds(j * tk, tk), :]
            return tuple(_dot(kw, qhs[g]) for g in range(NSA_KV_GROUPS))

        def win_mask(j, st, q0=q0):
            kpos, qpos = _positions_t(j, tk, q0, tq)
            return jnp.where((kpos <= qpos) & (kpos > qpos - WINDOW), st, NEG)

        values = lambda ref: lambda g, j: ref[0, rows_of(g), pl.ds(j * tk, tk)]
        n_end = (qi + 1) * (tq // tk)
        win_lo = max(n_end - (WINDOW + tq) // tk, 0)
        o = oc_ref[0, rows, :]
        for branch, args in ((1, (0, qi * (tq // tk), tq // tk, sel_scores, sel_mask, values(vst_ref))),
                             (2, (win_lo, 0, n_end - win_lo, win_scores, win_mask, values(vwt_ref)))):
            res = _stream(*args, st_ref, acc_ref)
            ot = jnp.concatenate([acc / l for l, acc in res], axis=0)
            gate = jnp.where(lane < HEAD_DIM, _gate_col(small, n, branch), _gate_col(small, NSA_HPG + n, branch))
            o = o + gate * ot.T
        o_ref[0, rows, :] = o.astype(BF16)


def _nsa_flash(main, sel, oc, small, hot, vt, tq, tk, init):
    b, t, _ = main.shape
    tc = TILES_PER_CALL * tq
    tile = lambda i, ch: pl.BlockSpec((1, tc, LANES), lambda bi, n: (bi, i, ch + n))
    shared = lambda i: pl.BlockSpec((1, tc, LANES), lambda bi, n: (bi, i, 0))
    full = lambda i, ch: pl.BlockSpec((1, (i + 1) * tc, LANES), lambda bi, n: (bi, 0, ch))
    vfull = lambda i, ch: pl.BlockSpec((1, LANES, (i + 1) * tc), lambda bi, n: (bi, ch, 0))
    in_specs = lambda i: [tile(i, CH_NQ), shared(i), tile(i, 0), shared(i), full(i, CH_KS), full(i, CH_KW),
                          pl.BlockSpec(((i + 1) * tc, LANES), lambda bi, n: (0, 0)),
                          vfull(i, VT_VS), vfull(i, VT_VW)]
    return _per_query_tile(
        partial(_nsa_flash_kernel, n_sub=TILES_PER_CALL, tq=tq, tk=tk), t // tc, (b, NSA_HPG), in_specs,
        lambda i: tile(i, 0),
        jax.ShapeDtypeStruct((b, t, NSA_W), BF16), _stream_scratch(NSA_KV_GROUPS, HEAD_DIM, tq, tk), "nsa_flash",
        (main, sel, oc, small, main, main, hot, vt, vt), init)


def _diff_kernel(q_ref, k_ref, vt_ref, lam_ref, g_ref, *rest, i, n_sub, tq, tk, lam_init):
    o_ref, st_ref, acc_ref = rest[-3:]
    lp = lam_ref[...]
    lam = (jnp.exp(jnp.sum(lp[0:1] * lp[1:2], axis=-1, keepdims=True))
           - jnp.exp(jnp.sum(lp[2:3] * lp[3:4], axis=-1, keepdims=True)) + lam_init)
    for s in range(n_sub):
        qi, rows = i * n_sub + s, slice(s * tq, (s + 1) * tq)
        q2 = q_ref[0, rows, :]
        qhs = [_half_mask(q2, comp) for comp in range(2)]

        def scores(j, qhs=qhs):
            k2 = k_ref[0, pl.ds(j * tk, tk), :]
            return tuple(_dot_nt(k2, qhs[comp]) for comp in range(2))

        def mask(j, st, qi=qi):
            kpos, qpos = _positions_t(j, tk, qi * tq, tq)
            return jnp.where(kpos <= qpos, st, NEG)

        def values(comp, j):
            return vt_ref[0, :, pl.ds(j * tk, tk)]

        (l1, acc1), (l2, acc2) = _stream(0, qi * (tq // tk), tq // tk, scores, mask, values, st_ref, acc_ref)
        o = (acc1 / l1 - lam * (acc2 / l2)).T
        o_ref[0, rows, :] = (_rms(o, g_ref[...]) * (1.0 - lam_init)).astype(BF16)


def _diff_attention(main, vt, lam_p, subln_g, tq, tk, lam_init, init):
    b, t, _ = main.shape
    nh = DIFF_HEADS
    tc = TILES_PER_CALL * tq
    in_specs = lambda i: [pl.BlockSpec((1, tc, LANES), lambda bi, h: (bi, i, h)),
                          pl.BlockSpec((1, (i + 1) * tc, LANES), lambda bi, h: (bi, 0, nh + h)),
                          pl.BlockSpec((1, LANES, (i + 1) * tc), lambda bi, h: (bi, h, 0)),
                          pl.BlockSpec((4, LANES), lambda bi, h: (0, 0)),
                          pl.BlockSpec((1, LANES), lambda bi, h: (0, 0))]
    return _per_query_tile(
        partial(_diff_kernel, n_sub=TILES_PER_CALL, tq=tq, tk=tk, lam_init=lam_init), t // tc, (b, nh), in_specs,
        lambda i: pl.BlockSpec((1, tc, LANES), lambda bi, h: (bi, i, h)),
        jax.ShapeDtypeStruct((b, t, nh * LANES), BF16), _stream_scratch(2, LANES, tq, tk), "diff_attention",
        (main, main, vt, lam_p, subln_g), init)


def _mem_kv_kernel(mem_ref, g_ref, wk_ref, wv_ref, k_ref, vt_ref):
    mn = _rms(mem_ref[0], g_ref[0]).astype(BF16)
    k_ref[0, 0] = _dot(mn, wk_ref[0]).astype(BF16)
    vt_ref[0, 0] = _dot(mn, wv_ref[0]).T.astype(BF16)


def _mem_kv(mem, mem_norm_g, wk, wv):
    depth = wk.shape[0]
    b, m, d = mem.shape
    wspec = pl.BlockSpec((1, d, MEM_W), lambda l, bi: (l, 0, 0))
    return pl.pallas_call(
        _mem_kv_kernel,
        grid=(depth, b),
        in_specs=[pl.BlockSpec((1, m, d), lambda l, bi: (bi, 0, 0)),
                  pl.BlockSpec((1, 1, d), lambda l, bi: (l, 0, 0)), wspec, wspec],
        out_specs=[pl.BlockSpec((1, 1, m, MEM_W), lambda l, bi: (l, bi, 0, 0)),
                   pl.BlockSpec((1, 1, MEM_W, m), lambda l, bi: (l, bi, 0, 0))],
        out_shape=[jax.ShapeDtypeStruct((depth, b, m, MEM_W), BF16),
                   jax.ShapeDtypeStruct((depth, b, MEM_W, m), BF16)],
        compiler_params=_cparams(("parallel", "parallel")),
        name="mem_kv",
    )(mem, mem_norm_g.reshape(depth, 1, d), wk, wv)


def _post_mixer_kernel(*refs, n_in, n_part):
    a_refs, w_refs = refs[:n_in], refs[n_in:2 * n_in]
    x_ref, g1_ref, g2_ref, g3_ref, wq_ref, k_ref, vt_ref, wo_ref, o_ref = refs[2 * n_in:]
    rows = x_ref.shape[0] // n_part
    parts = [slice(r * rows, (r + 1) * rows) for r in range(n_part)]
    n_mem = k_ref.shape[2]
    heads = [(ch, hh) for ch in range(MEM_W // LANES) for hh in range(2)]
    ones = jnp.ones((ONES_ROWS, n_mem), BF16)

    def mixer_out(ps):
        y = None
        for a_ref, w_ref in zip(a_refs, w_refs):
            t = _dot(a_ref[ps, :], w_ref[...])
            y = t if y is None else y + t
        return y

    def scores(q):
        return [_dot_nt(k_ref[0, 0, :, ch * LANES:(ch + 1) * LANES], _half_mask(q[:, ch * LANES:(ch + 1) * LANES], hh))
                for ch, hh in heads]

    def attend(logits):
        outs = []
        for (ch, hh), st in zip(heads, logits):
            p = jnp.exp2(st - jnp.max(st, axis=0, keepdims=True)).astype(BF16)
            r0 = ch * LANES + hh * HEAD_DIM
            pv = _dot(jnp.concatenate([vt_ref[0, 0, r0:r0 + HEAD_DIM, :], ones], axis=0), p)
            outs.append(pv[:HEAD_DIM] / pv[HEAD_DIM:HEAD_DIM + 1])
        return jnp.concatenate(outs, axis=0).T.astype(BF16)

    ys = [mixer_out(ps) for ps in parts]
    xs = [x_ref[ps, :] + _rms(y, g1_ref[...]) for ps, y in zip(parts, ys)]
    qs = [(_dot(_rms(x, g2_ref[...]).astype(BF16), wq_ref[...]) * Q_SCALE_LOG2).astype(BF16) for x in xs]
    logits = [scores(q) for q in qs]
    os_ = [attend(lg) for lg in logits]
    for ps, x, o in zip(parts, xs, os_):
        o_ref[ps, :] = x + _rms(_dot(o, wo_ref[...]), g3_ref[...])


def _post_mixer(acts, weights, x2, g1, g2, g3, wq_all, mem_k, mem_vt, wo_all, layer, tm, rows_per_batch, n_part=2):
    n = x2.shape[0]
    m = mem_k.shape[2]
    const = lambda shape: pl.BlockSpec(shape, lambda i: (0, 0))
    row = pl.BlockSpec((tm, D_MODEL), lambda i: (i, 0))
    gspec = const((1, D_MODEL))
    return pl.pallas_call(
        partial(_post_mixer_kernel, n_in=len(acts), n_part=n_part),
        grid=(n // tm,),
        in_specs=[pl.BlockSpec((tm, a.shape[1]), lambda i: (i, 0)) for a in acts]
                 + [const(w.shape) if l is None else _layer_spec(w, l) for w, l in weights]
                 + [row, gspec, gspec, gspec, _layer_spec(wq_all, layer),
                    pl.BlockSpec((1, 1, m, MEM_W), lambda i: (layer, i // rows_per_batch, 0, 0)),
                    pl.BlockSpec((1, 1, MEM_W, m), lambda i: (layer, i // rows_per_batch, 0, 0)),
                    _layer_spec(wo_all, layer)],
        out_specs=row,
        out_shape=jax.ShapeDtypeStruct((n, D_MODEL), F32),
        compiler_params=_cparams(("parallel",)),
        name="post_mixer",
    )(*acts, *[w for w, _ in weights], x2, g1, g2, g3, wq_all, mem_k, mem_vt, wo_all)


def _ffn_kernel(x_ref, gin_ref, gout_ref, wg_ref, wu_ref, wd_ref, o_ref, *, chunk):
    x = x_ref[...]
    h = _rms(x, gin_ref[...]).astype(BF16)
    d_ff = wg_ref.shape[1]
    y = None
    for c0 in range(0, d_ff, chunk):
        gate = _dot(h, wg_ref[:, c0:c0 + chunk])
        up = _dot(h, wu_ref[:, c0:c0 + chunk])
        a = (gate * (1.0 / (1.0 + jnp.exp(-gate))) * up).astype(BF16)
        t = _dot(a, wd_ref[c0:c0 + chunk, :])
        y = t if y is None else y + t
    o_ref[...] = x + _rms(y, gout_ref[...])


def _ffn(x2, gin, gout, wg_all, wu_all, wd_all, layer, tm, chunk):
    n = x2.shape[0]
    const = lambda shape: pl.BlockSpec(shape, lambda i: (0, 0), pipeline_mode=pl.Buffered(1))
    wspec = lambda w: _layer_spec(w, layer, pipeline_mode=pl.Buffered(1))
    return pl.pallas_call(
        partial(_ffn_kernel, chunk=chunk),
        grid=(n // tm,),
        in_specs=[pl.BlockSpec((tm, D_MODEL), lambda i: (i, 0)), const((1, D_MODEL)), const((1, D_MODEL)),
                  wspec(wg_all), wspec(wu_all), wspec(wd_all)],
        out_specs=pl.BlockSpec((tm, D_MODEL), lambda i: (i, 0)),
        out_shape=jax.ShapeDtypeStruct((n, D_MODEL), F32),
        compiler_params=_cparams(("parallel",)),
        name="ffn",
    )(x2, gin, gout, wg_all, wu_all, wd_all)


def _even_weights(w_in, w_out):
    kvw = NSA_KV_GROUPS * HEAD_DIM
    offs = [int(o) for o in np.cumsum((FOX_W, FOX_W, FOX_W, FOX_HEADS, NSA_W) + (kvw,) * 6 + (3 * NSA_HEADS,))]
    fk0, fv0, fl0, nq0, kc0, vc0, ks0, vs0, kw0, vw0, gl0, end = offs
    paired = [HEAD_DIM * (g * NSA_HPG + n) for n in range(NSA_HPG) for g in range(NSA_KV_GROUPS)]
    cols = lambda a, b: w_in[:, a:b]
    w = jnp.concatenate([cols(0, fv0)] + [cols(nq0 + h, nq0 + h + HEAD_DIM) for h in paired]
                        + [cols(ks0, vs0), cols(kw0, vw0), cols(kc0, vc0), cols(vc0, ks0),
                           cols(fv0, fl0), cols(vs0, kw0), cols(vw0, gl0)], axis=1).astype(BF16)
    w_small = jnp.concatenate([cols(fl0, nq0), cols(gl0, end),
                               jnp.zeros((D_MODEL, LANES - FOX_HEADS - 3 * NSA_HEADS), w_in.dtype)],
                              axis=1).astype(BF16)
    w_out_fox = w_out[:FOX_W].astype(BF16)
    w_out_nsa = jnp.concatenate([w_out[FOX_W + h:FOX_W + h + HEAD_DIM] for h in paired], axis=0).astype(BF16)
    return w, w_small, w_out_fox, w_out_nsa


def _overlap_matrix_t(t, ncp):
    nc = (t - CMP_BLOCK) // CMP_STRIDE + 1
    ns = t // SLC_BLOCK
    cs = np.arange(nc) * CMP_STRIDE
    ss = np.arange(ns) * SLC_BLOCK
    ov = np.clip(np.minimum(cs[:, None] + CMP_BLOCK, ss[None, :] + SLC_BLOCK)
                 - np.maximum(cs[:, None], ss[None, :]), 0, None) / CMP_BLOCK
    full = np.zeros((LANES // NSA_KV_GROUPS, ncp), np.float32)
    full[:ns, :nc] = ov.T
    return jnp.asarray(full, BF16), ns


def _compress_weights(pos_k, w1_k, w2_k, pos_v, w1_v, w2_v):
    g = NSA_KV_GROUPS
    pos2, w1bd, w2bd = [], [], []
    for pos, w1, w2 in ((pos_k, w1_k, w2_k), (pos_v, w1_v, w2_v)):
        pos2.append(jnp.tile(pos.astype(F32), (1, g)))
        w1l = w1.reshape(CMP_BLOCK, HEAD_DIM, CMP_HIDDEN).astype(BF16)
        bd = jnp.zeros((CMP_BLOCK, g * HEAD_DIM, g * CMP_HIDDEN), BF16)
        w2g = jnp.zeros((g * CMP_HIDDEN, g * HEAD_DIM), BF16)
        for gi in range(g):
            bd = bd.at[:, gi * HEAD_DIM:(gi + 1) * HEAD_DIM, gi * CMP_HIDDEN:(gi + 1) * CMP_HIDDEN].set(w1l)
            w2g = w2g.at[gi * CMP_HIDDEN:(gi + 1) * CMP_HIDDEN, gi * HEAD_DIM:(gi + 1) * HEAD_DIM].set(
                w2.astype(BF16))
        w1bd.append(bd)
        w2bd.append(w2g)
    return jnp.stack(pos2), jnp.stack(w1bd), jnp.stack(w2bd)


def kernel(x, mem, positions, sandwich_g, mem_norm_g, ev_w_in, ev_fox_fbias, ev_cmp_pos_k, ev_cmp_w1_k, ev_cmp_w2_k, ev_cmp_pos_v, ev_cmp_w1_v, ev_cmp_w2_v, ev_w_out, od_w_in, od_lambda, od_subln_g, od_w_out, ca_wq, ca_wk, ca_wv, ca_wo, ffn_wg, ffn_wu, ffn_wd):
    b, t, d = x.shape
    depth = sandwich_g.shape[0]
    n = b * t
    tm = 512
    tm_ffn = 512
    tm_mix = 1024
    tq, tk = 512, 256
    tk_diff = 512
    assert d == D_MODEL and t % tq == 0 and tq % tk == 0 and WINDOW % tk == 0 and t % (TILES_PER_CALL * tq) == 0 and t % tm == 0 and n % tm_ffn == 0

    tabs = tuple(a.reshape(b, t, LANES) for a in _rope_tables(positions, 512))
    mem_k, mem_vt = _mem_kv(mem, mem_norm_g, ca_wk.astype(BF16), ca_wv.astype(BF16))
    ncp = t // CMP_STRIDE
    overlap_t, ns = _overlap_matrix_t(t, ncp)
    assert ns <= overlap_t.shape[0]
    n_sel = min(SLC_TOPK, ns)
    hot = jnp.asarray((np.arange(t)[:, None] // SLC_BLOCK == np.arange(LANES)[None, :] % overlap_t.shape[0])
                      .astype(np.float32), BF16)
    gain = lambda l, j: sandwich_g[l, j].reshape(1, d)

    bf = lambda w: w.astype(BF16)
    od_w_in_b, od_w_out_b, ca_wq_b, ca_wo_b = bf(od_w_in), bf(od_w_out), bf(ca_wq), bf(ca_wo)
    ffn_wg_b, ffn_wu_b, ffn_wd_b = bf(ffn_wg), bf(ffn_wu), bf(ffn_wd)
    x2 = x.reshape(n, d)
    dead = {}
    for layer in range(depth):
        x3 = x2.reshape(b, t, d)
        if layer % 2 == 0:
            e = layer // 2
            w, w_small, w_out_fox, w_out_nsa = _even_weights(ev_w_in[e], ev_w_out[e])
            fb_row = jnp.zeros((1, LANES), F32).at[0, :FOX_HEADS].set(ev_fox_fbias[e].astype(F32))
            main, cmp_in, vt, small = _even_proj(x3, gain(layer, 0), w, w_small, fb_row, tabs, tm)
            aq, ak = _fox_aug(small, 512)
            o_fox = dead["fox"] = _fox_attention(main, aq, ak, vt, tq, tk, dead.get("fox"))
            kc, vct = _compress(cmp_in, *_compress_weights(
                ev_cmp_pos_k[e], ev_cmp_w1_k[e], ev_cmp_w2_k[e], ev_cmp_pos_v[e], ev_cmp_w1_v[e], ev_cmp_w2_v[e]))
            oc, sel = dead["sel"] = _nsa_select(main, kc, vct, overlap_t, small, 256, ns, n_sel, dead.get("sel"))
            o_nsa = dead["nsa"] = _nsa_flash(main, sel, oc, small, hot, vt, tq, tk, dead.get("nsa"))
            acts, w_outs = [o_fox.reshape(n, FOX_W), o_nsa.reshape(n, NSA_W)], [(w_out_fox, None), (w_out_nsa, None)]
        else:
            o = layer // 2
            main, vt = _odd_proj(x3, gain(layer, 0), od_w_in_b, o, tabs, tm)
            lam_init = 0.8 - 0.6 * math.exp(-0.3 * layer)
            lam_p = jnp.pad(od_lambda[o].astype(F32), ((0, 0), (0, LANES - HEAD_DIM)))
            attn = dead["diff"] = _diff_attention(main, vt, lam_p, od_subln_g[o].reshape(1, LANES).astype(F32),
                                                  tq, tk_diff, lam_init, dead.get("diff"))
            acts, w_outs = [attn.reshape(n, D_MODEL)], [(od_w_out_b, o)]
        x2 = _post_mixer(acts, w_outs, x2, gain(layer, 1), gain(layer, 2), gain(layer, 3),
                         ca_wq_b, mem_k, mem_vt, ca_wo_b, layer, tm_mix, t // tm_mix, n_part=tm_mix // 256)
        x2 = _ffn(x2, gain(layer, 4), gain(layer, 5), ffn_wg_b, ffn_wu_b, ffn_wd_b, layer, tm_ffn, 256)
    return x2.reshape(b, t, d)
```

```python
import math
from functools import partial

import numpy as np
import jax
import jax.numpy as jnp
from jax import lax
from jax.experimental import pallas as pl
from jax.experimental.pallas import tpu as pltpu

F32 = jnp.float32
BF16 = jnp.bfloat16

D_MODEL = 1024
HEAD_DIM = 64
LANES = 128
ROPE_DIM = HEAD_DIM // 4
ROPE_THETA = 500000.0
FOX_HEADS = 8
NSA_HEADS = 8
NSA_KV_GROUPS = 2
NSA_HPG = NSA_HEADS // NSA_KV_GROUPS
CMP_BLOCK = 32
CMP_STRIDE = 16
CMP_HIDDEN = 2 * HEAD_DIM
SLC_BLOCK = 64
SLC_TOPK = 16
WINDOW = 512
DIFF_HEADS = 8
MEM_HEADS = 4
MEM_W = MEM_HEADS * HEAD_DIM
RMS_EPS = 1e-6
Q_SCALE = HEAD_DIM ** -0.5
LOG2E = math.log2(math.e)
Q_SCALE_LOG2 = Q_SCALE * LOG2E
NEG = -1e30
AUG_PER_HEAD = 6
ONES_ROWS = 16

FOX_W = FOX_HEADS * HEAD_DIM
NSA_W = NSA_HEADS * HEAD_DIM
CH_FQ, CH_FK, CH_NQ, CH_KS, CH_KW = 0, 4, 8, 12, 13
EV_MAIN = 14 * LANES
EV_CMP = 2 * LANES
VT_FV, VT_VS, VT_VW = 0, 4, 5
EV_VT = 6 * LANES
EV_ROPE_CHUNKS = tuple(range(CH_NQ, CH_NQ + 4)) + (CH_KS, CH_KW, EV_MAIN // LANES)
EV_QSCALE_CHUNKS = tuple(range(CH_FQ, CH_FQ + 4)) + tuple(range(CH_NQ, CH_NQ + 4))
OD_MAIN = 2 * D_MODEL
OD_VT = D_MODEL

VMEM_LIMIT = 56 * 1024 * 1024


def _layer_spec(stacked, layer, **kw):
    zeros = (0,) * (stacked.ndim - 1)
    return pl.BlockSpec((None,) + stacked.shape[1:], lambda *idx: (layer,) + zeros, **kw)


def _cparams(sem):
    return pltpu.CompilerParams(dimension_semantics=sem, vmem_limit_bytes=VMEM_LIMIT)


def _rms(x, g):
    return x * lax.rsqrt(jnp.mean(x * x, axis=-1, keepdims=True) + RMS_EPS) * g


def _split3(x):
    hi = x.astype(BF16)
    r1 = x - hi.astype(F32)
    mid = r1.astype(BF16)
    lo = (r1 - mid.astype(F32)).astype(BF16)
    return hi, mid, lo


def _dot(a, b):
    return jnp.dot(a, b, preferred_element_type=F32)


def _dot_nt(a, b):
    return lax.dot_general(a, b, (((1,), (1,)), ((), ())), preferred_element_type=F32)


def _lane_iota(n=LANES):
    return lax.broadcasted_iota(jnp.int32, (1, n), 1)


def _half_mask(q2, half):
    return jnp.where(_lane_iota() // HEAD_DIM == half, q2, jnp.zeros_like(q2))


def _transposed(x):
    return x.astype(F32).T.astype(BF16)


def _half_mask_t(qt, half):
    row = lax.broadcasted_iota(jnp.int32, (qt.shape[0], 1), 0)
    return jnp.where(row // HEAD_DIM == half, qt, jnp.zeros_like(qt))


def _positions_t(j, tk, q0, tq):
    kpos = j * tk + lax.broadcasted_iota(jnp.int32, (tk, 1), 0)
    qpos = q0 + lax.broadcasted_iota(jnp.int32, (1, tq), 1)
    return kpos, qpos


def _stream(base, n_full, n_tail, scores, mask, values, st_ref, acc_ref):
    n_chain, dv, tq = acc_ref.shape

    def park(j, slot):
        for c, st in enumerate(scores(j)):
            st_ref[slot, c] = st

    def step(j, slot, stats, masked, prefetch):
        if prefetch:
            park(j + 1, 1 - slot)
        parts = []
        for c in range(n_chain):
            st = st_ref[slot, c]
            if masked:
                st = mask(j, st)
            m, _ = stats[c]
            m_new = jnp.maximum(m, jnp.max(st, axis=0, keepdims=True))
            p = jnp.exp2(st - m_new).astype(BF16)
            vt1 = jnp.concatenate([values(c, j), jnp.ones((ONES_ROWS, p.shape[0]), BF16)], axis=0)
            parts.append((m_new, jnp.exp2(m - m_new), _dot(vt1, p)))
        out = []
        for c, (m_new, alpha, pv) in enumerate(parts):
            acc_ref[c] = alpha * acc_ref[c] + pv[:dv]
            out.append((m_new, alpha * stats[c][1] + pv[dv:dv + 1]))
        return tuple(out)

    acc_ref[...] = jnp.zeros_like(acc_ref)
    park(base, 0)
    stats = ((jnp.full((1, tq), NEG, F32), jnp.zeros((1, tq), F32)),) * n_chain
    n_tiles = n_full + n_tail
    for s_ in range(n_tiles):
        stats = step(base + s_, s_ % 2, stats, s_ >= n_full, s_ + 1 < n_tiles)
    return [(l, acc_ref[c]) for c, (_, l) in enumerate(stats)]


TILES_PER_CALL = 4


def _per_query_tile(kernel_fn, n_q, grid, in_specs_fn, out_spec_fn, out_shape, scratch, name, args, init):
    out = jnp.zeros(out_shape.shape, out_shape.dtype) if init is None else init
    for i in range(n_q):
        in_specs = list(in_specs_fn(i)) + [pl.BlockSpec(memory_space=pl.ANY)]
        out = pl.pallas_call(
            partial(kernel_fn, i=i), grid=grid, in_specs=in_specs, out_specs=out_spec_fn(i), out_shape=out_shape,
            scratch_shapes=scratch, input_output_aliases={len(args): 0},
            compiler_params=_cparams(("parallel",) * len(grid)), name=f"{name}_q{i}",
        )(*args, out)
    return out


def _stream_scratch(n_chain, dv, tq, tk):
    return [pltpu.VMEM((2, n_chain, tk, tq), F32), pltpu.VMEM((n_chain, dv, tq), F32)]


def _rope_kernel(pos_ref, inv_ref, m1_ref, m2_ref, c_ref, s1_ref, s2_ref):
    ang = pos_ref[...].astype(F32) * inv_ref[...]
    c_ref[...] = jnp.cos(ang)
    sn = jnp.sin(ang)
    s1_ref[...] = -sn * m1_ref[...]
    s2_ref[...] = sn * m2_ref[...]


def _rope_tables(positions, tm):
    n = positions.size
    inv = ROPE_THETA ** (-jnp.arange(0, ROPE_DIM, 2, dtype=F32) / ROPE_DIM)
    lane = np.arange(LANES) % HEAD_DIM
    half = ROPE_DIM // 2
    inv_l = jnp.where(lane < ROPE_DIM, inv[lane % half], 0.0).reshape(1, LANES).astype(F32)
    m1 = jnp.asarray((lane < half).astype(np.float32)).reshape(1, LANES)
    m2 = jnp.asarray(((lane >= half) & (lane < ROPE_DIM)).astype(np.float32)).reshape(1, LANES)
    row = pl.BlockSpec((1, LANES), lambda i: (0, 0))
    tab = pl.BlockSpec((tm, LANES), lambda i: (i, 0))
    return pl.pallas_call(
        _rope_kernel,
        grid=(n // tm,),
        in_specs=[pl.BlockSpec((tm, 1), lambda i: (i, 0)), row, row, row],
        out_specs=[tab, tab, tab],
        out_shape=[jax.ShapeDtypeStruct((n, LANES), F32)] * 3,
        compiler_params=_cparams(("parallel",)),
        name="rope_tables",
    )(positions.reshape(n, 1), inv_l, m1, m2)


def _apply_rope(y, c, s1, s2):
    half = ROPE_DIM // 2
    return y * c + pltpu.roll(y, LANES - half, 1) * s1 + pltpu.roll(y, half, 1) * s2


def _project_chunks(h, w_ref, tabs, dests, rope_chunks, qscale_chunks):
    c, s1, s2 = tabs
    wide = 2 * LANES
    where = [(kind, ref, k) for kind, ref, n in dests for k in range(n)]
    for ch2 in range(len(where) // 2):
        y2 = _dot(h, w_ref[:, ch2 * wide:(ch2 + 1) * wide])
        for ch in (2 * ch2, 2 * ch2 + 1):
            y = y2[:, (ch % 2) * LANES:(ch % 2 + 1) * LANES]
            if ch in rope_chunks:
                y = _apply_rope(y, c, s1, s2)
            if ch in qscale_chunks:
                y = y * Q_SCALE_LOG2
            kind, ref, k = where[ch]
            if kind == "cols":
                ref[0, k * LANES:(k + 1) * LANES, :] = y.T.astype(BF16)
            elif kind == "rows_f32":
                ref[0, k] = y
            else:
                ref[0, :, k * LANES:(k + 1) * LANES] = y.astype(BF16)


def _even_proj_kernel(x_ref, g_ref, w_ref, ws_ref, fb_ref, c_ref, s1_ref, s2_ref,
                      main_ref, cmp_ref, vt_ref, small_ref):
    h = _rms(x_ref[0], g_ref[...]).astype(BF16)
    dests = [("rows", main_ref, EV_MAIN // LANES), ("rows_f32", cmp_ref, EV_CMP // LANES),
             ("cols", vt_ref, EV_VT // LANES)]
    _project_chunks(h, w_ref, (c_ref[0], s1_ref[0], s2_ref[0]), dests, EV_ROPE_CHUNKS, EV_QSCALE_CHUNKS)
    ys = _dot(h, ws_ref[...])
    z = ys + fb_ref[...]
    log_f = jnp.minimum(z, 0.0) - jnp.log(1.0 + jnp.exp(-jnp.abs(z)))
    gate = 1.0 / (1.0 + jnp.exp(-ys))
    small_ref[0] = jnp.where(_lane_iota() < FOX_HEADS, log_f, gate)


def _even_proj(x3, g, w, w_small, fb_row, tabs, tm):
    b, t, _ = x3.shape
    const = lambda shape: pl.BlockSpec(shape, lambda bi, i: (0, 0))
    tab = pl.BlockSpec((1, tm, LANES), lambda bi, i: (bi, i, 0))
    return pl.pallas_call(
        _even_proj_kernel,
        grid=(b, t // tm),
        in_specs=[pl.BlockSpec((1, tm, D_MODEL), lambda bi, i: (bi, i, 0)), const((1, D_MODEL)),
                  const((D_MODEL, EV_MAIN + EV_CMP + EV_VT)), const((D_MODEL, LANES)), const((1, LANES)),
                  tab, tab, tab],
        out_specs=[pl.BlockSpec((1, tm, EV_MAIN), lambda bi, i: (bi, i, 0)),
                   pl.BlockSpec((1, EV_CMP // LANES, tm, LANES), lambda bi, i: (bi, 0, i, 0)),
                   pl.BlockSpec((1, EV_VT, tm), lambda bi, i: (bi, 0, i)), tab],
        out_shape=[jax.ShapeDtypeStruct((b, t, EV_MAIN), BF16),
                   jax.ShapeDtypeStruct((b, EV_CMP // LANES, t, LANES), F32),
                   jax.ShapeDtypeStruct((b, EV_VT, t), BF16), jax.ShapeDtypeStruct((b, t, LANES), F32)],
        compiler_params=_cparams(("parallel", "parallel")),
        name="even_proj",
    )(x3, g, w, w_small, fb_row, *tabs)


def _odd_proj_kernel(x_ref, g_ref, w_ref, c_ref, s1_ref, s2_ref, main_ref, vt_ref):
    h = _rms(x_ref[0], g_ref[...]).astype(BF16)
    n_main = OD_MAIN // LANES
    dests = [("rows", main_ref, n_main), ("cols", vt_ref, OD_VT // LANES)]
    _project_chunks(h, w_ref, (c_ref[0], s1_ref[0], s2_ref[0]), dests, tuple(range(n_main)),
                    tuple(range(n_main // 2)))


def _odd_proj(x3, g, w_all, layer, tabs, tm):
    b, t, _ = x3.shape
    const = lambda shape: pl.BlockSpec(shape, lambda bi, i: (0, 0))
    tab = pl.BlockSpec((1, tm, LANES), lambda bi, i: (bi, i, 0))
    return pl.pallas_call(
        _odd_proj_kernel,
        grid=(b, t // tm),
        in_specs=[pl.BlockSpec((1, tm, D_MODEL), lambda bi, i: (bi, i, 0)), const((1, D_MODEL)),
                  _layer_spec(w_all, layer), tab, tab, tab],
        out_specs=[pl.BlockSpec((1, tm, OD_MAIN), lambda bi, i: (bi, i, 0)),
                   pl.BlockSpec((1, OD_VT, tm), lambda bi, i: (bi, 0, i))],
        out_shape=[jax.ShapeDtypeStruct((b, t, OD_MAIN), BF16), jax.ShapeDtypeStruct((b, OD_VT, t), BF16)],
        compiler_params=_cparams(("parallel", "parallel")),
        name="odd_proj",
    )(x3, g, w_all, *tabs)


def _fox_aug_kernel(lf_ref, tril_ref, e_ref, one_ref, aq_ref, ak_ref, carry_ref):
    @pl.when(pl.program_id(1) == 0)
    def _():
        carry_ref[...] = jnp.zeros_like(carry_ref)

    tril = tril_ref[...]
    sub = tril.shape[0]
    carry = carry_ref[...]
    blocks = []
    for r0 in range(0, lf_ref.shape[1], sub):
        c = carry
        for piece in _split3(lf_ref[0, r0:r0 + sub, :]):
            c = c + _dot(tril, piece)
        blocks.append(c)
        carry = c[-1:, :]
    carry_ref[...] = carry
    aug = one_ref[...]
    for r, piece in enumerate(_split3(jnp.concatenate(blocks, axis=0) * LOG2E)):
        aug = aug + _dot(piece, e_ref[r])
    aq_ref[0] = aug[:, :LANES].astype(BF16)
    ak_ref[0] = aug[:, LANES:].astype(BF16)


def _fox_aug(small, tc, sub=128):
    b, t, _ = small.shape
    tril = jnp.asarray(np.tril(np.ones((sub, sub), np.float32)), BF16)
    spread = np.zeros((3, LANES, 2 * LANES), np.float32)
    ones = np.zeros((1, 2 * LANES), np.float32)
    for h in range(FOX_HEADS):
        for r in range(3):
            spread[r, h, AUG_PER_HEAD * h + r] = 1.0
            spread[r, h, LANES + AUG_PER_HEAD * h + 3 + r] = -1.0
            ones[0, AUG_PER_HEAD * h + 3 + r] = 1.0
            ones[0, LANES + AUG_PER_HEAD * h + r] = 1.0
    const2 = lambda shape: pl.BlockSpec(shape, lambda bi, i: (0,) * len(shape))
    blk = pl.BlockSpec((1, tc, LANES), lambda bi, i: (bi, i, 0))
    return pl.pallas_call(
        _fox_aug_kernel,
        grid=(b, t // tc),
        in_specs=[blk, const2((sub, sub)), const2((3, LANES, 2 * LANES)), const2((1, 2 * LANES))],
        out_specs=[blk, blk],
        out_shape=[jax.ShapeDtypeStruct((b, t, LANES), BF16)] * 2,
        scratch_shapes=[pltpu.VMEM((1, LANES), F32)],
        compiler_params=_cparams(("parallel", "arbitrary")),
        name="fox_aug",
    )(small, tril, jnp.asarray(spread, BF16), jnp.asarray(ones))


def _fox_kernel(q_ref, aq_ref, k_ref, ak_ref, vt_ref, *rest, i, n_sub, tq, tk):
    o_ref, st_ref, acc_ref = rest[-3:]
    pair = pl.program_id(1)
    row = lax.broadcasted_iota(jnp.int32, (LANES, 1), 0)
    for s in range(n_sub):
        qi, rows = i * n_sub + s, slice(s * tq, (s + 1) * tq)
        qt = _transposed(q_ref[0, rows, :])
        qat = _transposed(aq_ref[0, rows, :])
        qcats = []
        for hh in range(2):
            head = 2 * pair + hh
            in_head = (row >= AUG_PER_HEAD * head) & (row < AUG_PER_HEAD * (head + 1))
            qcats.append(jnp.concatenate([_half_mask_t(qt, hh), jnp.where(in_head, qat, jnp.zeros_like(qat))],
                                         axis=0))

        def scores(j, qcats=qcats):
            ks = j * tk
            kcat = jnp.concatenate([k_ref[0, pl.ds(ks, tk), :], ak_ref[0, pl.ds(ks, tk), :]], axis=1)
            return tuple(_dot(kcat, qcats[hh]) for hh in range(2))

        def mask(j, st, qi=qi):
            kpos, qpos = _positions_t(j, tk, qi * tq, tq)
            return jnp.where(kpos <= qpos, st, NEG)

        def values(hh, j):
            return vt_ref[0, hh * HEAD_DIM:(hh + 1) * HEAD_DIM, pl.ds(j * tk, tk)]

        res = _stream(0, qi * (tq // tk), tq // tk, scores, mask, values, st_ref, acc_ref)
        ot = jnp.concatenate([acc / l for l, acc in res], axis=0)
        o_ref[0, rows, :] = ot.T.astype(BF16)


def _fox_attention(main, aq, ak, vt, tq, tk, init):
    b, t, _ = main.shape
    tc = TILES_PER_CALL * tq
    in_specs = lambda i: [pl.BlockSpec((1, tc, LANES), lambda bi, p: (bi, i, CH_FQ + p)),
                          pl.BlockSpec((1, tc, LANES), lambda bi, p: (bi, i, 0)),
                          pl.BlockSpec((1, (i + 1) * tc, LANES), lambda bi, p: (bi, 0, CH_FK + p)),
                          pl.BlockSpec((1, (i + 1) * tc, LANES), lambda bi, p: (bi, 0, 0)),
                          pl.BlockSpec((1, LANES, (i + 1) * tc), lambda bi, p: (bi, VT_FV + p, 0))]
    return _per_query_tile(
        partial(_fox_kernel, n_sub=TILES_PER_CALL, tq=tq, tk=tk), t // tc, (b, FOX_HEADS // 2), in_specs,
        lambda i: pl.BlockSpec((1, tc, LANES), lambda bi, p: (bi, i, p)),
        jax.ShapeDtypeStruct((b, t, FOX_W), BF16), _stream_scratch(2, HEAD_DIM, tq, tk), "fox_attention",
        (main, aq, main, ak, vt), init)


def _compress_kernel(x_ref, pos_ref, w1_ref, w2_ref, kc_ref, vct_ref):
    t = x_ref.shape[2]
    nchunk = t // CMP_STRIDE
    for kv, o_ref in enumerate((kc_ref, vct_ref)):
        first = second = None
        for l in range(CMP_STRIDE):
            xl = x_ref[0, kv, pl.ds(l, nchunk, stride=CMP_STRIDE), :]
            a = _dot((xl + pos_ref[kv, l:l + 1, :]).astype(BF16), w1_ref[kv, l])
            b = _dot((xl + pos_ref[kv, CMP_STRIDE + l:CMP_STRIDE + l + 1, :]).astype(BF16),
                     w1_ref[kv, CMP_STRIDE + l])
            first = a if first is None else first + a
            second = b if second is None else second + b
        hid = first + pltpu.roll(second, nchunk - 1, 0)
        out = _dot(jax.nn.gelu(hid, approximate=True).astype(BF16), w2_ref[kv])
        o_ref[0] = (out.T if o_ref is vct_ref else out).astype(BF16)


def _compress(cmp_in, pos2, w1bd, w2bd):
    b, _, t, _ = cmp_in.shape
    nchunk = t // CMP_STRIDE
    const = lambda a: pl.BlockSpec(a.shape, lambda bi: (0,) * a.ndim)
    return pl.pallas_call(
        _compress_kernel,
        grid=(b,),
        in_specs=[pl.BlockSpec((1, EV_CMP // LANES, t, LANES), lambda bi: (bi, 0, 0, 0)),
                  const(pos2), const(w1bd), const(w2bd)],
        out_specs=[pl.BlockSpec((1, nchunk, LANES), lambda bi: (bi, 0, 0)),
                   pl.BlockSpec((1, LANES, nchunk), lambda bi: (bi, 0, 0))],
        out_shape=[jax.ShapeDtypeStruct((b, nchunk, LANES), BF16), jax.ShapeDtypeStruct((b, LANES, nchunk), BF16)],
        compiler_params=_cparams(("parallel",)),
        name="nsa_compress",
    )(cmp_in, pos2, w1bd, w2bd)


def _gate_col(small, head, branch):
    idx = FOX_HEADS + 3 * head + branch
    return jnp.sum(jnp.where(_lane_iota() == idx, small, 0.0), axis=-1, keepdims=True)


def _nsa_select_kernel(q_ref, kc_ref, vct_ref, ovt_ref, small_ref, *rest, q_lo, tq, ns, n_sel):
    oc_ref, sel_ref = rest[-2:]
    q0 = q_lo + pl.program_id(1) * tq
    lane = _lane_iota()
    kc = kc_ref[0]
    vct = vct_ref[0]
    ncp = kc.shape[0]
    small = small_ref[0]
    qpos = q0 + lax.broadcasted_iota(jnp.int32, (1, tq), 1)
    cmp_end = lax.broadcasted_iota(jnp.int32, (ncp, 1), 0) * CMP_STRIDE + (CMP_BLOCK - 1)
    cmask = cmp_end <= qpos
    psum = [jnp.zeros((ncp, tq), F32) for _ in range(NSA_KV_GROUPS)]
    logits = [[_dot_nt(kc, _half_mask(q_ref[0, :, n * LANES:(n + 1) * LANES], g)) for g in range(NSA_KV_GROUPS)]
              for n in range(NSA_HPG)]
    for n in range(NSA_HPG):
        ots = []
        for g in range(NSA_KV_GROUPS):
            z = jnp.where(cmask, logits[n][g], -jnp.inf)
            m = jnp.max(z, axis=0, keepdims=True)
            m = jnp.where(m == -jnp.inf, 0.0, m)
            p = jnp.exp2(z - m)
            p = p / jnp.maximum(jnp.sum(p, axis=0, keepdims=True), 1e-30)
            psum[g] = psum[g] + p
            ots.append(_dot(vct[g * HEAD_DIM:(g + 1) * HEAD_DIM], p.astype(BF16)))
        gate = jnp.where(lane < HEAD_DIM, _gate_col(small, n, 0), _gate_col(small, NSA_HPG + n, 0))
        oc_ref[0, :, n * LANES:(n + 1) * LANES] = gate * jnp.concatenate(ots, axis=0).T

    nsp = ovt_ref.shape[0]
    blk = lax.broadcasted_iota(jnp.int32, (nsp, 1), 0)
    cur = qpos // SLC_BLOCK
    valid = blk * SLC_BLOCK <= qpos
    forced = (blk == 0) | (blk == cur) | (blk == cur - 1)
    scores = []
    for g in range(NSA_KV_GROUPS):
        imp = jnp.zeros((nsp, tq), F32)
        for piece in _split3(psum[g]):
            imp = imp + _dot(ovt_ref[...], piece)
        scores.append(jnp.where(valid, jnp.where(forced, jnp.inf, imp), -jnp.inf))
    slab = 8
    masks = []
    for g in range(NSA_KV_GROUPS):
        slabs = [scores[g][r:r + slab] for r in range(0, nsp, slab)]
        ranks = [jnp.zeros((slab, tq), jnp.int32) for _ in slabs]
        for i in range(ns):
            row = scores[g][i:i + 1, :]
            for r, sl in enumerate(slabs):
                if slab * r >= ns:
                    continue
                if slab * r > i:
                    ahead = row >= sl
                elif slab * (r + 1) - 1 <= i:
                    ahead = row > sl
                else:
                    ahead = (row > sl) | ((row == sl) & (blk[slab * r:slab * (r + 1)] > i))
                ranks[r] = jnp.where(ahead, ranks[r] + 1, ranks[r])
        rank = jnp.concatenate(ranks, axis=0)
        masks.append(jnp.where((rank < n_sel) & (blk < ns), 0.0, NEG))
    sel_ref[0] = jnp.concatenate(masks, axis=0).T.astype(BF16)


def _nsa_select(main, kc, vct, overlap_t, small, tq, ns, n_sel, init, n_span=4):
    b, t, _ = main.shape
    nsp = overlap_t.shape[0]
    span = t // n_span
    steps = span // tq
    shapes = [jax.ShapeDtypeStruct((b, t, NSA_W), F32), jax.ShapeDtypeStruct((b, t, NSA_KV_GROUPS * nsp), BF16)]
    outs = [jnp.zeros(s.shape, s.dtype) for s in shapes] if init is None else init
    for k in range(n_span):
        q_hi = (k + 1) * span
        ncl = min(-(-(q_hi // CMP_STRIDE) // LANES) * LANES, kc.shape[1])
        row = lambda bi, i, k=k: (bi, k * steps + i, 0)
        outs = pl.pallas_call(
            partial(_nsa_select_kernel, q_lo=k * span, tq=tq, ns=min(ns, q_hi // SLC_BLOCK), n_sel=n_sel),
            grid=(b, steps),
            in_specs=[pl.BlockSpec((1, tq, NSA_W), lambda bi, i, k=k: (bi, k * steps + i, CH_NQ * LANES // NSA_W)),
                      pl.BlockSpec((1, ncl, LANES), lambda bi, i: (bi, 0, 0)),
                      pl.BlockSpec((1, LANES, ncl), lambda bi, i: (bi, 0, 0)),
                      pl.BlockSpec((nsp, ncl), lambda bi, i: (0, 0)),
                      pl.BlockSpec((1, tq, LANES), row),
                      pl.BlockSpec(memory_space=pl.ANY), pl.BlockSpec(memory_space=pl.ANY)],
            out_specs=[pl.BlockSpec((1, tq, NSA_W), row), pl.BlockSpec((1, tq, NSA_KV_GROUPS * nsp), row)],
            out_shape=shapes,
            input_output_aliases={5: 0, 6: 1},
            compiler_params=_cparams(("parallel", "parallel")),
            name=f"nsa_select_s{k}",
        )(main, kc, vct, overlap_t, small, *outs)
    return outs


def _nsa_flash_kernel(q_ref, sel_ref, oc_ref, small_ref, ks_ref, kw_ref, hot_ref, vst_ref, vwt_ref, *rest,
                      i, n_sub, tq, tk):
    o_ref, st_ref, acc_ref = rest[-3:]
    n = pl.program_id(1)
    lane = _lane_iota()
    rows_of = lambda g: slice(g * HEAD_DIM, (g + 1) * HEAD_DIM)
    for s in range(n_sub):
        qi, rows = i * n_sub + s, slice(s * tq, (s + 1) * tq)
        q0 = qi * tq
        small = small_ref[0, rows, :]
        qt = _transposed(q_ref[0, rows, :])
        selt = _transposed(sel_ref[0, rows, :])
        qhs = [_half_mask_t(qt, g) for g in range(NSA_KV_GROUPS)]
        qcats = [jnp.concatenate([qhs[g], _half_mask_t(selt, g)], axis=0) for g in range(NSA_KV_GROUPS)]

        def sel_scores(j, qcats=qcats):
            ks = j * tk
            kcat = jnp.concatenate([ks_ref[0, pl.ds(ks, tk), :], hot_ref[pl.ds(ks, tk), :]], axis=1)
            return tuple(_dot(kcat, qcats[g]) for g in range(NSA_KV_GROUPS))

        def sel_mask(j, st, q0=q0):
            kpos, qpos = _positions_t(j, tk, q0, tq)
            return jnp.where(kpos <= qpos, st, NEG)

        def win_scores(j, qhs=qhs):
            kw = kw_ref[0, pl.ds(j * tk, tk), :]
            return tuple(_dot(kw, qhs[g]) for g in range(NSA_KV_GROUPS))

        def win_mask(j, st, q0=q0):
            kpos, qpos = _positions_t(j, tk, q0, tq)
            return jnp.where((kpos <= qpos) & (kpos > qpos - WINDOW), st, NEG)

        values = lambda ref: lambda g, j: ref[0, rows_of(g), pl.ds(j * tk, tk)]
        n_end = (qi + 1) * (tq // tk)
        win_lo = max(n_end - (WINDOW + tq) // tk, 0)
        o = oc_ref[0, rows, :]
        for branch, args in ((1, (0, qi * (tq // tk), tq // tk, sel_scores, sel_mask, values(vst_ref))),
                             (2, (win_lo, 0, n_end - win_lo, win_scores, win_mask, values(vwt_ref)))):
            res = _stream(*args, st_ref, acc_ref)
            ot = jnp.concatenate([acc / l for l, acc in res], axis=0)
            gate = jnp.where(lane < HEAD_DIM, _gate_col(small, n, branch), _gate_col(small, NSA_HPG + n, branch))
            o = o + gate * ot.T
        o_ref[0, rows, :] = o.astype(BF16)


def _nsa_flash(main, sel, oc, small, hot, vt, tq, tk, init):
    b, t, _ = main.shape
    tc = TILES_PER_CALL * tq
    tile = lambda i, ch: pl.BlockSpec((1, tc, LANES), lambda bi, n: (bi, i, ch + n))
    shared = lambda i: pl.BlockSpec((1, tc, LANES), lambda bi, n: (bi, i, 0))
    full = lambda i, ch: pl.BlockSpec((1, (i + 1) * tc, LANES), lambda bi, n: (bi, 0, ch))
    vfull = lambda i, ch: pl.BlockSpec((1, LANES, (i + 1) * tc), lambda bi, n: (bi, ch, 0))
    in_specs = lambda i: [tile(i, CH_NQ), shared(i), tile(i, 0), shared(i), full(i, CH_KS), full(i, CH_KW),
                          pl.BlockSpec(((i + 1) * tc, LANES), lambda bi, n: (0, 0)),
                          vfull(i, VT_VS), vfull(i, VT_VW)]
    return _per_query_tile(
        partial(_nsa_flash_kernel, n_sub=TILES_PER_CALL, tq=tq, tk=tk), t // tc, (b, NSA_HPG), in_specs,
        lambda i: tile(i, 0),
        jax.ShapeDtypeStruct((b, t, NSA_W), BF16), _stream_scratch(NSA_KV_GROUPS, HEAD_DIM, tq, tk), "nsa_flash",
        (main, sel, oc, small, main, main, hot, vt, vt), init)


def _diff_kernel(q_ref, k_ref, vt_ref, lam_ref, g_ref, *rest, i, n_sub, tq, tk, lam_init):
    o_ref, st_ref, acc_ref = rest[-3:]
    lp = lam_ref[...]
    lam = (jnp.exp(jnp.sum(lp[0:1] * lp[1:2], axis=-1, keepdims=True))
           - jnp.exp(jnp.sum(lp[2:3] * lp[3:4], axis=-1, keepdims=True)) + lam_init)
    for s in range(n_sub):
        qi, rows = i * n_sub + s, slice(s * tq, (s + 1) * tq)
        q2 = q_ref[0, rows, :]
        qhs = [_half_mask(q2, comp) for comp in range(2)]

        def scores(j, qhs=qhs):
            k2 = k_ref[0, pl.ds(j * tk, tk), :]
            return tuple(_dot_nt(k2, qhs[comp]) for comp in range(2))

        def mask(j, st, qi=qi):
            kpos, qpos = _positions_t(j, tk, qi * tq, tq)
            return jnp.where(kpos <= qpos, st, NEG)

        def values(comp, j):
            return vt_ref[0, :, pl.ds(j * tk, tk)]

        (l1, acc1), (l2, acc2) = _stream(0, qi * (tq // tk), tq // tk, scores, mask, values, st_ref, acc_ref)
        o = (acc1 / l1 - lam * (acc2 / l2)).T
        o_ref[0, rows, :] = (_rms(o, g_ref[...]) * (1.0 - lam_init)).astype(BF16)


def _diff_attention(main, vt, lam_p, subln_g, tq, tk, lam_init, init):
    b, t, _ = main.shape
    nh = DIFF_HEADS
    tc = TILES_PER_CALL * tq
    in_specs = lambda i: [pl.BlockSpec((1, tc, LANES), lambda bi, h: (bi, i, h)),
                          pl.BlockSpec((1, (i + 1) * tc, LANES), lambda bi, h: (bi, 0, nh + h)),
                          pl.BlockSpec((1, LANES, (i + 1) * tc), lambda bi, h: (bi, h, 0)),
                          pl.BlockSpec((4, LANES), lambda bi, h: (0, 0)),
                          pl.BlockSpec((1, LANES), lambda bi, h: (0, 0))]
    return _per_query_tile(
        partial(_diff_kernel, n_sub=TILES_PER_CALL, tq=tq, tk=tk, lam_init=lam_init), t // tc, (b, nh), in_specs,
        lambda i: pl.BlockSpec((1, tc, LANES), lambda bi, h: (bi, i, h)),
        jax.ShapeDtypeStruct((b, t, nh * LANES), BF16), _stream_scratch(2, LANES, tq, tk), "diff_attention",
        (main, main, vt, lam_p, subln_g), init)


def _mem_kv_kernel(mem_ref, g_ref, wk_ref, wv_ref, k_ref, vt_ref):
    mn = _rms(mem_ref[0], g_ref[0]).astype(BF16)
    k_ref[0, 0] = _dot(mn, wk_ref[0]).astype(BF16)
    vt_ref[0, 0] = _dot(mn, wv_ref[0]).T.astype(BF16)


def _mem_kv(mem, mem_norm_g, wk, wv):
    depth = wk.shape[0]
    b, m, d = mem.shape
    wspec = pl.BlockSpec((1, d, MEM_W), lambda l, bi: (l, 0, 0))
    return pl.pallas_call(
        _mem_kv_kernel,
        grid=(depth, b),
        in_specs=[pl.BlockSpec((1, m, d), lambda l, bi: (bi, 0, 0)),
                  pl.BlockSpec((1, 1, d), lambda l, bi: (l, 0, 0)), wspec, wspec],
        out_specs=[pl.BlockSpec((1, 1, m, MEM_W), lambda l, bi: (l, bi, 0, 0)),
                   pl.BlockSpec((1, 1, MEM_W, m), lambda l, bi: (l, bi, 0, 0))],
        out_shape=[jax.ShapeDtypeStruct((depth, b, m, MEM_W), BF16),
                   jax.ShapeDtypeStruct((depth, b, MEM_W, m), BF16)],
        compiler_params=_cparams(("parallel", "parallel")),
        name="mem_kv",
    )(mem, mem_norm_g.reshape(depth, 1, d), wk, wv)


def _post_mixer_kernel(*refs, n_in, n_part):
    a_refs, w_refs = refs[:n_in], refs[n_in:2 * n_in]
    x_ref, g1_ref, g2_ref, g3_ref, wq_ref, k_ref, vt_ref, wo_ref, o_ref = refs[2 * n_in:]
    rows = x_ref.shape[0] // n_part
    parts = [slice(r * rows, (r + 1) * rows) for r in range(n_part)]
    n_mem = k_ref.shape[2]
    heads = [(ch, hh) for ch in range(MEM_W // LANES) for hh in range(2)]
    ones = jnp.ones((ONES_ROWS, n_mem), BF16)

    def mixer_out(ps):
        y = None
        for a_ref, w_ref in zip(a_refs, w_refs):
            t = _dot(a_ref[ps, :], w_ref[...])
            y = t if y is None else y + t
        return y

    def scores(q):
        return [_dot_nt(k_ref[0, 0, :, ch * LANES:(ch + 1) * LANES], _half_mask(q[:, ch * LANES:(ch + 1) * LANES], hh))
                for ch, hh in heads]

    def attend(logits):
        outs = []
        for (ch, hh), st in zip(heads, logits):
            p = jnp.exp2(st - jnp.max(st, axis=0, keepdims=True)).astype(BF16)
            r0 = ch * LANES + hh * HEAD_DIM
            pv = _dot(jnp.concatenate([vt_ref[0, 0, r0:r0 + HEAD_DIM, :], ones], axis=0), p)
            outs.append(pv[:HEAD_DIM] / pv[HEAD_DIM:HEAD_DIM + 1])
        return jnp.concatenate(outs, axis=0).T.astype(BF16)

    ys = [mixer_out(ps) for ps in parts]
    xs = [x_ref[ps, :] + _rms(y, g1_ref[...]) for ps, y in zip(parts, ys)]
    qs = [(_dot(_rms(x, g2_ref[...]).astype(BF16), wq_ref[...]) * Q_SCALE_LOG2).astype(BF16) for x in xs]
    logits = [scores(q) for q in qs]
    os_ = [attend(lg) for lg in logits]
    for ps, x, o in zip(parts, xs, os_):
        o_ref[ps, :] = x + _rms(_dot(o, wo_ref[...]), g3_ref[...])


def _post_mixer(acts, weights, x2, g1, g2, g3, wq_all, mem_k, mem_vt, wo_all, layer, tm, rows_per_batch, n_part=2):
    n = x2.shape[0]
    m = mem_k.shape[2]
    const = lambda shape: pl.BlockSpec(shape, lambda i: (0, 0))
    row = pl.BlockSpec((tm, D_MODEL), lambda i: (i, 0))
    gspec = const((1, D_MODEL))
    return pl.pallas_call(
        partial(_post_mixer_kernel, n_in=len(acts), n_part=n_part),
        grid=(n // tm,),
        in_specs=[pl.BlockSpec((tm, a.shape[1]), lambda i: (i, 0)) for a in acts]
                 + [const(w.shape) if l is None else _layer_spec(w, l) for w, l in weights]
                 + [row, gspec, gspec, gspec, _layer_spec(wq_all, layer),
                    pl.BlockSpec((1, 1, m, MEM_W), lambda i: (layer, i // rows_per_batch, 0, 0)),
                    pl.BlockSpec((1, 1, MEM_W, m), lambda i: (layer, i // rows_per_batch, 0, 0)),
                    _layer_spec(wo_all, layer)],
        out_specs=row,
        out_shape=jax.ShapeDtypeStruct((n, D_MODEL), F32),
        compiler_params=_cparams(("parallel",)),
        name="post_mixer",
    )(*acts, *[w for w, _ in weights], x2, g1, g2, g3, wq_all, mem_k, mem_vt, wo_all)


def _ffn_kernel(x_ref, gin_ref, gout_ref, wg_ref, wu_ref, wd_ref, o_ref, *, chunk):
    x = x_ref[...]
    h = _rms(x, gin_ref[...]).astype(BF16)
    d_ff = wg_ref.shape[1]
    y = None
    for c0 in range(0, d_ff, chunk):
        gate = _dot(h, wg_ref[:, c0:c0 + chunk])
        up = _dot(h, wu_ref[:, c0:c0 + chunk])
        a = (gate * (1.0 / (1.0 + jnp.exp(-gate))) * up).astype(BF16)
        t = _dot(a, wd_ref[c0:c0 + chunk, :])
        y = t if y is None else y + t
    o_ref[...] = x + _rms(y, gout_ref[...])


def _ffn(x2, gin, gout, wg_all, wu_all, wd_all, layer, tm, chunk):
    n = x2.shape[0]
    const = lambda shape: pl.BlockSpec(shape, lambda i: (0, 0), pipeline_mode=pl.Buffered(1))
    wspec = lambda w: _layer_spec(w, layer, pipeline_mode=pl.Buffered(1))
    return pl.pallas_call(
        partial(_ffn_kernel, chunk=chunk),
        grid=(n // tm,),
        in_specs=[pl.BlockSpec((tm, D_MODEL), lambda i: (i, 0)), const((1, D_MODEL)), const((1, D_MODEL)),
                  wspec(wg_all), wspec(wu_all), wspec(wd_all)],
        out_specs=pl.BlockSpec((tm, D_MODEL), lambda i: (i, 0)),
        out_shape=jax.ShapeDtypeStruct((n, D_MODEL), F32),
        compiler_params=_cparams(("parallel",)),
        name="ffn",
    )(x2, gin, gout, wg_all, wu_all, wd_all)


def _even_weights(w_in, w_out):
    kvw = NSA_KV_GROUPS * HEAD_DIM
    offs = [int(o) for o in np.cumsum((FOX_W, FOX_W, FOX_W, FOX_HEADS, NSA_W) + (kvw,) * 6 + (3 * NSA_HEADS,))]
    fk0, fv0, fl0, nq0, kc0, vc0, ks0, vs0, kw0, vw0, gl0, end = offs
    paired = [HEAD_DIM * (g * NSA_HPG + n) for n in range(NSA_HPG) for g in range(NSA_KV_GROUPS)]
    cols = lambda a, b: w_in[:, a:b]
    w = jnp.concatenate([cols(0, fv0)] + [cols(nq0 + h, nq0 + h + HEAD_DIM) for h in paired]
                        + [cols(ks0, vs0), cols(kw0, vw0), cols(kc0, vc0), cols(vc0, ks0),
                           cols(fv0, fl0), cols(vs0, kw0), cols(vw0, gl0)], axis=1).astype(BF16)
    w_small = jnp.concatenate([cols(fl0, nq0), cols(gl0, end),
                               jnp.zeros((D_MODEL, LANES - FOX_HEADS - 3 * NSA_HEADS), w_in.dtype)],
                              axis=1).astype(BF16)
    w_out_fox = w_out[:FOX_W].astype(BF16)
    w_out_nsa = jnp.concatenate([w_out[FOX_W + h:FOX_W + h + HEAD_DIM] for h in paired], axis=0).astype(BF16)
    return w, w_small, w_out_fox, w_out_nsa


def _overlap_matrix_t(t, ncp):
    nc = (t - CMP_BLOCK) // CMP_STRIDE + 1
    ns = t // SLC_BLOCK
    cs = np.arange(nc) * CMP_STRIDE
    ss = np.arange(ns) * SLC_BLOCK
    ov = np.clip(np.minimum(cs[:, None] + CMP_BLOCK, ss[None, :] + SLC_BLOCK)
                 - np.maximum(cs[:, None], ss[None, :]), 0, None) / CMP_BLOCK
    full = np.zeros((LANES // NSA_KV_GROUPS, ncp), np.float32)
    full[:ns, :nc] = ov.T
    return jnp.asarray(full, BF16), ns


def _compress_weights(pos_k, w1_k, w2_k, pos_v, w1_v, w2_v):
    g = NSA_KV_GROUPS
    pos2, w1bd, w2bd = [], [], []
    for pos, w1, w2 in ((pos_k, w1_k, w2_k), (pos_v, w1_v, w2_v)):
        pos2.append(jnp.tile(pos.astype(F32), (1, g)))
        w1l = w1.reshape(CMP_BLOCK, HEAD_DIM, CMP_HIDDEN).astype(BF16)
        bd = jnp.zeros((CMP_BLOCK, g * HEAD_DIM, g * CMP_HIDDEN), BF16)
        w2g = jnp.zeros((g * CMP_HIDDEN, g * HEAD_DIM), BF16)
        for gi in range(g):
            bd = bd.at[:, gi * HEAD_DIM:(gi + 1) * HEAD_DIM, gi * CMP_HIDDEN:(gi + 1) * CMP_HIDDEN].set(w1l)
            w2g = w2g.at[gi * CMP_HIDDEN:(gi + 1) * CMP_HIDDEN, gi * HEAD_DIM:(gi + 1) * HEAD_DIM].set(
                w2.astype(BF16))
        w1bd.append(bd)
        w2bd.append(w2g)
    return jnp.stack(pos2), jnp.stack(w1bd), jnp.stack(w2bd)


def kernel(x, mem, positions, sandwich_g, mem_norm_g, ev_w_in, ev_fox_fbias, ev_cmp_pos_k, ev_cmp_w1_k, ev_cmp_w2_k, ev_cmp_pos_v, ev_cmp_w1_v, ev_cmp_w2_v, ev_w_out, od_w_in, od_lambda, od_subln_g, od_w_out, ca_wq, ca_wk, ca_wv, ca_wo, ffn_wg, ffn_wu, ffn_wd):
    b, t, d = x.shape
    depth = sandwich_g.shape[0]
    n = b * t
    tm = 512
    tm_ffn = 512
    tm_mix = 1024
    tq, tk = 512, 256
    tk_diff = 512
    assert d == D_MODEL and t % tq == 0 and tq % tk == 0 and WINDOW % tk == 0 and t % (TILES_PER_CALL * tq) == 0 and t % tm == 0 and n % tm_ffn == 0

    tabs = tuple(a.reshape(b, t, LANES) for a in _rope_tables(positions, 512))
    mem_k, mem_vt = _mem_kv(mem, mem_norm_g, ca_wk.astype(BF16), ca_wv.astype(BF16))
    ncp = t // CMP_STRIDE
    overlap_t, ns = _overlap_matrix_t(t, ncp)
    assert ns <= overlap_t.shape[0]
    n_sel = min(SLC_TOPK, ns)
    hot = jnp.asarray((np.arange(t)[:, None] // SLC_BLOCK == np.arange(LANES)[None, :] % overlap_t.shape[0])
                      .astype(np.float32), BF16)
    gain = lambda l, j: sandwich_g[l, j].reshape(1, d)

    bf = lambda w: w.astype(BF16)
    od_w_in_b, od_w_out_b, ca_wq_b, ca_wo_b = bf(od_w_in), bf(od_w_out), bf(ca_wq), bf(ca_wo)
    ffn_wg_b, ffn_wu_b, ffn_wd_b = bf(ffn_wg), bf(ffn_wu), bf(ffn_wd)
    x2 = x.reshape(n, d)
    dead = {}
    for layer in range(depth):
        x3 = x2.reshape(b, t, d)
        if layer % 2 == 0:
            e = layer // 2
            w, w_small, w_out_fox, w_out_nsa = _even_weights(ev_w_in[e], ev_w_out[e])
            fb_row = jnp.zeros((1, LANES), F32).at[0, :FOX_HEADS].set(ev_fox_fbias[e].astype(F32))
            main, cmp_in, vt, small = _even_proj(x3, gain(layer, 0), w, w_small, fb_row, tabs, tm)
            aq, ak = _fox_aug(small, 512)
            o_fox = dead["fox"] = _fox_attention(main, aq, ak, vt, tq, tk, dead.get("fox"))
            kc, vct = _compress(cmp_in, *_compress_weights(
                ev_cmp_pos_k[e], ev_cmp_w1_k[e], ev_cmp_w2_k[e], ev_cmp_pos_v[e], ev_cmp_w1_v[e], ev_cmp_w2_v[e]))
            oc, sel = dead["sel"] = _nsa_select(main, kc, vct, overlap_t, small, 256, ns, n_sel, dead.get("sel"))
            o_nsa = dead["nsa"] = _nsa_flash(main, sel, oc, small, hot, vt, tq, tk, dead.get("nsa"))
            acts, w_outs = [o_fox.reshape(n, FOX_W), o_nsa.reshape(n, NSA_W)], [(w_out_fox, None), (w_out_nsa, None)]
        else:
            o = layer // 2
            main, vt = _odd_proj(x3, gain(layer, 0), od_w_in_b, o, tabs, tm)
            lam_init = 0.8 - 0.6 * math.exp(-0.3 * layer)
            lam_p = jnp.pad(od_lambda[o].astype(F32), ((0, 0), (0, LANES - HEAD_DIM)))
            attn = dead["diff"] = _diff_attention(main, vt, lam_p, od_subln_g[o].reshape(1, LANES).astype(F32),
                                                  tq, tk_diff, lam_init, dead.get("diff"))
            acts, w_outs = [attn.reshape(n, D_MODEL)], [(od_w_out_b, o)]
        x2 = _post_mixer(acts, w_outs, x2, gain(layer, 1), gain(layer, 2), gain(layer, 3),
                         ca_wq_b, mem_k, mem_vt, ca_wo_b, layer, tm_mix, t // tm_mix, n_part=tm_mix // 256)
        x2 = _ffn(x2, gain(layer, 4), gain(layer, 5), ffn_wg_b, ffn_wu_b, ffn_wd_b, layer, tm_ffn, 256)
    return x2.reshape(b, t, d)
```

```python
import math
from functools import partial

import numpy as np
import jax
import jax.numpy as jnp
from jax import lax
from jax.experimental import pallas as pl
from jax.experimental.pallas import tpu as pltpu

F32 = jnp.float32
BF16 = jnp.bfloat16

D_MODEL = 1024
HEAD_DIM = 64
LANES = 128
ROPE_DIM = HEAD_DIM // 4
ROPE_THETA = 500000.0
FOX_HEADS = 8
NSA_HEADS = 8
NSA_KV_GROUPS = 2
NSA_HPG = NSA_HEADS // NSA_KV_GROUPS
CMP_BLOCK = 32
CMP_STRIDE = 16
CMP_HIDDEN = 2 * HEAD_DIM
SLC_BLOCK = 64
SLC_TOPK = 16
WINDOW = 512
DIFF_HEADS = 8
MEM_HEADS = 4
MEM_W = MEM_HEADS * HEAD_DIM
RMS_EPS = 1e-6
Q_SCALE = HEAD_DIM ** -0.5
LOG2E = math.log2(math.e)
Q_SCALE_LOG2 = Q_SCALE * LOG2E
NEG = -1e30
AUG_PER_HEAD = 6
ONES_ROWS = 16

FOX_W = FOX_HEADS * HEAD_DIM
NSA_W = NSA_HEADS * HEAD_DIM
CH_FQ, CH_FK, CH_NQ, CH_KS, CH_KW = 0, 4, 8, 12, 13
EV_MAIN = 14 * LANES
EV_CMP = 2 * LANES
VT_FV, VT_VS, VT_VW = 0, 4, 5
EV_VT = 6 * LANES
EV_ROPE_CHUNKS = tuple(range(CH_NQ, CH_NQ + 4)) + (CH_KS, CH_KW, EV_MAIN // LANES)
EV_QSCALE_CHUNKS = tuple(range(CH_FQ, CH_FQ + 4)) + tuple(range(CH_NQ, CH_NQ + 4))
OD_MAIN = 2 * D_MODEL
OD_VT = D_MODEL

VMEM_LIMIT = 56 * 1024 * 1024


def _layer_spec(stacked, layer, **kw):
    zeros = (0,) * (stacked.ndim - 1)
    return pl.BlockSpec((None,) + stacked.shape[1:], lambda *idx: (layer,) + zeros, **kw)


def _cparams(sem):
    return pltpu.CompilerParams(dimension_semantics=sem, vmem_limit_bytes=VMEM_LIMIT)


def _rms(x, g):
    return x * lax.rsqrt(jnp.mean(x * x, axis=-1, keepdims=True) + RMS_EPS) * g


def _split3(x):
    hi = x.astype(BF16)
    r1 = x - hi.astype(F32)
    mid = r1.astype(BF16)
    lo = (r1 - mid.astype(F32)).astype(BF16)
    return hi, mid, lo


def _dot(a, b):
    return jnp.dot(a, b, preferred_element_type=F32)


def _dot_nt(a, b):
    return lax.dot_general(a, b, (((1,), (1,)), ((), ())), preferred_element_type=F32)


def _lane_iota(n=LANES):
    return lax.broadcasted_iota(jnp.int32, (1, n), 1)


def _half_mask(q2, half):
    return jnp.where(_lane_iota() // HEAD_DIM == half, q2, jnp.zeros_like(q2))


def _transposed(x):
    return x.astype(F32).T.astype(BF16)


def _half_mask_t(qt, half):
    row = lax.broadcasted_iota(jnp.int32, (qt.shape[0], 1), 0)
    return jnp.where(row // HEAD_DIM == half, qt, jnp.zeros_like(qt))


def _positions_t(j, tk, q0, tq):
    kpos = j * tk + lax.broadcasted_iota(jnp.int32, (tk, 1), 0)
    qpos = q0 + lax.broadcasted_iota(jnp.int32, (1, tq), 1)
    return kpos, qpos


def _stream(base, n_full, n_tail, scores, mask, values, st_ref, acc_ref):
    n_chain, dv, tq = acc_ref.shape

    def park(j, slot):
        for c, st in enumerate(scores(j)):
            st_ref[slot, c] = st

    def step(j, slot, stats, masked, prefetch):
        if prefetch:
            park(j + 1, 1 - slot)
        parts = []
        for c in range(n_chain):
            st = st_ref[slot, c]
            if masked:
                st = mask(j, st)
            m, _ = stats[c]
            m_new = jnp.maximum(m, jnp.max(st, axis=0, keepdims=True))
            p = jnp.exp2(st - m_new).astype(BF16)
            vt1 = jnp.concatenate([values(c, j), jnp.ones((ONES_ROWS, p.shape[0]), BF16)], axis=0)
            parts.append((m_new, jnp.exp2(m - m_new), _dot(vt1, p)))
        out = []
        for c, (m_new, alpha, pv) in enumerate(parts):
            acc_ref[c] = alpha * acc_ref[c] + pv[:dv]
            out.append((m_new, alpha * stats[c][1] + pv[dv:dv + 1]))
        return tuple(out)

    acc_ref[...] = jnp.zeros_like(acc_ref)
    park(base, 0)
    stats = ((jnp.full((1, tq), NEG, F32), jnp.zeros((1, tq), F32)),) * n_chain
    n_tiles = n_full + n_tail
    for s_ in range(n_tiles):
        stats = step(base + s_, s_ % 2, stats, s_ >= n_full, s_ + 1 < n_tiles)
    return [(l, acc_ref[c]) for c, (_, l) in enumerate(stats)]


TILES_PER_CALL = 4


def _per_query_tile(kernel_fn, n_q, grid, in_specs_fn, out_spec_fn, out_shape, scratch, name, args, init):
    out = jnp.zeros(out_shape.shape, out_shape.dtype) if init is None else init
    for i in range(n_q):
        in_specs = list(in_specs_fn(i)) + [pl.BlockSpec(memory_space=pl.ANY)]
        out = pl.pallas_call(
            partial(kernel_fn, i=i), grid=grid, in_specs=in_specs, out_specs=out_spec_fn(i), out_shape=out_shape,
            scratch_shapes=scratch, input_output_aliases={len(args): 0},
            compiler_params=_cparams(("parallel",) * len(grid)), name=f"{name}_q{i}",
        )(*args, out)
    return out


def _stream_scratch(n_chain, dv, tq, tk):
    return [pltpu.VMEM((2, n_chain, tk, tq), F32), pltpu.VMEM((n_chain, dv, tq), F32)]


def _rope_kernel(pos_ref, inv_ref, m1_ref, m2_ref, c_ref, s1_ref, s2_ref):
    ang = pos_ref[...].astype(F32) * inv_ref[...]
    c_ref[...] = jnp.cos(ang)
    sn = jnp.sin(ang)
    s1_ref[...] = -sn * m1_ref[...]
    s2_ref[...] = sn * m2_ref[...]


def _rope_tables(positions, tm):
    n = positions.size
    inv = ROPE_THETA ** (-jnp.arange(0, ROPE_DIM, 2, dtype=F32) / ROPE_DIM)
    lane = np.arange(LANES) % HEAD_DIM
    half = ROPE_DIM // 2
    inv_l = jnp.where(lane < ROPE_DIM, inv[lane % half], 0.0).reshape(1, LANES).astype(F32)
    m1 = jnp.asarray((lane < half).astype(np.float32)).reshape(1, LANES)
    m2 = jnp.asarray(((lane >= half) & (lane < ROPE_DIM)).astype(np.float32)).reshape(1, LANES)
    row = pl.BlockSpec((1, LANES), lambda i: (0, 0))
    tab = pl.BlockSpec((tm, LANES), lambda i: (i, 0))
    return pl.pallas_call(
        _rope_kernel,
        grid=(n // tm,),
        in_specs=[pl.BlockSpec((tm, 1), lambda i: (i, 0)), row, row, row],
        out_specs=[tab, tab, tab],
        out_shape=[jax.ShapeDtypeStruct((n, LANES), F32)] * 3,
        compiler_params=_cparams(("parallel",)),
        name="rope_tables",
    )(positions.reshape(n, 1), inv_l, m1, m2)


def _apply_rope(y, c, s1, s2):
    half = ROPE_DIM // 2
    return y * c + pltpu.roll(y, LANES - half, 1) * s1 + pltpu.roll(y, half, 1) * s2


def _project_chunks(h, w_ref, tabs, dests, rope_chunks, qscale_chunks):
    c, s1, s2 = tabs
    wide = 2 * LANES
    where = [(kind, ref, k) for kind, ref, n in dests for k in range(n)]
    for ch2 in range(len(where) // 2):
        y2 = _dot(h, w_ref[:, ch2 * wide:(ch2 + 1) * wide])
        for ch in (2 * ch2, 2 * ch2 + 1):
            y = y2[:, (ch % 2) * LANES:(ch % 2 + 1) * LANES]
            if ch in rope_chunks:
                y = _apply_rope(y, c, s1, s2)
            if ch in qscale_chunks:
                y = y * Q_SCALE_LOG2
            kind, ref, k = where[ch]
            if kind == "cols":
                ref[0, k * LANES:(k + 1) * LANES, :] = y.T.astype(BF16)
            elif kind == "rows_f32":
                ref[0, k] = y
            else:
                ref[0, :, k * LANES:(k + 1) * LANES] = y.astype(BF16)


def _even_proj_kernel(x_ref, g_ref, w_ref, ws_ref, fb_ref, c_ref, s1_ref, s2_ref,
                      main_ref, cmp_ref, vt_ref, small_ref):
    h = _rms(x_ref[0], g_ref[...]).astype(BF16)
    dests = [("rows", main_ref, EV_MAIN // LANES), ("rows_f32", cmp_ref, EV_CMP // LANES),
             ("cols", vt_ref, EV_VT // LANES)]
    _project_chunks(h, w_ref, (c_ref[0], s1_ref[0], s2_ref[0]), dests, EV_ROPE_CHUNKS, EV_QSCALE_CHUNKS)
    ys = _dot(h, ws_ref[...])
    z = ys + fb_ref[...]
    log_f = jnp.minimum(z, 0.0) - jnp.log(1.0 + jnp.exp(-jnp.abs(z)))
    gate = 1.0 / (1.0 + jnp.exp(-ys))
    small_ref[0] = jnp.where(_lane_iota() < FOX_HEADS, log_f, gate)


def _even_proj(x3, g, w, w_small, fb_row, tabs, tm):
    b, t, _ = x3.shape
    const = lambda shape: pl.BlockSpec(shape, lambda bi, i: (0, 0))
    tab = pl.BlockSpec((1, tm, LANES), lambda bi, i: (bi, i, 0))
    return pl.pallas_call(
        _even_proj_kernel,
        grid=(b, t // tm),
        in_specs=[pl.BlockSpec((1, tm, D_MODEL), lambda bi, i: (bi, i, 0)), const((1, D_MODEL)),
                  const((D_MODEL, EV_MAIN + EV_CMP + EV_VT)), const((D_MODEL, LANES)), const((1, LANES)),
                  tab, tab, tab],
        out_specs=[pl.BlockSpec((1, tm, EV_MAIN), lambda bi, i: (bi, i, 0)),
                   pl.BlockSpec((1, EV_CMP // LANES, tm, LANES), lambda bi, i: (bi, 0, i, 0)),
                   pl.BlockSpec((1, EV_VT, tm), lambda bi, i: (bi, 0, i)), tab],
        out_shape=[jax.ShapeDtypeStruct((b, t, EV_MAIN), BF16),
                   jax.ShapeDtypeStruct((b, EV_CMP // LANES, t, LANES), F32),
                   jax.ShapeDtypeStruct((b, EV_VT, t), BF16), jax.ShapeDtypeStruct((b, t, LANES), F32)],
        compiler_params=_cparams(("parallel", "parallel")),
        name="even_proj",
    )(x3, g, w, w_small, fb_row, *tabs)


def _odd_proj_kernel(x_ref, g_ref, w_ref, c_ref, s1_ref, s2_ref, main_ref, vt_ref):
    h = _rms(x_ref[0], g_ref[...]).astype(BF16)
    n_main = OD_MAIN // LANES
    dests = [("rows", main_ref, n_main), ("cols", vt_ref, OD_VT // LANES)]
    _project_chunks(h, w_ref, (c_ref[0], s1_ref[0], s2_ref[0]), dests, tuple(range(n_main)),
                    tuple(range(n_main // 2)))


def _odd_proj(x3, g, w_all, layer, tabs, tm):
    b, t, _ = x3.shape
    const = lambda shape: pl.BlockSpec(shape, lambda bi, i: (0, 0))
    tab = pl.BlockSpec((1, tm, LANES), lambda bi, i: (bi, i, 0))
    return pl.pallas_call(
        _odd_proj_kernel,
        grid=(b, t // tm),
        in_specs=[pl.BlockSpec((1, tm, D_MODEL), lambda bi, i: (bi, i, 0)), const((1, D_MODEL)),
                  _layer_spec(w_all, layer), tab, tab, tab],
        out_specs=[pl.BlockSpec((1, tm, OD_MAIN), lambda bi, i: (bi, i, 0)),
                   pl.BlockSpec((1, OD_VT, tm), lambda bi, i: (bi, 0, i))],
        out_shape=[jax.ShapeDtypeStruct((b, t, OD_MAIN), BF16), jax.ShapeDtypeStruct((b, OD_VT, t), BF16)],
        compiler_params=_cparams(("parallel", "parallel")),
        name="odd_proj",
    )(x3, g, w_all, *tabs)


def _fox_aug_kernel(lf_ref, tril_ref, e_ref, one_ref, aq_ref, ak_ref, carry_ref):
    @pl.when(pl.program_id(1) == 0)
    def _():
        carry_ref[...] = jnp.zeros_like(carry_ref)

    tril = tril_ref[...]
    sub = tril.shape[0]
    carry = carry_ref[...]
    blocks = []
    for r0 in range(0, lf_ref.shape[1], sub):
        c = carry
        for piece in _split3(lf_ref[0, r0:r0 + sub, :]):
            c = c + _dot(tril, piece)
        blocks.append(c)
        carry = c[-1:, :]
    carry_ref[...] = carry
    aug = one_ref[...]
    for r, piece in enumerate(_split3(jnp.concatenate(blocks, axis=0) * LOG2E)):
        aug = aug + _dot(piece, e_ref[r])
    aq_ref[0] = aug[:, :LANES].astype(BF16)
    ak_ref[0] = aug[:, LANES:].astype(BF16)


def _fox_aug(small, tc, sub=128):
    b, t, _ = small.shape
    tril = jnp.asarray(np.tril(np.ones((sub, sub), np.float32)), BF16)
    spread = np.zeros((3, LANES, 2 * LANES), np.float32)
    ones = np.zeros((1, 2 * LANES), np.float32)
    for h in range(FOX_HEADS):
        for r in range(3):
            spread[r, h, AUG_PER_HEAD * h + r] = 1.0
            spread[r, h, LANES + AUG_PER_HEAD * h + 3 + r] = -1.0
            ones[0, AUG_PER_HEAD * h + 3 + r] = 1.0
            ones[0, LANES + AUG_PER_HEAD * h + r] = 1.0
    const2 = lambda shape: pl.BlockSpec(shape, lambda bi, i: (0,) * len(shape))
    blk = pl.BlockSpec((1, tc, LANES), lambda bi, i: (bi, i, 0))
    return pl.pallas_call(
        _fox_aug_kernel,
        grid=(b, t // tc),
        in_specs=[blk, const2((sub, sub)), const2((3, LANES, 2 * LANES)), const2((1, 2 * LANES))],
        out_specs=[blk, blk],
        out_shape=[jax.ShapeDtypeStruct((b, t, LANES), BF16)] * 2,
        scratch_shapes=[pltpu.VMEM((1, LANES), F32)],
        compiler_params=_cparams(("parallel", "arbitrary")),
        name="fox_aug",
    )(small, tril, jnp.asarray(spread, BF16), jnp.asarray(ones))


def _fox_kernel(q_ref, aq_ref, k_ref, ak_ref, vt_ref, *rest, i, n_sub, tq, tk):
    o_ref, st_ref, acc_ref = rest[-3:]
    pair = pl.program_id(1)
    row = lax.broadcasted_iota(jnp.int32, (LANES, 1), 0)
    for s in range(n_sub):
        qi, rows = i * n_sub + s, slice(s * tq, (s + 1) * tq)
        qt = _transposed(q_ref[0, rows, :])
        qat = _transposed(aq_ref[0, rows, :])
        qcats = []
        for hh in range(2):
            head = 2 * pair + hh
            in_head = (row >= AUG_PER_HEAD * head) & (row < AUG_PER_HEAD * (head + 1))
            qcats.append(jnp.concatenate([_half_mask_t(qt, hh), jnp.where(in_head, qat, jnp.zeros_like(qat))],
                                         axis=0))

        def scores(j, qcats=qcats):
            ks = j * tk
            kcat = jnp.concatenate([k_ref[0, pl.ds(ks, tk), :], ak_ref[0, pl.ds(ks, tk), :]], axis=1)
            return tuple(_dot(kcat, qcats[hh]) for hh in range(2))

        def mask(j, st, qi=qi):
            kpos, qpos = _positions_t(j, tk, qi * tq, tq)
            return jnp.where(kpos <= qpos, st, NEG)

        def values(hh, j):
            return vt_ref[0, hh * HEAD_DIM:(hh + 1) * HEAD_DIM, pl.ds(j * tk, tk)]

        res = _stream(0, qi * (tq // tk), tq // tk, scores, mask, values, st_ref, acc_ref)
        ot = jnp.concatenate([acc / l for l, acc in res], axis=0)
        o_ref[0, rows, :] = ot.T.astype(BF16)


def _fox_attention(main, aq, ak, vt, tq, tk, init):
    b, t, _ = main.shape
    tc = TILES_PER_CALL * tq
    in_specs = lambda i: [pl.BlockSpec((1, tc, LANES), lambda bi, p: (bi, i, CH_FQ + p)),
                          pl.BlockSpec((1, tc, LANES), lambda bi, p: (bi, i, 0)),
                          pl.BlockSpec((1, (i + 1) * tc, LANES), lambda bi, p: (bi, 0, CH_FK + p)),
                          pl.BlockSpec((1, (i + 1) * tc, LANES), lambda bi, p: (bi, 0, 0)),
                          pl.BlockSpec((1, LANES, (i + 1) * tc), lambda bi, p: (bi, VT_FV + p, 0))]
    return _per_query_tile(
        partial(_fox_kernel, n_sub=TILES_PER_CALL, tq=tq, tk=tk), t // tc, (b, FOX_HEADS // 2), in_specs,
        lambda i: pl.BlockSpec((1, tc, LANES), lambda bi, p: (bi, i, p)),
        jax.ShapeDtypeStruct((b, t, FOX_W), BF16), _stream_scratch(2, HEAD_DIM, tq, tk), "fox_attention",
        (main, aq, main, ak, vt), init)


def _compress_kernel(x_ref, pos_ref, w1_ref, w2_ref, kc_ref, vct_ref):
    t = x_ref.shape[2]
    nchunk = t // CMP_STRIDE
    for kv, o_ref in enumerate((kc_ref, vct_ref)):
        first = second = None
        for l in range(CMP_STRIDE):
            xl = x_ref[0, kv, pl.ds(l, nchunk, stride=CMP_STRIDE), :]
            a = _dot((xl + pos_ref[kv, l:l + 1, :]).astype(BF16), w1_ref[kv, l])
            b = _dot((xl + pos_ref[kv, CMP_STRIDE + l:CMP_STRIDE + l + 1, :]).astype(BF16),
                     w1_ref[kv, CMP_STRIDE + l])
            first = a if first is None else first + a
            second = b if second is None else second + b
        hid = first + pltpu.roll(second, nchunk - 1, 0)
        out = _dot(jax.nn.gelu(hid, approximate=True).astype(BF16), w2_ref[kv])
        o_ref[0] = (out.T if o_ref is vct_ref else out).astype(BF16)


def _compress(cmp_in, pos2, w1bd, w2bd):
    b, _, t, _ = cmp_in.shape
    nchunk = t // CMP_STRIDE
    const = lambda a: pl.BlockSpec(a.shape, lambda bi: (0,) * a.ndim)
    return pl.pallas_call(
        _compress_kernel,
        grid=(b,),
        in_specs=[pl.BlockSpec((1, EV_CMP // LANES, t, LANES), lambda bi: (bi, 0, 0, 0)),
                  const(pos2), const(w1bd), const(w2bd)],
        out_specs=[pl.BlockSpec((1, nchunk, LANES), lambda bi: (bi, 0, 0)),
                   pl.BlockSpec((1, LANES, nchunk), lambda bi: (bi, 0, 0))],
        out_shape=[jax.ShapeDtypeStruct((b, nchunk, LANES), BF16), jax.ShapeDtypeStruct((b, LANES, nchunk), BF16)],
        compiler_params=_cparams(("parallel",)),
        name="nsa_compress",
    )(cmp_in, pos2, w1bd, w2bd)


def _gate_col(small, head, branch):
    idx = FOX_HEADS + 3 * head + branch
    return jnp.sum(jnp.where(_lane_iota() == idx, small, 0.0), axis=-1, keepdims=True)


def _nsa_select_kernel(q_ref, kc_ref, vct_ref, ovt_ref, small_ref, *rest, q_lo, tq, ns, n_sel):
    oc_ref, sel_ref = rest[-2:]
    q0 = q_lo + pl.program_id(1) * tq
    lane = _lane_iota()
    kc = kc_ref[0]
    vct = vct_ref[0]
    ncp = kc.shape[0]
    small = small_ref[0]
    qpos = q0 + lax.broadcasted_iota(jnp.int32, (1, tq), 1)
    cmp_end = lax.broadcasted_iota(jnp.int32, (ncp, 1), 0) * CMP_STRIDE + (CMP_BLOCK - 1)
    cmask = cmp_end <= qpos
    psum = [jnp.zeros((ncp, tq), F32) for _ in range(NSA_KV_GROUPS)]
    logits = [[_dot_nt(kc, _half_mask(q_ref[0, :, n * LANES:(n + 1) * LANES], g)) for g in range(NSA_KV_GROUPS)]
              for n in range(NSA_HPG)]
    for n in range(NSA_HPG):
        ots = []
        for g in range(NSA_KV_GROUPS):
            z = jnp.where(cmask, logits[n][g], -jnp.inf)
            m = jnp.max(z, axis=0, keepdims=True)
            m = jnp.where(m == -jnp.inf, 0.0, m)
            p = jnp.exp2(z - m)
            p = p / jnp.maximum(jnp.sum(p, axis=0, keepdims=True), 1e-30)
            psum[g] = psum[g] + p
            ots.append(_dot(vct[g * HEAD_DIM:(g + 1) * HEAD_DIM], p.astype(BF16)))
        gate = jnp.where(lane < HEAD_DIM, _gate_col(small, n, 0), _gate_col(small, NSA_HPG + n, 0))
        oc_ref[0, :, n * LANES:(n + 1) * LANES] = gate * jnp.concatenate(ots, axis=0).T

    nsp = ovt_ref.shape[0]
    blk = lax.broadcasted_iota(jnp.int32, (nsp, 1), 0)
    cur = qpos // SLC_BLOCK
    valid = blk * SLC_BLOCK <= qpos
    forced = (blk == 0) | (blk == cur) | (blk == cur - 1)
    scores = []
    for g in range(NSA_KV_GROUPS):
        imp = jnp.zeros((nsp, tq), F32)
        for piece in _split3(psum[g]):
            imp = imp + _dot(ovt_ref[...], piece)
        scores.append(jnp.where(valid, jnp.where(forced, jnp.inf, imp), -jnp.inf))
    slab = 8
    masks = []
    for g in range(NSA_KV_GROUPS):
        slabs = [scores[g][r:r + slab] for r in range(0, nsp, slab)]
        ranks = [jnp.zeros((slab, tq), jnp.int32) for _ in slabs]
        for i in range(ns):
            row = scores[g][i:i + 1, :]
            for r, sl in enumerate(slabs):
                if slab * r >= ns:
                    continue
                if slab * r > i:
                    ahead = row >= sl
                elif slab * (r + 1) - 1 <= i:
                    ahead = row > sl
                else:
                    ahead = (row > sl) | ((row == sl) & (blk[slab * r:slab * (r + 1)] > i))
                ranks[r] = jnp.where(ahead, ranks[r] + 1, ranks[r])
        rank = jnp.concatenate(ranks, axis=0)
        masks.append(jnp.where((rank < n_sel) & (blk < ns), 0.0, NEG))
    sel_ref[0] = jnp.concatenate(masks, axis=0).T.astype(BF16)


def _nsa_select(main, kc, vct, overlap_t, small, tq, ns, n_sel, init, n_span=4):
    b, t, _ = main.shape
    nsp = overlap_t.shape[0]
    span = t // n_span
    steps = span // tq
    shapes = [jax.ShapeDtypeStruct((b, t, NSA_W), F32), jax.ShapeDtypeStruct((b, t, NSA_KV_GROUPS * nsp), BF16)]
    outs = [jnp.zeros(s.shape, s.dtype) for s in shapes] if init is None else init
    for k in range(n_span):
        q_hi = (k + 1) * span
        ncl = min(-(-(q_hi // CMP_STRIDE) // LANES) * LANES, kc.shape[1])
        row = lambda bi, i, k=k: (bi, k * steps + i, 0)
        outs = pl.pallas_call(
            partial(_nsa_select_kernel, q_lo=k * span, tq=tq, ns=min(ns, q_hi // SLC_BLOCK), n_sel=n_sel),
            grid=(b, steps),
            in_specs=[pl.BlockSpec((1, tq, NSA_W), lambda bi, i, k=k: (bi, k * steps + i, CH_NQ * LANES // NSA_W)),
                      pl.BlockSpec((1, ncl, LANES), lambda bi, i: (bi, 0, 0)),
                      pl.BlockSpec((1, LANES, ncl), lambda bi, i: (bi, 0, 0)),
                      pl.BlockSpec((nsp, ncl), lambda bi, i: (0, 0)),
                      pl.BlockSpec((1, tq, LANES), row),
                      pl.BlockSpec(memory_space=pl.ANY), pl.BlockSpec(memory_space=pl.ANY)],
            out_specs=[pl.BlockSpec((1, tq, NSA_W), row), pl.BlockSpec((1, tq, NSA_KV_GROUPS * nsp), row)],
            out_shape=shapes,
            input_output_aliases={5: 0, 6: 1},
            compiler_params=_cparams(("parallel", "parallel")),
            name=f"nsa_select_s{k}",
        )(main, kc, vct, overlap_t, small, *outs)
    return outs


def _nsa_flash_kernel(q_ref, sel_ref, oc_ref, small_ref, ks_ref, kw_ref, hot_ref, vst_ref, vwt_ref, *rest,
                      i, n_sub, tq, tk):
    o_ref, st_ref, acc_ref = rest[-3:]
    n = pl.program_id(1)
    lane = _lane_iota()
    rows_of = lambda g: slice(g * HEAD_DIM, (g + 1) * HEAD_DIM)
    for s in range(n_sub):
        qi, rows = i * n_sub + s, slice(s * tq, (s + 1) * tq)
        q0 = qi * tq
        small = small_ref[0, rows, :]
        qt = _transposed(q_ref[0, rows, :])
        selt = _transposed(sel_ref[0, rows, :])
        qhs = [_half_mask_t(qt, g) for g in range(NSA_KV_GROUPS)]
        qcats = [jnp.concatenate([qhs[g], _half_mask_t(selt, g)], axis=0) for g in range(NSA_KV_GROUPS)]

        def sel_scores(j, qcats=qcats):
            ks = j * tk
            kcat = jnp.concatenate([ks_ref[0, pl.ds(ks, tk), :], hot_ref[pl.ds(ks, tk), :]], axis=1)
            return tuple(_dot(kcat, qcats[g]) for g in range(NSA_KV_GROUPS))

        def sel_mask(j, st, q0=q0):
            kpos, qpos = _positions_t(j, tk, q0, tq)
            return jnp.where(kpos <= qpos, st, NEG)

        def win_scores(j, qhs=qhs):
            kw = kw_ref[0, pl.ds(j * tk, tk), :]
            return tuple(_dot(kw, qhs[g]) for g in range(NSA_KV_GROUPS))

        def win_mask(j, st, q0=q0):
            kpos, qpos = _positions_t(j, tk, q0, tq)
            return jnp.where((kpos <= qpos) & (kpos > qpos - WINDOW), st, NEG)

        values = lambda ref: lambda g, j: ref[0, rows_of(g), pl.ds(j * tk, tk)]
        n_end = (qi + 1) * (tq // tk)
        win_lo = max(n_end - (WINDOW + tq) // tk, 0)
        o = oc_ref[0, rows, :]
        for branch, args in ((1, (0, qi * (tq // tk), tq // tk, sel_scores, sel_mask, values(vst_ref))),
                             (2, (win_lo, 0, n_end - win_lo, win_scores, win_mask, values(vwt_ref)))):
            res = _stream(*args, st_ref, acc_ref)
            ot = jnp.concatenate([acc / l for l, acc in res], axis=0)
            gate = jnp.where(lane < HEAD_DIM, _gate_col(small, n, branch), _gate_col(small, NSA_HPG + n, branch))
            o = o + gate * ot.T
        o_ref[0, rows, :] = o.astype(BF16)


def _nsa_flash(main, sel, oc, small, hot, vt, tq, tk, init):
    b, t, _ = main.shape
    tc = TILES_PER_CALL * tq
    tile = lambda i, ch: pl.BlockSpec((1, tc, LANES), lambda bi, n: (bi, i, ch + n))
    shared = lambda i: pl.BlockSpec((1, tc, LANES), lambda bi, n: (bi, i, 0))
    full = lambda i, ch: pl.BlockSpec((1, (i + 1) * tc, LANES), lambda bi, n: (bi, 0, ch))
    vfull = lambda i, ch: pl.BlockSpec((1, LANES, (i + 1) * tc), lambda bi, n: (bi, ch, 0))
    in_specs = lambda i: [tile(i, CH_NQ), shared(i), tile(i, 0), shared(i), full(i, CH_KS), full(i, CH_KW),
                          pl.BlockSpec(((i + 1) * tc, LANES), lambda bi, n: (0, 0)),
                          vfull(i, VT_VS), vfull(i, VT_VW)]
    return _per_query_tile(
        partial(_nsa_flash_kernel, n_sub=TILES_PER_CALL, tq=tq, tk=tk), t // tc, (b, NSA_HPG), in_specs,
        lambda i: tile(i, 0),
        jax.ShapeDtypeStruct((b, t, NSA_W), BF16), _stream_scratch(NSA_KV_GROUPS, HEAD_DIM, tq, tk), "nsa_flash",
        (main, sel, oc, small, main, main, hot, vt, vt), init)


def _diff_kernel(q_ref, k_ref, vt_ref, lam_ref, g_ref, *rest, i, n_sub, tq, tk, lam_init):
    o_ref, st_ref, acc_ref = rest[-3:]
    lp = lam_ref[...]
    lam = (jnp.exp(jnp.sum(lp[0:1] * lp[1:2], axis=-1, keepdims=True))
           - jnp.exp(jnp.sum(lp[2:3] * lp[3:4], axis=-1, keepdims=True)) + lam_init)
    for s in range(n_sub):
        qi, rows = i * n_sub + s, slice(s * tq, (s + 1) * tq)
        q2 = q_ref[0, rows, :]
        qhs = [_half_mask(q2, comp) for comp in range(2)]

        def scores(j, qhs=qhs):
            k2 = k_ref[0, pl.ds(j * tk, tk), :]
            return tuple(_dot_nt(k2, qhs[comp]) for comp in range(2))

        def mask(j, st, qi=qi):
            kpos, qpos = _positions_t(j, tk, qi * tq, tq)
            return jnp.where(kpos <= qpos, st, NEG)

        def values(comp, j):
            return vt_ref[0, :, pl.ds(j * tk, tk)]

        (l1, acc1), (l2, acc2) = _stream(0, qi * (tq // tk), tq // tk, scores, mask, values, st_ref, acc_ref)
        o = (acc1 / l1 - lam * (acc2 / l2)).T
        o_ref[0, rows, :] = (_rms(o, g_ref[...]) * (1.0 - lam_init)).astype(BF16)


def _diff_attention(main, vt, lam_p, subln_g, tq, tk, lam_init, init):
    b, t, _ = main.shape
    nh = DIFF_HEADS
    tc = TILES_PER_CALL * tq
    in_specs = lambda i: [pl.BlockSpec((1, tc, LANES), lambda bi, h: (bi, i, h)),
                          pl.BlockSpec((1, (i + 1) * tc, LANES), lambda bi, h: (bi, 0, nh + h)),
                          pl.BlockSpec((1, LANES, (i + 1) * tc), lambda bi, h: (bi, h, 0)),
                          pl.BlockSpec((4, LANES), lambda bi, h: (0, 0)),
                          pl.BlockSpec((1, LANES), lambda bi, h: (0, 0))]
    return _per_query_tile(
        partial(_diff_kernel, n_sub=TILES_PER_CALL, tq=tq, tk=tk, lam_init=lam_init), t // tc, (b, nh), in_specs,
        lambda i: pl.BlockSpec((1, tc, LANES), lambda bi, h: (bi, i, h)),
        jax.ShapeDtypeStruct((b, t, nh * LANES), BF16), _stream_scratch(2, LANES, tq, tk), "diff_attention",
        (main, main, vt, lam_p, subln_g), init)


def _mem_kv_kernel(mem_ref, g_ref, wk_ref, wv_ref, k_ref, vt_ref):
    mn = _rms(mem_ref[0], g_ref[0]).astype(BF16)
    k_ref[0, 0] = _dot(mn, wk_ref[0]).astype(BF16)
    vt_ref[0, 0] = _dot(mn, wv_ref[0]).T.astype(BF16)


def _mem_kv(mem, mem_norm_g, wk, wv):
    depth = wk.shape[0]
    b, m, d = mem.shape
    wspec = pl.BlockSpec((1, d, MEM_W), lambda l, bi: (l, 0, 0))
    return pl.pallas_call(
        _mem_kv_kernel,
        grid=(depth, b),
        in_specs=[pl.BlockSpec((1, m, d), lambda l, bi: (bi, 0, 0)),
                  pl.BlockSpec((1, 1, d), lambda l, bi: (l, 0, 0)), wspec, wspec],
        out_specs=[pl.BlockSpec((1, 1, m, MEM_W), lambda l, bi: (l, bi, 0, 0)),
                   pl.BlockSpec((1, 1, MEM_W, m), lambda l, bi: (l, bi, 0, 0))],
        out_shape=[jax.ShapeDtypeStruct((depth, b, m, MEM_W), BF16),
                   jax.ShapeDtypeStruct((depth, b, MEM_W, m), BF16)],
        compiler_params=_cparams(("parallel", "parallel")),
        name="mem_kv",
    )(mem, mem_norm_g.reshape(depth, 1, d), wk, wv)


def _post_mixer_kernel(*refs, n_in, n_part):
    a_refs, w_refs = refs[:n_in], refs[n_in:2 * n_in]
    x_ref, g1_ref, g2_ref, g3_ref, wq_ref, k_ref, vt_ref, wo_ref, o_ref = refs[2 * n_in:]
    rows = x_ref.shape[0] // n_part
    parts = [slice(r * rows, (r + 1) * rows) for r in range(n_part)]
    n_mem = k_ref.shape[2]
    heads = [(ch, hh) for ch in range(MEM_W // LANES) for hh in range(2)]
    ones = jnp.ones((ONES_ROWS, n_mem), BF16)

    def mixer_out(ps):
        y = None
        for a_ref, w_ref in zip(a_refs, w_refs):
            t = _dot(a_ref[ps, :], w_ref[...])
            y = t if y is None else y + t
        return y

    def scores(q):
        return [_dot_nt(k_ref[0, 0, :, ch * LANES:(ch + 1) * LANES], _half_mask(q[:, ch * LANES:(ch + 1) * LANES], hh))
                for ch, hh in heads]

    def attend(logits):
        outs = []
        for (ch, hh), st in zip(heads, logits):
            p = jnp.exp2(st - jnp.max(st, axis=0, keepdims=True)).astype(BF16)
            r0 = ch * LANES + hh * HEAD_DIM
            pv = _dot(jnp.concatenate([vt_ref[0, 0, r0:r0 + HEAD_DIM, :], ones], axis=0), p)
            outs.append(pv[:HEAD_DIM] / pv[HEAD_DIM:HEAD_DIM + 1])
        return jnp.concatenate(outs, axis=0).T.astype(BF16)

    ys = [mixer_out(ps) for ps in parts]
    xs = [x_ref[ps, :] + _rms(y, g1_ref[...]) for ps, y in zip(parts, ys)]
    qs = [(_dot(_rms(x, g2_ref[...]).astype(BF16), wq_ref[...]) * Q_SCALE_LOG2).astype(BF16) for x in xs]
    logits = [scores(q) for q in qs]
    os_ = [attend(lg) for lg in logits]
    for ps, x, o in zip(parts, xs, os_):
        o_ref[ps, :] = x + _rms(_dot(o, wo_ref[...]), g3_ref[...])


def _post_mixer(acts, weights, x2, g1, g2, g3, wq_all, mem_k, mem_vt, wo_all, layer, tm, rows_per_batch, n_part=2):
    n = x2.shape[0]
    m = mem_k.shape[2]
    const = lambda shape: pl.BlockSpec(shape, lambda i: (0, 0))
    row = pl.BlockSpec((tm, D_MODEL), lambda i: (i, 0))
    gspec = const((1, D_MODEL))
    return pl.pallas_call(
        partial(_post_mixer_kernel, n_in=len(acts), n_part=n_part),
        grid=(n // tm,),
        in_specs=[pl.BlockSpec((tm, a.shape[1]), lambda i: (i, 0)) for a in acts]
                 + [const(w.shape) if l is None else _layer_spec(w, l) for w, l in weights]
                 + [row, gspec, gspec, gspec, _layer_spec(wq_all, layer),
                    pl.BlockSpec((1, 1, m, MEM_W), lambda i: (layer, i // rows_per_batch, 0, 0)),
                    pl.BlockSpec((1, 1, MEM_W, m), lambda i: (layer, i // rows_per_batch, 0, 0)),
                    _layer_spec(wo_all, layer)],
        out_specs=row,
        out_shape=jax.ShapeDtypeStruct((n, D_MODEL), F32),
        compiler_params=_cparams(("parallel",)),
        name="post_mixer",
    )(*acts, *[w for w, _ in weights], x2, g1, g2, g3, wq_all, mem_k, mem_vt, wo_all)


def _ffn_kernel(x_ref, gin_ref, gout_ref, wg_ref, wu_ref, wd_ref, o_ref, *, chunk):
    x = x_ref[...]
    h = _rms(x, gin_ref[...]).astype(BF16)
    d_ff = wg_ref.shape[1]
    y = None
    for c0 in range(0, d_ff, chunk):
        gate = _dot(h, wg_ref[:, c0:c0 + chunk])
        up = _dot(h, wu_ref[:, c0:c0 + chunk])
        a = (gate * (1.0 / (1.0 + jnp.exp(-gate))) * up).astype(BF16)
        t = _dot(a, wd_ref[c0:c0 + chunk, :])
        y = t if y is None else y + t
    o_ref[...] = x + _rms(y, gout_ref[...])


def _ffn(x2, gin, gout, wg_all, wu_all, wd_all, layer, tm, chunk):
    n = x2.shape[0]
    const = lambda shape: pl.BlockSpec(shape, lambda i: (0, 0), pipeline_mode=pl.Buffered(1))
    wspec = lambda w: _layer_spec(w, layer, pipeline_mode=pl.Buffered(1))
    return pl.pallas_call(
        partial(_ffn_kernel, chunk=chunk),
        grid=(n // tm,),
        in_specs=[pl.BlockSpec((tm, D_MODEL), lambda i: (i, 0)), const((1, D_MODEL)), const((1, D_MODEL)),
                  wspec(wg_all), wspec(wu_all), wspec(wd_all)],
        out_specs=pl.BlockSpec((tm, D_MODEL), lambda i: (i, 0)),
        out_shape=jax.ShapeDtypeStruct((n, D_MODEL), F32),
        compiler_params=_cparams(("parallel",)),
        name="ffn",
    )(x2, gin, gout, wg_all, wu_all, wd_all)


def _even_weights(w_in, w_out):
    kvw = NSA_KV_GROUPS * HEAD_DIM
    offs = [int(o) for o in np.cumsum((FOX_W, FOX_W, FOX_W, FOX_HEADS, NSA_W) + (kvw,) * 6 + (3 * NSA_HEADS,))]
    fk0, fv0, fl0, nq0, kc0, vc0, ks0, vs0, kw0, vw0, gl0, end = offs
    paired = [HEAD_DIM * (g * NSA_HPG + n) for n in range(NSA_HPG) for g in range(NSA_KV_GROUPS)]
    cols = lambda a, b: w_in[:, a:b]
    w = jnp.concatenate([cols(0, fv0)] + [cols(nq0 + h, nq0 + h + HEAD_DIM) for h in paired]
                        + [cols(ks0, vs0), cols(kw0, vw0), cols(kc0, vc0), cols(vc0, ks0),
                           cols(fv0, fl0), cols(vs0, kw0), cols(vw0, gl0)], axis=1).astype(BF16)
    w_small = jnp.concatenate([cols(fl0, nq0), cols(gl0, end),
                               jnp.zeros((D_MODEL, LANES - FOX_HEADS - 3 * NSA_HEADS), w_in.dtype)],
                              axis=1).astype(BF16)
    w_out_fox = w_out[:FOX_W].astype(BF16)
    w_out_nsa = jnp.concatenate([w_out[FOX_W + h:FOX_W + h + HEAD_DIM] for h in paired], axis=0).astype(BF16)
    return w, w_small, w_out_fox, w_out_nsa


def _overlap_matrix_t(t, ncp):
    nc = (t - CMP_BLOCK) // CMP_STRIDE + 1
    ns = t // SLC_BLOCK
    cs = np.arange(nc) * CMP_STRIDE
    ss = np.arange(ns) * SLC_BLOCK
    ov = np.clip(np.minimum(cs[:, None] + CMP_BLOCK, ss[None, :] + SLC_BLOCK)
                 - np.maximum(cs[:, None], ss[None, :]), 0, None) / CMP_BLOCK
    full = np.zeros((LANES // NSA_KV_GROUPS, ncp), np.float32)
    full[:ns, :nc] = ov.T
    return jnp.asarray(full, BF16), ns


def _compress_weights(pos_k, w1_k, w2_k, pos_v, w1_v, w2_v):
    g = NSA_KV_GROUPS
    pos2, w1bd, w2bd = [], [], []
    for pos, w1, w2 in ((pos_k, w1_k, w2_k), (pos_v, w1_v, w2_v)):
        pos2.append(jnp.tile(pos.astype(F32), (1, g)))
        w1l = w1.reshape(CMP_BLOCK, HEAD_DIM, CMP_HIDDEN).astype(BF16)
        bd = jnp.zeros((CMP_BLOCK, g * HEAD_DIM, g * CMP_HIDDEN), BF16)
        w2g = jnp.zeros((g * CMP_HIDDEN, g * HEAD_DIM), BF16)
        for gi in range(g):
            bd = bd.at[:, gi * HEAD_DIM:(gi + 1) * HEAD_DIM, gi * CMP_HIDDEN:(gi + 1) * CMP_HIDDEN].set(w1l)
            w2g = w2g.at[gi * CMP_HIDDEN:(gi + 1) * CMP_HIDDEN, gi * HEAD_DIM:(gi + 1) * HEAD_DIM].set(
                w2.astype(BF16))
        w1bd.append(bd)
        w2bd.append(w2g)
    return jnp.stack(pos2), jnp.stack(w1bd), jnp.stack(w2bd)


def kernel(x, mem, positions, sandwich_g, mem_norm_g, ev_w_in, ev_fox_fbias, ev_cmp_pos_k, ev_cmp_w1_k, ev_cmp_w2_k, ev_cmp_pos_v, ev_cmp_w1_v, ev_cmp_w2_v, ev_w_out, od_w_in, od_lambda, od_subln_g, od_w_out, ca_wq, ca_wk, ca_wv, ca_wo, ffn_wg, ffn_wu, ffn_wd):
    b, t, d = x.shape
    depth = sandwich_g.shape[0]
    n = b * t
    tm = 1024
    tm_ffn = 1024
    tm_mix = 1024
    tq, tk = 512, 256
    tk_diff = 512
    assert d == D_MODEL and t % tq == 0 and tq % tk == 0 and WINDOW % tk == 0 and t % (TILES_PER_CALL * tq) == 0 and t % tm == 0 and n % tm_ffn == 0

    tabs = tuple(a.reshape(b, t, LANES) for a in _rope_tables(positions, 512))
    mem_k, mem_vt = _mem_kv(mem, mem_norm_g, ca_wk.astype(BF16), ca_wv.astype(BF16))
    ncp = t // CMP_STRIDE
    overlap_t, ns = _overlap_matrix_t(t, ncp)
    assert ns <= overlap_t.shape[0]
    n_sel = min(SLC_TOPK, ns)
    hot = jnp.asarray((np.arange(t)[:, None] // SLC_BLOCK == np.arange(LANES)[None, :] % overlap_t.shape[0])
                      .astype(np.float32), BF16)
    gain = lambda l, j: sandwich_g[l, j].reshape(1, d)

    bf = lambda w: w.astype(BF16)
    od_w_in_b, od_w_out_b, ca_wq_b, ca_wo_b = bf(od_w_in), bf(od_w_out), bf(ca_wq), bf(ca_wo)
    ffn_wg_b, ffn_wu_b, ffn_wd_b = bf(ffn_wg), bf(ffn_wu), bf(ffn_wd)
    x2 = x.reshape(n, d)
    dead = {}
    for layer in range(depth):
        x3 = x2.reshape(b, t, d)
        if layer % 2 == 0:
            e = layer // 2
            w, w_small, w_out_fox, w_out_nsa = _even_weights(ev_w_in[e], ev_w_out[e])
            fb_row = jnp.zeros((1, LANES), F32).at[0, :FOX_HEADS].set(ev_fox_fbias[e].astype(F32))
            main, cmp_in, vt, small = _even_proj(x3, gain(layer, 0), w, w_small, fb_row, tabs, tm)
            aq, ak = _fox_aug(small, 512)
            o_fox = dead["fox"] = _fox_attention(main, aq, ak, vt, tq, tk, dead.get("fox"))
            kc, vct = _compress(cmp_in, *_compress_weights(
                ev_cmp_pos_k[e], ev_cmp_w1_k[e], ev_cmp_w2_k[e], ev_cmp_pos_v[e], ev_cmp_w1_v[e], ev_cmp_w2_v[e]))
            oc, sel = dead["sel"] = _nsa_select(main, kc, vct, overlap_t, small, 256, ns, n_sel, dead.get("sel"))
            o_nsa = dead["nsa"] = _nsa_flash(main, sel, oc, small, hot, vt, tq, tk, dead.get("nsa"))
            acts, w_outs = [o_fox.reshape(n, FOX_W), o_nsa.reshape(n, NSA_W)], [(w_out_fox, None), (w_out_nsa, None)]
        else:
            o = layer // 2
            main, vt = _odd_proj(x3, gain(layer, 0), od_w_in_b, o, tabs, tm)
            lam_init = 0.8 - 0.6 * math.exp(-0.3 * layer)
            lam_p = jnp.pad(od_lambda[o].astype(F32), ((0, 0), (0, LANES - HEAD_DIM)))
            attn = dead["diff"] = _diff_attention(main, vt, lam_p, od_subln_g[o].reshape(1, LANES).astype(F32),
                                                  tq, tk_diff, lam_init, dead.get("diff"))
            acts, w_outs = [attn.reshape(n, D_MODEL)], [(od_w_out_b, o)]
        x2 = _post_mixer(acts, w_outs, x2, gain(layer, 1), gain(layer, 2), gain(layer, 3),
                         ca_wq_b, mem_k, mem_vt, ca_wo_b, layer, tm_mix, t // tm_mix, n_part=tm_mix // 256)
        x2 = _ffn(x2, gain(layer, 4), gain(layer, 5), ffn_wg_b, ffn_wu_b, ffn_wd_b, layer, tm_ffn, 256)
    return x2.reshape(b, t, d)
```

```python
import math
from functools import partial

import numpy as np
import jax
import jax.numpy as jnp
from jax import lax
from jax.experimental import pallas as pl
from jax.experimental.pallas import tpu as pltpu

F32 = jnp.float32
BF16 = jnp.bfloat16

D_MODEL = 1024
HEAD_DIM = 64
LANES = 128
ROPE_DIM = HEAD_DIM // 4
ROPE_THETA = 500000.0
FOX_HEADS = 8
NSA_HEADS = 8
NSA_KV_GROUPS = 2
NSA_HPG = NSA_HEADS // NSA_KV_GROUPS
CMP_BLOCK = 32
CMP_STRIDE = 16
CMP_HIDDEN = 2 * HEAD_DIM
SLC_BLOCK = 64
SLC_TOPK = 16
WINDOW = 512
DIFF_HEADS = 8
MEM_HEADS = 4
MEM_W = MEM_HEADS * HEAD_DIM
RMS_EPS = 1e-6
Q_SCALE = HEAD_DIM ** -0.5
LOG2E = math.log2(math.e)
Q_SCALE_LOG2 = Q_SCALE * LOG2E
NEG = -1e30
AUG_PER_HEAD = 6
ONES_ROWS = 16

FOX_W = FOX_HEADS * HEAD_DIM
NSA_W = NSA_HEADS * HEAD_DIM
CH_FQ, CH_FK, CH_NQ, CH_KS, CH_KW = 0, 4, 8, 12, 13
EV_MAIN = 14 * LANES
EV_CMP = 2 * LANES
VT_FV, VT_VS, VT_VW = 0, 4, 5
EV_VT = 6 * LANES
EV_ROPE_CHUNKS = tuple(range(CH_NQ, CH_NQ + 4)) + (CH_KS, CH_KW, EV_MAIN // LANES)
EV_QSCALE_CHUNKS = tuple(range(CH_FQ, CH_FQ + 4)) + tuple(range(CH_NQ, CH_NQ + 4))
OD_MAIN = 2 * D_MODEL
OD_VT = D_MODEL

VMEM_LIMIT = 56 * 1024 * 1024


def _layer_spec(stacked, layer, **kw):
    zeros = (0,) * (stacked.ndim - 1)
    return pl.BlockSpec((None,) + stacked.shape[1:], lambda *idx: (layer,) + zeros, **kw)


def _cparams(sem):
    return pltpu.CompilerParams(dimension_semantics=sem, vmem_limit_bytes=VMEM_LIMIT)


def _rms(x, g):
    return x * lax.rsqrt(jnp.mean(x * x, axis=-1, keepdims=True) + RMS_EPS) * g


def _split3(x):
    hi = x.astype(BF16)
    r1 = x - hi.astype(F32)
    mid = r1.astype(BF16)
    lo = (r1 - mid.astype(F32)).astype(BF16)
    return hi, mid, lo


def _dot(a, b):
    return jnp.dot(a, b, preferred_element_type=F32)


def _dot_nt(a, b):
    return lax.dot_general(a, b, (((1,), (1,)), ((), ())), preferred_element_type=F32)


def _lane_iota(n=LANES):
    return lax.broadcasted_iota(jnp.int32, (1, n), 1)


def _half_mask(q2, half):
    return jnp.where(_lane_iota() // HEAD_DIM == half, q2, jnp.zeros_like(q2))


def _transposed(x):
    return x.astype(F32).T.astype(BF16)


def _half_mask_t(qt, half):
    row = lax.broadcasted_iota(jnp.int32, (qt.shape[0], 1), 0)
    return jnp.where(row // HEAD_DIM == half, qt, jnp.zeros_like(qt))


def _positions_t(j, tk, q0, tq):
    kpos = j * tk + lax.broadcasted_iota(jnp.int32, (tk, 1), 0)
    qpos = q0 + lax.broadcasted_iota(jnp.int32, (1, tq), 1)
    return kpos, qpos


def _stream(base, n_full, n_tail, scores, mask, values, st_ref, acc_ref):
    n_chain, dv, tq = acc_ref.shape

    def park(j, slot):
        for c, st in enumerate(scores(j)):
            st_ref[slot, c] = st

    def step(j, slot, stats, masked, prefetch):
        if prefetch:
            park(j + 1, 1 - slot)
        parts = []
        for c in range(n_chain):
            st = st_ref[slot, c]
            if masked:
                st = mask(j, st)
            m, _ = stats[c]
            m_new = jnp.maximum(m, jnp.max(st, axis=0, keepdims=True))
            p = jnp.exp2(st - m_new).astype(BF16)
            vt1 = jnp.concatenate([values(c, j), jnp.ones((ONES_ROWS, p.shape[0]), BF16)], axis=0)
            parts.append((m_new, jnp.exp2(m - m_new), _dot(vt1, p)))
        out = []
        for c, (m_new, alpha, pv) in enumerate(parts):
            acc_ref[c] = alpha * acc_ref[c] + pv[:dv]
            out.append((m_new, alpha * stats[c][1] + pv[dv:dv + 1]))
        return tuple(out)

    acc_ref[...] = jnp.zeros_like(acc_ref)
    park(base, 0)
    stats = ((jnp.full((1, tq), NEG, F32), jnp.zeros((1, tq), F32)),) * n_chain
    n_tiles = n_full + n_tail
    for s_ in range(n_tiles):
        stats = step(base + s_, s_ % 2, stats, s_ >= n_full, s_ + 1 < n_tiles)
    return [(l, acc_ref[c]) for c, (_, l) in enumerate(stats)]


TILES_PER_CALL = 4


def _per_query_tile(kernel_fn, n_q, grid, in_specs_fn, out_spec_fn, out_shape, scratch, name, args, init):
    out = jnp.zeros(out_shape.shape, out_shape.dtype) if init is None else init
    for i in range(n_q):
        in_specs = list(in_specs_fn(i)) + [pl.BlockSpec(memory_space=pl.ANY)]
        out = pl.pallas_call(
            partial(kernel_fn, i=i), grid=grid, in_specs=in_specs, out_specs=out_spec_fn(i), out_shape=out_shape,
            scratch_shapes=scratch, input_output_aliases={len(args): 0},
            compiler_params=_cparams(("parallel",) * len(grid)), name=f"{name}_q{i}",
        )(*args, out)
    return out


def _stream_scratch(n_chain, dv, tq, tk):
    return [pltpu.VMEM((2, n_chain, tk, tq), F32), pltpu.VMEM((n_chain, dv, tq), F32)]


def _rope_kernel(pos_ref, inv_ref, m1_ref, m2_ref, c_ref, s1_ref, s2_ref):
    ang = pos_ref[...].astype(F32) * inv_ref[...]
    c_ref[...] = jnp.cos(ang)
    sn = jnp.sin(ang)
    s1_ref[...] = -sn * m1_ref[...]
    s2_ref[...] = sn * m2_ref[...]


def _rope_tables(positions, tm):
    n = positions.size
    inv = ROPE_THETA ** (-jnp.arange(0, ROPE_DIM, 2, dtype=F32) / ROPE_DIM)
    lane = np.arange(LANES) % HEAD_DIM
    half = ROPE_DIM // 2
    inv_l = jnp.where(lane < ROPE_DIM, inv[lane % half], 0.0).reshape(1, LANES).astype(F32)
    m1 = jnp.asarray((lane < half).astype(np.float32)).reshape(1, LANES)
    m2 = jnp.asarray(((lane >= half) & (lane < ROPE_DIM)).astype(np.float32)).reshape(1, LANES)
    row = pl.BlockSpec((1, LANES), lambda i: (0, 0))
    tab = pl.BlockSpec((tm, LANES), lambda i: (i, 0))
    return pl.pallas_call(
        _rope_kernel,
        grid=(n // tm,),
        in_specs=[pl.BlockSpec((tm, 1), lambda i: (i, 0)), row, row, row],
        out_specs=[tab, tab, tab],
        out_shape=[jax.ShapeDtypeStruct((n, LANES), F32)] * 3,
        compiler_params=_cparams(("parallel",)),
        name="rope_tables",
    )(positions.reshape(n, 1), inv_l, m1, m2)


def _apply_rope(y, c, s1, s2):
    half = ROPE_DIM // 2
    return y * c + pltpu.roll(y, LANES - half, 1) * s1 + pltpu.roll(y, half, 1) * s2


def _project_chunks(h, w_ref, tabs, dests, rope_chunks, qscale_chunks):
    c, s1, s2 = tabs
    wide = 2 * LANES
    where = [(kind, ref, k) for kind, ref, n in dests for k in range(n)]
    for ch2 in range(len(where) // 2):
        y2 = _dot(h, w_ref[:, ch2 * wide:(ch2 + 1) * wide])
        for ch in (2 * ch2, 2 * ch2 + 1):
            y = y2[:, (ch % 2) * LANES:(ch % 2 + 1) * LANES]
            if ch in rope_chunks:
                y = _apply_rope(y, c, s1, s2)
            if ch in qscale_chunks:
                y = y * Q_SCALE_LOG2
            kind, ref, k = where[ch]
            if kind == "cols":
                ref[0, k * LANES:(k + 1) * LANES, :] = y.T.astype(BF16)
            elif kind == "rows_f32":
                ref[0, k] = y
            else:
                ref[0, :, k * LANES:(k + 1) * LANES] = y.astype(BF16)


def _even_proj_kernel(x_ref, g_ref, w_ref, ws_ref, fb_ref, c_ref, s1_ref, s2_ref,
                      main_ref, cmp_ref, vt_ref, small_ref):
    h = _rms(x_ref[0], g_ref[...]).astype(BF16)
    dests = [("rows", main_ref, EV_MAIN // LANES), ("rows_f32", cmp_ref, EV_CMP // LANES),
             ("cols", vt_ref, EV_VT // LANES)]
    _project_chunks(h, w_ref, (c_ref[0], s1_ref[0], s2_ref[0]), dests, EV_ROPE_CHUNKS, EV_QSCALE_CHUNKS)
    ys = _dot(h, ws_ref[...])
    z = ys + fb_ref[...]
    log_f = jnp.minimum(z, 0.0) - jnp.log(1.0 + jnp.exp(-jnp.abs(z)))
    gate = 1.0 / (1.0 + jnp.exp(-ys))
    small_ref[0] = jnp.where(_lane_iota() < FOX_HEADS, log_f, gate)


def _even_proj(x3, g, w, w_small, fb_row, tabs, tm):
    b, t, _ = x3.shape
    const = lambda shape: pl.BlockSpec(shape, lambda bi, i: (0, 0))
    tab = pl.BlockSpec((1, tm, LANES), lambda bi, i: (bi, i, 0))
    return pl.pallas_call(
        _even_proj_kernel,
        grid=(b, t // tm),
        in_specs=[pl.BlockSpec((1, tm, D_MODEL), lambda bi, i: (bi, i, 0)), const((1, D_MODEL)),
                  const((D_MODEL, EV_MAIN + EV_CMP + EV_VT)), const((D_MODEL, LANES)), const((1, LANES)),
                  tab, tab, tab],
        out_specs=[pl.BlockSpec((1, tm, EV_MAIN), lambda bi, i: (bi, i, 0)),
                   pl.BlockSpec((1, EV_CMP // LANES, tm, LANES), lambda bi, i: (bi, 0, i, 0)),
                   pl.BlockSpec((1, EV_VT, tm), lambda bi, i: (bi, 0, i)), tab],
        out_shape=[jax.ShapeDtypeStruct((b, t, EV_MAIN), BF16),
                   jax.ShapeDtypeStruct((b, EV_CMP // LANES, t, LANES), F32),
                   jax.ShapeDtypeStruct((b, EV_VT, t), BF16), jax.ShapeDtypeStruct((b, t, LANES), F32)],
        compiler_params=_cparams(("parallel", "parallel")),
        name="even_proj",
    )(x3, g, w, w_small, fb_row, *tabs)


def _odd_proj_kernel(x_ref, g_ref, w_ref, c_ref, s1_ref, s2_ref, main_ref, vt_ref):
    h = _rms(x_ref[0], g_ref[...]).astype(BF16)
    n_main = OD_MAIN // LANES
    dests = [("rows", main_ref, n_main), ("cols", vt_ref, OD_VT // LANES)]
    _project_chunks(h, w_ref, (c_ref[0], s1_ref[0], s2_ref[0]), dests, tuple(range(n_main)),
                    tuple(range(n_main // 2)))


def _odd_proj(x3, g, w_all, layer, tabs, tm):
    b, t, _ = x3.shape
    const = lambda shape: pl.BlockSpec(shape, lambda bi, i: (0, 0))
    tab = pl.BlockSpec((1, tm, LANES), lambda bi, i: (bi, i, 0))
    return pl.pallas_call(
        _odd_proj_kernel,
        grid=(b, t // tm),
        in_specs=[pl.BlockSpec((1, tm, D_MODEL), lambda bi, i: (bi, i, 0)), const((1, D_MODEL)),
                  _layer_spec(w_all, layer), tab, tab, tab],
        out_specs=[pl.BlockSpec((1, tm, OD_MAIN), lambda bi, i: (bi, i, 0)),
                   pl.BlockSpec((1, OD_VT, tm), lambda bi, i: (bi, 0, i))],
        out_shape=[jax.ShapeDtypeStruct((b, t, OD_MAIN), BF16), jax.ShapeDtypeStruct((b, OD_VT, t), BF16)],
        compiler_params=_cparams(("parallel", "parallel")),
        name="odd_proj",
    )(x3, g, w_all, *tabs)


def _fox_aug_kernel(lf_ref, tril_ref, e_ref, one_ref, aq_ref, ak_ref, carry_ref):
    @pl.when(pl.program_id(1) == 0)
    def _():
        carry_ref[...] = jnp.zeros_like(carry_ref)

    tril = tril_ref[...]
    sub = tril.shape[0]
    carry = carry_ref[...]
    blocks = []
    for r0 in range(0, lf_ref.shape[1], sub):
        c = carry
        for piece in _split3(lf_ref[0, r0:r0 + sub, :]):
            c = c + _dot(tril, piece)
        blocks.append(c)
        carry = c[-1:, :]
    carry_ref[...] = carry
    aug = one_ref[...]
    for r, piece in enumerate(_split3(jnp.concatenate(blocks, axis=0) * LOG2E)):
        aug = aug + _dot(piece, e_ref[r])
    aq_ref[0] = aug[:, :LANES].astype(BF16)
    ak_ref[0] = aug[:, LANES:].astype(BF16)


def _fox_aug(small, tc, sub=128):
    b, t, _ = small.shape
    tril = jnp.asarray(np.tril(np.ones((sub, sub), np.float32)), BF16)
    spread = np.zeros((3, LANES, 2 * LANES), np.float32)
    ones = np.zeros((1, 2 * LANES), np.float32)
    for h in range(FOX_HEADS):
        for r in range(3):
            spread[r, h, AUG_PER_HEAD * h + r] = 1.0
            spread[r, h, LANES + AUG_PER_HEAD * h + 3 + r] = -1.0
            ones[0, AUG_PER_HEAD * h + 3 + r] = 1.0
            ones[0, LANES + AUG_PER_HEAD * h + r] = 1.0
    const2 = lambda shape: pl.BlockSpec(shape, lambda bi, i: (0,) * len(shape))
    blk = pl.BlockSpec((1, tc, LANES), lambda bi, i: (bi, i, 0))
    return pl.pallas_call(
        _fox_aug_kernel,
        grid=(b, t // tc),
        in_specs=[blk, const2((sub, sub)), const2((3, LANES, 2 * LANES)), const2((1, 2 * LANES))],
        out_specs=[blk, blk],
        out_shape=[jax.ShapeDtypeStruct((b, t, LANES), BF16)] * 2,
        scratch_shapes=[pltpu.VMEM((1, LANES), F32)],
        compiler_params=_cparams(("parallel", "arbitrary")),
        name="fox_aug",
    )(small, tril, jnp.asarray(spread, BF16), jnp.asarray(ones))


def _fox_kernel(q_ref, aq_ref, k_ref, ak_ref, vt_ref, *rest, i, n_sub, tq, tk):
    o_ref, st_ref, acc_ref = rest[-3:]
    pair = pl.program_id(1)
    row = lax.broadcasted_iota(jnp.int32, (LANES, 1), 0)
    for s in range(n_sub):
        qi, rows = i * n_sub + s, slice(s * tq, (s + 1) * tq)
        qt = _transposed(q_ref[0, rows, :])
        qat = _transposed(aq_ref[0, rows, :])
        qcats = []
        for hh in range(2):
            head = 2 * pair + hh
            in_head = (row >= AUG_PER_HEAD * head) & (row < AUG_PER_HEAD * (head + 1))
            qcats.append(jnp.concatenate([_half_mask_t(qt, hh), jnp.where(in_head, qat, jnp.zeros_like(qat))],
                                         axis=0))

        def scores(j, qcats=qcats):
            ks = j * tk
            kcat = jnp.concatenate([k_ref[0, pl.ds(ks, tk), :], ak_ref[0, pl.ds(ks, tk), :]], axis=1)
            return tuple(_dot(kcat, qcats[hh]) for hh in range(2))

        def mask(j, st, qi=qi):
            kpos, qpos = _positions_t(j, tk, qi * tq, tq)
            return jnp.where(kpos <= qpos, st, NEG)

        def values(hh, j):
            return vt_ref[0, hh * HEAD_DIM:(hh + 1) * HEAD_DIM, pl.ds(j * tk, tk)]

        res = _stream(0, qi * (tq // tk), tq // tk, scores, mask, values, st_ref, acc_ref)
        ot = jnp.concatenate([acc / l for l, acc in res], axis=0)
        o_ref[0, rows, :] = ot.T.astype(BF16)


def _fox_attention(main, aq, ak, vt, tq, tk, init):
    b, t, _ = main.shape
    tc = TILES_PER_CALL * tq
    in_specs = lambda i: [pl.BlockSpec((1, tc, LANES), lambda bi, p: (bi, i, CH_FQ + p)),
                          pl.BlockSpec((1, tc, LANES), lambda bi, p: (bi, i, 0)),
                          pl.BlockSpec((1, (i + 1) * tc, LANES), lambda bi, p: (bi, 0, CH_FK + p)),
                          pl.BlockSpec((1, (i + 1) * tc, LANES), lambda bi, p: (bi, 0, 0)),
                          pl.BlockSpec((1, LANES, (i + 1) * tc), lambda bi, p: (bi, VT_FV + p, 0))]
    return _per_query_tile(
        partial(_fox_kernel, n_sub=TILES_PER_CALL, tq=tq, tk=tk), t // tc, (b, FOX_HEADS // 2), in_specs,
        lambda i: pl.BlockSpec((1, tc, LANES), lambda bi, p: (bi, i, p)),
        jax.ShapeDtypeStruct((b, t, FOX_W), BF16), _stream_scratch(2, HEAD_DIM, tq, tk), "fox_attention",
        (main, aq, main, ak, vt), init)


def _compress_kernel(x_ref, pos_ref, w1_ref, w2_ref, kc_ref, vct_ref):
    t = x_ref.shape[2]
    nchunk = t // CMP_STRIDE
    for kv, o_ref in enumerate((kc_ref, vct_ref)):
        first = second = None
        for l in range(CMP_STRIDE):
            xl = x_ref[0, kv, pl.ds(l, nchunk, stride=CMP_STRIDE), :]
            a = _dot((xl + pos_ref[kv, l:l + 1, :]).astype(BF16), w1_ref[kv, l])
            b = _dot((xl + pos_ref[kv, CMP_STRIDE + l:CMP_STRIDE + l + 1, :]).astype(BF16),
                     w1_ref[kv, CMP_STRIDE + l])
            first = a if first is None else first + a
            second = b if second is None else second + b
        hid = first + pltpu.roll(second, nchunk - 1, 0)
        out = _dot(jax.nn.gelu(hid, approximate=True).astype(BF16), w2_ref[kv])
        o_ref[0] = (out.T if o_ref is vct_ref else out).astype(BF16)


def _compress(cmp_in, pos2, w1bd, w2bd):
    b, _, t, _ = cmp_in.shape
    nchunk = t // CMP_STRIDE
    const = lambda a: pl.BlockSpec(a.shape, lambda bi: (0,) * a.ndim)
    return pl.pallas_call(
        _compress_kernel,
        grid=(b,),
        in_specs=[pl.BlockSpec((1, EV_CMP // LANES, t, LANES), lambda bi: (bi, 0, 0, 0)),
                  const(pos2), const(w1bd), const(w2bd)],
        out_specs=[pl.BlockSpec((1, nchunk, LANES), lambda bi: (bi, 0, 0)),
                   pl.BlockSpec((1, LANES, nchunk), lambda bi: (bi, 0, 0))],
        out_shape=[jax.ShapeDtypeStruct((b, nchunk, LANES), BF16), jax.ShapeDtypeStruct((b, LANES, nchunk), BF16)],
        compiler_params=_cparams(("parallel",)),
        name="nsa_compress",
    )(cmp_in, pos2, w1bd, w2bd)


def _gate_col(small, head, branch):
    idx = FOX_HEADS + 3 * head + branch
    return jnp.sum(jnp.where(_lane_iota() == idx, small, 0.0), axis=-1, keepdims=True)


def _nsa_select_kernel(q_ref, kc_ref, vct_ref, ovt_ref, small_ref, *rest, q_lo, tq, ns, n_sel):
    oc_ref, sel_ref = rest[-2:]
    q0 = q_lo + pl.program_id(1) * tq
    lane = _lane_iota()
    kc = kc_ref[0]
    vct = vct_ref[0]
    ncp = kc.shape[0]
    small = small_ref[0]
    qpos = q0 + lax.broadcasted_iota(jnp.int32, (1, tq), 1)
    cmp_end = lax.broadcasted_iota(jnp.int32, (ncp, 1), 0) * CMP_STRIDE + (CMP_BLOCK - 1)
    cmask = cmp_end <= qpos
    psum = [jnp.zeros((ncp, tq), F32) for _ in range(NSA_KV_GROUPS)]
    logits = [[_dot_nt(kc, _half_mask(q_ref[0, :, n * LANES:(n + 1) * LANES], g)) for g in range(NSA_KV_GROUPS)]
              for n in range(NSA_HPG)]
    for n in range(NSA_HPG):
        ots = []
        for g in range(NSA_KV_GROUPS):
            z = jnp.where(cmask, logits[n][g], -jnp.inf)
            m = jnp.max(z, axis=0, keepdims=True)
            m = jnp.where(m == -jnp.inf, 0.0, m)
            p = jnp.exp2(z - m)
            p = p / jnp.maximum(jnp.sum(p, axis=0, keepdims=True), 1e-30)
            psum[g] = psum[g] + p
            ots.append(_dot(vct[g * HEAD_DIM:(g + 1) * HEAD_DIM], p.astype(BF16)))
        gate = jnp.where(lane < HEAD_DIM, _gate_col(small, n, 0), _gate_col(small, NSA_HPG + n, 0))
        oc_ref[0, :, n * LANES:(n + 1) * LANES] = gate * jnp.concatenate(ots, axis=0).T

    nsp = ovt_ref.shape[0]
    blk = lax.broadcasted_iota(jnp.int32, (nsp, 1), 0)
    cur = qpos // SLC_BLOCK
    valid = blk * SLC_BLOCK <= qpos
    forced = (blk == 0) | (blk == cur) | (blk == cur - 1)
    scores = []
    for g in range(NSA_KV_GROUPS):
        imp = jnp.zeros((nsp, tq), F32)
        for piece in _split3(psum[g]):
            imp = imp + _dot(ovt_ref[...], piece)
        scores.append(jnp.where(valid, jnp.where(forced, jnp.inf, imp), -jnp.inf))
    slab = 8
    masks = []
    for g in range(NSA_KV_GROUPS):
        slabs = [scores[g][r:r + slab] for r in range(0, nsp, slab)]
        ranks = [jnp.zeros((slab, tq), jnp.int32) for _ in slabs]
        for i in range(ns):
            row = scores[g][i:i + 1, :]
            for r, sl in enumerate(slabs):
                if slab * r >= ns:
                    continue
                if slab * r > i:
                    ahead = row >= sl
                elif slab * (r + 1) - 1 <= i:
                    ahead = row > sl
                else:
                    ahead = (row > sl) | ((row == sl) & (blk[slab * r:slab * (r + 1)] > i))
                ranks[r] = jnp.where(ahead, ranks[r] + 1, ranks[r])
        rank = jnp.concatenate(ranks, axis=0)
        masks.append(jnp.where((rank < n_sel) & (blk < ns), 0.0, NEG))
    sel_ref[0] = jnp.concatenate(masks, axis=0).T.astype(BF16)


def _nsa_select(main, kc, vct, overlap_t, small, tq, ns, n_sel, init, n_span=4):
    b, t, _ = main.shape
    nsp = overlap_t.shape[0]
    span = t // n_span
    steps = span // tq
    shapes = [jax.ShapeDtypeStruct((b, t, NSA_W), F32), jax.ShapeDtypeStruct((b, t, NSA_KV_GROUPS * nsp), BF16)]
    outs = [jnp.zeros(s.shape, s.dtype) for s in shapes] if init is None else init
    for k in range(n_span):
        q_hi = (k + 1) * span
        ncl = min(-(-(q_hi // CMP_STRIDE) // LANES) * LANES, kc.shape[1])
        row = lambda bi, i, k=k: (bi, k * steps + i, 0)
        outs = pl.pallas_call(
            partial(_nsa_select_kernel, q_lo=k * span, tq=tq, ns=min(ns, q_hi // SLC_BLOCK), n_sel=n_sel),
            grid=(b, steps),
            in_specs=[pl.BlockSpec((1, tq, NSA_W), lambda bi, i, k=k: (bi, k * steps + i, CH_NQ * LANES // NSA_W)),
                      pl.BlockSpec((1, ncl, LANES), lambda bi, i: (bi, 0, 0)),
                      pl.BlockSpec((1, LANES, ncl), lambda bi, i: (bi, 0, 0)),
                      pl.BlockSpec((nsp, ncl), lambda bi, i: (0, 0)),
                      pl.BlockSpec((1, tq, LANES), row),
                      pl.BlockSpec(memory_space=pl.ANY), pl.BlockSpec(memory_space=pl.ANY)],
            out_specs=[pl.BlockSpec((1, tq, NSA_W), row), pl.BlockSpec((1, tq, NSA_KV_GROUPS * nsp), row)],
            out_shape=shapes,
            input_output_aliases={5: 0, 6: 1},
            compiler_params=_cparams(("parallel", "parallel")),
            name=f"nsa_select_s{k}",
        )(main, kc, vct, overlap_t, small, *outs)
    return outs


def _nsa_flash_kernel(q_ref, sel_ref, oc_ref, small_ref, ks_ref, kw_ref, hot_ref, vst_ref, vwt_ref, *rest,
                      i, n_sub, tq, tk):
    o_ref, st_ref, acc_ref = rest[-3:]
    n = pl.program_id(1)
    lane = _lane_iota()
    rows_of = lambda g: slice(g * HEAD_DIM, (g + 1) * HEAD_DIM)
    for s in range(n_sub):
        qi, rows = i * n_sub + s, slice(s * tq, (s + 1) * tq)
        q0 = qi * tq
        small = small_ref[0, rows, :]
        qt = _transposed(q_ref[0, rows, :])
        selt = _transposed(sel_ref[0, rows, :])
        qhs = [_half_mask_t(qt, g) for g in range(NSA_KV_GROUPS)]
        qcats = [jnp.concatenate([qhs[g], _half_mask_t(selt, g)], axis=0) for g in range(NSA_KV_GROUPS)]

        def sel_scores(j, qcats=qcats):
            ks = j * tk
            kcat = jnp.concatenate([ks_ref[0, pl.ds(ks, tk), :], hot_ref[pl.ds(ks, tk), :]], axis=1)
            return tuple(_dot(kcat, qcats[g]) for g in range(NSA_KV_GROUPS))

        def sel_mask(j, st, q0=q0):
            kpos, qpos = _positions_t(j, tk, q0, tq)
            return jnp.where(kpos <= qpos, st, NEG)

        def win_scores(j, qhs=qhs):
            kw = kw_ref[0, pl.ds(j * tk, tk), :]
            return tuple(_dot(kw, qhs[g]) for g in range(NSA_KV_GROUPS))

        def win_mask(j, st, q0=q0):
            kpos, qpos = _positions_t(j, tk, q0, tq)
            return jnp.where((kpos <= qpos) & (kpos > qpos - WINDOW), st, NEG)

        values = lambda ref: lambda g, j: ref[0, rows_of(g), pl.ds(j * tk, tk)]
        n_end = (qi + 1) * (tq // tk)
        win_lo = max(n_end - (WINDOW + tq) // tk, 0)
        o = oc_ref[0, rows, :]
        for branch, args in ((1, (0, qi * (tq // tk), tq // tk, sel_scores, sel_mask, values(vst_ref))),
                             (2, (win_lo, 0, n_end - win_lo, win_scores, win_mask, values(vwt_ref)))):
            res = _stream(*args, st_ref, acc_ref)
            ot = jnp.concatenate([acc / l for l, acc in res], axis=0)
            gate = jnp.where(lane < HEAD_DIM, _gate_col(small, n, branch), _gate_col(small, NSA_HPG + n, branch))
            o = o + gate * ot.T
        o_ref[0, rows, :] = o.astype(BF16)


def _nsa_flash(main, sel, oc, small, hot, vt, tq, tk, init):
    b, t, _ = main.shape
    tc = TILES_PER_CALL * tq
    tile = lambda i, ch: pl.BlockSpec((1, tc, LANES), lambda bi, n: (bi, i, ch + n))
    shared = lambda i: pl.BlockSpec((1, tc, LANES), lambda bi, n: (bi, i, 0))
    full = lambda i, ch: pl.BlockSpec((1, (i + 1) * tc, LANES), lambda bi, n: (bi, 0, ch))
    vfull = lambda i, ch: pl.BlockSpec((1, LANES, (i + 1) * tc), lambda bi, n: (bi, ch, 0))
    in_specs = lambda i: [tile(i, CH_NQ), shared(i), tile(i, 0), shared(i), full(i, CH_KS), full(i, CH_KW),
                          pl.BlockSpec(((i + 1) * tc, LANES), lambda bi, n: (0, 0)),
                          vfull(i, VT_VS), vfull(i, VT_VW)]
    return _per_query_tile(
        partial(_nsa_flash_kernel, n_sub=TILES_PER_CALL, tq=tq, tk=tk), t // tc, (b, NSA_HPG), in_specs,
        lambda i: tile(i, 0),
        jax.ShapeDtypeStruct((b, t, NSA_W), BF16), _stream_scratch(NSA_KV_GROUPS, HEAD_DIM, tq, tk), "nsa_flash",
        (main, sel, oc, small, main, main, hot, vt, vt), init)


def _diff_kernel(q_ref, k_ref, vt_ref, lam_ref, g_ref, *rest, i, n_sub, tq, tk, lam_init):
    o_ref, st_ref, acc_ref = rest[-3:]
    lp = lam_ref[...]
    lam = (jnp.exp(jnp.sum(lp[0:1] * lp[1:2], axis=-1, keepdims=True))
           - jnp.exp(jnp.sum(lp[2:3] * lp[3:4], axis=-1, keepdims=True)) + lam_init)
    for s in range(n_sub):
        qi, rows = i * n_sub + s, slice(s * tq, (s + 1) * tq)
        q2 = q_ref[0, rows, :]
        qhs = [_half_mask(q2, comp) for comp in range(2)]

        def scores(j, qhs=qhs):
            k2 = k_ref[0, pl.ds(j * tk, tk), :]
            return tuple(_dot_nt(k2, qhs[comp]) for comp in range(2))

        def mask(j, st, qi=qi):
            kpos, qpos = _positions_t(j, tk, qi * tq, tq)
            return jnp.where(kpos <= qpos, st, NEG)

        def values(comp, j):
            return vt_ref[0, :, pl.ds(j * tk, tk)]

        (l1, acc1), (l2, acc2) = _stream(0, qi * (tq // tk), tq // tk, scores, mask, values, st_ref, acc_ref)
        o = (acc1 / l1 - lam * (acc2 / l2)).T
        o_ref[0, rows, :] = (_rms(o, g_ref[...]) * (1.0 - lam_init)).astype(BF16)


def _diff_attention(main, vt, lam_p, subln_g, tq, tk, lam_init, init):
    b, t, _ = main.shape
    nh = DIFF_HEADS
    tc = TILES_PER_CALL * tq
    in_specs = lambda i: [pl.BlockSpec((1, tc, LANES), lambda bi, h: (bi, i, h)),
                          pl.BlockSpec((1, (i + 1) * tc, LANES), lambda bi, h: (bi, 0, nh + h)),
                          pl.BlockSpec((1, LANES, (i + 1) * tc), lambda bi, h: (bi, h, 0)),
                          pl.BlockSpec((4, LANES), lambda bi, h: (0, 0)),
                          pl.BlockSpec((1, LANES), lambda bi, h: (0, 0))]
    return _per_query_tile(
        partial(_diff_kernel, n_sub=TILES_PER_CALL, tq=tq, tk=tk, lam_init=lam_init), t // tc, (b, nh), in_specs,
        lambda i: pl.BlockSpec((1, tc, LANES), lambda bi, h: (bi, i, h)),
        jax.ShapeDtypeStruct((b, t, nh * LANES), BF16), _stream_scratch(2, LANES, tq, tk), "diff_attention",
        (main, main, vt, lam_p, subln_g), init)


def _mem_kv_kernel(mem_ref, g_ref, wk_ref, wv_ref, k_ref, vt_ref):
    mn = _rms(mem_ref[0], g_ref[0]).astype(BF16)
    k_ref[0, 0] = _dot(mn, wk_ref[0]).astype(BF16)
    vt_ref[0, 0] = _dot(mn, wv_ref[0]).T.astype(BF16)


def _mem_kv(mem, mem_norm_g, wk, wv):
    depth = wk.shape[0]
    b, m, d = mem.shape
    wspec = pl.BlockSpec((1, d, MEM_W), lambda l, bi: (l, 0, 0))
    return pl.pallas_call(
        _mem_kv_kernel,
        grid=(depth, b),
        in_specs=[pl.BlockSpec((1, m, d), lambda l, bi: (bi, 0, 0)),
                  pl.BlockSpec((1, 1, d), lambda l, bi: (l, 0, 0)), wspec, wspec],
        out_specs=[pl.BlockSpec((1, 1, m, MEM_W), lambda l, bi: (l, bi, 0, 0)),
                   pl.BlockSpec((1, 1, MEM_W, m), lambda l, bi: (l, bi, 0, 0))],
        out_shape=[jax.ShapeDtypeStruct((depth, b, m, MEM_W), BF16),
                   jax.ShapeDtypeStruct((depth, b, MEM_W, m), BF16)],
        compiler_params=_cparams(("parallel", "parallel")),
        name="mem_kv",
    )(mem, mem_norm_g.reshape(depth, 1, d), wk, wv)


def _post_mixer_kernel(*refs, n_in, n_part):
    a_refs, w_refs = refs[:n_in], refs[n_in:2 * n_in]
    x_ref, g1_ref, g2_ref, g3_ref, wq_ref, k_ref, vt_ref, wo_ref, o_ref = refs[2 * n_in:]
    rows = x_ref.shape[0] // n_part
    parts = [slice(r * rows, (r + 1) * rows) for r in range(n_part)]
    n_mem = k_ref.shape[2]
    heads = [(ch, hh) for ch in range(MEM_W // LANES) for hh in range(2)]
    ones = jnp.ones((ONES_ROWS, n_mem), BF16)

    def mixer_out(ps):
        y = None
        for a_ref, w_ref in zip(a_refs, w_refs):
            t = _dot(a_ref[ps, :], w_ref[...])
            y = t if y is None else y + t
        return y

    def scores(q):
        return [_dot_nt(k_ref[0, 0, :, ch * LANES:(ch + 1) * LANES], _half_mask(q[:, ch * LANES:(ch + 1) * LANES], hh))
                for ch, hh in heads]

    def attend(logits):
        outs = []
        for (ch, hh), st in zip(heads, logits):
            p = jnp.exp2(st - jnp.max(st, axis=0, keepdims=True)).astype(BF16)
            r0 = ch * LANES + hh * HEAD_DIM
            pv = _dot(jnp.concatenate([vt_ref[0, 0, r0:r0 + HEAD_DIM, :], ones], axis=0), p)
            outs.append(pv[:HEAD_DIM] / pv[HEAD_DIM:HEAD_DIM + 1])
        return jnp.concatenate(outs, axis=0).T.astype(BF16)

    ys = [mixer_out(ps) for ps in parts]
    xs = [x_ref[ps, :] + _rms(y, g1_ref[...]) for ps, y in zip(parts, ys)]
    qs = [(_dot(_rms(x, g2_ref[...]).astype(BF16), wq_ref[...]) * Q_SCALE_LOG2).astype(BF16) for x in xs]
    logits = [scores(q) for q in qs]
    os_ = [attend(lg) for lg in logits]
    for ps, x, o in zip(parts, xs, os_):
        o_ref[ps, :] = x + _rms(_dot(o, wo_ref[...]), g3_ref[...])


def _post_mixer(acts, weights, x2, g1, g2, g3, wq_all, mem_k, mem_vt, wo_all, layer, tm, rows_per_batch, n_part=2):
    n = x2.shape[0]
    m = mem_k.shape[2]
    const = lambda shape: pl.BlockSpec(shape, lambda i: (0, 0))
    row = pl.BlockSpec((tm, D_MODEL), lambda i: (i, 0))
    gspec = const((1, D_MODEL))
    return pl.pallas_call(
        partial(_post_mixer_kernel, n_in=len(acts), n_part=n_part),
        grid=(n // tm,),
        in_specs=[pl.BlockSpec((tm, a.shape[1]), lambda i: (i, 0)) for a in acts]
                 + [const(w.shape) if l is None else _layer_spec(w, l) for w, l in weights]
                 + [row, gspec, gspec, gspec, _layer_spec(wq_all, layer),
                    pl.BlockSpec((1, 1, m, MEM_W), lambda i: (layer, i // rows_per_batch, 0, 0)),
                    pl.BlockSpec((1, 1, MEM_W, m), lambda i: (layer, i // rows_per_batch, 0, 0)),
                    _layer_spec(wo_all, layer)],
        out_specs=row,
        out_shape=jax.ShapeDtypeStruct((n, D_MODEL), F32),
        compiler_params=_cparams(("parallel",)),
        name="post_mixer",
    )(*acts, *[w for w, _ in weights], x2, g1, g2, g3, wq_all, mem_k, mem_vt, wo_all)


def _ffn_kernel(x_ref, gin_ref, gout_ref, wg_ref, wu_ref, wd_ref, o_ref, *, chunk):
    x = x_ref[...]
    h = _rms(x, gin_ref[...]).astype(BF16)
    d_ff = wg_ref.shape[1]
    y = None
    for c0 in range(0, d_ff, chunk):
        gate = _dot(h, wg_ref[:, c0:c0 + chunk])
        up = _dot(h, wu_ref[:, c0:c0 + chunk])
        a = (gate * (1.0 / (1.0 + jnp.exp(-gate))) * up).astype(BF16)
        t = _dot(a, wd_ref[c0:c0 + chunk, :])
        y = t if y is None else y + t
    o_ref[...] = x + _rms(y, gout_ref[...])


def _ffn(x2, gin, gout, wg_all, wu_all, wd_all, layer, tm, chunk):
    n = x2.shape[0]
    const = lambda shape: pl.BlockSpec(shape, lambda i: (0, 0), pipeline_mode=pl.Buffered(1))
    wspec = lambda w: _layer_spec(w, layer, pipeline_mode=pl.Buffered(1))
    return pl.pallas_call(
        partial(_ffn_kernel, chunk=chunk),
        grid=(n // tm,),
        in_specs=[pl.BlockSpec((tm, D_MODEL), lambda i: (i, 0)), const((1, D_MODEL)), const((1, D_MODEL)),
                  wspec(wg_all), wspec(wu_all), wspec(wd_all)],
        out_specs=pl.BlockSpec((tm, D_MODEL), lambda i: (i, 0)),
        out_shape=jax.ShapeDtypeStruct((n, D_MODEL), F32),
        compiler_params=_cparams(("parallel",)),
        name="ffn",
    )(x2, gin, gout, wg_all, wu_all, wd_all)


def _even_weights(w_in, w_out):
    kvw = NSA_KV_GROUPS * HEAD_DIM
    offs = [int(o) for o in np.cumsum((FOX_W, FOX_W, FOX_W, FOX_HEADS, NSA_W) + (kvw,) * 6 + (3 * NSA_HEADS,))]
    fk0, fv0, fl0, nq0, kc0, vc0, ks0, vs0, kw0, vw0, gl0, end = offs
    paired = [HEAD_DIM * (g * NSA_HPG + n) for n in range(NSA_HPG) for g in range(NSA_KV_GROUPS)]
    cols = lambda a, b: w_in[:, a:b]
    w = jnp.concatenate([cols(0, fv0)] + [cols(nq0 + h, nq0 + h + HEAD_DIM) for h in paired]
                        + [cols(ks0, vs0), cols(kw0, vw0), cols(kc0, vc0), cols(vc0, ks0),
                           cols(fv0, fl0), cols(vs0, kw0), cols(vw0, gl0)], axis=1).astype(BF16)
    w_small = jnp.concatenate([cols(fl0, nq0), cols(gl0, end),
                               jnp.zeros((D_MODEL, LANES - FOX_HEADS - 3 * NSA_HEADS), w_in.dtype)],
                              axis=1).astype(BF16)
    w_out_fox = w_out[:FOX_W].astype(BF16)
    w_out_nsa = jnp.concatenate([w_out[FOX_W + h:FOX_W + h + HEAD_DIM] for h in paired], axis=0).astype(BF16)
    return w, w_small, w_out_fox, w_out_nsa


def _overlap_matrix_t(t, ncp):
    nc = (t - CMP_BLOCK) // CMP_STRIDE + 1
    ns = t // SLC_BLOCK
    cs = np.arange(nc) * CMP_STRIDE
    ss = np.arange(ns) * SLC_BLOCK
    ov = np.clip(np.minimum(cs[:, None] + CMP_BLOCK, ss[None, :] + SLC_BLOCK)
                 - np.maximum(cs[:, None], ss[None, :]), 0, None) / CMP_BLOCK
    full = np.zeros((LANES // NSA_KV_GROUPS, ncp), np.float32)
    full[:ns, :nc] = ov.T
    return jnp.asarray(full, BF16), ns


def _compress_weights(pos_k, w1_k, w2_k, pos_v, w1_v, w2_v):
    g = NSA_KV_GROUPS
    pos2, w1bd, w2bd = [], [], []
    for pos, w1, w2 in ((pos_k, w1_k, w2_k), (pos_v, w1_v, w2_v)):
        pos2.append(jnp.tile(pos.astype(F32), (1, g)))
        w1l = w1.reshape(CMP_BLOCK, HEAD_DIM, CMP_HIDDEN).astype(BF16)
        bd = jnp.zeros((CMP_BLOCK, g * HEAD_DIM, g * CMP_HIDDEN), BF16)
        w2g = jnp.zeros((g * CMP_HIDDEN, g * HEAD_DIM), BF16)
        for gi in range(g):
            bd = bd.at[:, gi * HEAD_DIM:(gi + 1) * HEAD_DIM, gi * CMP_HIDDEN:(gi + 1) * CMP_HIDDEN].set(w1l)
            w2g = w2g.at[gi * CMP_HIDDEN:(gi + 1) * CMP_HIDDEN, gi * HEAD_DIM:(gi + 1) * HEAD_DIM].set(
                w2.astype(BF16))
        w1bd.append(bd)
        w2bd.append(w2g)
    return jnp.stack(pos2), jnp.stack(w1bd), jnp.stack(w2bd)


def kernel(x, mem, positions, sandwich_g, mem_norm_g, ev_w_in, ev_fox_fbias, ev_cmp_pos_k, ev_cmp_w1_k, ev_cmp_w2_k, ev_cmp_pos_v, ev_cmp_w1_v, ev_cmp_w2_v, ev_w_out, od_w_in, od_lambda, od_subln_g, od_w_out, ca_wq, ca_wk, ca_wv, ca_wo, ffn_wg, ffn_wu, ffn_wd):
    b, t, d = x.shape
    depth = sandwich_g.shape[0]
    n = b * t
    tm = 1024
    tm_ffn = 1024
    tm_mix = 1024
    tq, tk = 512, 256
    tk_diff = 512
    t_rope, t_aug, tq_sel = min(2048, n), min(1024, t), min(512, t // 4)
    assert d == D_MODEL and t % tq == 0 and tq % tk == 0 and WINDOW % tk == 0 and t % (TILES_PER_CALL * tq) == 0 and t % tm == 0 and n % tm_ffn == 0

    tabs = tuple(a.reshape(b, t, LANES) for a in _rope_tables(positions, t_rope))
    mem_k, mem_vt = _mem_kv(mem, mem_norm_g, ca_wk.astype(BF16), ca_wv.astype(BF16))
    ncp = t // CMP_STRIDE
    overlap_t, ns = _overlap_matrix_t(t, ncp)
    assert ns <= overlap_t.shape[0]
    n_sel = min(SLC_TOPK, ns)
    hot = jnp.asarray((np.arange(t)[:, None] // SLC_BLOCK == np.arange(LANES)[None, :] % overlap_t.shape[0])
                      .astype(np.float32), BF16)
    gain = lambda l, j: sandwich_g[l, j].reshape(1, d)

    bf = lambda w: w.astype(BF16)
    od_w_in_b, od_w_out_b, ca_wq_b, ca_wo_b = bf(od_w_in), bf(od_w_out), bf(ca_wq), bf(ca_wo)
    ffn_wg_b, ffn_wu_b, ffn_wd_b = bf(ffn_wg), bf(ffn_wu), bf(ffn_wd)
    x2 = x.reshape(n, d)
    dead = {}
    for layer in range(depth):
        x3 = x2.reshape(b, t, d)
        if layer % 2 == 0:
            e = layer // 2
            w, w_small, w_out_fox, w_out_nsa = _even_weights(ev_w_in[e], ev_w_out[e])
            fb_row = jnp.zeros((1, LANES), F32).at[0, :FOX_HEADS].set(ev_fox_fbias[e].astype(F32))
            main, cmp_in, vt, small = _even_proj(x3, gain(layer, 0), w, w_small, fb_row, tabs, tm)
            aq, ak = _fox_aug(small, t_aug)
            o_fox = dead["fox"] = _fox_attention(main, aq, ak, vt, tq, tk, dead.get("fox"))
            kc, vct = _compress(cmp_in, *_compress_weights(
                ev_cmp_pos_k[e], ev_cmp_w1_k[e], ev_cmp_w2_k[e], ev_cmp_pos_v[e], ev_cmp_w1_v[e], ev_cmp_w2_v[e]))
            oc, sel = dead["sel"] = _nsa_select(main, kc, vct, overlap_t, small, tq_sel, ns, n_sel, dead.get("sel"))
            o_nsa = dead["nsa"] = _nsa_flash(main, sel, oc, small, hot, vt, tq, tk, dead.get("nsa"))
            acts, w_outs = [o_fox.reshape(n, FOX_W), o_nsa.reshape(n, NSA_W)], [(w_out_fox, None), (w_out_nsa, None)]
        else:
            o = layer // 2
            main, vt = _odd_proj(x3, gain(layer, 0), od_w_in_b, o, tabs, tm)
            lam_init = 0.8 - 0.6 * math.exp(-0.3 * layer)
            lam_p = jnp.pad(od_lambda[o].astype(F32), ((0, 0), (0, LANES - HEAD_DIM)))
            attn = dead["diff"] = _diff_attention(main, vt, lam_p, od_subln_g[o].reshape(1, LANES).astype(F32),
                                                  tq, tk_diff, lam_init, dead.get("diff"))
            acts, w_outs = [attn.reshape(n, D_MODEL)], [(od_w_out_b, o)]
        x2 = _post_mixer(acts, w_outs, x2, gain(layer, 1), gain(layer, 2), gain(layer, 3),
                         ca_wq_b, mem_k, mem_vt, ca_wo_b, layer, tm_mix, t // tm_mix, n_part=tm_mix // 256)
        x2 = _ffn(x2, gain(layer, 4), gain(layer, 5), ffn_wg_b, ffn_wu_b, ffn_wd_b, layer, tm_ffn, 256)
    return x2.reshape(b, t, d)
```
